```python
import jax, jax.numpy as jnp
from jax import lax
import numpy as np

D_MODEL = 1024
BATCH = 8
SEQ = 8192
DEPTH = 2
DEC_BATCH = 16
DEC_SEQ = 16
PAST_LEN = 1024

CHUNK = 64
LIN_BLOCK = 16
N_EVEN = (DEPTH + 1) // 2
N_ODD = DEPTH // 2
F32 = jnp.float32
EPS = 1e-6

H_A = 8
N_A = 64
A_DIM = H_A * N_A
A_W_RANK = 64
A_A_RANK = 64
A_G_RANK = 128
A_PROJ = 3 * A_DIM + A_W_RANK + A_A_RANK + A_G_RANK
A_SPLITS = (A_DIM, 2 * A_DIM, 3 * A_DIM, 3 * A_DIM + A_W_RANK, 3 * A_DIM + A_W_RANK + A_A_RANK)
RWKV_LN_EPS = 64e-5
H_B = 4
DK_B = 128
DV_B = 128
CONV_W = 4
B_CONV_DIM = H_B * (2 * DK_B + DV_B)
B_PROJ = B_CONV_DIM + 2 * H_B + H_B * DV_B
B_SPLITS = (B_CONV_DIM, B_CONV_DIM + H_B, B_CONV_DIM + 2 * H_B)
EVEN_PROJ = A_PROJ + B_PROJ
EVEN_MIX = A_DIM + H_B * DV_B
H_C = 4
DK_C = 64
DV_C = 128
C_A_RANK = 16
GLA_NORMALIZER = 16.0
C_PROJ = H_C * (2 * DK_C + 2 * DV_C) + C_A_RANK
C_SPLITS = (H_C * DK_C, 2 * H_C * DK_C, 2 * H_C * DK_C + H_C * DV_C, 2 * H_C * DK_C + H_C * DV_C + C_A_RANK)
H_D = 4
DK_D = 128
DV_D = 128
D_PROJ = H_D * (2 * DK_D + 2 * DV_D)
D_SPLITS = (H_D * DK_D, 2 * H_D * DK_D, 2 * H_D * DK_D + H_D * DV_D)
ROPE_BASE = 10000.0
ODD_PROJ = C_PROJ + D_PROJ
ODD_MIX = H_C * DV_C + H_D * DV_D
N_MEM = 256
MEM_HEADS = 4
MEM_HD = D_MODEL // MEM_HEADS
D_FF = 2816

kernel_name = 'hybrid_streaming_encoder_step'


def _rmsnorm(x, w):
    xf = x.astype(F32)
    y = xf * lax.rsqrt(jnp.mean(xf * xf, axis=-1, keepdims=True) + EPS)
    return (y * w.astype(F32)).astype(x.dtype)


def _swiglu(h, wg, wu, wd):
    return (jax.nn.silu(h @ wg) * (h @ wu)) @ wd


def _l2norm(x):
    xf = x.astype(F32)
    return xf * lax.rsqrt(jnp.sum(xf * xf, axis=-1, keepdims=True) + EPS)


def _head_rmsnorm(x, w):
    xf = x.astype(F32)
    return xf * lax.rsqrt(jnp.mean(xf * xf, axis=-1, keepdims=True) + EPS) * w.astype(F32)


def _head_layernorm(x, eps):
    xf = x.astype(F32)
    mu = jnp.mean(xf, axis=-1, keepdims=True)
    xc = xf - mu
    return xc * lax.rsqrt(jnp.mean(xc * xc, axis=-1, keepdims=True) + eps)


def _rotary(x, pos):
    d = x.shape[-1]
    inv = ROPE_BASE ** (-jnp.arange(0, d, 2, dtype=F32) / d)
    ang = pos[:, None] * inv[None, :]
    cos, sin = jnp.cos(ang)[:, None, :], jnp.sin(ang)[:, None, :]
    x1, x2 = x[..., : d // 2].astype(F32), x[..., d // 2:].astype(F32)
    return jnp.concatenate([x1 * cos - x2 * sin, x1 * sin + x2 * cos], axis=-1)


def _pad_time(x, pad):
    return jnp.pad(x, [(0, 0), (0, pad)] + [(0, 0)] * (x.ndim - 2))


def _to_blocks(x, L):
    b, t, h, d = x.shape
    return x.reshape(b, t // L, L, h, d).transpose(1, 0, 3, 2, 4)


def _to_blocks3(x, L):
    b, t, h = x.shape
    return x.reshape(b, t // L, L, h).transpose(1, 0, 3, 2)


def _from_blocks(x):
    n, b, h, l, d = x.shape
    return x.transpose(1, 0, 3, 2, 4).reshape(b, n * l, h, d)


def _chunked_decay_linear_attn(q, k, v, log_a, s0, block):
    t = q.shape[1]
    pad = (-t) % block
    q, k, v, log_a = (_to_blocks(_pad_time(z.astype(F32), pad), block) for z in (q, k, v, log_a))
    cum = jnp.cumsum(log_a, axis=3)
    total = cum[:, :, :, -1:, :]
    q_in = q * jnp.exp(cum)
    k_in = k * jnp.exp(-cum)
    k_out = k * jnp.exp(total - cum)
    causal = jnp.tril(jnp.ones((block, block), dtype=bool))
    scores = jnp.where(causal, jnp.einsum('nbhld,nbhmd->nbhlm', q_in, k_in), 0.0)
    intra = jnp.einsum('nbhlm,nbhmv->nbhlv', scores, v)
    dec = jnp.exp(jnp.swapaxes(total, -1, -2))

    def step(s, inp):
        qi, ko, vi, di, oi = inp
        o = jnp.einsum('bhld,bhdv->bhlv', qi, s) + oi
        s = s * di + jnp.einsum('bhld,bhlv->bhdv', ko, vi)
        return s, o

    s_fin, o = lax.scan(step, s0.astype(F32), (q_in, k_out, v, dec, intra))
    return _from_blocks(o)[:, :t], s_fin


def _chunked_gated_delta(q, k, v, g, beta, s0):
    t = q.shape[1]
    L = min(CHUNK, t)
    pad = (-t) % L
    q, k, v = (_to_blocks(_pad_time(z.astype(F32), pad), L) for z in (q, k, v))
    g, beta = (_to_blocks3(_pad_time(z.astype(F32), pad), L) for z in (g, beta))
    dv = v.shape[-1]
    G = jnp.cumsum(g, axis=-1)
    diff = G[..., :, None] - G[..., None, :]
    strict = jnp.tril(jnp.ones((L, L), dtype=bool), -1)
    incl = jnp.tril(jnp.ones((L, L), dtype=bool))
    gam_strict = jnp.where(strict, jnp.exp(jnp.where(strict, diff, 0.0)), 0.0)
    gam_incl = jnp.where(incl, jnp.exp(jnp.where(incl, diff, 0.0)), 0.0)
    eG = jnp.exp(G)
    m = jnp.eye(L, dtype=F32) + beta[..., :, None] * jnp.einsum('nbhld,nbhmd->nbhlm', k, k) * gam_strict
    rhs = jnp.concatenate([beta[..., None] * v, (beta * eG)[..., None] * k], axis=-1)
    sol = lax.linalg.triangular_solve(m, rhs, left_side=True, lower=True, unit_diagonal=True)
    vb, wk = sol[..., :dv], sol[..., dv:]
    qk = jnp.einsum('nbhld,nbhmd->nbhlm', q, k) * gam_incl
    qg = q * eG[..., None]
    kdec = k * jnp.exp(G[..., -1:] - G)[..., None]
    gl = eG[..., -1][..., None, None]

    def step(s, inp):
        vb_i, wk_i, qk_i, qg_i, kd_i, gl_i = inp
        u = vb_i - jnp.einsum('bhld,bhdv->bhlv', wk_i, s)
        o = jnp.einsum('bhld,bhdv->bhlv', qg_i, s) + jnp.einsum('bhlm,bhmv->bhlv', qk_i, u)
        s = gl_i * s + jnp.einsum('bhld,bhlv->bhdv', kd_i, u)
        return s, o

    s_fin, o = lax.scan(step, s0.astype(F32), (vb, wk, qk, qg, kdec, gl))
    return _from_blocks(o)[:, :t], s_fin


def _rwkv7_recurrence(r, w, k, v, a, b, s0):
    def step(s, inp):
        rt, wt, kt, vt, at, bt = inp
        sa = jnp.einsum('bhvk,bhk->bhv', s, at)
        s = s * wt[:, :, None, :] + sa[..., None] * bt[:, :, None, :] + vt[..., None] * kt[:, :, None, :]
        return s, jnp.einsum('bhvk,bhk->bhv', s, rt)

    xs = tuple(jnp.moveaxis(z.astype(F32), 1, 0) for z in (r, w, k, v, a, b))
    s_fin, y = lax.scan(step, s0.astype(F32), xs)
    return jnp.moveaxis(y, 0, 1), s_fin


def _even_mixer(h, shift0, rwkv0, conv0, delta0, W, i):
    b, t, _ = h.shape
    p = h @ W['even_w_in'][i]
    pa, pb = p[..., :A_PROJ], p[..., A_PROJ:]
    prev = jnp.concatenate([shift0.astype(pa.dtype), pa[:, :-1]], axis=1)
    xa = pa + (prev - pa) * W['rwkv_mu'][i]
    new_shift = pa[:, -1:]
    r, k, v, xw, xk_a, xg = jnp.split(xa, A_SPLITS, axis=-1)
    wlog = -jax.nn.softplus(-(W['rwkv_w0'][i] + jnp.tanh(xw) @ W['rwkv_w2'][i]).astype(F32)) - 0.5
    decay = jnp.exp(-jnp.exp(wlog))
    a = jax.nn.sigmoid((W['rwkv_a0'][i] + xk_a @ W['rwkv_a2'][i]).astype(F32))
    gate = jax.nn.sigmoid(xg) @ W['rwkv_g2'][i]
    heads = lambda z: z.astype(F32).reshape(b, t, H_A, N_A)
    r, k, v, a, decay = heads(r), heads(k), heads(v), heads(a), heads(decay)
    kk = _l2norm(k * W['rwkv_kk'][i].reshape(H_A, N_A))
    k = k * (1.0 + (a - 1.0) * W['rwkv_ka'][i].reshape(H_A, N_A))
    y, rwkv_new = _rwkv7_recurrence(r, decay, k, v, -kk, kk * a, rwkv0)
    y = _head_layernorm(y, RWKV_LN_EPS) * W['rwkv_ln_w'][i].reshape(H_A, N_A) + W['rwkv_ln_b'][i].reshape(H_A, N_A)
    y = y + jnp.sum(r * k * W['rwkv_rk'][i], axis=-1, keepdims=True) * v
    y_a = y.reshape(b, t, A_DIM) * gate
    qkv, ba, bb, z = jnp.split(pb, B_SPLITS, axis=-1)
    cat = jnp.concatenate([conv0.astype(qkv.dtype), qkv], axis=1)
    cw = W['delta_conv_w'][i]
    conv = cat[:, 0:t] * cw[0]
    for j in range(1, CONV_W):
        conv = conv + cat[:, j:j + t] * cw[j]
    new_conv = cat[:, t:]
    conv = jax.nn.silu(conv)
    q, kb, vb = jnp.split(conv, (H_B * DK_B, 2 * H_B * DK_B), axis=-1)
    q = _l2norm(q.reshape(b, t, H_B, DK_B)) * DK_B ** -0.5
    kb = _l2norm(kb.reshape(b, t, H_B, DK_B))
    vb = vb.reshape(b, t, H_B, DV_B)
    g = -jnp.exp(W['delta_A_log'][i].astype(F32)) * jax.nn.softplus((ba + W['delta_dt_bias'][i]).astype(F32))
    beta = jax.nn.sigmoid(bb.astype(F32))
    o, delta_new = _chunked_gated_delta(q, kb, vb, g, beta, delta0)
    o = _head_rmsnorm(o, W['delta_norm_w'][i]) * jax.nn.silu(z.reshape(b, t, H_B, DV_B).astype(F32))
    y_b = o.reshape(b, t, H_B * DV_B)
    y = jnp.concatenate([y_a, y_b], axis=-1).astype(h.dtype) @ W['even_w_out'][i]
    return y, new_shift, rwkv_new, new_conv, delta_new


def _odd_mixer(h, pos0, gla0, ret0, W, i):
    b, t, _ = h.shape
    p = h @ W['odd_w_in'][i]
    pc, pd = p[..., :C_PROJ], p[..., C_PROJ:]
    cq, ck, cv, cad, cg = jnp.split(pc, C_SPLITS, axis=-1)
    log_a = jax.nn.log_sigmoid((cad @ W['gla_a2'][i] + W['gla_a_bias'][i]).astype(F32)) / GLA_NORMALIZER
    q = cq.reshape(b, t, H_C, DK_C) * DK_C ** -0.5
    o, gla_new = _chunked_decay_linear_attn(q, ck.reshape(b, t, H_C, DK_C), cv.reshape(b, t, H_C, DV_C),
                                            log_a.reshape(b, t, H_C, DK_C), gla0, LIN_BLOCK)
    y_c = (_head_rmsnorm(o, W['gla_norm_w'][i]) * jax.nn.silu(cg.reshape(b, t, H_C, DV_C).astype(F32))).reshape(b, t, H_C * DV_C)
    dq, dk, dv, dg = jnp.split(pd, D_SPLITS, axis=-1)
    pos = (jnp.arange(t) + pos0).astype(F32)
    q = _rotary(dq.reshape(b, t, H_D, DK_D), pos)
    k = _rotary(dk.reshape(b, t, H_D, DK_D), pos) * DK_D ** -0.5
    log_gamma = jnp.log(1.0 - 2.0 ** (-5.0 - jnp.arange(H_D, dtype=F32)))
    la = jnp.broadcast_to(log_gamma[:, None], (b, t, H_D, 1))
    o, ret_new = _chunked_decay_linear_attn(q, k, dv.reshape(b, t, H_D, DV_D), la, ret0, LIN_BLOCK)
    y_d = (_head_layernorm(o, EPS) * jax.nn.silu(dg.reshape(b, t, H_D, DV_D).astype(F32))).reshape(b, t, H_D * DV_D)
    y = jnp.concatenate([y_c, y_d], axis=-1).astype(h.dtype) @ W['odd_w_out'][i]
    return y, gla_new, ret_new


def _mem_kv(mem, norm_w, wk, wv):
    m = _rmsnorm(mem, norm_w)
    b, n, _ = m.shape
    return (m @ wk).reshape(b, n, MEM_HEADS, MEM_HD), (m @ wv).reshape(b, n, MEM_HEADS, MEM_HD)


def _mem_attn(h, mk, mv, wq, wo):
    b, t, _ = h.shape
    q = (h @ wq).reshape(b, t, MEM_HEADS, MEM_HD)
    s = jnp.einsum('bthd,bmhd->bhtm', q, mk.astype(q.dtype)).astype(F32) * MEM_HD ** -0.5
    pr = jax.nn.softmax(s, axis=-1).astype(h.dtype)
    o = jnp.einsum('bhtm,bmhd->bthd', pr, mv.astype(h.dtype)).reshape(b, t, D_MODEL)
    return o @ wo


def _trunk(x, pos0, mem_k, mem_v, shift, rwkv, conv, delta, gla, ret, W):
    n_shift, n_rwkv, n_conv, n_delta, n_gla, n_ret = [], [], [], [], [], []
    for l in range(DEPTH):
        x = x + 0.5 * _swiglu(_rmsnorm(x, W['norm_ffn1'][l]), W['ffn1_wg'][l], W['ffn1_wu'][l], W['ffn1_wd'][l])
        h = _rmsnorm(x, W['norm_mix'][l])
        i = l // 2
        if l % 2 == 0:
            y, s1, s2, s3, s4 = _even_mixer(h, shift[i], rwkv[i], conv[i], delta[i], W, i)
            n_shift.append(s1)
            n_rwkv.append(s2)
            n_conv.append(s3)
            n_delta.append(s4)
        else:
            y, s5, s6 = _odd_mixer(h, pos0, gla[i], ret[i], W, i)
            n_gla.append(s5)
            n_ret.append(s6)
        x = x + y
        x = x + _mem_attn(_rmsnorm(x, W['norm_mem'][l]), mem_k[l], mem_v[l], W['mem_wq'][l], W['mem_wo'][l])
        x = x + 0.5 * _swiglu(_rmsnorm(x, W['norm_ffn2'][l]), W['ffn2_wg'][l], W['ffn2_wu'][l], W['ffn2_wd'][l])
    return _rmsnorm(x, W['final_norm']), (n_shift, n_rwkv, n_conv, n_delta, n_gla, n_ret)


def setup_inputs(seed: int = 0) -> dict:
    key = jax.random.key(seed)
    keys = iter(jax.random.split(key, 96))

    def nrm(shape, scale=1.0):
        return jax.random.normal(next(keys), shape, F32) * scale

    def unif(shape, lo, hi):
        return jax.random.uniform(next(keys), shape, F32, lo, hi)

    def gain(shape):
        return 1.0 + 0.05 * nrm(shape)

    D = D_MODEL
    return {
        'x_prompt': nrm((BATCH, SEQ, D)),
        'x_sample': nrm((DEC_BATCH, DEC_SEQ, D)),
        'mem_prompt': nrm((BATCH, N_MEM, D)),
        'state_rwkv_shift': nrm((N_EVEN, DEC_BATCH, 1, A_PROJ)),
        'state_rwkv': nrm((N_EVEN, DEC_BATCH, H_A, N_A, N_A), 0.3),
        'state_delta_conv': nrm((N_EVEN, DEC_BATCH, CONV_W - 1, B_CONV_DIM)),
        'state_delta': nrm((N_EVEN, DEC_BATCH, H_B, DK_B, DV_B), DK_B ** -0.5),
        'state_gla': nrm((N_ODD, DEC_BATCH, H_C, DK_C, DV_C), 0.3),
        'state_ret': nrm((N_ODD, DEC_BATCH, H_D, DK_D, DV_D), 0.3),
        'cache_mem_k': nrm((DEPTH, DEC_BATCH, N_MEM, MEM_HEADS, MEM_HD)),
        'cache_mem_v': nrm((DEPTH, DEC_BATCH, N_MEM, MEM_HEADS, MEM_HD)),
        'norm_ffn1': gain((DEPTH, D)),
        'ffn1_wg': nrm((DEPTH, D, D_FF), D ** -0.5),
        'ffn1_wu': nrm((DEPTH, D, D_FF), D ** -0.5),
        'ffn1_wd': nrm((DEPTH, D_FF, D), D_FF ** -0.5),
        'norm_mix': gain((DEPTH, D)),
        'even_w_in': nrm((N_EVEN, D, EVEN_PROJ), D ** -0.5),
        'even_w_out': nrm((N_EVEN, EVEN_MIX, D), EVEN_MIX ** -0.5),
        'rwkv_mu': unif((N_EVEN, A_PROJ), 0.0, 1.0),
        'rwkv_w0': unif((N_EVEN, A_DIM), -4.0, 0.0),
        'rwkv_w2': nrm((N_EVEN, A_W_RANK, A_DIM), 0.5 * A_W_RANK ** -0.5),
        'rwkv_a0': nrm((N_EVEN, A_DIM), 0.5),
        'rwkv_a2': nrm((N_EVEN, A_A_RANK, A_DIM), A_A_RANK ** -0.5),
        'rwkv_g2': nrm((N_EVEN, A_G_RANK, A_DIM), A_G_RANK ** -0.5),
        'rwkv_kk': 0.85 + 0.05 * nrm((N_EVEN, A_DIM)),
        'rwkv_ka': gain((N_EVEN, A_DIM)),
        'rwkv_rk': nrm((N_EVEN, H_A, N_A), 0.1),
        'rwkv_ln_w': gain((N_EVEN, A_DIM)),
        'rwkv_ln_b': nrm((N_EVEN, A_DIM), 0.02),
        'delta_conv_w': nrm((N_EVEN, CONV_W, B_CONV_DIM), 0.5),
        'delta_A_log': jnp.log(unif((N_EVEN, H_B), 1.0, 16.0)),
        'delta_dt_bias': nrm((N_EVEN, H_B), 0.1),
        'delta_norm_w': gain((N_EVEN, DV_B)),
        'odd_w_in': nrm((N_ODD, D, ODD_PROJ), D ** -0.5),
        'odd_w_out': nrm((N_ODD, ODD_MIX, D), ODD_MIX ** -0.5),
        'gla_a2': nrm((N_ODD, C_A_RANK, H_C * DK_C), C_A_RANK ** -0.5),
        'gla_a_bias': nrm((N_ODD, H_C * DK_C), 0.1),
        'gla_norm_w': gain((N_ODD, DV_C)),
        'norm_mem': gain((DEPTH, D)),
        'mem_norm_kv': gain((DEPTH, D)),
        'mem_wq': nrm((DEPTH, D, D), D ** -0.5),
        'mem_wk': nrm((DEPTH, D, D), D ** -0.5),
        'mem_wv': nrm((DEPTH, D, D), D ** -0.5),
        'mem_wo': nrm((DEPTH, D, D), D ** -0.5),
        'norm_ffn2': gain((DEPTH, D)),
        'ffn2_wg': nrm((DEPTH, D, D_FF), D ** -0.5),
        'ffn2_wu': nrm((DEPTH, D, D_FF), D ** -0.5),
        'ffn2_wd': nrm((DEPTH, D_FF, D), D_FF ** -0.5),
        'final_norm': gain((D,)),
    }


def reference(x_prompt, x_sample, mem_prompt, state_rwkv_shift, state_rwkv, state_delta_conv, state_delta,
              state_gla, state_ret, cache_mem_k, cache_mem_v, norm_ffn1, ffn1_wg, ffn1_wu, ffn1_wd, norm_mix,
              even_w_in, even_w_out, rwkv_mu, rwkv_w0, rwkv_w2, rwkv_a0, rwkv_a2, rwkv_g2, rwkv_kk, rwkv_ka,
              rwkv_rk, rwkv_ln_w, rwkv_ln_b, delta_conv_w, delta_A_log, delta_dt_bias, delta_norm_w, odd_w_in,
              odd_w_out, gla_a2, gla_a_bias, gla_norm_w, norm_mem, mem_norm_kv, mem_wq, mem_wk, mem_wv, mem_wo,
              norm_ffn2, ffn2_wg, ffn2_wu, ffn2_wd, final_norm):
    W = dict(norm_ffn1=norm_ffn1, ffn1_wg=ffn1_wg, ffn1_wu=ffn1_wu, ffn1_wd=ffn1_wd, norm_mix=norm_mix,
             even_w_in=even_w_in, even_w_out=even_w_out, rwkv_mu=rwkv_mu, rwkv_w0=rwkv_w0, rwkv_w2=rwkv_w2,
             rwkv_a0=rwkv_a0, rwkv_a2=rwkv_a2, rwkv_g2=rwkv_g2, rwkv_kk=rwkv_kk, rwkv_ka=rwkv_ka,
             rwkv_rk=rwkv_rk, rwkv_ln_w=rwkv_ln_w, rwkv_ln_b=rwkv_ln_b, delta_conv_w=delta_conv_w,
             delta_A_log=delta_A_log, delta_dt_bias=delta_dt_bias, delta_norm_w=delta_norm_w,
             odd_w_in=odd_w_in, odd_w_out=odd_w_out, gla_a2=gla_a2, gla_a_bias=gla_a_bias,
             gla_norm_w=gla_norm_w, norm_mem=norm_mem, mem_wq=mem_wq, mem_wo=mem_wo, norm_ffn2=norm_ffn2,
             ffn2_wg=ffn2_wg, ffn2_wu=ffn2_wu, ffn2_wd=ffn2_wd, final_norm=final_norm)
    dt = x_prompt.dtype
    bp = x_prompt.shape[0]
    pk, pv = [], []
    for l in range(DEPTH):
        mk, mv = _mem_kv(mem_prompt, mem_norm_kv[l], mem_wk[l], mem_wv[l])
        pk.append(mk)
        pv.append(mv)
    z_shift = [jnp.zeros((bp, 1, A_PROJ), F32)] * N_EVEN
    z_rwkv = [jnp.zeros((bp, H_A, N_A, N_A), F32)] * N_EVEN
    z_conv = [jnp.zeros((bp, CONV_W - 1, B_CONV_DIM), F32)] * N_EVEN
    z_delta = [jnp.zeros((bp, H_B, DK_B, DV_B), F32)] * N_EVEN
    z_gla = [jnp.zeros((bp, H_C, DK_C, DV_C), F32)] * N_ODD
    z_ret = [jnp.zeros((bp, H_D, DK_D, DV_D), F32)] * N_ODD
    y_prompt, ps = _trunk(x_prompt, 0, pk, pv, z_shift, z_rwkv, z_conv, z_delta, z_gla, z_ret, W)
    y_sample, ss = _trunk(x_sample, PAST_LEN,
                          [cache_mem_k[l] for l in range(DEPTH)], [cache_mem_v[l] for l in range(DEPTH)],
                          [state_rwkv_shift[i] for i in range(N_EVEN)], [state_rwkv[i] for i in range(N_EVEN)],
                          [state_delta_conv[i] for i in range(N_EVEN)], [state_delta[i] for i in range(N_EVEN)],
                          [state_gla[i] for i in range(N_ODD)], [state_ret[i] for i in range(N_ODD)], W)
    p_shift = jnp.stack(ps[0]).astype(dt)
    p_rwkv = jnp.stack(ps[1]).astype(dt)
    p_conv = jnp.stack(ps[2]).astype(dt)
    p_delta = jnp.stack(ps[3]).astype(dt)
    p_gla = jnp.stack(ps[4]).astype(dt)
    p_ret = jnp.stack(ps[5]).astype(dt)
    p_mem_k = jnp.stack(pk).astype(dt)
    p_mem_v = jnp.stack(pv).astype(dt)
    s_shift = jnp.stack(ss[0]).astype(dt)
    s_rwkv = jnp.stack(ss[1]).astype(dt)
    s_conv = jnp.stack(ss[2]).astype(dt)
    s_delta = jnp.stack(ss[3]).astype(dt)
    s_gla = jnp.stack(ss[4]).astype(dt)
    s_ret = jnp.stack(ss[5]).astype(dt)
    return (y_prompt, y_sample, p_shift, p_rwkv, p_conv, p_delta, p_gla, p_ret, p_mem_k, p_mem_v,
            s_shift, s_rwkv, s_conv, s_delta, s_gla, s_ret)
```

```python
import functools
import math

import jax
import jax.numpy as jnp
from jax import lax
from jax.experimental import pallas as pl
from jax.experimental.pallas import tpu as pltpu

F32 = jnp.float32
BF16 = jnp.bfloat16
EPS = 1e-6
RWKV_LN_EPS = 64e-5
GLA_NORMALIZER = 16.0
ROPE_BASE = 10000.0
PAST_LEN = 1024

LANES = 128
VMEM_LIMIT = 56 * 1024 * 1024

H_A, N_A = 8, 64
H_B, DK_B = 4, 128
H_C, DK_C, DV_C = 4, 64, 128
H_D, DK_D = 4, 128
CONV_W = 4
CONV_CARRY = 8


def _cparams(sem):
    return pltpu.CompilerParams(dimension_semantics=sem, vmem_limit_bytes=VMEM_LIMIT)


def _mm(a, b):
    return jnp.dot(a.astype(BF16), b.astype(BF16), preferred_element_type=F32)


def _mm_nt(a, b):
    return lax.dot_general(a.astype(BF16), b.astype(BF16), (((1,), (1,)), ((), ())),
                           preferred_element_type=F32)


def _mm_tn(a, b):
    return lax.dot_general(a.astype(BF16), b.astype(BF16), (((0,), (0,)), ((), ())),
                           preferred_element_type=F32)


def _split3(x):
    hi = x.astype(BF16)
    r = x - hi.astype(F32)
    mid = r.astype(BF16)
    lo = (r - mid.astype(F32)).astype(BF16)
    return hi, mid, lo


def _mm_exact_lhs01(a01, x):
    hi, mid, lo = _split3(x)
    a = a01.astype(BF16)
    return (jnp.dot(a, hi, preferred_element_type=F32) + jnp.dot(a, mid, preferred_element_type=F32)
            + jnp.dot(a, lo, preferred_element_type=F32))


def _mm_exact_rhs01(x, b01):
    hi, mid, lo = _split3(x)
    b = b01.astype(BF16)
    return (jnp.dot(hi, b, preferred_element_type=F32) + jnp.dot(mid, b, preferred_element_type=F32)
            + jnp.dot(lo, b, preferred_element_type=F32))


def _iota(shape, dim):
    return lax.broadcasted_iota(jnp.int32, shape, dim)


def _tri_incl(n):
    return (_iota((n, n), 1) <= _iota((n, n), 0)).astype(F32)


def _sigmoid(x):
    return 1.0 / (1.0 + jnp.exp(-x))


def _silu(x):
    return x * _sigmoid(x)


def _softplus(x):
    return jnp.maximum(x, 0.0) + jnp.log1p(jnp.exp(-jnp.abs(x)))


def _rms(x, w):
    return x * lax.rsqrt(jnp.mean(x * x, axis=-1, keepdims=True) + EPS) * w


def _inv_unit_lower(n_mat, size):
    dim = n_mat.shape[0]
    eye = (_iota((dim, dim), 0) == _iota((dim, dim), 1)).astype(F32)
    t = eye + n_mat
    m = n_mat
    power = 2
    while power < size:
        m = _mm(m, m)
        t = t + _mm(t, m)
        power *= 2
    return t


def _row_tile(n, target):
    t = min(n, target)
    while n % t:
        t //= 2
    return t


def _resident(shape):
    nd = len(shape)
    return pl.BlockSpec(shape, lambda *_: (0,) * nd, pipeline_mode=pl.Buffered(1))


def _ffn_body(x_ref, nw_ref, wg_ref, wu_ref, wd_ref, fw_ref, o_ref, *, chunk, final):
    x = x_ref[...]
    h = _rms(x, nw_ref[...]).astype(BF16)
    d_ff = wg_ref.shape[1]
    acc = jnp.zeros(x.shape, F32)
    for c in range(d_ff // chunk):
        sl = slice(c * chunk, (c + 1) * chunk)
        g = jnp.dot(h, wg_ref[:, sl], preferred_element_type=F32)
        u = jnp.dot(h, wu_ref[:, sl], preferred_element_type=F32)
        a = (_silu(g) * u).astype(BF16)
        acc = acc + jnp.dot(a, wd_ref[sl, :], preferred_element_type=F32)
    y = x + 0.5 * acc
    if final:
        y = _rms(y, fw_ref[...])
    o_ref[...] = y


def _ffn(x, nw, wg, wu, wd, fw=None):
    n, d = x.shape
    d_ff = wg.shape[1]
    tm = _row_tile(n, 512)
    final = fw is not None
    if fw is None:
        fw = nw
    return pl.pallas_call(
        functools.partial(_ffn_body, chunk=d_ff // 2, final=final),
        out_shape=jax.ShapeDtypeStruct((n, d), F32),
        grid=(n // tm,),
        in_specs=[pl.BlockSpec((tm, d), lambda i: (i, 0)), _resident((1, d)), _resident((d, d_ff)),
                  _resident((d, d_ff)), _resident((d_ff, d)), _resident((1, d))],
        out_specs=pl.BlockSpec((tm, d), lambda i: (i, 0)),
        compiler_params=_cparams(("parallel",)),
        name="ffn",
    )(x, nw.reshape(1, d), wg, wu, wd, fw.reshape(1, d))


def _norm_proj_body(*refs, n_w):
    x_ref, nw_ref = refs[0], refs[1]
    w_refs = refs[2:2 + n_w]
    o_refs = refs[2 + n_w:]
    h = _rms(x_ref[...], nw_ref[...]).astype(BF16)
    for w_ref, o_ref in zip(w_refs, o_refs):
        o_ref[...] = jnp.dot(h, w_ref[...], preferred_element_type=F32)


def _norm_proj(x, nw, ws):
    n, d = x.shape
    tm = _row_tile(n, 512)
    return pl.pallas_call(
        functools.partial(_norm_proj_body, n_w=len(ws)),
        out_shape=[jax.ShapeDtypeStruct((n, w.shape[1]), F32) for w in ws],
        grid=(n // tm,),
        in_specs=[pl.BlockSpec((tm, d), lambda i: (i, 0)), _resident((1, d))] + [_resident(w.shape) for w in ws],
        out_specs=[pl.BlockSpec((tm, w.shape[1]), lambda i: (i, 0)) for w in ws],
        compiler_params=_cparams(("parallel",)),
        name="norm_proj",
    )(x, nw.reshape(1, d), *ws)


def _out_res_body(*refs, n_y):
    x_ref = refs[0]
    y_refs = refs[1:1 + n_y]
    w_refs = refs[1 + n_y:1 + 2 * n_y]
    o_ref = refs[1 + 2 * n_y]
    acc = x_ref[...]
    for y_ref, w_ref in zip(y_refs, w_refs):
        acc = acc + jnp.dot(y_ref[...], w_ref[...], preferred_element_type=F32)
    o_ref[...] = acc


def _out_res(x, ys, ws):
    n, d = x.shape
    tm = _row_tile(n, 1024)
    return pl.pallas_call(
        functools.partial(_out_res_body, n_y=len(ys)),
        out_shape=jax.ShapeDtypeStruct((n, d), F32),
        grid=(n // tm,),
        in_specs=[pl.BlockSpec((tm, d), lambda i: (i, 0))]
        + [pl.BlockSpec((tm, y.shape[1]), lambda i: (i, 0)) for y in ys] + [_resident(w.shape) for w in ws],
        out_specs=pl.BlockSpec((tm, d), lambda i: (i, 0)),
        compiler_params=_cparams(("parallel",)),
        name="out_res",
    )(x, *ys, *ws)


def _mem_attn_body(x_ref, nw_ref, wq_ref, k_ref, v_ref, wo_ref, o_ref, *, heads):
    x = x_ref[...]
    h = _rms(x, nw_ref[...]).astype(BF16)
    q = jnp.dot(h, wq_ref[...], preferred_element_type=F32)
    d = q.shape[1]
    hd = d // heads
    outs = []
    for i in range(heads):
        sl = slice(i * hd, (i + 1) * hd)
        s = _mm_nt(q[:, sl], k_ref[:, sl]) * hd ** -0.5
        e = jnp.exp(s - jnp.max(s, axis=-1, keepdims=True))
        pr = e / jnp.sum(e, axis=-1, keepdims=True)
        outs.append(_mm(pr, v_ref[:, sl]))
    o = jnp.concatenate(outs, axis=-1).astype(BF16)
    o_ref[...] = x + jnp.dot(o, wo_ref[...], preferred_element_type=F32)


def _mem_attn(x, nw, wq, mk, mv, wo, heads):
    b, t, d = x.shape
    m = mk.shape[1]
    tm = _row_tile(t, 512)
    row = pl.BlockSpec((None, tm, d), lambda i, j: (i, j, 0))
    mem = pl.BlockSpec((None, m, d), lambda i, j: (i, 0, 0))
    return pl.pallas_call(
        functools.partial(_mem_attn_body, heads=heads),
        out_shape=jax.ShapeDtypeStruct((b, t, d), F32),
        grid=(b, t // tm),
        in_specs=[row, _resident((1, d)), _resident((d, d)), mem, mem, _resident((d, d))],
        out_specs=row,
        compiler_params=_cparams(("parallel", "parallel")),
        name="mem_attn",
    )(x, nw.reshape(1, d), wq, mk, mv, wo)


def _seg_sum(x, bd):
    return _mm_exact_rhs01(x, bd)


def _rwkv_body(pa_ref, shift0_ref, s0_ref, mu_ref, w0_ref, w2_ref, a0_ref, a2_ref, g2_ref, kkw_ref, ka_ref,
               rk_ref, lnw_ref, lnb_ref, bd_ref, y_ref, shift_ref, s_ref, carry, y_scr, *, tb, L):
    t_idx = pl.program_id(1)
    a_dim = H_A * N_A
    n_pairs = a_dim // LANES

    @pl.when(t_idx == 0)
    def _():
        carry[...] = shift0_ref[...]
        s_ref[...] = s0_ref[...]

    pa = pa_ref[...]
    row = _iota(pa.shape, 0)
    prev = jnp.where(row == 0, carry[...], pltpu.roll(pa, 1, 0))
    carry[...] = pa[tb - 1:tb, :]
    shift_ref[...] = pa[tb - 1:tb, :]
    xa = pa + (prev - pa) * mu_ref[...]
    r = xa[:, 0:a_dim]
    k = xa[:, a_dim:2 * a_dim]
    v = xa[:, 2 * a_dim:3 * a_dim]
    xwa = xa[:, 3 * a_dim:3 * a_dim + LANES]
    xg = xa[:, 3 * a_dim + LANES:]
    wlog = -_softplus(-(w0_ref[...] + _mm(jnp.tanh(xwa), w2_ref[...]))) - 0.5
    logdec = -jnp.exp(wlog)
    a = _sigmoid(a0_ref[...] + _mm(xwa, a2_ref[...]))
    gate = _mm(_sigmoid(xg), g2_ref[...])
    bd = bd_ref[...]
    kq = k * kkw_ref[...]
    kk = kq * lax.rsqrt(_seg_sum(kq * kq, bd) + EPS)
    k2 = k * (1.0 + (a - 1.0) * ka_ref[...])
    av = -kk
    bv = kk * a

    lane = _iota((L, LANES), 1)
    m0 = lane < N_A
    r2 = _iota((2 * L, 2 * L), 0)
    c2 = _iota((2 * L, 2 * L), 1)
    bdm = jnp.logical_not(jnp.logical_xor(r2 >= L, c2 >= L))
    r_loc = jnp.where(r2 >= L, r2 - L, r2)
    c_loc = jnp.where(c2 >= L, c2 - L, c2)
    strict = jnp.logical_and(bdm, c_loc < r_loc)
    incl = jnp.logical_and(bdm, c_loc <= r_loc)
    bdl = jnp.logical_not(jnp.logical_xor(_iota((2 * L, LANES), 0) >= L, _iota((2 * L, LANES), 1) >= N_A))
    tri = _tri_incl(L)

    def stack2(x):
        return jnp.concatenate([jnp.where(m0, x, 0.0), jnp.where(m0, 0.0, x)], axis=0)

    def dup2(x):
        return jnp.concatenate([x, x], axis=0)

    for p in range(n_pairs):
        ls = slice(p * LANES, (p + 1) * LANES)
        s = s_ref[p]
        for c in range(tb // L):
            rs = slice(c * L, (c + 1) * L)
            rr, kc, vc, ac, bc, ld = r[rs, ls], k2[rs, ls], v[rs, ls], av[rs, ls], bv[rs, ls], logdec[rs, ls]
            cum = _mm_exact_lhs01(tri, ld)
            cl = cum[L - 1:L, :]
            e_neg = jnp.exp(-cum)
            e_out = jnp.exp(cl - cum)
            xa_s = stack2(ac * jnp.exp(cum - ld))
            xr_s = stack2(rr * jnp.exp(cum))
            yb = dup2(bc * e_neg)
            yk = dup2(kc * e_neg)
            n_st = jnp.where(strict, _mm_nt(xa_s, yb), 0.0)
            ak = jnp.where(strict, _mm_nt(xa_s, yk), 0.0)
            rb = jnp.where(incl, _mm_nt(xr_s, yb), 0.0)
            rkm = jnp.where(incl, _mm_nt(xr_s, yk), 0.0)
            v2 = jnp.where(bdl, dup2(vc), 0.0)
            t_inv = _inv_unit_lower(n_st, L)
            u_st = _mm(t_inv, _mm_nt(xa_s, s) + _mm(ak, v2))
            y_st = _mm_nt(xr_s, s) + _mm(rb, u_st) + _mm(rkm, v2)
            y_scr[rs, ls] = y_st[0:L] + y_st[L:2 * L]
            s = s * jnp.exp(cl) + _mm_tn(jnp.concatenate([u_st, v2], axis=0),
                                         jnp.concatenate([stack2(bc * e_out), stack2(kc * e_out)], axis=0))
        s_ref[p] = s

    y = y_scr[...]
    inv_n = 1.0 / N_A
    mu_y = _seg_sum(y, bd) * inv_n
    yc = y - mu_y
    var = _seg_sum(yc * yc, bd) * inv_n
    yn = yc * lax.rsqrt(var + RWKV_LN_EPS) * lnw_ref[...] + lnb_ref[...]
    bonus = _seg_sum(r * k2 * rk_ref[...], bd) * v
    y_ref[...] = ((yn + bonus) * gate).astype(y_ref.dtype)


def _pair_blockdiag(s):
    b, h, n, _ = s.shape
    s = s.reshape(b, h // 2, 2, n, n)
    z = jnp.zeros((b, h // 2, n, n), s.dtype)
    top = jnp.concatenate([s[:, :, 0], z], axis=-1)
    bot = jnp.concatenate([z, s[:, :, 1]], axis=-1)
    return jnp.concatenate([top, bot], axis=-2)


def _pair_unblock(s):
    b, p, n2, _ = s.shape
    n = n2 // 2
    return jnp.stack([s[:, :, :n, :n], s[:, :, n:, n:]], axis=2).reshape(b, 2 * p, n, n)


def _rwkv(pa, shift0, s0, wts, L):
    b, t, pw = pa.shape
    a_dim = H_A * N_A
    tb = min(t, 256)
    n_pairs = a_dim // LANES
    seg = _iota((a_dim, a_dim), 0) // N_A == _iota((a_dim, a_dim), 1) // N_A
    bd = seg.astype(BF16)
    z64 = jnp.zeros((N_A, a_dim), F32)
    w2p = jnp.concatenate([wts['w2'], z64], axis=0)
    a2p = jnp.concatenate([z64, wts['a2']], axis=0)
    vec = lambda x: x.reshape(1, -1).astype(F32)
    params = [vec(wts['mu']), vec(wts['w0']), w2p.astype(BF16), vec(wts['a0']), a2p.astype(BF16),
              wts['g2'].astype(BF16), vec(wts['kk']), vec(wts['ka']), vec(wts['rk']), vec(wts['ln_w']),
              vec(wts['ln_b']), bd]
    blk_t = pl.BlockSpec((None, tb, pw), lambda i, j: (i, j, 0))
    per_b = lambda shape: pl.BlockSpec((None,) + shape, lambda i, j: (i,) + (0,) * len(shape))
    y, shift, s_new = pl.pallas_call(
        functools.partial(_rwkv_body, tb=tb, L=L),
        out_shape=[jax.ShapeDtypeStruct((b, t, a_dim), BF16), jax.ShapeDtypeStruct((b, 1, pw), F32),
                   jax.ShapeDtypeStruct((b, n_pairs, LANES, LANES), F32)],
        grid=(b, t // tb),
        in_specs=[blk_t, per_b((1, pw)), per_b((n_pairs, LANES, LANES))] + [_resident(p.shape) for p in params],
        out_specs=[pl.BlockSpec((None, tb, a_dim), lambda i, j: (i, j, 0)), per_b((1, pw)),
                   per_b((n_pairs, LANES, LANES))],
        scratch_shapes=[pltpu.VMEM((1, pw), F32), pltpu.VMEM((tb, a_dim), F32)],
        compiler_params=_cparams(("parallel", "arbitrary")),
        name="rwkv7",
    )(pa, shift0, _pair_blockdiag(s0.astype(F32)), *params)
    return y, shift, _pair_unblock(s_new)


def _delta_body(qkv_ref, bab_ref, z_ref, conv0_ref, s0_ref, cw_ref, alog_ref, dtb_ref, nw_ref,
                y_ref, conv_ref, s_ref, *, tb, L):
    t_idx = pl.program_id(1)
    hd = DK_B
    inner = H_B * hd

    @pl.when(t_idx == 0)
    def _():
        conv_ref[...] = conv0_ref[...]
        s_ref[...] = s0_ref[...]

    x = qkv_ref[...]
    cat = jnp.concatenate([conv_ref[...], x], axis=0)
    conv_ref[...] = cat[tb:tb + CONV_CARRY, :]
    cw = cw_ref[...]
    conv = x * cw[CONV_W - 1:CONV_W, :]
    for j in range(1, CONV_W):
        conv = conv + pltpu.roll(cat, j, 0)[CONV_CARRY:, :] * cw[CONV_W - 1 - j:CONV_W - j, :]
    conv = _silu(conv)
    bab = bab_ref[...]
    g_all = -jnp.exp(alog_ref[...]) * _softplus(bab + dtb_ref[...])
    beta_all = _sigmoid(bab)

    tri = _tri_incl(L)
    rw = _iota((L, L), 0)
    cl_ = _iota((L, L), 1)
    strict = cl_ < rw
    incl = cl_ <= rw
    nw = nw_ref[...]
    z = z_ref[...]

    for h in range(H_B):
        qs = conv[:, h * hd:(h + 1) * hd]
        ks = conv[:, inner + h * hd:inner + (h + 1) * hd]
        vs = conv[:, 2 * inner + h * hd:2 * inner + (h + 1) * hd]
        qs = qs * lax.rsqrt(jnp.sum(qs * qs, axis=-1, keepdims=True) + EPS) * hd ** -0.5
        ks = ks * lax.rsqrt(jnp.sum(ks * ks, axis=-1, keepdims=True) + EPS)
        gb = jnp.broadcast_to(g_all[:, h:h + 1], (tb, LANES))
        bb = jnp.broadcast_to(beta_all[:, H_B + h:H_B + h + 1], (tb, LANES))
        s = s_ref[h]
        for c in range(tb // L):
            rs = slice(c * L, (c + 1) * L)
            q, k, v, g, beta = qs[rs], ks[rs], vs[rs], gb[rs], bb[rs]
            g_cum = _mm_exact_lhs01(tri, g)
            diff = _mm_exact_lhs01(tri, jnp.where(strict, g[:, :L], 0.0))
            e_diff = jnp.exp(jnp.where(incl, diff, 0.0))
            e_g = jnp.exp(g_cum)
            g_last = g_cum[L - 1:L, :]
            n_mat = -(beta[:, :L] * jnp.where(strict, _mm_nt(k, k) * e_diff, 0.0))
            t_inv = _inv_unit_lower(n_mat, L)
            sol = _mm(t_inv, jnp.concatenate([beta * v, beta * e_g * k], axis=1))
            vb, wk = sol[:, :hd], sol[:, hd:]
            qk = jnp.where(incl, _mm_nt(q, k) * e_diff, 0.0)
            u = vb - _mm(wk, s)
            o = _mm(q * e_g, s) + _mm(qk, u)
            s = jnp.exp(g_last) * s + _mm_tn(k * jnp.exp(g_last - g_cum), u)
            on = o * lax.rsqrt(jnp.mean(o * o, axis=-1, keepdims=True) + EPS) * nw
            y_ref[rs, h * hd:(h + 1) * hd] = (on * _silu(z[rs, h * hd:(h + 1) * hd])).astype(y_ref.dtype)
        s_ref[h] = s


def _delta(qkv, bab, z, conv0, s0, wts, L):
    b, t, cdim = qkv.shape
    inner = H_B * DK_B
    tb = min(t, 256)
    pad = lambda x: jnp.pad(x.reshape(1, -1).astype(F32), ((0, 0), (0, LANES - x.size)))
    conv0p = jnp.pad(conv0.astype(F32), ((0, 0), (CONV_CARRY - (CONV_W - 1), 0), (0, 0)))
    params = [wts['conv_w'].astype(F32), pad(wts['A_log']), pad(wts['dt_bias']), wts['norm_w'].reshape(1, -1).astype(F32)]
    blk = lambda w: pl.BlockSpec((None, tb, w), lambda i, j: (i, j, 0))
    per_b = lambda shape: pl.BlockSpec((None,) + shape, lambda i, j: (i,) + (0,) * len(shape))
    y, conv_new, s_new = pl.pallas_call(
        functools.partial(_delta_body, tb=tb, L=L),
        out_shape=[jax.ShapeDtypeStruct((b, t, inner), BF16), jax.ShapeDtypeStruct((b, CONV_CARRY, cdim), F32),
                   jax.ShapeDtypeStruct((b, H_B, DK_B, DK_B), F32)],
        grid=(b, t // tb),
        in_specs=[blk(cdim), blk(LANES), blk(inner), per_b((CONV_CARRY, cdim)), per_b((H_B, DK_B, DK_B))]
        + [_resident(p.shape) for p in params],
        out_specs=[blk(inner), per_b((CONV_CARRY, cdim)), per_b((H_B, DK_B, DK_B))],
        compiler_params=_cparams(("parallel", "arbitrary")),
        name="gated_delta",
    )(qkv, bab, z, conv0p, s0.astype(F32), *params)
    return y, conv_new[:, CONV_CARRY - (CONV_W - 1):], s_new


def _gla_body(qk_ref, v_ref, ad_ref, g_ref, s0_ref, a2_ref, ab_ref, nw_ref, y_ref, s_ref, *, tb, L):
    t_idx = pl.program_id(1)
    kdim = H_C * DK_C
    n_pairs = kdim // LANES

    @pl.when(t_idx == 0)
    def _():
        s_ref[...] = s0_ref[...]

    qk = qk_ref[...]
    log_a = -_softplus(-(_mm(ad_ref[...], a2_ref[...]) + ab_ref[...])) * (1.0 / GLA_NORMALIZER)
    vv = v_ref[...]
    gg = g_ref[...]
    nw = nw_ref[...]
    tri = _tri_incl(L)
    incl = _iota((L, L), 1) <= _iota((L, L), 0)
    m0 = _iota((L, LANES), 1) < DK_C

    for p in range(n_pairs):
        ls = slice(p * LANES, (p + 1) * LANES)
        s = s_ref[p]
        for c in range(tb // L):
            rs = slice(c * L, (c + 1) * L)
            q = qk[rs, ls] * DK_C ** -0.5
            k = qk[rs, kdim + p * LANES:kdim + (p + 1) * LANES]
            cum = _mm_exact_lhs01(tri, log_a[rs, ls])
            tot = cum[L - 1:L, :]
            q_in = q * jnp.exp(cum)
            k_in = k * jnp.exp(-cum)
            k_out = k * jnp.exp(tot - cum)
            upd = None
            for j in range(2):
                h = 2 * p + j
                mask = m0 if j == 0 else jnp.logical_not(m0)
                qm = jnp.where(mask, q_in, 0.0)
                vh = vv[rs, h * DV_C:(h + 1) * DV_C]
                scores = jnp.where(incl, _mm_nt(qm, k_in), 0.0)
                o = _mm_nt(qm, s) + _mm(scores, vh)
                on = o * lax.rsqrt(jnp.mean(o * o, axis=-1, keepdims=True) + EPS) * nw
                y_ref[rs, h * DV_C:(h + 1) * DV_C] = (on * _silu(gg[rs, h * DV_C:(h + 1) * DV_C])).astype(y_ref.dtype)
                d = _mm_tn(vh, jnp.where(mask, k_out, 0.0))
                upd = d if upd is None else upd + d
            s = s * jnp.exp(tot) + upd
        s_ref[p] = s


def _gla(cqk, cv, cad, cg, s0, wts, L):
    b, t, _ = cqk.shape
    kdim = H_C * DK_C
    vdim = H_C * DV_C
    n_pairs = kdim // LANES
    tb = min(t, 256)
    rank = wts['a2'].shape[0]
    a2p = jnp.pad(wts['a2'].astype(F32), ((0, LANES - rank), (0, 0))).astype(BF16)
    params = [a2p, wts['a_bias'].reshape(1, -1).astype(F32), wts['norm_w'].reshape(1, -1).astype(F32)]
    st0 = jnp.swapaxes(s0.astype(F32), -1, -2).reshape(b, n_pairs, 2, DV_C, DK_C)
    st0 = jnp.concatenate([st0[:, :, 0], st0[:, :, 1]], axis=-1)
    blk = lambda w: pl.BlockSpec((None, tb, w), lambda i, j: (i, j, 0))
    per_b = lambda shape: pl.BlockSpec((None,) + shape, lambda i, j: (i,) + (0,) * len(shape))
    y, st = pl.pallas_call(
        functools.partial(_gla_body, tb=tb, L=L),
        out_shape=[jax.ShapeDtypeStruct((b, t, vdim), BF16), jax.ShapeDtypeStruct((b, n_pairs, DV_C, LANES), F32)],
        grid=(b, t // tb),
        in_specs=[blk(2 * kdim), blk(vdim), blk(LANES), blk(vdim), per_b((n_pairs, DV_C, LANES))]
        + [_resident(p.shape) for p in params],
        out_specs=[blk(vdim), per_b((n_pairs, DV_C, LANES))],
        compiler_params=_cparams(("parallel", "arbitrary")),
        name="gla",
    )(cqk, cv, cad, cg, st0, *params)
    st = jnp.stack([st[..., :DK_C], st[..., DK_C:]], axis=2).reshape(b, H_C, DV_C, DK_C)
    return y, jnp.swapaxes(st, -1, -2)


def _ret_body(q_ref, k_ref, v_ref, g_ref, cos_ref, sin_ref, s0_ref, y_ref, s_ref, *, tb, L):
    t_idx = pl.program_id(1)
    hd = DK_D

    @pl.when(t_idx == 0)
    def _():
        s_ref[...] = s0_ref[...]

    cos = cos_ref[...]
    sin = sin_ref[...]
    qq, kk, vv, gg = q_ref[...], k_ref[...], v_ref[...], g_ref[...]
    rw = _iota((L, L), 0)
    cl_ = _iota((L, L), 1)
    incl = cl_ <= rw
    dist = jnp.where(incl, rw - cl_, 0).astype(F32)
    pos = _iota((L, LANES), 0).astype(F32)

    for h in range(H_D):
        lg = math.log(1.0 - 2.0 ** (-5.0 - h))
        hs = slice(h * hd, (h + 1) * hd)
        qh = qq[:, hs] * cos + pltpu.roll(qq[:, hs], hd // 2, 1) * sin
        kh = (kk[:, hs] * cos + pltpu.roll(kk[:, hs], hd // 2, 1) * sin) * hd ** -0.5
        dmat = jnp.where(incl, jnp.exp(dist * lg), 0.0)
        e_in = jnp.exp((pos + 1.0) * lg)
        e_out = jnp.exp((L - 1.0 - pos) * lg)
        s = s_ref[h]
        for c in range(tb // L):
            rs = slice(c * L, (c + 1) * L)
            q, k, v = qh[rs], kh[rs], vv[rs, hs]
            o = _mm(q * e_in, s) + _mm(_mm_nt(q, k) * dmat, v)
            s = s * math.exp(lg * L) + _mm_tn(k * e_out, v)
            mu = jnp.mean(o, axis=-1, keepdims=True)
            oc = o - mu
            on = oc * lax.rsqrt(jnp.mean(oc * oc, axis=-1, keepdims=True) + EPS)
            y_ref[rs, hs] = (on * _silu(gg[rs, hs])).astype(y_ref.dtype)
        s_ref[h] = s


def _retention(dq, dk, dv, dg, s0, pos0, L):
    b, t, dim = dq.shape
    tb = min(t, 256)
    half = DK_D // 2
    inv = ROPE_BASE ** (-jnp.arange(0, DK_D, 2, dtype=F32) / DK_D)
    ang = (jnp.arange(t) + pos0).astype(F32)[:, None] * inv[None, :]
    cos = jnp.concatenate([jnp.cos(ang), jnp.cos(ang)], axis=-1)
    sin = jnp.concatenate([-jnp.sin(ang), jnp.sin(ang)], axis=-1)
    del half
    blk = pl.BlockSpec((None, tb, dim), lambda i, j: (i, j, 0))
    tab = pl.BlockSpec((tb, DK_D), lambda i, j: (j, 0))
    st = pl.BlockSpec((None, H_D, DK_D, DK_D), lambda i, j: (i, 0, 0, 0))
    return pl.pallas_call(
        functools.partial(_ret_body, tb=tb, L=L),
        out_shape=[jax.ShapeDtypeStruct((b, t, dim), BF16), jax.ShapeDtypeStruct((b, H_D, DK_D, DK_D), F32)],
        grid=(b, t // tb),
        in_specs=[blk, blk, blk, blk, tab, tab, st],
        out_specs=[blk, st],
        compiler_params=_cparams(("parallel", "arbitrary")),
        name="retention",
    )(dq, dk, dv, dg, cos, sin, s0.astype(F32))


def _prep_weights(W):
    bf = lambda x: x.astype(BF16)
    a_proj = 3 * H_A * N_A + 64 + 64 + 128
    b_conv = 3 * H_B * DK_B
    P = {}
    for name in ('ffn1_wg', 'ffn1_wu', 'ffn1_wd', 'ffn2_wg', 'ffn2_wu', 'ffn2_wd', 'mem_wq', 'mem_wo', 'mem_wk',
                 'mem_wv', 'even_w_out', 'odd_w_out'):
        P[name] = bf(W[name])
    ew = W['even_w_in']
    o = a_proj
    bab = ew[:, :, o + b_conv:o + b_conv + 2 * H_B]
    P['even_in'] = [bf(ew[:, :, :a_proj]), bf(ew[:, :, o:o + b_conv]),
                    bf(jnp.pad(bab, ((0, 0), (0, 0), (0, LANES - 2 * H_B)))), bf(ew[:, :, o + b_conv + 2 * H_B:])]
    ow = W['odd_w_in']
    kd, vd, rank = H_C * DK_C, H_C * DV_C, W['gla_a2'].shape[1]
    c_proj = 2 * kd + 2 * vd + rank
    dd = H_D * DK_D
    cad = ow[:, :, 2 * kd + vd:2 * kd + vd + rank]
    P['odd_in'] = [bf(ow[:, :, :2 * kd]), bf(ow[:, :, 2 * kd:2 * kd + vd]),
                   bf(jnp.pad(cad, ((0, 0), (0, 0), (0, LANES - rank)))), bf(ow[:, :, 2 * kd + vd + rank:c_proj])] \
        + [bf(ow[:, :, c_proj + j * dd:c_proj + (j + 1) * dd]) for j in range(4)]
    return P


def _trunk(x, pos0, mem_k, mem_v, shift, rwkv, conv, delta, gla, ret, W, P):
    b, t, d = x.shape
    n = b * t
    depth = W['norm_ffn1'].shape[0]
    L_delta = min(64, t)
    L_lin = min(64, t)
    flat = lambda z: z.reshape(n, z.shape[-1])
    unflat = lambda z: z.reshape(b, t, z.shape[-1])
    new = {k: [] for k in ('shift', 'rwkv', 'conv', 'delta', 'gla', 'ret')}
    x = flat(x)
    for l in range(depth):
        i = l // 2
        x = _ffn(x, W['norm_ffn1'][l], P['ffn1_wg'][l], P['ffn1_wu'][l], P['ffn1_wd'][l])
        if l % 2 == 0:
            pa, qkv, bab, z = _norm_proj(x, W['norm_mix'][l], [w[i] for w in P['even_in']])
            wa = dict(mu=W['rwkv_mu'][i], w0=W['rwkv_w0'][i], w2=W['rwkv_w2'][i], a0=W['rwkv_a0'][i],
                      a2=W['rwkv_a2'][i], g2=W['rwkv_g2'][i], kk=W['rwkv_kk'][i], ka=W['rwkv_ka'][i],
                      rk=W['rwkv_rk'][i], ln_w=W['rwkv_ln_w'][i], ln_b=W['rwkv_ln_b'][i])
            y_a, s1, s2 = _rwkv(unflat(pa), shift[i], rwkv[i], wa, L_delta)
            wb = dict(conv_w=W['delta_conv_w'][i], A_log=W['delta_A_log'][i], dt_bias=W['delta_dt_bias'][i],
                      norm_w=W['delta_norm_w'][i])
            y_b, s3, s4 = _delta(unflat(qkv), unflat(bab), unflat(z), conv[i], delta[i], wb, L_delta)
            new['shift'].append(s1)
            new['rwkv'].append(s2)
            new['conv'].append(s3)
            new['delta'].append(s4)
            a_dim = H_A * N_A
            x = _out_res(x, [flat(y_a), flat(y_b)], [P['even_w_out'][i][:a_dim], P['even_w_out'][i][a_dim:]])
        else:
            cqk, cv, cad, cg, dq, dk, dv, dg = _norm_proj(x, W['norm_mix'][l], [w[i] for w in P['odd_in']])
            wc = dict(a2=W['gla_a2'][i], a_bias=W['gla_a_bias'][i], norm_w=W['gla_norm_w'][i])
            y_c, s5 = _gla(unflat(cqk), unflat(cv), unflat(cad), unflat(cg), gla[i], wc, L_lin)
            y_d, s6 = _retention(unflat(dq), unflat(dk), unflat(dv), unflat(dg), ret[i], pos0, L_lin)
            new['gla'].append(s5)
            new['ret'].append(s6)
            c_dim = H_C * DV_C
            x = _out_res(x, [flat(y_c), flat(y_d)], [P['odd_w_out'][i][:c_dim], P['odd_w_out'][i][c_dim:]])
        mk = mem_k[l].reshape(b, -1, d).astype(BF16)
        mv = mem_v[l].reshape(b, -1, d).astype(BF16)
        heads = mem_k[l].shape[2]
        x = flat(_mem_attn(unflat(x), W['norm_mem'][l], P['mem_wq'][l], mk, mv, P['mem_wo'][l], heads))
        fw = W['final_norm'] if l == depth - 1 else None
        x = _ffn(x, W['norm_ffn2'][l], P['ffn2_wg'][l], P['ffn2_wu'][l], P['ffn2_wd'][l], fw)
    return unflat(x), new


def kernel(x_prompt, x_sample, mem_prompt, state_rwkv_shift, state_rwkv, state_delta_conv, state_delta, state_gla, state_ret, cache_mem_k, cache_mem_v, norm_ffn1, ffn1_wg, ffn1_wu, ffn1_wd, norm_mix, even_w_in, even_w_out, rwkv_mu, rwkv_w0, rwkv_w2, rwkv_a0, rwkv_a2, rwkv_g2, rwkv_kk, rwkv_ka, rwkv_rk, rwkv_ln_w, rwkv_ln_b, delta_conv_w, delta_A_log, delta_dt_bias, delta_norm_w, odd_w_in, odd_w_out, gla_a2, gla_a_bias, gla_norm_w, norm_mem, mem_norm_kv, mem_wq, mem_wk, mem_wv, mem_wo, norm_ffn2, ffn2_wg, ffn2_wu, ffn2_wd, final_norm):
    W = dict(norm_ffn1=norm_ffn1, ffn1_wg=ffn1_wg, ffn1_wu=ffn1_wu, ffn1_wd=ffn1_wd, norm_mix=norm_mix,
             even_w_in=even_w_in, even_w_out=even_w_out, rwkv_mu=rwkv_mu, rwkv_w0=rwkv_w0, rwkv_w2=rwkv_w2,
             rwkv_a0=rwkv_a0, rwkv_a2=rwkv_a2, rwkv_g2=rwkv_g2, rwkv_kk=rwkv_kk, rwkv_ka=rwkv_ka,
             rwkv_rk=rwkv_rk, rwkv_ln_w=rwkv_ln_w, rwkv_ln_b=rwkv_ln_b, delta_conv_w=delta_conv_w,
             delta_A_log=delta_A_log, delta_dt_bias=delta_dt_bias, delta_norm_w=delta_norm_w,
             odd_w_in=odd_w_in, odd_w_out=odd_w_out, gla_a2=gla_a2, gla_a_bias=gla_a_bias,
             gla_norm_w=gla_norm_w, norm_mem=norm_mem, mem_wq=mem_wq, mem_wk=mem_wk, mem_wv=mem_wv,
             mem_wo=mem_wo, norm_ffn2=norm_ffn2, ffn2_wg=ffn2_wg, ffn2_wu=ffn2_wu, ffn2_wd=ffn2_wd,
             final_norm=final_norm)
    P = _prep_weights(W)
    dt = x_prompt.dtype
    bp, _, d = x_prompt.shape
    depth = norm_ffn1.shape[0]
    n_even, n_odd = (depth + 1) // 2, depth // 2
    heads, hd = cache_mem_k.shape[3], cache_mem_k.shape[4]
    n_mem = mem_prompt.shape[1]
    pk, pv = [], []
    mem_flat = mem_prompt.reshape(bp * n_mem, d)
    for l in range(depth):
        mk, mv = _norm_proj(mem_flat, mem_norm_kv[l], [P['mem_wk'][l], P['mem_wv'][l]])
        pk.append(mk.reshape(bp, n_mem, heads, hd))
        pv.append(mv.reshape(bp, n_mem, heads, hd))
    zeros = lambda ref, cnt: [jnp.zeros((bp,) + ref.shape[2:], F32)] * cnt
    y_prompt, ps = _trunk(x_prompt, 0, pk, pv, zeros(state_rwkv_shift, n_even), zeros(state_rwkv, n_even),
                          zeros(state_delta_conv, n_even), zeros(state_delta, n_even), zeros(state_gla, n_odd),
                          zeros(state_ret, n_odd), W, P)
    y_sample, ss = _trunk(x_sample, PAST_LEN, [cache_mem_k[l] for l in range(depth)],
                          [cache_mem_v[l] for l in range(depth)],
                          [state_rwkv_shift[i] for i in range(n_even)], [state_rwkv[i] for i in range(n_even)],
                          [state_delta_conv[i] for i in range(n_even)], [state_delta[i] for i in range(n_even)],
                          [state_gla[i] for i in range(n_odd)], [state_ret[i] for i in range(n_odd)], W, P)
    st = lambda xs: jnp.stack(xs).astype(dt)
    order = ('shift', 'rwkv', 'conv', 'delta', 'gla', 'ret')
    return ((y_prompt, y_sample) + tuple(st(ps[k]) for k in order) + (st(pk), st(pv))
            + tuple(st(ss[k]) for k in order))
```

```python
import functools
import math

import jax
import jax.numpy as jnp
from jax import lax
from jax.experimental import pallas as pl
from jax.experimental.pallas import tpu as pltpu

F32 = jnp.float32
BF16 = jnp.bfloat16
EPS = 1e-6
RWKV_LN_EPS = 64e-5
GLA_NORMALIZER = 16.0
ROPE_BASE = 10000.0
PAST_LEN = 1024

LANES = 128
VMEM_LIMIT = 56 * 1024 * 1024

H_A, N_A = 8, 64
H_B, DK_B = 4, 128
H_C, DK_C, DV_C = 4, 64, 128
H_D, DK_D = 4, 128
CONV_W = 4
CONV_CARRY = 8


def _cparams(sem):
    return pltpu.CompilerParams(dimension_semantics=sem, vmem_limit_bytes=VMEM_LIMIT)


def _mm(a, b):
    return jnp.dot(a.astype(BF16), b.astype(BF16), preferred_element_type=F32)


def _mm_nt(a, b):
    return lax.dot_general(a.astype(BF16), b.astype(BF16), (((1,), (1,)), ((), ())),
                           preferred_element_type=F32)


def _mm_tn(a, b):
    return lax.dot_general(a.astype(BF16), b.astype(BF16), (((0,), (0,)), ((), ())),
                           preferred_element_type=F32)


def _split3(x):
    hi = x.astype(BF16)
    r = x - hi.astype(F32)
    mid = r.astype(BF16)
    lo = (r - mid.astype(F32)).astype(BF16)
    return hi, mid, lo


def _mm_exact_lhs01(a01, x):
    hi, mid, lo = _split3(x)
    a = a01.astype(BF16)
    return (jnp.dot(a, hi, preferred_element_type=F32) + jnp.dot(a, mid, preferred_element_type=F32)
            + jnp.dot(a, lo, preferred_element_type=F32))


def _mm_exact_rhs01(x, b01):
    hi, mid, lo = _split3(x)
    b = b01.astype(BF16)
    return (jnp.dot(hi, b, preferred_element_type=F32) + jnp.dot(mid, b, preferred_element_type=F32)
            + jnp.dot(lo, b, preferred_element_type=F32))


def _iota(shape, dim):
    return lax.broadcasted_iota(jnp.int32, shape, dim)


def _tri_incl(n):
    return (_iota((n, n), 1) <= _iota((n, n), 0)).astype(F32)


def _chunk_masks(tb, L):
    shift = L.bit_length() - 1
    assert 1 << shift == L
    rt, ct = _iota((tb, tb), 0), _iota((tb, tb), 1)
    same = jnp.right_shift(rt, shift) == jnp.right_shift(ct, shift)
    return jnp.logical_and(same, ct <= rt).astype(F32), same.astype(F32)


def _sigmoid(x):
    return 1.0 / (1.0 + jnp.exp(-x))


def _silu(x):
    return x * _sigmoid(x)


def _softplus(x):
    return jnp.maximum(x, 0.0) + jnp.log1p(jnp.exp(-jnp.abs(x)))


def _rms(x, w):
    return x * lax.rsqrt(jnp.mean(x * x, axis=-1, keepdims=True) + EPS) * w


def _inv_unit_lower(n_mat, size):
    dim = n_mat.shape[0]
    eye = (_iota((dim, dim), 0) == _iota((dim, dim), 1)).astype(F32)
    t = eye + n_mat
    m = n_mat
    power = 2
    while power < size:
        m = _mm(m, m)
        t = t + _mm(t, m)
        power *= 2
    return t


def _row_tile(n, target):
    t = min(n, target)
    while n % t:
        t //= 2
    return t


def _resident(shape):
    nd = len(shape)
    return pl.BlockSpec(shape, lambda *_: (0,) * nd, pipeline_mode=pl.Buffered(1))


def _ffn_body(x_ref, nw_ref, wg_ref, wu_ref, wd_ref, fw_ref, o_ref, *, chunk, final):
    x = x_ref[...]
    h = _rms(x, nw_ref[...]).astype(BF16)
    d_ff = wg_ref.shape[1]
    acc = jnp.zeros(x.shape, F32)
    for c in range(d_ff // chunk):
        sl = slice(c * chunk, (c + 1) * chunk)
        g = jnp.dot(h, wg_ref[:, sl], preferred_element_type=F32)
        u = jnp.dot(h, wu_ref[:, sl], preferred_element_type=F32)
        a = (_silu(g) * u).astype(BF16)
        acc = acc + jnp.dot(a, wd_ref[sl, :], preferred_element_type=F32)
    y = x + 0.5 * acc
    if final:
        y = _rms(y, fw_ref[...])
    o_ref[...] = y


def _ffn(x, nw, wg, wu, wd, fw=None):
    n, d = x.shape
    d_ff = wg.shape[1]
    tm = _row_tile(n, 512)
    final = fw is not None
    if fw is None:
        fw = nw
    return pl.pallas_call(
        functools.partial(_ffn_body, chunk=d_ff // 2, final=final),
        out_shape=jax.ShapeDtypeStruct((n, d), F32),
        grid=(n // tm,),
        in_specs=[pl.BlockSpec((tm, d), lambda i: (i, 0)), _resident((1, d)), _resident((d, d_ff)),
                  _resident((d, d_ff)), _resident((d_ff, d)), _resident((1, d))],
        out_specs=pl.BlockSpec((tm, d), lambda i: (i, 0)),
        compiler_params=_cparams(("parallel",)),
        name="ffn",
    )(x, nw.reshape(1, d), wg, wu, wd, fw.reshape(1, d))


def _norm_proj_body(*refs, n_w):
    x_ref, nw_ref = refs[0], refs[1]
    w_refs = refs[2:2 + n_w]
    o_refs = refs[2 + n_w:]
    h = _rms(x_ref[...], nw_ref[...]).astype(BF16)
    for w_ref, o_ref in zip(w_refs, o_refs):
        o_ref[...] = jnp.dot(h, w_ref[...], preferred_element_type=F32)


def _norm_proj(x, nw, ws):
    n, d = x.shape
    tm = _row_tile(n, 512)
    return pl.pallas_call(
        functools.partial(_norm_proj_body, n_w=len(ws)),
        out_shape=[jax.ShapeDtypeStruct((n, w.shape[1]), F32) for w in ws],
        grid=(n // tm,),
        in_specs=[pl.BlockSpec((tm, d), lambda i: (i, 0)), _resident((1, d))] + [_resident(w.shape) for w in ws],
        out_specs=[pl.BlockSpec((tm, w.shape[1]), lambda i: (i, 0)) for w in ws],
        compiler_params=_cparams(("parallel",)),
        name="norm_proj",
    )(x, nw.reshape(1, d), *ws)


def _out_res_body(*refs, n_y):
    x_ref = refs[0]
    y_refs = refs[1:1 + n_y]
    w_refs = refs[1 + n_y:1 + 2 * n_y]
    o_ref = refs[1 + 2 * n_y]
    acc = x_ref[...]
    for y_ref, w_ref in zip(y_refs, w_refs):
        acc = acc + jnp.dot(y_ref[...], w_ref[...], preferred_element_type=F32)
    o_ref[...] = acc


def _out_res(x, ys, ws):
    n, d = x.shape
    tm = _row_tile(n, 1024)
    return pl.pallas_call(
        functools.partial(_out_res_body, n_y=len(ys)),
        out_shape=jax.ShapeDtypeStruct((n, d), F32),
        grid=(n // tm,),
        in_specs=[pl.BlockSpec((tm, d), lambda i: (i, 0))]
        + [pl.BlockSpec((tm, y.shape[1]), lambda i: (i, 0)) for y in ys] + [_resident(w.shape) for w in ws],
        out_specs=pl.BlockSpec((tm, d), lambda i: (i, 0)),
        compiler_params=_cparams(("parallel",)),
        name="out_res",
    )(x, *ys, *ws)


def _mem_attn_body(x_ref, nw_ref, wq_ref, k_ref, v_ref, wo_ref, o_ref, *, heads):
    x = x_ref[...]
    h = _rms(x, nw_ref[...]).astype(BF16)
    q = jnp.dot(h, wq_ref[...], preferred_element_type=F32)
    d = q.shape[1]
    hd = d // heads
    outs = []
    for i in range(heads):
        sl = slice(i * hd, (i + 1) * hd)
        s = _mm_nt(q[:, sl], k_ref[:, sl]) * hd ** -0.5
        e = jnp.exp(s - jnp.max(s, axis=-1, keepdims=True))
        pr = e / jnp.sum(e, axis=-1, keepdims=True)
        outs.append(_mm(pr, v_ref[:, sl]))
    o = jnp.concatenate(outs, axis=-1).astype(BF16)
    o_ref[...] = x + jnp.dot(o, wo_ref[...], preferred_element_type=F32)


def _mem_attn(x, nw, wq, mk, mv, wo, heads):
    b, t, d = x.shape
    m = mk.shape[1]
    tm = _row_tile(t, 512)
    row = pl.BlockSpec((None, tm, d), lambda i, j: (i, j, 0))
    mem = pl.BlockSpec((None, m, d), lambda i, j: (i, 0, 0))
    return pl.pallas_call(
        functools.partial(_mem_attn_body, heads=heads),
        out_shape=jax.ShapeDtypeStruct((b, t, d), F32),
        grid=(b, t // tm),
        in_specs=[row, _resident((1, d)), _resident((d, d)), mem, mem, _resident((d, d))],
        out_specs=row,
        compiler_params=_cparams(("parallel", "parallel")),
        name="mem_attn",
    )(x, nw.reshape(1, d), wq, mk, mv, wo)


def _seg_sum(x, bd):
    return _mm_exact_rhs01(x, bd)


def _rwkv_body(pa_ref, shift0_ref, s0_ref, mu_ref, w0_ref, w2_ref, a0_ref, a2_ref, g2_ref, kkw_ref, ka_ref,
               rk_ref, lnw_ref, lnb_ref, bd_ref, y_ref, shift_ref, s_ref, carry, y_scr, *, tb, L):
    t_idx = pl.program_id(1)
    a_dim = H_A * N_A
    n_pairs = a_dim // LANES

    @pl.when(t_idx == 0)
    def _():
        carry[...] = shift0_ref[...]
        s_ref[...] = s0_ref[...]

    pa = pa_ref[...]
    row = _iota(pa.shape, 0)
    prev = jnp.where(row == 0, carry[...], pltpu.roll(pa, 1, 0))
    carry[...] = pa[tb - 1:tb, :]
    shift_ref[...] = pa[tb - 1:tb, :]
    xa = pa + (prev - pa) * mu_ref[...]
    r = xa[:, 0:a_dim]
    k = xa[:, a_dim:2 * a_dim]
    v = xa[:, 2 * a_dim:3 * a_dim]
    xwa = xa[:, 3 * a_dim:3 * a_dim + LANES]
    xg = xa[:, 3 * a_dim + LANES:]
    wlog = -_softplus(-(w0_ref[...] + _mm(jnp.tanh(xwa), w2_ref[...]))) - 0.5
    logdec = -jnp.exp(wlog)
    a = _sigmoid(a0_ref[...] + _mm(xwa, a2_ref[...]))
    gate = _mm(_sigmoid(xg), g2_ref[...])
    bd = bd_ref[...]
    kq = k * kkw_ref[...]
    kk = kq * lax.rsqrt(_seg_sum(kq * kq, bd) + EPS)
    k2 = k * (1.0 + (a - 1.0) * ka_ref[...])
    av = -kk
    bv = kk * a

    lane = _iota((L, LANES), 1)
    m0 = lane < N_A
    r2 = _iota((2 * L, 2 * L), 0)
    c2 = _iota((2 * L, 2 * L), 1)
    bdm = jnp.logical_not(jnp.logical_xor(r2 >= L, c2 >= L))
    r_loc = jnp.where(r2 >= L, r2 - L, r2)
    c_loc = jnp.where(c2 >= L, c2 - L, c2)
    strict = jnp.logical_and(bdm, c_loc < r_loc)
    incl = jnp.logical_and(bdm, c_loc <= r_loc)
    bdl = jnp.logical_not(jnp.logical_xor(_iota((2 * L, LANES), 0) >= L, _iota((2 * L, LANES), 1) >= N_A))
    eye2 = (r2 == c2).astype(F32)

    def stack2(x):
        return jnp.concatenate([jnp.where(m0, x, 0.0), jnp.where(m0, 0.0, x)], axis=0).astype(BF16)

    def dup2(x):
        return jnp.concatenate([x, x], axis=0).astype(BF16)

    tri_bd, ones_bd = _chunk_masks(tb, L)
    cum = _mm_exact_lhs01(tri_bd, logdec)
    tot = _mm_exact_lhs01(ones_bd, logdec)
    e_neg = jnp.exp(-cum)
    e_out = jnp.exp(tot - cum)
    a_t = av * jnp.exp(cum - logdec)
    r_t = r * jnp.exp(cum)
    b_t = bv * e_neg
    k_t = k2 * e_neg
    b_o = bv * e_out
    k_o = k2 * e_out
    g_l = jnp.exp(tot)

    items = [(c, p) for c in range(tb // L) for p in range(n_pairs)]
    sl = {(c, p): (slice(c * L, (c + 1) * L), slice(p * LANES, (p + 1) * LANES)) for c, p in items}
    xa = {it: stack2(a_t[sl[it]]) for it in items}
    xr = {it: stack2(r_t[sl[it]]) for it in items}
    yb = {it: dup2(b_t[sl[it]]) for it in items}
    yk = {it: dup2(k_t[sl[it]]) for it in items}
    v2 = {it: jnp.where(bdl, jnp.concatenate([v[sl[it]], v[sl[it]]], axis=0), 0.0).astype(BF16) for it in items}
    wst = {it: jnp.concatenate([stack2(b_o[sl[it]]), stack2(k_o[sl[it]])], axis=0) for it in items}
    n_st = {it: jnp.where(strict, _mm_nt(xa[it], yb[it]), 0.0) for it in items}
    ak = {it: jnp.where(strict, _mm_nt(xa[it], yk[it]), 0.0).astype(BF16) for it in items}
    rb = {it: jnp.where(incl, _mm_nt(xr[it], yb[it]), 0.0).astype(BF16) for it in items}
    rkm = {it: jnp.where(incl, _mm_nt(xr[it], yk[it]), 0.0).astype(BF16) for it in items}
    t_inv = {it: eye2 + n_st[it] for it in items}
    m_pow = n_st
    power = 2
    while power < L:
        m_pow = {it: _mm(m_pow[it], m_pow[it]) for it in items}
        t_inv = {it: t_inv[it] + _mm(t_inv[it], m_pow[it]) for it in items}
        power *= 2
    omega = {it: _mm(t_inv[it], xa[it]).astype(BF16) for it in items}
    akv = {it: _mm(ak[it], v2[it]) for it in items}
    ups = {it: _mm(t_inv[it], akv[it]) for it in items}
    state = [s_ref[p] for p in range(n_pairs)]
    s_prev, u_st = {}, {}
    for c in range(tb // L):
        for p in range(n_pairs):
            s_prev[(c, p)] = state[p].astype(BF16)
            u_st[(c, p)] = (_mm_nt(omega[(c, p)], s_prev[(c, p)]) + ups[(c, p)]).astype(BF16)
        for p in range(n_pairs):
            it = (c, p)
            state[p] = state[p] * g_l[c * L:c * L + 1, sl[it][1]] + _mm_tn(
                jnp.concatenate([u_st[it], v2[it]], axis=0), wst[it])
    for p in range(n_pairs):
        s_ref[p] = state[p]
    for it in items:
        y_st = _mm_nt(xr[it], s_prev[it]) + _mm(rb[it], u_st[it]) + _mm(rkm[it], v2[it])
        y_scr[sl[it]] = y_st[0:L] + y_st[L:2 * L]

    y = y_scr[...]
    inv_n = 1.0 / N_A
    mu_y = _seg_sum(y, bd) * inv_n
    yc = y - mu_y
    var = _seg_sum(yc * yc, bd) * inv_n
    yn = yc * lax.rsqrt(var + RWKV_LN_EPS) * lnw_ref[...] + lnb_ref[...]
    bonus = _seg_sum(r * k2 * rk_ref[...], bd) * v
    y_ref[...] = ((yn + bonus) * gate).astype(y_ref.dtype)


def _pair_blockdiag(s):
    b, h, n, _ = s.shape
    s = s.reshape(b, h // 2, 2, n, n)
    z = jnp.zeros((b, h // 2, n, n), s.dtype)
    top = jnp.concatenate([s[:, :, 0], z], axis=-1)
    bot = jnp.concatenate([z, s[:, :, 1]], axis=-1)
    return jnp.concatenate([top, bot], axis=-2)


def _pair_unblock(s):
    b, p, n2, _ = s.shape
    n = n2 // 2
    return jnp.stack([s[:, :, :n, :n], s[:, :, n:, n:]], axis=2).reshape(b, 2 * p, n, n)


def _rwkv(pa, shift0, s0, wts, L):
    b, t, pw = pa.shape
    a_dim = H_A * N_A
    tb = min(t, 256)
    n_pairs = a_dim // LANES
    seg = _iota((a_dim, a_dim), 0) // N_A == _iota((a_dim, a_dim), 1) // N_A
    bd = seg.astype(BF16)
    z64 = jnp.zeros((N_A, a_dim), F32)
    w2p = jnp.concatenate([wts['w2'], z64], axis=0)
    a2p = jnp.concatenate([z64, wts['a2']], axis=0)
    vec = lambda x: x.reshape(1, -1).astype(F32)
    params = [vec(wts['mu']), vec(wts['w0']), w2p.astype(BF16), vec(wts['a0']), a2p.astype(BF16),
              wts['g2'].astype(BF16), vec(wts['kk']), vec(wts['ka']), vec(wts['rk']), vec(wts['ln_w']),
              vec(wts['ln_b']), bd]
    blk_t = pl.BlockSpec((None, tb, pw), lambda i, j: (i, j, 0))
    per_b = lambda shape: pl.BlockSpec((None,) + shape, lambda i, j: (i,) + (0,) * len(shape))
    y, shift, s_new = pl.pallas_call(
        functools.partial(_rwkv_body, tb=tb, L=L),
        out_shape=[jax.ShapeDtypeStruct((b, t, a_dim), BF16), jax.ShapeDtypeStruct((b, 1, pw), F32),
                   jax.ShapeDtypeStruct((b, n_pairs, LANES, LANES), F32)],
        grid=(b, t // tb),
        in_specs=[blk_t, per_b((1, pw)), per_b((n_pairs, LANES, LANES))] + [_resident(p.shape) for p in params],
        out_specs=[pl.BlockSpec((None, tb, a_dim), lambda i, j: (i, j, 0)), per_b((1, pw)),
                   per_b((n_pairs, LANES, LANES))],
        scratch_shapes=[pltpu.VMEM((1, pw), F32), pltpu.VMEM((tb, a_dim), F32)],
        compiler_params=_cparams(("parallel", "arbitrary")),
        name="rwkv7",
    )(pa, shift0, _pair_blockdiag(s0.astype(F32)), *params)
    return y, shift, _pair_unblock(s_new)


def _delta_body(qkv_ref, bab_ref, z_ref, conv0_ref, s0_ref, cw_ref, alog_ref, dtb_ref, nw_ref,
                y_ref, conv_ref, s_ref, *, tb, L):
    t_idx = pl.program_id(1)
    hd = DK_B
    inner = H_B * hd

    @pl.when(t_idx == 0)
    def _():
        conv_ref[...] = conv0_ref[...]
        s_ref[...] = s0_ref[...]

    x = qkv_ref[...]
    cat = jnp.concatenate([conv_ref[...], x], axis=0)
    conv_ref[...] = cat[tb:tb + CONV_CARRY, :]
    cw = cw_ref[...]
    conv = x * cw[CONV_W - 1:CONV_W, :]
    for j in range(1, CONV_W):
        conv = conv + pltpu.roll(cat, j, 0)[CONV_CARRY:, :] * cw[CONV_W - 1 - j:CONV_W - j, :]
    conv = _silu(conv)
    bab = bab_ref[...]
    g_all = -jnp.exp(alog_ref[...]) * _softplus(bab + dtb_ref[...])
    beta_all = _sigmoid(bab)

    tri = _tri_incl(L).astype(BF16)
    rw = _iota((L, L), 0)
    cl_ = _iota((L, L), 1)
    strict = cl_ < rw
    incl = cl_ <= rw
    eye = (rw == cl_).astype(F32)
    nw = nw_ref[...]
    z = z_ref[...]
    n_chunks = tb // L

    tri_bd, ones_bd = _chunk_masks(tb, L)
    g_cum_all = _mm_exact_lhs01(tri_bd, g_all)
    g_tot_all = _mm_exact_lhs01(ones_bd, g_all)

    qh, kh, vh, kdec, qg, bv_, bk_, gb, beta_b, e_last = {}, {}, {}, {}, {}, {}, {}, {}, {}, {}
    for h in range(H_B):
        qs = conv[:, h * hd:(h + 1) * hd]
        ks = conv[:, inner + h * hd:inner + (h + 1) * hd]
        vs = conv[:, 2 * inner + h * hd:2 * inner + (h + 1) * hd]
        qs = qs * lax.rsqrt(jnp.sum(qs * qs, axis=-1, keepdims=True) + EPS) * hd ** -0.5
        ks = ks * lax.rsqrt(jnp.sum(ks * ks, axis=-1, keepdims=True) + EPS)
        gb[h] = jnp.broadcast_to(g_all[:, h:h + 1], (tb, LANES))
        g_cum = jnp.broadcast_to(g_cum_all[:, h:h + 1], (tb, LANES))
        g_tot = jnp.broadcast_to(g_tot_all[:, h:h + 1], (tb, LANES))
        bb = jnp.broadcast_to(beta_all[:, H_B + h:H_B + h + 1], (tb, LANES))
        e_g = jnp.exp(g_cum)
        qh[h], kh[h], vh[h] = qs.astype(BF16), ks.astype(BF16), vs
        kdec[h] = (ks * jnp.exp(g_tot - g_cum)).astype(BF16)
        qg[h] = qs * e_g
        bv_[h] = bb * vs
        bk_[h] = bb * e_g * ks
        beta_b[h] = bb
        e_last[h] = jnp.exp(g_tot)

    items = [(c, h) for c in range(n_chunks) for h in range(H_B)]
    rs = {c: slice(c * L, (c + 1) * L) for c in range(n_chunks)}
    diff = {}
    for c, h in items:
        hi, mid, lo = _split3(jnp.where(strict, gb[h][rs[c], :L], 0.0))
        diff[(c, h)] = (jnp.dot(tri, hi, preferred_element_type=F32) + jnp.dot(tri, mid, preferred_element_type=F32)
                        + jnp.dot(tri, lo, preferred_element_type=F32))
    kk_ = {(c, h): _mm_nt(kh[h][rs[c]], kh[h][rs[c]]) for c, h in items}
    qk_ = {(c, h): _mm_nt(qh[h][rs[c]], kh[h][rs[c]]) for c, h in items}
    n_mat, qk_m = {}, {}
    for c, h in items:
        e_diff = jnp.exp(jnp.where(incl, diff[(c, h)], 0.0))
        n_mat[(c, h)] = -(beta_b[h][rs[c], :L] * jnp.where(strict, kk_[(c, h)] * e_diff, 0.0))
        qk_m[(c, h)] = jnp.where(incl, qk_[(c, h)] * e_diff, 0.0).astype(BF16)
    t_inv = {it: eye + n_mat[it] for it in items}
    m_pow = n_mat
    power = 2
    while power < L:
        m_pow = {it: _mm(m_pow[it], m_pow[it]) for it in items}
        t_inv = {it: t_inv[it] + _mm(t_inv[it], m_pow[it]) for it in items}
        power *= 2
    sol = {(c, h): _mm(t_inv[(c, h)], jnp.concatenate([bv_[h][rs[c]], bk_[h][rs[c]]], axis=1)) for c, h in items}
    lhs = {(c, h): jnp.concatenate([sol[(c, h)][:, hd:], qg[h][rs[c]]], axis=0).astype(BF16) for c, h in items}
    state = [s_ref[h] for h in range(H_B)]
    u_, o_inter = {}, {}
    for c in range(n_chunks):
        for h in range(H_B):
            ws = _mm(lhs[(c, h)], state[h])
            u_[(c, h)] = (sol[(c, h)][:, :hd] - ws[:L]).astype(BF16)
            o_inter[(c, h)] = ws[L:]
        for h in range(H_B):
            state[h] = e_last[h][c * L:c * L + 1, :] * state[h] + _mm_tn(kdec[h][rs[c]], u_[(c, h)])
    for h in range(H_B):
        s_ref[h] = state[h]
    for c, h in items:
        o = o_inter[(c, h)] + _mm(qk_m[(c, h)], u_[(c, h)])
        on = o * lax.rsqrt(jnp.mean(o * o, axis=-1, keepdims=True) + EPS) * nw
        y_ref[rs[c], h * hd:(h + 1) * hd] = (on * _silu(z[rs[c], h * hd:(h + 1) * hd])).astype(y_ref.dtype)


def _delta(qkv, bab, z, conv0, s0, wts, L):
    b, t, cdim = qkv.shape
    inner = H_B * DK_B
    tb = min(t, 256)
    pad = lambda x: jnp.pad(x.reshape(1, -1).astype(F32), ((0, 0), (0, LANES - x.size)))
    conv0p = jnp.pad(conv0.astype(F32), ((0, 0), (CONV_CARRY - (CONV_W - 1), 0), (0, 0)))
    params = [wts['conv_w'].astype(F32), pad(wts['A_log']), pad(wts['dt_bias']), wts['norm_w'].reshape(1, -1).astype(F32)]
    blk = lambda w: pl.BlockSpec((None, tb, w), lambda i, j: (i, j, 0))
    per_b = lambda shape: pl.BlockSpec((None,) + shape, lambda i, j: (i,) + (0,) * len(shape))
    y, conv_new, s_new = pl.pallas_call(
        functools.partial(_delta_body, tb=tb, L=L),
        out_shape=[jax.ShapeDtypeStruct((b, t, inner), BF16), jax.ShapeDtypeStruct((b, CONV_CARRY, cdim), F32),
                   jax.ShapeDtypeStruct((b, H_B, DK_B, DK_B), F32)],
        grid=(b, t // tb),
        in_specs=[blk(cdim), blk(LANES), blk(inner), per_b((CONV_CARRY, cdim)), per_b((H_B, DK_B, DK_B))]
        + [_resident(p.shape) for p in params],
        out_specs=[blk(inner), per_b((CONV_CARRY, cdim)), per_b((H_B, DK_B, DK_B))],
        compiler_params=_cparams(("parallel", "arbitrary")),
        name="gated_delta",
    )(qkv, bab, z, conv0p, s0.astype(F32), *params)
    return y, conv_new[:, CONV_CARRY - (CONV_W - 1):], s_new


def _gla_body(qk_ref, v_ref, ad_ref, g_ref, s0_ref, a2_ref, ab_ref, nw_ref, y_ref, s_ref, *, tb, L):
    t_idx = pl.program_id(1)
    kdim = H_C * DK_C
    n_pairs = kdim // LANES

    @pl.when(t_idx == 0)
    def _():
        s_ref[...] = s0_ref[...]

    qk = qk_ref[...]
    log_a = -_softplus(-(_mm(ad_ref[...], a2_ref[...]) + ab_ref[...])) * (1.0 / GLA_NORMALIZER)
    vv = v_ref[...]
    gg = g_ref[...]
    nw = nw_ref[...]
    incl = _iota((L, L), 1) <= _iota((L, L), 0)
    m0 = _iota((tb, LANES), 1) < DK_C
    n_chunks = tb // L

    tri_bd, ones_bd = _chunk_masks(tb, L)
    cum = _mm_exact_lhs01(tri_bd, log_a)
    tot = _mm_exact_lhs01(ones_bd, log_a)
    q_in = qk[:, :kdim] * DK_C ** -0.5 * jnp.exp(cum)
    k_in = (qk[:, kdim:] * jnp.exp(-cum)).astype(BF16)
    k_out = qk[:, kdim:] * jnp.exp(tot - cum)
    dec = jnp.exp(tot)
    vb16 = vv.astype(BF16)

    rs = {c: slice(c * L, (c + 1) * L) for c in range(n_chunks)}
    items = [(c, h) for c in range(n_chunks) for h in range(H_C)]
    qm, km = {}, {}
    for h in range(H_C):
        ls = slice((h // 2) * LANES, (h // 2 + 1) * LANES)
        mask = m0 if h % 2 == 0 else jnp.logical_not(m0)
        qm[h] = jnp.where(mask, q_in[:, ls], 0.0).astype(BF16)
        km[h] = jnp.where(mask, k_out[:, ls], 0.0).astype(BF16)
    scores = {(c, h): jnp.where(incl, _mm_nt(qm[h][rs[c]], k_in[rs[c], (h // 2) * LANES:(h // 2 + 1) * LANES]), 0.0)
              .astype(BF16) for c, h in items}
    upd = {(c, h): _mm_tn(vb16[rs[c], h * DV_C:(h + 1) * DV_C], km[h][rs[c]]) for c, h in items}
    state = [s_ref[p] for p in range(n_pairs)]
    s_prev = {}
    for c in range(n_chunks):
        for p in range(n_pairs):
            s_prev[(c, p)] = state[p].astype(BF16)
            state[p] = (state[p] * dec[c * L:c * L + 1, p * LANES:(p + 1) * LANES]
                        + upd[(c, 2 * p)] + upd[(c, 2 * p + 1)])
    for p in range(n_pairs):
        s_ref[p] = state[p]
    for c, h in items:
        hs = slice(h * DV_C, (h + 1) * DV_C)
        o = _mm_nt(qm[h][rs[c]], s_prev[(c, h // 2)]) + _mm(scores[(c, h)], vb16[rs[c], hs])
        on = o * lax.rsqrt(jnp.mean(o * o, axis=-1, keepdims=True) + EPS) * nw
        y_ref[rs[c], hs] = (on * _silu(gg[rs[c], hs])).astype(y_ref.dtype)


def _gla(cqk, cv, cad, cg, s0, wts, L):
    b, t, _ = cqk.shape
    kdim = H_C * DK_C
    vdim = H_C * DV_C
    n_pairs = kdim // LANES
    tb = min(t, 256)
    rank = wts['a2'].shape[0]
    a2p = jnp.pad(wts['a2'].astype(F32), ((0, LANES - rank), (0, 0))).astype(BF16)
    params = [a2p, wts['a_bias'].reshape(1, -1).astype(F32), wts['norm_w'].reshape(1, -1).astype(F32)]
    st0 = jnp.swapaxes(s0.astype(F32), -1, -2).reshape(b, n_pairs, 2, DV_C, DK_C)
    st0 = jnp.concatenate([st0[:, :, 0], st0[:, :, 1]], axis=-1)
    blk = lambda w: pl.BlockSpec((None, tb, w), lambda i, j: (i, j, 0))
    per_b = lambda shape: pl.BlockSpec((None,) + shape, lambda i, j: (i,) + (0,) * len(shape))
    y, st = pl.pallas_call(
        functools.partial(_gla_body, tb=tb, L=L),
        out_shape=[jax.ShapeDtypeStruct((b, t, vdim), BF16), jax.ShapeDtypeStruct((b, n_pairs, DV_C, LANES), F32)],
        grid=(b, t // tb),
        in_specs=[blk(2 * kdim), blk(vdim), blk(LANES), blk(vdim), per_b((n_pairs, DV_C, LANES))]
        + [_resident(p.shape) for p in params],
        out_specs=[blk(vdim), per_b((n_pairs, DV_C, LANES))],
        compiler_params=_cparams(("parallel", "arbitrary")),
        name="gla",
    )(cqk, cv, cad, cg, st0, *params)
    st = jnp.stack([st[..., :DK_C], st[..., DK_C:]], axis=2).reshape(b, H_C, DV_C, DK_C)
    return y, jnp.swapaxes(st, -1, -2)


def _ret_body(q_ref, k_ref, v_ref, g_ref, cos_ref, sin_ref, s0_ref, y_ref, s_ref, *, tb, L):
    t_idx = pl.program_id(1)
    hd = DK_D

    @pl.when(t_idx == 0)
    def _():
        s_ref[...] = s0_ref[...]

    cos = cos_ref[...]
    sin = sin_ref[...]
    qq, kk, vv, gg = q_ref[...], k_ref[...], v_ref[...], g_ref[...]
    rw = _iota((L, L), 0)
    cl_ = _iota((L, L), 1)
    incl = cl_ <= rw
    dist = jnp.where(incl, rw - cl_, 0).astype(F32)
    pos = _iota((L, LANES), 0).astype(F32)

    n_chunks = tb // L
    rs = {c: slice(c * L, (c + 1) * L) for c in range(n_chunks)}
    items = [(c, h) for c in range(n_chunks) for h in range(H_D)]
    lg = [math.log(1.0 - 2.0 ** (-5.0 - h)) for h in range(H_D)]
    vb16 = vv.astype(BF16)
    qh, kh, q_in, k_out, dmat = {}, {}, {}, {}, {}
    for h in range(H_D):
        hs = slice(h * hd, (h + 1) * hd)
        q_rot = qq[:, hs] * cos + pltpu.roll(qq[:, hs], hd // 2, 1) * sin
        k_rot = (kk[:, hs] * cos + pltpu.roll(kk[:, hs], hd // 2, 1) * sin) * hd ** -0.5
        dmat[h] = jnp.where(incl, jnp.exp(dist * lg[h]), 0.0)
        e_in = jnp.exp((pos + 1.0) * lg[h])
        e_out = jnp.exp((L - 1.0 - pos) * lg[h])
        qh[h], kh[h] = q_rot.astype(BF16), k_rot.astype(BF16)
        for c in range(n_chunks):
            q_in[(c, h)] = (q_rot[rs[c]] * e_in).astype(BF16)
            k_out[(c, h)] = (k_rot[rs[c]] * e_out).astype(BF16)
    scores = {(c, h): (_mm_nt(qh[h][rs[c]], kh[h][rs[c]]) * dmat[h]).astype(BF16) for c, h in items}
    upd = {(c, h): _mm_tn(k_out[(c, h)], vb16[rs[c], h * hd:(h + 1) * hd]) for c, h in items}
    state = [s_ref[h] for h in range(H_D)]
    s_prev = {}
    for c in range(n_chunks):
        for h in range(H_D):
            s_prev[(c, h)] = state[h].astype(BF16)
            state[h] = state[h] * math.exp(lg[h] * L) + upd[(c, h)]
    for h in range(H_D):
        s_ref[h] = state[h]
    for c, h in items:
        hs = slice(h * hd, (h + 1) * hd)
        o = _mm(q_in[(c, h)], s_prev[(c, h)]) + _mm(scores[(c, h)], vb16[rs[c], hs])
        mu = jnp.mean(o, axis=-1, keepdims=True)
        oc = o - mu
        on = oc * lax.rsqrt(jnp.mean(oc * oc, axis=-1, keepdims=True) + EPS)
        y_ref[rs[c], hs] = (on * _silu(gg[rs[c], hs])).astype(y_ref.dtype)


def _retention(dq, dk, dv, dg, s0, pos0, L):
    b, t, dim = dq.shape
    tb = min(t, 256)
    half = DK_D // 2
    inv = ROPE_BASE ** (-jnp.arange(0, DK_D, 2, dtype=F32) / DK_D)
    ang = (jnp.arange(t) + pos0).astype(F32)[:, None] * inv[None, :]
    cos = jnp.concatenate([jnp.cos(ang), jnp.cos(ang)], axis=-1)
    sin = jnp.concatenate([-jnp.sin(ang), jnp.sin(ang)], axis=-1)
    del half
    blk = pl.BlockSpec((None, tb, dim), lambda i, j: (i, j, 0))
    tab = pl.BlockSpec((tb, DK_D), lambda i, j: (j, 0))
    st = pl.BlockSpec((None, H_D, DK_D, DK_D), lambda i, j: (i, 0, 0, 0))
    return pl.pallas_call(
        functools.partial(_ret_body, tb=tb, L=L),
        out_shape=[jax.ShapeDtypeStruct((b, t, dim), BF16), jax.ShapeDtypeStruct((b, H_D, DK_D, DK_D), F32)],
        grid=(b, t // tb),
        in_specs=[blk, blk, blk, blk, tab, tab, st],
        out_specs=[blk, st],
        compiler_params=_cparams(("parallel", "arbitrary")),
        name="retention",
    )(dq, dk, dv, dg, cos, sin, s0.astype(F32))


def _prep_weights(W):
    bf = lambda x: x.astype(BF16)
    a_proj = 3 * H_A * N_A + 64 + 64 + 128
    b_conv = 3 * H_B * DK_B
    P = {}
    for name in ('ffn1_wg', 'ffn1_wu', 'ffn1_wd', 'ffn2_wg', 'ffn2_wu', 'ffn2_wd', 'mem_wq', 'mem_wo', 'mem_wk',
                 'mem_wv', 'even_w_out', 'odd_w_out'):
        P[name] = bf(W[name])
    ew = W['even_w_in']
    o = a_proj
    bab = ew[:, :, o + b_conv:o + b_conv + 2 * H_B]
    P['even_in'] = [bf(ew[:, :, :a_proj]), bf(ew[:, :, o:o + b_conv]),
                    bf(jnp.pad(bab, ((0, 0), (0, 0), (0, LANES - 2 * H_B)))), bf(ew[:, :, o + b_conv + 2 * H_B:])]
    ow = W['odd_w_in']
    kd, vd, rank = H_C * DK_C, H_C * DV_C, W['gla_a2'].shape[1]
    c_proj = 2 * kd + 2 * vd + rank
    dd = H_D * DK_D
    cad = ow[:, :, 2 * kd + vd:2 * kd + vd + rank]
    P['odd_in'] = [bf(ow[:, :, :2 * kd]), bf(ow[:, :, 2 * kd:2 * kd + vd]),
                   bf(jnp.pad(cad, ((0, 0), (0, 0), (0, LANES - rank)))), bf(ow[:, :, 2 * kd + vd + rank:c_proj])] \
        + [bf(ow[:, :, c_proj + j * dd:c_proj + (j + 1) * dd]) for j in range(4)]
    return P


def _trunk(x, pos0, mem_k, mem_v, shift, rwkv, conv, delta, gla, ret, W, P):
    b, t, d = x.shape
    n = b * t
    depth = W['norm_ffn1'].shape[0]
    L_delta = min(64, t)
    L_lin = min(64, t)
    flat = lambda z: z.reshape(n, z.shape[-1])
    unflat = lambda z: z.reshape(b, t, z.shape[-1])
    new = {k: [] for k in ('shift', 'rwkv', 'conv', 'delta', 'gla', 'ret')}
    x = flat(x)
    for l in range(depth):
        i = l // 2
        x = _ffn(x, W['norm_ffn1'][l], P['ffn1_wg'][l], P['ffn1_wu'][l], P['ffn1_wd'][l])
        if l % 2 == 0:
            pa, qkv, bab, z = _norm_proj(x, W['norm_mix'][l], [w[i] for w in P['even_in']])
            wa = dict(mu=W['rwkv_mu'][i], w0=W['rwkv_w0'][i], w2=W['rwkv_w2'][i], a0=W['rwkv_a0'][i],
                      a2=W['rwkv_a2'][i], g2=W['rwkv_g2'][i], kk=W['rwkv_kk'][i], ka=W['rwkv_ka'][i],
                      rk=W['rwkv_rk'][i], ln_w=W['rwkv_ln_w'][i], ln_b=W['rwkv_ln_b'][i])
            y_a, s1, s2 = _rwkv(unflat(pa), shift[i], rwkv[i], wa, L_delta)
            wb = dict(conv_w=W['delta_conv_w'][i], A_log=W['delta_A_log'][i], dt_bias=W['delta_dt_bias'][i],
                      norm_w=W['delta_norm_w'][i])
            y_b, s3, s4 = _delta(unflat(qkv), unflat(bab), unflat(z), conv[i], delta[i], wb, L_delta)
            new['shift'].append(s1)
            new['rwkv'].append(s2)
            new['conv'].append(s3)
            new['delta'].append(s4)
            a_dim = H_A * N_A
            x = _out_res(x, [flat(y_a), flat(y_b)], [P['even_w_out'][i][:a_dim], P['even_w_out'][i][a_dim:]])
        else:
            cqk, cv, cad, cg, dq, dk, dv, dg = _norm_proj(x, W['norm_mix'][l], [w[i] for w in P['odd_in']])
            wc = dict(a2=W['gla_a2'][i], a_bias=W['gla_a_bias'][i], norm_w=W['gla_norm_w'][i])
            y_c, s5 = _gla(unflat(cqk), unflat(cv), unflat(cad), unflat(cg), gla[i], wc, L_lin)
            y_d, s6 = _retention(unflat(dq), unflat(dk), unflat(dv), unflat(dg), ret[i], pos0, L_lin)
            new['gla'].append(s5)
            new['ret'].append(s6)
            c_dim = H_C * DV_C
            x = _out_res(x, [flat(y_c), flat(y_d)], [P['odd_w_out'][i][:c_dim], P['odd_w_out'][i][c_dim:]])
        mk = mem_k[l].reshape(b, -1, d).astype(BF16)
        mv = mem_v[l].reshape(b, -1, d).astype(BF16)
        heads = mem_k[l].shape[2]
        x = flat(_mem_attn(unflat(x), W['norm_mem'][l], P['mem_wq'][l], mk, mv, P['mem_wo'][l], heads))
        fw = W['final_norm'] if l == depth - 1 else None
        x = _ffn(x, W['norm_ffn2'][l], P['ffn2_wg'][l], P['ffn2_wu'][l], P['ffn2_wd'][l], fw)
    return unflat(x), new


def kernel(x_prompt, x_sample, mem_prompt, state_rwkv_shift, state_rwkv, state_delta_conv, state_delta, state_gla, state_ret, cache_mem_k, cache_mem_v, norm_ffn1, ffn1_wg, ffn1_wu, ffn1_wd, norm_mix, even_w_in, even_w_out, rwkv_mu, rwkv_w0, rwkv_w2, rwkv_a0, rwkv_a2, rwkv_g2, rwkv_kk, rwkv_ka, rwkv_rk, rwkv_ln_w, rwkv_ln_b, delta_conv_w, delta_A_log, delta_dt_bias, delta_norm_w, odd_w_in, odd_w_out, gla_a2, gla_a_bias, gla_norm_w, norm_mem, mem_norm_kv, mem_wq, mem_wk, mem_wv, mem_wo, norm_ffn2, ffn2_wg, ffn2_wu, ffn2_wd, final_norm):
    W = dict(norm_ffn1=norm_ffn1, ffn1_wg=ffn1_wg, ffn1_wu=ffn1_wu, ffn1_wd=ffn1_wd, norm_mix=norm_mix,
             even_w_in=even_w_in, even_w_out=even_w_out, rwkv_mu=rwkv_mu, rwkv_w0=rwkv_w0, rwkv_w2=rwkv_w2,
             rwkv_a0=rwkv_a0, rwkv_a2=rwkv_a2, rwkv_g2=rwkv_g2, rwkv_kk=rwkv_kk, rwkv_ka=rwkv_ka,
             rwkv_rk=rwkv_rk, rwkv_ln_w=rwkv_ln_w, rwkv_ln_b=rwkv_ln_b, delta_conv_w=delta_conv_w,
             delta_A_log=delta_A_log, delta_dt_bias=delta_dt_bias, delta_norm_w=delta_norm_w,
             odd_w_in=odd_w_in, odd_w_out=odd_w_out, gla_a2=gla_a2, gla_a_bias=gla_a_bias,
             gla_norm_w=gla_norm_w, norm_mem=norm_mem, mem_wq=mem_wq, mem_wk=mem_wk, mem_wv=mem_wv,
             mem_wo=mem_wo, norm_ffn2=norm_ffn2, ffn2_wg=ffn2_wg, ffn2_wu=ffn2_wu, ffn2_wd=ffn2_wd,
             final_norm=final_norm)
    P = _prep_weights(W)
    dt = x_prompt.dtype
    bp, _, d = x_prompt.shape
    depth = norm_ffn1.shape[0]
    n_even, n_odd = (depth + 1) // 2, depth // 2
    heads, hd = cache_mem_k.shape[3], cache_mem_k.shape[4]
    n_mem = mem_prompt.shape[1]
    pk, pv = [], []
    mem_flat = mem_prompt.reshape(bp * n_mem, d)
    for l in range(depth):
        mk, mv = _norm_proj(mem_flat, mem_norm_kv[l], [P['mem_wk'][l], P['mem_wv'][l]])
        pk.append(mk.reshape(bp, n_mem, heads, hd))
        pv.append(mv.reshape(bp, n_mem, heads, hd))
    zeros = lambda ref, cnt: [jnp.zeros((bp,) + ref.shape[2:], F32)] * cnt
    y_prompt, ps = _trunk(x_prompt, 0, pk, pv, zeros(state_rwkv_shift, n_even), zeros(state_rwkv, n_even),
                          zeros(state_delta_conv, n_even), zeros(state_delta, n_even), zeros(state_gla, n_odd),
                          zeros(state_ret, n_odd), W, P)
    y_sample, ss = _trunk(x_sample, PAST_LEN, [cache_mem_k[l] for l in range(depth)],
                          [cache_mem_v[l] for l in range(depth)],
                          [state_rwkv_shift[i] for i in range(n_even)], [state_rwkv[i] for i in range(n_even)],
                          [state_delta_conv[i] for i in range(n_even)], [state_delta[i] for i in range(n_even)],
                          [state_gla[i] for i in range(n_odd)], [state_ret[i] for i in range(n_odd)], W, P)
    st = lambda xs: jnp.stack(xs).astype(dt)
    order = ('shift', 'rwkv', 'conv', 'delta', 'gla', 'ret')
    return ((y_prompt, y_sample) + tuple(st(ps[k]) for k in order) + (st(pk), st(pv))
            + tuple(st(ss[k]) for k in order))
```

```python
import functools
import math

import jax
import jax.numpy as jnp
from jax import lax
from jax.experimental import pallas as pl
from jax.experimental.pallas import tpu as pltpu

F32 = jnp.float32
BF16 = jnp.bfloat16
EPS = 1e-6
RWKV_LN_EPS = 64e-5
GLA_NORMALIZER = 16.0
ROPE_BASE = 10000.0
PAST_LEN = 1024

LANES = 128
MXU_DIM = 256
VMEM_LIMIT = 56 * 1024 * 1024

H_A, N_A = 8, 64
H_B, DK_B = 4, 128
H_C, DK_C, DV_C = 4, 64, 128
H_D, DK_D = 4, 128
CONV_W = 4
CONV_CARRY = 8


def _cparams(sem):
    return pltpu.CompilerParams(dimension_semantics=sem, vmem_limit_bytes=VMEM_LIMIT)


def _mm(a, b):
    return jnp.dot(a.astype(BF16), b.astype(BF16), preferred_element_type=F32)


def _mm_nt(a, b):
    return lax.dot_general(a.astype(BF16), b.astype(BF16), (((1,), (1,)), ((), ())),
                           preferred_element_type=F32)


def _mm_tn(a, b):
    return lax.dot_general(a.astype(BF16), b.astype(BF16), (((0,), (0,)), ((), ())),
                           preferred_element_type=F32)


def _split3(x):
    hi = x.astype(BF16)
    r = x - hi.astype(F32)
    mid = r.astype(BF16)
    lo = (r - mid.astype(F32)).astype(BF16)
    return hi, mid, lo


def _mm_exact_lhs01(a01, x):
    hi, mid, lo = _split3(x)
    a = a01.astype(BF16)
    return (jnp.dot(a, hi, preferred_element_type=F32) + jnp.dot(a, mid, preferred_element_type=F32)
            + jnp.dot(a, lo, preferred_element_type=F32))


def _iota(shape, dim):
    return lax.broadcasted_iota(jnp.int32, shape, dim)


def _tri_incl(n):
    return (_iota((n, n), 1) <= _iota((n, n), 0)).astype(F32)


def _chunk_tri(tb, L):
    shift = L.bit_length() - 1
    assert 1 << shift == L
    rt, ct = _iota((tb, tb), 0), _iota((tb, tb), 1)
    same = jnp.right_shift(rt, shift) == jnp.right_shift(ct, shift)
    return jnp.logical_and(same, ct <= rt).astype(F32)


def _chunk_last(cum, L):
    tb, w = cum.shape
    return jnp.concatenate([jnp.broadcast_to(cum[c * L + L - 1:c * L + L, :], (L, w)) for c in range(tb // L)], axis=0)


def _sigmoid(x):
    return 0.5 * (jnp.tanh(0.5 * x) + 1.0)


def _silu(x):
    return x * _sigmoid(x)


def _softplus(x):
    return jnp.maximum(x, 0.0) + jnp.log1p(jnp.exp(-jnp.abs(x)))


def _rms(x, w):
    return x * lax.rsqrt(jnp.mean(x * x, axis=-1, keepdims=True) + EPS) * w


def _inv_unit_lower(n_mat, size):
    dim = n_mat.shape[0]
    eye = (_iota((dim, dim), 0) == _iota((dim, dim), 1)).astype(F32)
    t = eye + n_mat
    m = n_mat
    power = 2
    while power < size:
        m = _mm(m, m)
        t = t + _mm(t, m)
        power *= 2
    return t


def _row_tile(n, target):
    t = min(n, target)
    while n % t:
        t //= 2
    return t


def _resident(shape):
    nd = len(shape)
    return pl.BlockSpec(shape, lambda *_: (0,) * nd, pipeline_mode=pl.Buffered(1))


def _ffn_body(x_ref, nw_ref, wg_ref, wu_ref, wd_ref, fw_ref, o_ref, *, chunk, final):
    x = x_ref[...]
    h = _rms(x, nw_ref[...]).astype(BF16)
    d_ff = wg_ref.shape[1]
    acc = jnp.zeros(x.shape, F32)
    for c in range(d_ff // chunk):
        sl = slice(c * chunk, (c + 1) * chunk)
        g = jnp.dot(h, wg_ref[:, sl], preferred_element_type=F32)
        u = jnp.dot(h, wu_ref[:, sl], preferred_element_type=F32)
        a = (_silu(g) * u).astype(BF16)
        acc = acc + jnp.dot(a, wd_ref[sl, :], preferred_element_type=F32)
    y = x + 0.5 * acc
    if final:
        y = _rms(y, fw_ref[...])
    o_ref[...] = y


def _ffn(x, nw, wg, wu, wd, fw=None):
    n, d = x.shape
    d_ff = wg.shape[1]
    tm = _row_tile(n, 512)
    final = fw is not None
    if fw is None:
        fw = nw
    return pl.pallas_call(
        functools.partial(_ffn_body, chunk=d_ff // 2, final=final),
        out_shape=jax.ShapeDtypeStruct((n, d), F32),
        grid=(n // tm,),
        in_specs=[pl.BlockSpec((tm, d), lambda i: (i, 0)), _resident((1, d)), _resident((d, d_ff)),
                  _resident((d, d_ff)), _resident((d_ff, d)), _resident((1, d))],
        out_specs=pl.BlockSpec((tm, d), lambda i: (i, 0)),
        compiler_params=_cparams(("parallel",)),
        name="ffn",
    )(x, nw.reshape(1, d), wg, wu, wd, fw.reshape(1, d))


def _norm_proj_body(*refs, n_w):
    x_ref, nw_ref = refs[0], refs[1]
    w_refs = refs[2:2 + n_w]
    o_refs = refs[2 + n_w:]
    h = _rms(x_ref[...], nw_ref[...]).astype(BF16)
    for w_ref, o_ref in zip(w_refs, o_refs):
        o_ref[...] = jnp.dot(h, w_ref[...], preferred_element_type=F32)


def _norm_proj(x, nw, ws):
    n, d = x.shape
    tm = _row_tile(n, 512)
    return pl.pallas_call(
        functools.partial(_norm_proj_body, n_w=len(ws)),
        out_shape=[jax.ShapeDtypeStruct((n, w.shape[1]), F32) for w in ws],
        grid=(n // tm,),
        in_specs=[pl.BlockSpec((tm, d), lambda i: (i, 0)), _resident((1, d))] + [_resident(w.shape) for w in ws],
        out_specs=[pl.BlockSpec((tm, w.shape[1]), lambda i: (i, 0)) for w in ws],
        compiler_params=_cparams(("parallel",)),
        name="norm_proj",
    )(x, nw.reshape(1, d), *ws)


def _mix_out_mem_attn_body(x_ref, ya_ref, yb_ref, wa_ref, wb_ref, nw_ref, wq_ref, k_ref, v_ref, wo_ref, o_ref, *,
                           heads):
    x = (x_ref[...] + jnp.dot(ya_ref[...], wa_ref[...], preferred_element_type=F32)
         + jnp.dot(yb_ref[...], wb_ref[...], preferred_element_type=F32))
    h = _rms(x, nw_ref[...]).astype(BF16)
    q = jnp.dot(h, wq_ref[...], preferred_element_type=F32)
    d = q.shape[1]
    hd = d // heads
    outs = []
    for i in range(heads):
        sl = slice(i * hd, (i + 1) * hd)
        s = _mm_nt(q[:, sl], k_ref[:, sl]) * hd ** -0.5
        e = jnp.exp(s - jnp.max(s, axis=-1, keepdims=True))
        pr = e / jnp.sum(e, axis=-1, keepdims=True)
        outs.append(_mm(pr, v_ref[:, sl]))
    o = jnp.concatenate(outs, axis=-1).astype(BF16)
    o_ref[...] = x + jnp.dot(o, wo_ref[...], preferred_element_type=F32)


def _mix_out_mem_attn(x, ya, yb, wa, wb, nw, wq, mk, mv, wo, heads):
    b, t, d = x.shape
    m = mk.shape[1]
    tm = _row_tile(t, 512)
    row = lambda w: pl.BlockSpec((None, tm, w), lambda i, j: (i, j, 0))
    mem = pl.BlockSpec((None, m, d), lambda i, j: (i, 0, 0))
    return pl.pallas_call(
        functools.partial(_mix_out_mem_attn_body, heads=heads),
        out_shape=jax.ShapeDtypeStruct((b, t, d), F32),
        grid=(b, t // tm),
        in_specs=[row(d), row(ya.shape[2]), row(yb.shape[2]), _resident(wa.shape), _resident(wb.shape),
                  _resident((1, d)), _resident((d, d)), mem, mem, _resident((d, d))],
        out_specs=row(d),
        compiler_params=_cparams(("parallel", "parallel")),
        name="mix_out_mem_attn",
    )(x, ya, yb, wa, wb, nw.reshape(1, d), wq, mk, mv, wo)


def _seg_sum(x, bd):
    g = bd.shape[0]
    hi = x.astype(BF16)
    lo = (x - hi.astype(F32)).astype(BF16)
    outs = []
    for j in range(x.shape[1] // g):
        sl = slice(j * g, (j + 1) * g)
        outs.append(jnp.dot(hi[:, sl], bd, preferred_element_type=F32) + jnp.dot(lo[:, sl], bd, preferred_element_type=F32))
    return jnp.concatenate(outs, axis=1)


def _rwkv_body(pa_ref, shift0_ref, s0_ref, mu_ref, w0_ref, w2_ref, a0_ref, a2_ref, g2_ref, kkw_ref, ka_ref,
               rk_ref, lnw_ref, lnb_ref, bd_ref, y_ref, shift_ref, s_ref, carry, y_scr, *, tb, L):
    t_idx = pl.program_id(1)
    a_dim = H_A * N_A
    n_pairs = a_dim // LANES

    @pl.when(t_idx == 0)
    def _():
        carry[...] = shift0_ref[...]
        s_ref[...] = s0_ref[...]

    pa = pa_ref[...]
    row = _iota(pa.shape, 0)
    prev = jnp.where(row == 0, carry[...], pltpu.roll(pa, 1, 0))
    carry[...] = pa[tb - 1:tb, :]
    shift_ref[...] = pa[tb - 1:tb, :]
    xa = pa + (prev - pa) * mu_ref[...]
    r = xa[:, 0:a_dim]
    k = xa[:, a_dim:2 * a_dim]
    v = xa[:, 2 * a_dim:3 * a_dim]
    xwa = xa[:, 3 * a_dim:3 * a_dim + LANES]
    xg = xa[:, 3 * a_dim + LANES:]
    wlog = -_softplus(-(w0_ref[...] + _mm(jnp.tanh(xwa), w2_ref[...]))) - 0.5
    logdec = -jnp.exp(wlog)
    a = _sigmoid(a0_ref[...] + _mm(xwa, a2_ref[...]))
    gate = _mm(_sigmoid(xg), g2_ref[...])
    bd = bd_ref[...]
    kq = k * kkw_ref[...]
    kk = kq * lax.rsqrt(_seg_sum(kq * kq, bd) + EPS)
    k2 = k * (1.0 + (a - 1.0) * ka_ref[...])
    av = -kk
    bv = kk * a

    lane = _iota((L, LANES), 1)
    m0 = lane < N_A
    r2 = _iota((2 * L, 2 * L), 0)
    c2 = _iota((2 * L, 2 * L), 1)
    bdm = jnp.logical_not(jnp.logical_xor(r2 >= L, c2 >= L))
    r_loc = jnp.where(r2 >= L, r2 - L, r2)
    c_loc = jnp.where(c2 >= L, c2 - L, c2)
    strict = jnp.logical_and(bdm, c_loc < r_loc)
    incl = jnp.logical_and(bdm, c_loc <= r_loc)
    strict_t = jnp.logical_and(bdm, c_loc > r_loc)
    r4 = _iota((2 * L, 4 * L), 0)
    c4 = _iota((2 * L, 4 * L), 1)
    c4 = jnp.where(c4 >= 2 * L, c4 - 2 * L, c4)
    incl_2 = jnp.logical_and(jnp.logical_not(jnp.logical_xor(r4 >= L, c4 >= L)),
                             jnp.where(c4 >= L, c4 - L, c4) <= jnp.where(r4 >= L, r4 - L, r4))
    bdl = jnp.logical_not(jnp.logical_xor(_iota((2 * L, LANES), 0) >= L, _iota((2 * L, LANES), 1) >= N_A))
    eye2 = (r2 == c2).astype(F32)

    def stack2(x):
        return jnp.concatenate([jnp.where(m0, x, 0.0), jnp.where(m0, 0.0, x)], axis=0).astype(BF16)

    def dup2(x):
        return jnp.concatenate([x, x], axis=0).astype(BF16)

    cum = _mm_exact_lhs01(_chunk_tri(tb, L), logdec)
    tot = _chunk_last(cum, L)
    e_neg = jnp.exp(-cum)
    e_out = jnp.exp(tot - cum)
    a_t = av * jnp.exp(cum - logdec)
    r_t = r * jnp.exp(cum)
    b_t = bv * e_neg
    k_t = k2 * e_neg
    b_o = bv * e_out
    k_o = k2 * e_out
    g_l = jnp.exp(tot)

    items = [(c, p) for c in range(tb // L) for p in range(n_pairs)]
    sl = {(c, p): (slice(c * L, (c + 1) * L), slice(p * LANES, (p + 1) * LANES)) for c, p in items}
    xa = {it: stack2(a_t[sl[it]]) for it in items}
    xr = {it: stack2(r_t[sl[it]]) for it in items}
    yb = {it: dup2(b_t[sl[it]]) for it in items}
    yk = {it: dup2(k_t[sl[it]]) for it in items}
    v2 = {it: jnp.where(bdl, jnp.concatenate([v[sl[it]], v[sl[it]]], axis=0), 0.0).astype(BF16) for it in items}
    wst = {it: jnp.concatenate([stack2(b_o[sl[it]]), stack2(k_o[sl[it]])], axis=0) for it in items}
    n_t = {it: jnp.where(strict_t, _mm_nt(yb[it], xa[it]), 0.0) for it in items}
    ak = {it: jnp.where(strict, _mm_nt(xa[it], yk[it]), 0.0).astype(BF16) for it in items}
    rbk = {it: jnp.where(incl_2, _mm_nt(xr[it], jnp.concatenate([yb[it], yk[it]], axis=0)), 0.0).astype(BF16)
           for it in items}
    akv = {it: _mm(ak[it], v2[it]) for it in items}
    t_t = {it: eye2 + n_t[it] for it in items}
    m_t = {it: _mm(n_t[it], n_t[it]) for it in items}
    power = 4
    while power < L:
        prod = {it: _mm(m_t[it], jnp.concatenate([m_t[it], t_t[it]], axis=1)) for it in items}
        m_t = {it: prod[it][:, :2 * L] for it in items}
        t_t = {it: t_t[it] + prod[it][:, 2 * L:] for it in items}
        power *= 2
    if L > 2:
        t_t = {it: t_t[it] + _mm(m_t[it], t_t[it]) for it in items}
    om_up = {it: _mm_tn(t_t[it], jnp.concatenate([xa[it], akv[it].astype(BF16)], axis=1)) for it in items}
    omega = {it: om_up[it][:, :LANES].astype(BF16) for it in items}
    state = [s_ref[p] for p in range(n_pairs)]
    s_prev, uv = {}, {}
    for c in range(tb // L):
        for p in range(n_pairs):
            it = (c, p)
            s_prev[it] = state[p].astype(BF16)
            u_st = _mm_nt(omega[it], s_prev[it]) + om_up[it][:, LANES:]
            uv[it] = jnp.concatenate([u_st.astype(BF16), v2[it]], axis=0)
        for p in range(n_pairs):
            it = (c, p)
            state[p] = state[p] * g_l[c * L:c * L + 1, sl[it][1]] + _mm_tn(uv[it], wst[it])
    for p in range(n_pairs):
        s_ref[p] = state[p]
    for it in items:
        y_st = _mm_nt(xr[it], s_prev[it]) + _mm(rbk[it], uv[it])
        y_scr[sl[it]] = y_st[0:L] + y_st[L:2 * L]

    y = y_scr[...]
    inv_n = 1.0 / N_A
    mu_y = _seg_sum(y, bd) * inv_n
    yc = y - mu_y
    var = _seg_sum(yc * yc, bd) * inv_n
    yn = yc * lax.rsqrt(var + RWKV_LN_EPS) * lnw_ref[...] + lnb_ref[...]
    bonus = _seg_sum(r * k2 * rk_ref[...], bd) * v
    y_ref[...] = ((yn + bonus) * gate).astype(y_ref.dtype)


def _pair_blockdiag(s):
    b, h, n, _ = s.shape
    s = s.reshape(b, h // 2, 2, n, n)
    z = jnp.zeros((b, h // 2, n, n), s.dtype)
    top = jnp.concatenate([s[:, :, 0], z], axis=-1)
    bot = jnp.concatenate([z, s[:, :, 1]], axis=-1)
    return jnp.concatenate([top, bot], axis=-2)


def _pair_unblock(s):
    b, p, n2, _ = s.shape
    n = n2 // 2
    return jnp.stack([s[:, :, :n, :n], s[:, :, n:, n:]], axis=2).reshape(b, 2 * p, n, n)


def _rwkv(pa, shift0, s0, wts, L):
    b, t, pw = pa.shape
    a_dim = H_A * N_A
    tb = min(t, 256)
    n_pairs = a_dim // LANES
    seg = _iota((MXU_DIM, MXU_DIM), 0) // N_A == _iota((MXU_DIM, MXU_DIM), 1) // N_A
    bd = seg.astype(BF16)
    z64 = jnp.zeros((N_A, a_dim), F32)
    w2p = jnp.concatenate([wts['w2'], z64], axis=0)
    a2p = jnp.concatenate([z64, wts['a2']], axis=0)
    vec = lambda x: x.reshape(1, -1).astype(F32)
    params = [vec(wts['mu']), vec(wts['w0']), w2p.astype(BF16), vec(wts['a0']), a2p.astype(BF16),
              wts['g2'].astype(BF16), vec(wts['kk']), vec(wts['ka']), vec(wts['rk']), vec(wts['ln_w']),
              vec(wts['ln_b']), bd]
    blk_t = pl.BlockSpec((None, tb, pw), lambda i, j: (i, j, 0))
    per_b = lambda shape: pl.BlockSpec((None,) + shape, lambda i, j: (i,) + (0,) * len(shape))
    y, shift, s_new = pl.pallas_call(
        functools.partial(_rwkv_body, tb=tb, L=L),
        out_shape=[jax.ShapeDtypeStruct((b, t, a_dim), BF16), jax.ShapeDtypeStruct((b, 1, pw), F32),
                   jax.ShapeDtypeStruct((b, n_pairs, LANES, LANES), F32)],
        grid=(b, t // tb),
        in_specs=[blk_t, per_b((1, pw)), per_b((n_pairs, LANES, LANES))] + [_resident(p.shape) for p in params],
        out_specs=[pl.BlockSpec((None, tb, a_dim), lambda i, j: (i, j, 0)), per_b((1, pw)),
                   per_b((n_pairs, LANES, LANES))],
        scratch_shapes=[pltpu.VMEM((1, pw), F32), pltpu.VMEM((tb, a_dim), F32)],
        compiler_params=_cparams(("parallel", "arbitrary")),
        name="rwkv7",
    )(pa, shift0, _pair_blockdiag(s0.astype(F32)), *params)
    return y, shift, _pair_unblock(s_new)


def _delta_body(qkv_ref, bab_ref, z_ref, conv0_ref, s0_ref, cw_ref, alog_ref, dtb_ref, nw_ref,
                y_ref, conv_ref, s_ref, cat_scr, *, tb, L):
    t_idx = pl.program_id(1)
    hd = DK_B
    inner = H_B * hd

    @pl.when(t_idx == 0)
    def _():
        conv_ref[...] = conv0_ref[...]
        s_ref[...] = s0_ref[...]

    x = qkv_ref[...]
    cat_scr[0:CONV_CARRY, :] = conv_ref[...]
    cat_scr[CONV_CARRY:, :] = x
    conv_ref[...] = x[tb - CONV_CARRY:, :]
    cw = cw_ref[...]
    conv = x * cw[CONV_W - 1:CONV_W, :]
    for j in range(1, CONV_W):
        conv = conv + cat_scr[CONV_CARRY - j:CONV_CARRY - j + tb, :] * cw[CONV_W - 1 - j:CONV_W - j, :]
    conv = _silu(conv)
    bab = bab_ref[...]
    g_all = -jnp.exp(alog_ref[...]) * _softplus(bab + dtb_ref[...])
    beta_all = _sigmoid(bab)

    tri = _tri_incl(L).astype(BF16)
    rw = _iota((L, L), 0)
    cl_ = _iota((L, L), 1)
    strict = cl_ < rw
    incl = cl_ <= rw
    eye = (rw == cl_).astype(F32)
    nw = nw_ref[...]
    z = z_ref[...]
    n_chunks = tb // L

    g_cum_all = _mm_exact_lhs01(_chunk_tri(tb, L), g_all)
    g_tot_all = _chunk_last(g_cum_all, L)

    qh, kh, vh, kdec, qg, bv_, bk_, gb, beta_b, e_last = {}, {}, {}, {}, {}, {}, {}, {}, {}, {}
    for h in range(H_B):
        qs = conv[:, h * hd:(h + 1) * hd]
        ks = conv[:, inner + h * hd:inner + (h + 1) * hd]
        vs = conv[:, 2 * inner + h * hd:2 * inner + (h + 1) * hd]
        qs = qs * lax.rsqrt(jnp.sum(qs * qs, axis=-1, keepdims=True) + EPS) * hd ** -0.5
        ks = ks * lax.rsqrt(jnp.sum(ks * ks, axis=-1, keepdims=True) + EPS)
        gb[h] = jnp.broadcast_to(g_all[:, h:h + 1], (tb, LANES))
        g_cum = jnp.broadcast_to(g_cum_all[:, h:h + 1], (tb, LANES))
        g_tot = jnp.broadcast_to(g_tot_all[:, h:h + 1], (tb, LANES))
        bb = jnp.broadcast_to(beta_all[:, H_B + h:H_B + h + 1], (tb, LANES))
        e_g = jnp.exp(g_cum)
        qh[h], kh[h], vh[h] = qs.astype(BF16), ks.astype(BF16), vs
        kdec[h] = (ks * jnp.exp(g_tot - g_cum)).astype(BF16)
        qg[h] = qs * e_g
        bv_[h] = bb * vs
        bk_[h] = bb * e_g * ks
        beta_b[h] = bb
        e_last[h] = jnp.exp(g_tot)

    items = [(c, h) for c in range(n_chunks) for h in range(H_B)]
    rs = {c: slice(c * L, (c + 1) * L) for c in range(n_chunks)}
    diff = {}
    for c, h in items:
        hi, mid, lo = _split3(jnp.where(strict, gb[h][rs[c], :L], 0.0))
        diff[(c, h)] = (jnp.dot(tri, hi, preferred_element_type=F32) + jnp.dot(tri, mid, preferred_element_type=F32)
                        + jnp.dot(tri, lo, preferred_element_type=F32))
    kk_ = {(c, h): _mm_nt(kh[h][rs[c]], kh[h][rs[c]]) for c, h in items}
    qk_ = {(c, h): _mm_nt(qh[h][rs[c]], kh[h][rs[c]]) for c, h in items}
    n_mat, qk_m = {}, {}
    for c, h in items:
        e_diff = jnp.exp(jnp.where(incl, diff[(c, h)], 0.0))
        n_mat[(c, h)] = -(beta_b[h][rs[c], :L] * jnp.where(strict, kk_[(c, h)] * e_diff, 0.0))
        qk_m[(c, h)] = jnp.where(incl, qk_[(c, h)] * e_diff, 0.0).astype(BF16)
    t_inv = {it: eye + n_mat[it] for it in items}
    m_pow = n_mat
    power = 2
    while power < L:
        m_pow = {it: _mm(m_pow[it], m_pow[it]) for it in items}
        t_inv = {it: t_inv[it] + _mm(t_inv[it], m_pow[it]) for it in items}
        power *= 2
    sol = {(c, h): _mm(t_inv[(c, h)], jnp.concatenate([bv_[h][rs[c]], bk_[h][rs[c]]], axis=1)) for c, h in items}
    lhs = {(c, h): jnp.concatenate([sol[(c, h)][:, hd:], qg[h][rs[c]]], axis=0).astype(BF16) for c, h in items}
    state = [s_ref[h] for h in range(H_B)]
    u_, o_inter = {}, {}
    for c in range(n_chunks):
        for h in range(H_B):
            ws = _mm(lhs[(c, h)], state[h])
            u_[(c, h)] = (sol[(c, h)][:, :hd] - ws[:L]).astype(BF16)
            o_inter[(c, h)] = ws[L:]
        for h in range(H_B):
            state[h] = e_last[h][c * L:c * L + 1, :] * state[h] + _mm_tn(kdec[h][rs[c]], u_[(c, h)])
    for h in range(H_B):
        s_ref[h] = state[h]
    for c, h in items:
        o = o_inter[(c, h)] + _mm(qk_m[(c, h)], u_[(c, h)])
        on = o * lax.rsqrt(jnp.mean(o * o, axis=-1, keepdims=True) + EPS) * nw
        y_ref[rs[c], h * hd:(h + 1) * hd] = (on * _silu(z[rs[c], h * hd:(h + 1) * hd])).astype(y_ref.dtype)


def _delta(qkv, bab, z, conv0, s0, wts, L):
    b, t, cdim = qkv.shape
    inner = H_B * DK_B
    tb = min(t, 256)
    pad = lambda x: jnp.pad(x.reshape(1, -1).astype(F32), ((0, 0), (0, LANES - x.size)))
    conv0p = jnp.pad(conv0.astype(F32), ((0, 0), (CONV_CARRY - (CONV_W - 1), 0), (0, 0)))
    params = [wts['conv_w'].astype(F32), pad(wts['A_log']), pad(wts['dt_bias']), wts['norm_w'].reshape(1, -1).astype(F32)]
    blk = lambda w: pl.BlockSpec((None, tb, w), lambda i, j: (i, j, 0))
    per_b = lambda shape: pl.BlockSpec((None,) + shape, lambda i, j: (i,) + (0,) * len(shape))
    y, conv_new, s_new = pl.pallas_call(
        functools.partial(_delta_body, tb=tb, L=L),
        out_shape=[jax.ShapeDtypeStruct((b, t, inner), BF16), jax.ShapeDtypeStruct((b, CONV_CARRY, cdim), F32),
                   jax.ShapeDtypeStruct((b, H_B, DK_B, DK_B), F32)],
        grid=(b, t // tb),
        in_specs=[blk(cdim), blk(LANES), blk(inner), per_b((CONV_CARRY, cdim)), per_b((H_B, DK_B, DK_B))]
        + [_resident(p.shape) for p in params],
        out_specs=[blk(inner), per_b((CONV_CARRY, cdim)), per_b((H_B, DK_B, DK_B))],
        scratch_shapes=[pltpu.VMEM((tb + CONV_CARRY, cdim), F32)],
        compiler_params=_cparams(("parallel", "arbitrary")),
        name="gated_delta",
    )(qkv, bab, z, conv0p, s0.astype(F32), *params)
    return y, conv_new[:, CONV_CARRY - (CONV_W - 1):], s_new


def _gla_body(qk_ref, v_ref, ad_ref, g_ref, s0_ref, a2_ref, ab_ref, nw_ref, y_ref, s_ref, *, tb, L):
    t_idx = pl.program_id(1)
    kdim = H_C * DK_C
    n_pairs = kdim // LANES

    @pl.when(t_idx == 0)
    def _():
        s_ref[...] = s0_ref[...]

    qk = qk_ref[...]
    log_a = -_softplus(-(_mm(ad_ref[...], a2_ref[...]) + ab_ref[...])) * (1.0 / GLA_NORMALIZER)
    vv = v_ref[...]
    gg = g_ref[...]
    nw = nw_ref[...]
    incl = _iota((L, L), 1) <= _iota((L, L), 0)
    m0 = _iota((tb, LANES), 1) < DK_C
    n_chunks = tb // L

    cum = _mm_exact_lhs01(_chunk_tri(tb, L), log_a)
    tot = _chunk_last(cum, L)
    q_in = qk[:, :kdim] * DK_C ** -0.5 * jnp.exp(cum)
    k_in = (qk[:, kdim:] * jnp.exp(-cum)).astype(BF16)
    k_out = qk[:, kdim:] * jnp.exp(tot - cum)
    dec = jnp.exp(tot)
    vb16 = vv.astype(BF16)

    rs = {c: slice(c * L, (c + 1) * L) for c in range(n_chunks)}
    items = [(c, h) for c in range(n_chunks) for h in range(H_C)]
    qm, km = {}, {}
    for h in range(H_C):
        ls = slice((h // 2) * LANES, (h // 2 + 1) * LANES)
        mask = m0 if h % 2 == 0 else jnp.logical_not(m0)
        qm[h] = jnp.where(mask, q_in[:, ls], 0.0).astype(BF16)
        km[h] = jnp.where(mask, k_out[:, ls], 0.0).astype(BF16)
    scores = {(c, h): jnp.where(incl, _mm_nt(qm[h][rs[c]], k_in[rs[c], (h // 2) * LANES:(h // 2 + 1) * LANES]), 0.0)
              .astype(BF16) for c, h in items}
    upd = {(c, h): _mm_tn(vb16[rs[c], h * DV_C:(h + 1) * DV_C], km[h][rs[c]]) for c, h in items}
    state = [s_ref[p] for p in range(n_pairs)]
    s_prev = {}
    for c in range(n_chunks):
        for p in range(n_pairs):
            s_prev[(c, p)] = state[p].astype(BF16)
            state[p] = (state[p] * dec[c * L:c * L + 1, p * LANES:(p + 1) * LANES]
                        + upd[(c, 2 * p)] + upd[(c, 2 * p + 1)])
    for p in range(n_pairs):
        s_ref[p] = state[p]
    for c, h in items:
        hs = slice(h * DV_C, (h + 1) * DV_C)
        o = _mm_nt(qm[h][rs[c]], s_prev[(c, h // 2)]) + _mm(scores[(c, h)], vb16[rs[c], hs])
        on = o * lax.rsqrt(jnp.mean(o * o, axis=-1, keepdims=True) + EPS) * nw
        y_ref[rs[c], hs] = (on * _silu(gg[rs[c], hs])).astype(y_ref.dtype)


def _gla(cqk, cv, cad, cg, s0, wts, L):
    b, t, _ = cqk.shape
    kdim = H_C * DK_C
    vdim = H_C * DV_C
    n_pairs = kdim // LANES
    tb = min(t, 256)
    rank = wts['a2'].shape[0]
    a2p = jnp.pad(wts['a2'].astype(F32), ((0, LANES - rank), (0, 0))).astype(BF16)
    params = [a2p, wts['a_bias'].reshape(1, -1).astype(F32), wts['norm_w'].reshape(1, -1).astype(F32)]
    st0 = jnp.swapaxes(s0.astype(F32), -1, -2).reshape(b, n_pairs, 2, DV_C, DK_C)
    st0 = jnp.concatenate([st0[:, :, 0], st0[:, :, 1]], axis=-1)
    blk = lambda w: pl.BlockSpec((None, tb, w), lambda i, j: (i, j, 0))
    per_b = lambda shape: pl.BlockSpec((None,) + shape, lambda i, j: (i,) + (0,) * len(shape))
    y, st = pl.pallas_call(
        functools.partial(_gla_body, tb=tb, L=L),
        out_shape=[jax.ShapeDtypeStruct((b, t, vdim), BF16), jax.ShapeDtypeStruct((b, n_pairs, DV_C, LANES), F32)],
        grid=(b, t // tb),
        in_specs=[blk(2 * kdim), blk(vdim), blk(LANES), blk(vdim), per_b((n_pairs, DV_C, LANES))]
        + [_resident(p.shape) for p in params],
        out_specs=[blk(vdim), per_b((n_pairs, DV_C, LANES))],
        compiler_params=_cparams(("parallel", "arbitrary")),
        name="gla",
    )(cqk, cv, cad, cg, st0, *params)
    st = jnp.stack([st[..., :DK_C], st[..., DK_C:]], axis=2).reshape(b, H_C, DV_C, DK_C)
    return y, jnp.swapaxes(st, -1, -2)


def _ret_body(q_ref, k_ref, v_ref, g_ref, cos_ref, sin_ref, s0_ref, y_ref, s_ref, *, tb, L):
    t_idx = pl.program_id(1)
    hd = DK_D

    @pl.when(t_idx == 0)
    def _():
        s_ref[...] = s0_ref[...]

    cos = cos_ref[...]
    sin = sin_ref[...]
    qq, kk, vv, gg = q_ref[...], k_ref[...], v_ref[...], g_ref[...]
    rw = _iota((L, L), 0)
    cl_ = _iota((L, L), 1)
    incl = cl_ <= rw
    dist = jnp.where(incl, rw - cl_, 0).astype(F32)
    pos = _iota((L, LANES), 0).astype(F32)

    n_chunks = tb // L
    rs = {c: slice(c * L, (c + 1) * L) for c in range(n_chunks)}
    items = [(c, h) for c in range(n_chunks) for h in range(H_D)]
    lg = [math.log(1.0 - 2.0 ** (-5.0 - h)) for h in range(H_D)]
    vb16 = vv.astype(BF16)
    qh, kh, q_in, k_out, dmat = {}, {}, {}, {}, {}
    for h in range(H_D):
        hs = slice(h * hd, (h + 1) * hd)
        q_rot = qq[:, hs] * cos + pltpu.roll(qq[:, hs], hd // 2, 1) * sin
        k_rot = (kk[:, hs] * cos + pltpu.roll(kk[:, hs], hd // 2, 1) * sin) * hd ** -0.5
        dmat[h] = jnp.where(incl, jnp.exp(dist * lg[h]), 0.0)
        e_in = jnp.exp((pos + 1.0) * lg[h])
        e_out = jnp.exp((L - 1.0 - pos) * lg[h])
        qh[h], kh[h] = q_rot.astype(BF16), k_rot.astype(BF16)
        for c in range(n_chunks):
            q_in[(c, h)] = (q_rot[rs[c]] * e_in).astype(BF16)
            k_out[(c, h)] = (k_rot[rs[c]] * e_out).astype(BF16)
    scores = {(c, h): (_mm_nt(qh[h][rs[c]], kh[h][rs[c]]) * dmat[h]).astype(BF16) for c, h in items}
    upd = {(c, h): _mm_tn(k_out[(c, h)], vb16[rs[c], h * hd:(h + 1) * hd]) for c, h in items}
    state = [s_ref[h] for h in range(H_D)]
    s_prev = {}
    for c in range(n_chunks):
        for h in range(H_D):
            s_prev[(c, h)] = state[h].astype(BF16)
            state[h] = state[h] * math.exp(lg[h] * L) + upd[(c, h)]
    for h in range(H_D):
        s_ref[h] = state[h]
    for c, h in items:
        hs = slice(h * hd, (h + 1) * hd)
        o = _mm(q_in[(c, h)], s_prev[(c, h)]) + _mm(scores[(c, h)], vb16[rs[c], hs])
        mu = jnp.mean(o, axis=-1, keepdims=True)
        oc = o - mu
        on = oc * lax.rsqrt(jnp.mean(oc * oc, axis=-1, keepdims=True) + EPS)
        y_ref[rs[c], hs] = (on * _silu(gg[rs[c], hs])).astype(y_ref.dtype)


def _retention(dq, dk, dv, dg, s0, pos0, L):
    b, t, dim = dq.shape
    tb = min(t, 256)
    half = DK_D // 2
    inv = ROPE_BASE ** (-jnp.arange(0, DK_D, 2, dtype=F32) / DK_D)
    ang = (jnp.arange(t) + pos0).astype(F32)[:, None] * inv[None, :]
    cos = jnp.concatenate([jnp.cos(ang), jnp.cos(ang)], axis=-1)
    sin = jnp.concatenate([-jnp.sin(ang), jnp.sin(ang)], axis=-1)
    del half
    blk = pl.BlockSpec((None, tb, dim), lambda i, j: (i, j, 0))
    tab = pl.BlockSpec((tb, DK_D), lambda i, j: (j, 0))
    st = pl.BlockSpec((None, H_D, DK_D, DK_D), lambda i, j: (i, 0, 0, 0))
    return pl.pallas_call(
        functools.partial(_ret_body, tb=tb, L=L),
        out_shape=[jax.ShapeDtypeStruct((b, t, dim), BF16), jax.ShapeDtypeStruct((b, H_D, DK_D, DK_D), F32)],
        grid=(b, t // tb),
        in_specs=[blk, blk, blk, blk, tab, tab, st],
        out_specs=[blk, st],
        compiler_params=_cparams(("parallel", "arbitrary")),
        name="retention",
    )(dq, dk, dv, dg, cos, sin, s0.astype(F32))


def _prep_weights(W):
    bf = lambda x: x.astype(BF16)
    a_proj = 3 * H_A * N_A + 64 + 64 + 128
    b_conv = 3 * H_B * DK_B
    P = {}
    for name in ('ffn1_wg', 'ffn1_wu', 'ffn1_wd', 'ffn2_wg', 'ffn2_wu', 'ffn2_wd', 'mem_wq', 'mem_wo', 'mem_wk',
                 'mem_wv', 'even_w_out', 'odd_w_out'):
        P[name] = bf(W[name])
    ew = W['even_w_in']
    o = a_proj
    bab = ew[:, :, o + b_conv:o + b_conv + 2 * H_B]
    P['even_in'] = [bf(ew[:, :, :a_proj]), bf(ew[:, :, o:o + b_conv]),
                    bf(jnp.pad(bab, ((0, 0), (0, 0), (0, LANES - 2 * H_B)))), bf(ew[:, :, o + b_conv + 2 * H_B:])]
    ow = W['odd_w_in']
    kd, vd, rank = H_C * DK_C, H_C * DV_C, W['gla_a2'].shape[1]
    c_proj = 2 * kd + 2 * vd + rank
    dd = H_D * DK_D
    cad = ow[:, :, 2 * kd + vd:2 * kd + vd + rank]
    P['odd_in'] = [bf(ow[:, :, :2 * kd]), bf(ow[:, :, 2 * kd:2 * kd + vd]),
                   bf(jnp.pad(cad, ((0, 0), (0, 0), (0, LANES - rank)))), bf(ow[:, :, 2 * kd + vd + rank:c_proj])] \
        + [bf(ow[:, :, c_proj + j * dd:c_proj + (j + 1) * dd]) for j in range(4)]
    return P


def _trunk(x, pos0, mem_k, mem_v, shift, rwkv, conv, delta, gla, ret, W, P):
    b, t, d = x.shape
    n = b * t
    depth = W['norm_ffn1'].shape[0]
    L_delta = min(64, t)
    L_lin = min(64, t)
    flat = lambda z: z.reshape(n, z.shape[-1])
    unflat = lambda z: z.reshape(b, t, z.shape[-1])
    new = {k: [] for k in ('shift', 'rwkv', 'conv', 'delta', 'gla', 'ret')}
    x = flat(x)
    for l in range(depth):
        i = l // 2
        x = _ffn(x, W['norm_ffn1'][l], P['ffn1_wg'][l], P['ffn1_wu'][l], P['ffn1_wd'][l])
        if l % 2 == 0:
            pa, qkv, bab, z = _norm_proj(x, W['norm_mix'][l], [w[i] for w in P['even_in']])
            wa = dict(mu=W['rwkv_mu'][i], w0=W['rwkv_w0'][i], w2=W['rwkv_w2'][i], a0=W['rwkv_a0'][i],
                      a2=W['rwkv_a2'][i], g2=W['rwkv_g2'][i], kk=W['rwkv_kk'][i], ka=W['rwkv_ka'][i],
                      rk=W['rwkv_rk'][i], ln_w=W['rwkv_ln_w'][i], ln_b=W['rwkv_ln_b'][i])
            y_a, s1, s2 = _rwkv(unflat(pa), shift[i], rwkv[i], wa, L_delta)
            wb = dict(conv_w=W['delta_conv_w'][i], A_log=W['delta_A_log'][i], dt_bias=W['delta_dt_bias'][i],
                      norm_w=W['delta_norm_w'][i])
            y_b, s3, s4 = _delta(unflat(qkv), unflat(bab), unflat(z), conv[i], delta[i], wb, L_delta)
            new['shift'].append(s1)
            new['rwkv'].append(s2)
            new['conv'].append(s3)
            new['delta'].append(s4)
            split = H_A * N_A
            halves, w_out = (y_a, y_b), P['even_w_out'][i]
        else:
            cqk, cv, cad, cg, dq, dk, dv, dg = _norm_proj(x, W['norm_mix'][l], [w[i] for w in P['odd_in']])
            wc = dict(a2=W['gla_a2'][i], a_bias=W['gla_a_bias'][i], norm_w=W['gla_norm_w'][i])
            y_c, s5 = _gla(unflat(cqk), unflat(cv), unflat(cad), unflat(cg), gla[i], wc, L_lin)
            y_d, s6 = _retention(unflat(dq), unflat(dk), unflat(dv), unflat(dg), ret[i], pos0, L_lin)
            new['gla'].append(s5)
            new['ret'].append(s6)
            split = H_C * DV_C
            halves, w_out = (y_c, y_d), P['odd_w_out'][i]
        mk = mem_k[l].reshape(b, -1, d).astype(BF16)
        mv = mem_v[l].reshape(b, -1, d).astype(BF16)
        heads = mem_k[l].shape[2]
        x = flat(_mix_out_mem_attn(unflat(x), halves[0], halves[1], w_out[:split], w_out[split:], W['norm_mem'][l],
                                   P['mem_wq'][l], mk, mv, P['mem_wo'][l], heads))
        fw = W['final_norm'] if l == depth - 1 else None
        x = _ffn(x, W['norm_ffn2'][l], P['ffn2_wg'][l], P['ffn2_wu'][l], P['ffn2_wd'][l], fw)
    return unflat(x), new


def kernel(x_prompt, x_sample, mem_prompt, state_rwkv_shift, state_rwkv, state_delta_conv, state_delta, state_gla, state_ret, cache_mem_k, cache_mem_v, norm_ffn1, ffn1_wg, ffn1_wu, ffn1_wd, norm_mix, even_w_in, even_w_out, rwkv_mu, rwkv_w0, rwkv_w2, rwkv_a0, rwkv_a2, rwkv_g2, rwkv_kk, rwkv_ka, rwkv_rk, rwkv_ln_w, rwkv_ln_b, delta_conv_w, delta_A_log, delta_dt_bias, delta_norm_w, odd_w_in, odd_w_out, gla_a2, gla_a_bias, gla_norm_w, norm_mem, mem_norm_kv, mem_wq, mem_wk, mem_wv, mem_wo, norm_ffn2, ffn2_wg, ffn2_wu, ffn2_wd, final_norm):
    W = dict(norm_ffn1=norm_ffn1, ffn1_wg=ffn1_wg, ffn1_wu=ffn1_wu, ffn1_wd=ffn1_wd, norm_mix=norm_mix,
             even_w_in=even_w_in, even_w_out=even_w_out, rwkv_mu=rwkv_mu, rwkv_w0=rwkv_w0, rwkv_w2=rwkv_w2,
             rwkv_a0=rwkv_a0, rwkv_a2=rwkv_a2, rwkv_g2=rwkv_g2, rwkv_kk=rwkv_kk, rwkv_ka=rwkv_ka,
             rwkv_rk=rwkv_rk, rwkv_ln_w=rwkv_ln_w, rwkv_ln_b=rwkv_ln_b, delta_conv_w=delta_conv_w,
             delta_A_log=delta_A_log, delta_dt_bias=delta_dt_bias, delta_norm_w=delta_norm_w,
             odd_w_in=odd_w_in, odd_w_out=odd_w_out, gla_a2=gla_a2, gla_a_bias=gla_a_bias,
             gla_norm_w=gla_norm_w, norm_mem=norm_mem, mem_wq=mem_wq, mem_wk=mem_wk, mem_wv=mem_wv,
             mem_wo=mem_wo, norm_ffn2=norm_ffn2, ffn2_wg=ffn2_wg, ffn2_wu=ffn2_wu, ffn2_wd=ffn2_wd,
             final_norm=final_norm)
    P = _prep_weights(W)
    dt = x_prompt.dtype
    bp, _, d = x_prompt.shape
    depth = norm_ffn1.shape[0]
    n_even, n_odd = (depth + 1) // 2, depth // 2
    heads, hd = cache_mem_k.shape[3], cache_mem_k.shape[4]
    n_mem = mem_prompt.shape[1]
    pk, pv = [], []
    mem_flat = mem_prompt.reshape(bp * n_mem, d)
    for l in range(depth):
        mk, mv = _norm_proj(mem_flat, mem_norm_kv[l], [P['mem_wk'][l], P['mem_wv'][l]])
        pk.append(mk.reshape(bp, n_mem, heads, hd))
        pv.append(mv.reshape(bp, n_mem, heads, hd))
    zeros = lambda ref, cnt: [jnp.zeros((bp,) + ref.shape[2:], F32)] * cnt
    y_prompt, ps = _trunk(x_prompt, 0, pk, pv, zeros(state_rwkv_shift, n_even), zeros(state_rwkv, n_even),
                          zeros(state_delta_conv, n_even), zeros(state_delta, n_even), zeros(state_gla, n_odd),
                          zeros(state_ret, n_odd), W, P)
    y_sample, ss = _trunk(x_sample, PAST_LEN, [cache_mem_k[l] for l in range(depth)],
                          [cache_mem_v[l] for l in range(depth)],
                          [state_rwkv_shift[i] for i in range(n_even)], [state_rwkv[i] for i in range(n_even)],
                          [state_delta_conv[i] for i in range(n_even)], [state_delta[i] for i in range(n_even)],
                          [state_gla[i] for i in range(n_odd)], [state_ret[i] for i in range(n_odd)], W, P)
    st = lambda xs: jnp.stack(xs).astype(dt)
    order = ('shift', 'rwkv', 'conv', 'delta', 'gla', 'ret')
    return ((y_prompt, y_sample) + tuple(st(ps[k]) for k in order) + (st(pk), st(pv))
            + tuple(st(ss[k]) for k in order))
```

```python
import functools
import math

import jax
import jax.numpy as jnp
from jax import lax
from jax.experimental import pallas as pl
from jax.experimental.pallas import tpu as pltpu

F32 = jnp.float32
BF16 = jnp.bfloat16
EPS = 1e-6
RWKV_LN_EPS = 64e-5
GLA_NORMALIZER = 16.0
ROPE_BASE = 10000.0
PAST_LEN = 1024

LANES = 128
MXU_DIM = 256
VMEM_LIMIT = 56 * 1024 * 1024

H_A, N_A = 8, 64
H_B, DK_B = 4, 128
H_C, DK_C, DV_C = 4, 64, 128
H_D, DK_D = 4, 128
CONV_W = 4
CONV_CARRY = 8


def _cparams(sem):
    return pltpu.CompilerParams(dimension_semantics=sem, vmem_limit_bytes=VMEM_LIMIT)


def _mm(a, b):
    return jnp.dot(a.astype(BF16), b.astype(BF16), preferred_element_type=F32)


def _mm_nt(a, b):
    return lax.dot_general(a.astype(BF16), b.astype(BF16), (((1,), (1,)), ((), ())),
                           preferred_element_type=F32)


def _mm_tn(a, b):
    return lax.dot_general(a.astype(BF16), b.astype(BF16), (((0,), (0,)), ((), ())),
                           preferred_element_type=F32)


def _split3(x):
    hi = x.astype(BF16)
    r = x - hi.astype(F32)
    mid = r.astype(BF16)
    lo = (r - mid.astype(F32)).astype(BF16)
    return hi, mid, lo


def _mm_exact_lhs01(a01, x):
    hi, mid, lo = _split3(x)
    a = a01.astype(BF16)
    return (jnp.dot(a, hi, preferred_element_type=F32) + jnp.dot(a, mid, preferred_element_type=F32)
            + jnp.dot(a, lo, preferred_element_type=F32))


def _iota(shape, dim):
    return lax.broadcasted_iota(jnp.int32, shape, dim)


def _tri_incl(n):
    return (_iota((n, n), 1) <= _iota((n, n), 0)).astype(F32)


def _chunk_tri(tb, L):
    shift = L.bit_length() - 1
    assert 1 << shift == L
    rt, ct = _iota((tb, tb), 0), _iota((tb, tb), 1)
    same = jnp.right_shift(rt, shift) == jnp.right_shift(ct, shift)
    return jnp.logical_and(same, ct <= rt).astype(F32)


def _chunk_last(cum, L):
    tb, w = cum.shape
    return jnp.concatenate([jnp.broadcast_to(cum[c * L + L - 1:c * L + L, :], (L, w)) for c in range(tb // L)], axis=0)


def _sigmoid(x):
    return 0.5 * (jnp.tanh(0.5 * x) + 1.0)


def _silu(x):
    return x * _sigmoid(x)


def _softplus(x):
    return jnp.maximum(x, 0.0) + jnp.log1p(jnp.exp(-jnp.abs(x)))


def _rms(x, w):
    return x * lax.rsqrt(jnp.mean(x * x, axis=-1, keepdims=True) + EPS) * w


def _inv_unit_lower(n_mat, size):
    dim = n_mat.shape[0]
    eye = (_iota((dim, dim), 0) == _iota((dim, dim), 1)).astype(F32)
    t = eye + n_mat
    m = n_mat
    power = 2
    while power < size:
        m = _mm(m, m)
        t = t + _mm(t, m)
        power *= 2
    return t


def _row_tile(n, target):
    t = min(n, target)
    while n % t:
        t //= 2
    return t


def _resident(shape):
    nd = len(shape)
    return pl.BlockSpec(shape, lambda *_: (0,) * nd, pipeline_mode=pl.Buffered(1))


def _ffn_body(x_ref, nw_ref, wg_ref, wu_ref, wd_ref, fw_ref, o_ref, *, chunk, final):
    x = x_ref[...]
    h = _rms(x, nw_ref[...]).astype(BF16)
    d_ff = wg_ref.shape[1]
    n_chunks = d_ff // chunk

    def gate_up(c):
        sl = slice(c * chunk, (c + 1) * chunk)
        return (jnp.dot(h, wg_ref[:, sl], preferred_element_type=F32),
                jnp.dot(h, wu_ref[:, sl], preferred_element_type=F32))

    acc = None
    nxt = gate_up(0)
    for c in range(n_chunks):
        g, u = nxt
        if c + 1 < n_chunks:
            nxt = gate_up(c + 1)
        a = (_silu(g) * u).astype(BF16)
        d = jnp.dot(a, wd_ref[c * chunk:(c + 1) * chunk, :], preferred_element_type=F32)
        acc = d if acc is None else acc + d
    y = x + 0.5 * acc
    if final:
        y = _rms(y, fw_ref[...])
    o_ref[...] = y


def _ffn(x, nw, wg, wu, wd, fw=None):
    n, d = x.shape
    d_ff = wg.shape[1]
    tm = _row_tile(n, 512)
    final = fw is not None
    if fw is None:
        fw = nw
    return pl.pallas_call(
        functools.partial(_ffn_body, chunk=MXU_DIM, final=final),
        out_shape=jax.ShapeDtypeStruct((n, d), F32),
        grid=(n // tm,),
        in_specs=[pl.BlockSpec((tm, d), lambda i: (i, 0)), _resident((1, d)), _resident((d, d_ff)),
                  _resident((d, d_ff)), _resident((d_ff, d)), _resident((1, d))],
        out_specs=pl.BlockSpec((tm, d), lambda i: (i, 0)),
        compiler_params=_cparams(("parallel",)),
        name="ffn",
    )(x, nw.reshape(1, d), wg, wu, wd, fw.reshape(1, d))


def _norm_proj_body(*refs, n_w):
    x_ref, nw_ref = refs[0], refs[1]
    w_refs = refs[2:2 + n_w]
    o_refs = refs[2 + n_w:]
    h = _rms(x_ref[...], nw_ref[...]).astype(BF16)
    for w_ref, o_ref in zip(w_refs, o_refs):
        o_ref[...] = jnp.dot(h, w_ref[...], preferred_element_type=F32)


def _norm_proj(x, nw, ws):
    n, d = x.shape
    tm = _row_tile(n, 512)
    return pl.pallas_call(
        functools.partial(_norm_proj_body, n_w=len(ws)),
        out_shape=[jax.ShapeDtypeStruct((n, w.shape[1]), F32) for w in ws],
        grid=(n // tm,),
        in_specs=[pl.BlockSpec((tm, d), lambda i: (i, 0)), _resident((1, d))] + [_resident(w.shape) for w in ws],
        out_specs=[pl.BlockSpec((tm, w.shape[1]), lambda i: (i, 0)) for w in ws],
        compiler_params=_cparams(("parallel",)),
        name="norm_proj",
    )(x, nw.reshape(1, d), *ws)


def _mix_out_mem_attn_body(x_ref, ya_ref, yb_ref, wa_ref, wb_ref, nw_ref, wq_ref, k_ref, v_ref, wo_ref, o_ref, *,
                           heads):
    x = (x_ref[...] + jnp.dot(ya_ref[...], wa_ref[...], preferred_element_type=F32)
         + jnp.dot(yb_ref[...], wb_ref[...], preferred_element_type=F32))
    h = _rms(x, nw_ref[...]).astype(BF16)
    q = jnp.dot(h, wq_ref[...], preferred_element_type=F32).astype(BF16)
    d = q.shape[1]
    hd = d // heads

    def scores(i):
        sl = slice(i * hd, (i + 1) * hd)
        return _mm_nt(q[:, sl], k_ref[:, sl]) * hd ** -0.5

    outs = []
    nxt = scores(0)
    for i in range(heads):
        s = nxt
        if i + 1 < heads:
            nxt = scores(i + 1)
        e = jnp.exp(s - jnp.max(s, axis=-1, keepdims=True))
        pr = e / jnp.sum(e, axis=-1, keepdims=True)
        outs.append(_mm(pr, v_ref[:, i * hd:(i + 1) * hd]))
    o = jnp.concatenate(outs, axis=-1).astype(BF16)
    o_ref[...] = x + jnp.dot(o, wo_ref[...], preferred_element_type=F32)


def _mix_out_mem_attn(x, ya, yb, wa, wb, nw, wq, mk, mv, wo, heads):
    b, t, d = x.shape
    m = mk.shape[1]
    tm = _row_tile(t, 512)
    row = lambda w: pl.BlockSpec((None, tm, w), lambda i, j: (i, j, 0))
    mem = pl.BlockSpec((None, m, d), lambda i, j: (i, 0, 0))
    return pl.pallas_call(
        functools.partial(_mix_out_mem_attn_body, heads=heads),
        out_shape=jax.ShapeDtypeStruct((b, t, d), F32),
        grid=(b, t // tm),
        in_specs=[row(d), row(ya.shape[2]), row(yb.shape[2]), _resident(wa.shape), _resident(wb.shape),
                  _resident((1, d)), _resident((d, d)), mem, mem, _resident((d, d))],
        out_specs=row(d),
        compiler_params=_cparams(("parallel", "parallel")),
        name="mix_out_mem_attn",
    )(x, ya, yb, wa, wb, nw.reshape(1, d), wq, mk, mv, wo)


def _seg_sum(x, bd):
    g = bd.shape[0]
    hi = x.astype(BF16)
    lo = (x - hi.astype(F32)).astype(BF16)
    outs = []
    for j in range(x.shape[1] // g):
        sl = slice(j * g, (j + 1) * g)
        outs.append(jnp.dot(hi[:, sl], bd, preferred_element_type=F32) + jnp.dot(lo[:, sl], bd, preferred_element_type=F32))
    return jnp.concatenate(outs, axis=1)


def _rwkv_body(pa_ref, shift0_ref, s0_ref, mu_ref, w0_ref, w2_ref, a0_ref, a2_ref, g2_ref, kkw_ref, ka_ref,
               rk_ref, lnw_ref, lnb_ref, bd_ref, y_ref, shift_ref, s_ref, carry, y_scr, *, tb, L):
    t_idx = pl.program_id(1)
    a_dim = H_A * N_A
    n_pairs = a_dim // LANES

    @pl.when(t_idx == 0)
    def _():
        carry[...] = shift0_ref[...]
        s_ref[...] = s0_ref[...]

    pa = pa_ref[...]
    row = _iota(pa.shape, 0)
    prev = jnp.where(row == 0, carry[...], pltpu.roll(pa, 1, 0))
    carry[...] = pa[tb - 1:tb, :]
    shift_ref[...] = pa[tb - 1:tb, :]
    xa = pa + (prev - pa) * mu_ref[...]
    r = xa[:, 0:a_dim]
    k = xa[:, a_dim:2 * a_dim]
    v = xa[:, 2 * a_dim:3 * a_dim]
    xwa = xa[:, 3 * a_dim:3 * a_dim + LANES]
    xg = xa[:, 3 * a_dim + LANES:]
    wlog = -_softplus(-(w0_ref[...] + _mm(jnp.tanh(xwa), w2_ref[...]))) - 0.5
    logdec = -jnp.exp(wlog)
    a = _sigmoid(a0_ref[...] + _mm(xwa, a2_ref[...]))
    gate = _mm(_sigmoid(xg), g2_ref[...])
    bd = bd_ref[...]
    kq = k * kkw_ref[...]
    kk = kq * lax.rsqrt(_seg_sum(kq * kq, bd) + EPS)
    k2 = k * (1.0 + (a - 1.0) * ka_ref[...])
    av = -kk
    bv = kk * a

    lane = _iota((L, LANES), 1)
    m0 = lane < N_A
    r2 = _iota((2 * L, 2 * L), 0)
    c2 = _iota((2 * L, 2 * L), 1)
    bdm = jnp.logical_not(jnp.logical_xor(r2 >= L, c2 >= L))
    r_loc = jnp.where(r2 >= L, r2 - L, r2)
    c_loc = jnp.where(c2 >= L, c2 - L, c2)
    strict = jnp.logical_and(bdm, c_loc < r_loc)
    incl = jnp.logical_and(bdm, c_loc <= r_loc)
    strict_t = jnp.logical_and(bdm, c_loc > r_loc)
    r4 = _iota((2 * L, 4 * L), 0)
    c4 = _iota((2 * L, 4 * L), 1)
    c4 = jnp.where(c4 >= 2 * L, c4 - 2 * L, c4)
    incl_2 = jnp.logical_and(jnp.logical_not(jnp.logical_xor(r4 >= L, c4 >= L)),
                             jnp.where(c4 >= L, c4 - L, c4) <= jnp.where(r4 >= L, r4 - L, r4))
    bdl = jnp.logical_not(jnp.logical_xor(_iota((2 * L, LANES), 0) >= L, _iota((2 * L, LANES), 1) >= N_A))
    eye2 = (r2 == c2).astype(F32)

    def stack2(x):
        return jnp.concatenate([jnp.where(m0, x, 0.0), jnp.where(m0, 0.0, x)], axis=0).astype(BF16)

    def dup2(x):
        return jnp.concatenate([x, x], axis=0).astype(BF16)

    cum = _mm_exact_lhs01(_chunk_tri(tb, L), logdec)
    tot = _chunk_last(cum, L)
    e_neg = jnp.exp(-cum)
    e_out = jnp.exp(tot - cum)
    a_t = av * jnp.exp(cum - logdec)
    r_t = r * jnp.exp(cum)
    b_t = bv * e_neg
    k_t = k2 * e_neg
    b_o = bv * e_out
    k_o = k2 * e_out
    g_l = jnp.exp(tot)

    items = [(c, p) for c in range(tb // L) for p in range(n_pairs)]
    sl = {(c, p): (slice(c * L, (c + 1) * L), slice(p * LANES, (p + 1) * LANES)) for c, p in items}
    xa = {it: stack2(a_t[sl[it]]) for it in items}
    xr = {it: stack2(r_t[sl[it]]) for it in items}
    yb = {it: dup2(b_t[sl[it]]) for it in items}
    yk = {it: dup2(k_t[sl[it]]) for it in items}
    v2 = {it: jnp.where(bdl, jnp.concatenate([v[sl[it]], v[sl[it]]], axis=0), 0.0).astype(BF16) for it in items}
    wst = {it: jnp.concatenate([stack2(b_o[sl[it]]), stack2(k_o[sl[it]])], axis=0) for it in items}
    n_t = {it: jnp.where(strict_t, _mm_nt(yb[it], xa[it]), 0.0) for it in items}
    ak = {it: jnp.where(strict, _mm_nt(xa[it], yk[it]), 0.0).astype(BF16) for it in items}
    rbk = {it: jnp.where(incl_2, _mm_nt(xr[it], jnp.concatenate([yb[it], yk[it]], axis=0)), 0.0).astype(BF16)
           for it in items}
    akv = {it: _mm(ak[it], v2[it]) for it in items}
    t_t = {it: eye2 + n_t[it] for it in items}
    m_t = {it: _mm(n_t[it], n_t[it]) for it in items}
    power = 4
    while power < L:
        prod = {it: _mm(m_t[it], jnp.concatenate([m_t[it], t_t[it]], axis=1)) for it in items}
        m_t = {it: prod[it][:, :2 * L] for it in items}
        t_t = {it: t_t[it] + prod[it][:, 2 * L:] for it in items}
        power *= 2
    if L > 2:
        t_t = {it: t_t[it] + _mm(m_t[it], t_t[it]) for it in items}
    om_up = {it: _mm_tn(t_t[it], jnp.concatenate([xa[it], akv[it].astype(BF16)], axis=1)) for it in items}
    omega = {it: om_up[it][:, :LANES].astype(BF16) for it in items}
    state = [s_ref[p] for p in range(n_pairs)]
    s_prev, uv = {}, {}
    for c in range(tb // L):
        for p in range(n_pairs):
            it = (c, p)
            s_prev[it] = state[p].astype(BF16)
            u_st = _mm_nt(omega[it], s_prev[it]) + om_up[it][:, LANES:]
            uv[it] = jnp.concatenate([u_st.astype(BF16), v2[it]], axis=0)
        for p in range(n_pairs):
            it = (c, p)
            state[p] = state[p] * g_l[c * L:c * L + 1, sl[it][1]] + _mm_tn(uv[it], wst[it])
    for p in range(n_pairs):
        s_ref[p] = state[p]
    for it in items:
        y_st = _mm_nt(xr[it], s_prev[it]) + _mm(rbk[it], uv[it])
        y_scr[sl[it]] = y_st[0:L] + y_st[L:2 * L]

    y = y_scr[...]
    inv_n = 1.0 / N_A
    mu_y = _seg_sum(y, bd) * inv_n
    yc = y - mu_y
    var = _seg_sum(yc * yc, bd) * inv_n
    yn = yc * lax.rsqrt(var + RWKV_LN_EPS) * lnw_ref[...] + lnb_ref[...]
    bonus = _seg_sum(r * k2 * rk_ref[...], bd) * v
    y_ref[...] = ((yn + bonus) * gate).astype(y_ref.dtype)


def _pair_blockdiag(s):
    b, h, n, _ = s.shape
    s = s.reshape(b, h // 2, 2, n, n)
    z = jnp.zeros((b, h // 2, n, n), s.dtype)
    top = jnp.concatenate([s[:, :, 0], z], axis=-1)
    bot = jnp.concatenate([z, s[:, :, 1]], axis=-1)
    return jnp.concatenate([top, bot], axis=-2)


def _pair_unblock(s):
    b, p, n2, _ = s.shape
    n = n2 // 2
    return jnp.stack([s[:, :, :n, :n], s[:, :, n:, n:]], axis=2).reshape(b, 2 * p, n, n)


def _rwkv(pa, shift0, s0, wts, L):
    b, t, pw = pa.shape
    a_dim = H_A * N_A
    tb = min(t, 256)
    n_pairs = a_dim // LANES
    seg = _iota((MXU_DIM, MXU_DIM), 0) // N_A == _iota((MXU_DIM, MXU_DIM), 1) // N_A
    bd = seg.astype(BF16)
    z64 = jnp.zeros((N_A, a_dim), F32)
    w2p = jnp.concatenate([wts['w2'], z64], axis=0)
    a2p = jnp.concatenate([z64, wts['a2']], axis=0)
    vec = lambda x: x.reshape(1, -1).astype(F32)
    params = [vec(wts['mu']), vec(wts['w0']), w2p.astype(BF16), vec(wts['a0']), a2p.astype(BF16),
              wts['g2'].astype(BF16), vec(wts['kk']), vec(wts['ka']), vec(wts['rk']), vec(wts['ln_w']),
              vec(wts['ln_b']), bd]
    blk_t = pl.BlockSpec((None, tb, pw), lambda i, j: (i, j, 0))
    per_b = lambda shape: pl.BlockSpec((None,) + shape, lambda i, j: (i,) + (0,) * len(shape))
    y, shift, s_new = pl.pallas_call(
        functools.partial(_rwkv_body, tb=tb, L=L),
        out_shape=[jax.ShapeDtypeStruct((b, t, a_dim), BF16), jax.ShapeDtypeStruct((b, 1, pw), F32),
                   jax.ShapeDtypeStruct((b, n_pairs, LANES, LANES), F32)],
        grid=(b, t // tb),
        in_specs=[blk_t, per_b((1, pw)), per_b((n_pairs, LANES, LANES))] + [_resident(p.shape) for p in params],
        out_specs=[pl.BlockSpec((None, tb, a_dim), lambda i, j: (i, j, 0)), per_b((1, pw)),
                   per_b((n_pairs, LANES, LANES))],
        scratch_shapes=[pltpu.VMEM((1, pw), F32), pltpu.VMEM((tb, a_dim), F32)],
        compiler_params=_cparams(("parallel", "arbitrary")),
        name="rwkv7",
    )(pa, shift0, _pair_blockdiag(s0.astype(F32)), *params)
    return y, shift, _pair_unblock(s_new)


def _delta_body(qkv_ref, bab_ref, z_ref, conv0_ref, s0_ref, cw_ref, alog_ref, dtb_ref, nw_ref,
                y_ref, conv_ref, s_ref, *, tb, L):
    t_idx = pl.program_id(1)
    hd = DK_B
    inner = H_B * hd

    @pl.when(t_idx == 0)
    def _():
        conv_ref[...] = conv0_ref[...]
        s_ref[...] = s0_ref[...]

    x = qkv_ref[...]
    cat = jnp.concatenate([conv_ref[...], x], axis=0)
    conv_ref[...] = x[tb - CONV_CARRY:, :]
    cw = cw_ref[...]
    acc = cat * cw[0:1, :]
    for j in range(1, CONV_W):
        acc = pltpu.roll(acc, 1, 0) + cat * cw[j:j + 1, :]
    conv = _silu(acc[CONV_CARRY:, :])
    bab = bab_ref[...]
    g_all = -jnp.exp(alog_ref[...]) * _softplus(bab + dtb_ref[...])
    beta_all = _sigmoid(bab)

    tri = _tri_incl(L).astype(BF16)
    rw = _iota((L, L), 0)
    cl_ = _iota((L, L), 1)
    strict = cl_ < rw
    incl = cl_ <= rw
    eye = (rw == cl_).astype(F32)
    nw = nw_ref[...]
    z = z_ref[...]
    n_chunks = tb // L

    g_cum_all = _mm_exact_lhs01(_chunk_tri(tb, L), g_all)
    g_tot_all = _chunk_last(g_cum_all, L)

    qh, kh, vh, kdec, qg, bv_, bk_, gb, beta_b, e_last = {}, {}, {}, {}, {}, {}, {}, {}, {}, {}
    for h in range(H_B):
        qs = conv[:, h * hd:(h + 1) * hd]
        ks = conv[:, inner + h * hd:inner + (h + 1) * hd]
        vs = conv[:, 2 * inner + h * hd:2 * inner + (h + 1) * hd]
        qs = qs * lax.rsqrt(jnp.sum(qs * qs, axis=-1, keepdims=True) + EPS) * hd ** -0.5
        ks = ks * lax.rsqrt(jnp.sum(ks * ks, axis=-1, keepdims=True) + EPS)
        gb[h] = jnp.broadcast_to(g_all[:, h:h + 1], (tb, LANES))
        g_cum = jnp.broadcast_to(g_cum_all[:, h:h + 1], (tb, LANES))
        g_tot = jnp.broadcast_to(g_tot_all[:, h:h + 1], (tb, LANES))
        bb = jnp.broadcast_to(beta_all[:, H_B + h:H_B + h + 1], (tb, LANES))
        e_g = jnp.exp(g_cum)
        qh[h], kh[h], vh[h] = qs.astype(BF16), ks.astype(BF16), vs
        kdec[h] = (ks * jnp.exp(g_tot - g_cum)).astype(BF16)
        qg[h] = qs * e_g
        bv_[h] = bb * vs
        bk_[h] = bb * e_g * ks
        beta_b[h] = bb
        e_last[h] = jnp.exp(g_tot)

    items = [(c, h) for c in range(n_chunks) for h in range(H_B)]
    rs = {c: slice(c * L, (c + 1) * L) for c in range(n_chunks)}
    diff = {}
    for c, h in items:
        hi, mid, lo = _split3(jnp.where(strict, gb[h][rs[c], :L], 0.0))
        diff[(c, h)] = (jnp.dot(tri, hi, preferred_element_type=F32) + jnp.dot(tri, mid, preferred_element_type=F32)
                        + jnp.dot(tri, lo, preferred_element_type=F32))
    kk_ = {(c, h): _mm_nt(kh[h][rs[c]], kh[h][rs[c]]) for c, h in items}
    qk_ = {(c, h): _mm_nt(qh[h][rs[c]], kh[h][rs[c]]) for c, h in items}
    n_mat, qk_m = {}, {}
    for c, h in items:
        e_diff = jnp.exp(jnp.where(incl, diff[(c, h)], 0.0))
        n_mat[(c, h)] = -(beta_b[h][rs[c], :L] * jnp.where(strict, kk_[(c, h)] * e_diff, 0.0))
        qk_m[(c, h)] = jnp.where(incl, qk_[(c, h)] * e_diff, 0.0).astype(BF16)
    t_inv = {it: eye + n_mat[it] for it in items}
    m_pow = n_mat
    power = 2
    while power < L:
        m_pow = {it: _mm(m_pow[it], m_pow[it]) for it in items}
        t_inv = {it: t_inv[it] + _mm(t_inv[it], m_pow[it]) for it in items}
        power *= 2
    sol = {(c, h): _mm(t_inv[(c, h)], jnp.concatenate([bv_[h][rs[c]], bk_[h][rs[c]]], axis=1)) for c, h in items}
    lhs = {(c, h): jnp.concatenate([sol[(c, h)][:, hd:], qg[h][rs[c]]], axis=0).astype(BF16) for c, h in items}
    state = [s_ref[h] for h in range(H_B)]
    u_, o_inter = {}, {}
    for c in range(n_chunks):
        for h in range(H_B):
            ws = _mm(lhs[(c, h)], state[h])
            u_[(c, h)] = (sol[(c, h)][:, :hd] - ws[:L]).astype(BF16)
            o_inter[(c, h)] = ws[L:]
        for h in range(H_B):
            state[h] = e_last[h][c * L:c * L + 1, :] * state[h] + _mm_tn(kdec[h][rs[c]], u_[(c, h)])
    for h in range(H_B):
        s_ref[h] = state[h]
    for c, h in items:
        o = o_inter[(c, h)] + _mm(qk_m[(c, h)], u_[(c, h)])
        on = o * lax.rsqrt(jnp.mean(o * o, axis=-1, keepdims=True) + EPS) * nw
        y_ref[rs[c], h * hd:(h + 1) * hd] = (on * _silu(z[rs[c], h * hd:(h + 1) * hd])).astype(y_ref.dtype)


def _delta(qkv, bab, z, conv0, s0, wts, L):
    b, t, cdim = qkv.shape
    inner = H_B * DK_B
    tb = min(t, 256)
    pad = lambda x: jnp.pad(x.reshape(1, -1).astype(F32), ((0, 0), (0, LANES - x.size)))
    conv0p = jnp.pad(conv0.astype(F32), ((0, 0), (CONV_CARRY - (CONV_W - 1), 0), (0, 0)))
    params = [wts['conv_w'].astype(F32), pad(wts['A_log']), pad(wts['dt_bias']), wts['norm_w'].reshape(1, -1).astype(F32)]
    blk = lambda w: pl.BlockSpec((None, tb, w), lambda i, j: (i, j, 0))
    per_b = lambda shape: pl.BlockSpec((None,) + shape, lambda i, j: (i,) + (0,) * len(shape))
    y, conv_new, s_new = pl.pallas_call(
        functools.partial(_delta_body, tb=tb, L=L),
        out_shape=[jax.ShapeDtypeStruct((b, t, inner), BF16), jax.ShapeDtypeStruct((b, CONV_CARRY, cdim), F32),
                   jax.ShapeDtypeStruct((b, H_B, DK_B, DK_B), F32)],
        grid=(b, t // tb),
        in_specs=[blk(cdim), blk(LANES), blk(inner), per_b((CONV_CARRY, cdim)), per_b((H_B, DK_B, DK_B))]
        + [_resident(p.shape) for p in params],
        out_specs=[blk(inner), per_b((CONV_CARRY, cdim)), per_b((H_B, DK_B, DK_B))],
        compiler_params=_cparams(("parallel", "arbitrary")),
        name="gated_delta",
    )(qkv, bab, z, conv0p, s0.astype(F32), *params)
    return y, conv_new[:, CONV_CARRY - (CONV_W - 1):], s_new


def _gla_body(qk_ref, v_ref, ad_ref, g_ref, s0_ref, a2_ref, ab_ref, nw_ref, y_ref, s_ref, *, tb, L):
    t_idx = pl.program_id(1)
    kdim = H_C * DK_C
    n_pairs = kdim // LANES

    @pl.when(t_idx == 0)
    def _():
        s_ref[...] = s0_ref[...]

    qk = qk_ref[...]
    log_a = -_softplus(-(_mm(ad_ref[...], a2_ref[...]) + ab_ref[...])) * (1.0 / GLA_NORMALIZER)
    vv = v_ref[...]
    gg = g_ref[...]
    nw = nw_ref[...]
    incl = _iota((L, L), 1) <= _iota((L, L), 0)
    m0 = _iota((tb, LANES), 1) < DK_C
    n_chunks = tb // L

    cum = _mm_exact_lhs01(_chunk_tri(tb, L), log_a)
    tot = _chunk_last(cum, L)
    q_in = qk[:, :kdim] * DK_C ** -0.5 * jnp.exp(cum)
    k_in = (qk[:, kdim:] * jnp.exp(-cum)).astype(BF16)
    k_out = qk[:, kdim:] * jnp.exp(tot - cum)
    dec = jnp.exp(tot)
    vb16 = vv.astype(BF16)

    rs = {c: slice(c * L, (c + 1) * L) for c in range(n_chunks)}
    items = [(c, h) for c in range(n_chunks) for h in range(H_C)]
    qm, km = {}, {}
    for h in range(H_C):
        ls = slice((h // 2) * LANES, (h // 2 + 1) * LANES)
        mask = m0 if h % 2 == 0 else jnp.logical_not(m0)
        qm[h] = jnp.where(mask, q_in[:, ls], 0.0).astype(BF16)
        km[h] = jnp.where(mask, k_out[:, ls], 0.0).astype(BF16)
    scores = {(c, h): jnp.where(incl, _mm_nt(qm[h][rs[c]], k_in[rs[c], (h // 2) * LANES:(h // 2 + 1) * LANES]), 0.0)
              .astype(BF16) for c, h in items}
    upd = {(c, h): _mm_tn(vb16[rs[c], h * DV_C:(h + 1) * DV_C], km[h][rs[c]]) for c, h in items}
    state = [s_ref[p] for p in range(n_pairs)]
    s_prev = {}
    for c in range(n_chunks):
        for p in range(n_pairs):
            s_prev[(c, p)] = state[p].astype(BF16)
            state[p] = (state[p] * dec[c * L:c * L + 1, p * LANES:(p + 1) * LANES]
                        + upd[(c, 2 * p)] + upd[(c, 2 * p + 1)])
    for p in range(n_pairs):
        s_ref[p] = state[p]
    for c, h in items:
        hs = slice(h * DV_C, (h + 1) * DV_C)
        o = _mm_nt(qm[h][rs[c]], s_prev[(c, h // 2)]) + _mm(scores[(c, h)], vb16[rs[c], hs])
        on = o * lax.rsqrt(jnp.mean(o * o, axis=-1, keepdims=True) + EPS) * nw
        y_ref[rs[c], hs] = (on * _silu(gg[rs[c], hs])).astype(y_ref.dtype)


def _gla(cqk, cv, cad, cg, s0, wts, L):
    b, t, _ = cqk.shape
    kdim = H_C * DK_C
    vdim = H_C * DV_C
    n_pairs = kdim // LANES
    tb = min(t, 256)
    rank = wts['a2'].shape[0]
    a2p = jnp.pad(wts['a2'].astype(F32), ((0, LANES - rank), (0, 0))).astype(BF16)
    params = [a2p, wts['a_bias'].reshape(1, -1).astype(F32), wts['norm_w'].reshape(1, -1).astype(F32)]
    st0 = jnp.swapaxes(s0.astype(F32), -1, -2).reshape(b, n_pairs, 2, DV_C, DK_C)
    st0 = jnp.concatenate([st0[:, :, 0], st0[:, :, 1]], axis=-1)
    blk = lambda w: pl.BlockSpec((None, tb, w), lambda i, j: (i, j, 0))
    per_b = lambda shape: pl.BlockSpec((None,) + shape, lambda i, j: (i,) + (0,) * len(shape))
    y, st = pl.pallas_call(
        functools.partial(_gla_body, tb=tb, L=L),
        out_shape=[jax.ShapeDtypeStruct((b, t, vdim), BF16), jax.ShapeDtypeStruct((b, n_pairs, DV_C, LANES), F32)],
        grid=(b, t // tb),
        in_specs=[blk(2 * kdim), blk(vdim), blk(LANES), blk(vdim), per_b((n_pairs, DV_C, LANES))]
        + [_resident(p.shape) for p in params],
        out_specs=[blk(vdim), per_b((n_pairs, DV_C, LANES))],
        compiler_params=_cparams(("parallel", "arbitrary")),
        name="gla",
    )(cqk, cv, cad, cg, st0, *params)
    st = jnp.stack([st[..., :DK_C], st[..., DK_C:]], axis=2).reshape(b, H_C, DV_C, DK_C)
    return y, jnp.swapaxes(st, -1, -2)


def _ret_body(q_ref, k_ref, v_ref, g_ref, cos_ref, sin_ref, s0_ref, y_ref, s_ref, *, tb, L):
    t_idx = pl.program_id(1)
    hd = DK_D

    @pl.when(t_idx == 0)
    def _():
        s_ref[...] = s0_ref[...]

    cos = cos_ref[...]
    sin = sin_ref[...]
    qq, kk, vv, gg = q_ref[...], k_ref[...], v_ref[...], g_ref[...]
    rw = _iota((L, L), 0)
    cl_ = _iota((L, L), 1)
    incl = cl_ <= rw
    dist = jnp.where(incl, rw - cl_, 0).astype(F32)
    pos = _iota((L, LANES), 0).astype(F32)

    n_chunks = tb // L
    rs = {c: slice(c * L, (c + 1) * L) for c in range(n_chunks)}
    items = [(c, h) for c in range(n_chunks) for h in range(H_D)]
    lg = [math.log(1.0 - 2.0 ** (-5.0 - h)) for h in range(H_D)]
    vb16 = vv.astype(BF16)
    qh, kh, q_in, k_out, dmat = {}, {}, {}, {}, {}
    for h in range(H_D):
        hs = slice(h * hd, (h + 1) * hd)
        q_rot = qq[:, hs] * cos + pltpu.roll(qq[:, hs], hd // 2, 1) * sin
        k_rot = (kk[:, hs] * cos + pltpu.roll(kk[:, hs], hd // 2, 1) * sin) * hd ** -0.5
        dmat[h] = jnp.where(incl, jnp.exp(dist * lg[h]), 0.0)
        e_in = jnp.exp((pos + 1.0) * lg[h])
        e_out = jnp.exp((L - 1.0 - pos) * lg[h])
        qh[h], kh[h] = q_rot.astype(BF16), k_rot.astype(BF16)
        for c in range(n_chunks):
            q_in[(c, h)] = (q_rot[rs[c]] * e_in).astype(BF16)
            k_out[(c, h)] = (k_rot[rs[c]] * e_out).astype(BF16)
    scores = {(c, h): (_mm_nt(qh[h][rs[c]], kh[h][rs[c]]) * dmat[h]).astype(BF16) for c, h in items}
    upd = {(c, h): _mm_tn(k_out[(c, h)], vb16[rs[c], h * hd:(h + 1) * hd]) for c, h in items}
    state = [s_ref[h] for h in range(H_D)]
    s_prev = {}
    for c in range(n_chunks):
        for h in range(H_D):
            s_prev[(c, h)] = state[h].astype(BF16)
            state[h] = state[h] * math.exp(lg[h] * L) + upd[(c, h)]
    for h in range(H_D):
        s_ref[h] = state[h]
    for c, h in items:
        hs = slice(h * hd, (h + 1) * hd)
        o = _mm(q_in[(c, h)], s_prev[(c, h)]) + _mm(scores[(c, h)], vb16[rs[c], hs])
        mu = jnp.mean(o, axis=-1, keepdims=True)
        oc = o - mu
        on = oc * lax.rsqrt(jnp.mean(oc * oc, axis=-1, keepdims=True) + EPS)
        y_ref[rs[c], hs] = (on * _silu(gg[rs[c], hs])).astype(y_ref.dtype)


def _retention(dq, dk, dv, dg, s0, pos0, L):
    b, t, dim = dq.shape
    tb = min(t, 256)
    half = DK_D // 2
    inv = ROPE_BASE ** (-jnp.arange(0, DK_D, 2, dtype=F32) / DK_D)
    ang = (jnp.arange(t) + pos0).astype(F32)[:, None] * inv[None, :]
    cos = jnp.concatenate([jnp.cos(ang), jnp.cos(ang)], axis=-1)
    sin = jnp.concatenate([-jnp.sin(ang), jnp.sin(ang)], axis=-1)
    del half
    blk = pl.BlockSpec((None, tb, dim), lambda i, j: (i, j, 0))
    tab = pl.BlockSpec((tb, DK_D), lambda i, j: (j, 0))
    st = pl.BlockSpec((None, H_D, DK_D, DK_D), lambda i, j: (i, 0, 0, 0))
    return pl.pallas_call(
        functools.partial(_ret_body, tb=tb, L=L),
        out_shape=[jax.ShapeDtypeStruct((b, t, dim), BF16), jax.ShapeDtypeStruct((b, H_D, DK_D, DK_D), F32)],
        grid=(b, t // tb),
        in_specs=[blk, blk, blk, blk, tab, tab, st],
        out_specs=[blk, st],
        compiler_params=_cparams(("parallel", "arbitrary")),
        name="retention",
    )(dq, dk, dv, dg, cos, sin, s0.astype(F32))


def _prep_weights(W):
    bf = lambda x: x.astype(BF16)
    a_proj = 3 * H_A * N_A + 64 + 64 + 128
    b_conv = 3 * H_B * DK_B
    P = {}
    for name in ('ffn1_wg', 'ffn1_wu', 'ffn1_wd', 'ffn2_wg', 'ffn2_wu', 'ffn2_wd', 'mem_wq', 'mem_wo', 'mem_wk',
                 'mem_wv', 'even_w_out', 'odd_w_out'):
        P[name] = bf(W[name])
    ew = W['even_w_in']
    o = a_proj
    bab = ew[:, :, o + b_conv:o + b_conv + 2 * H_B]
    P['even_in'] = [bf(ew[:, :, :a_proj]), bf(ew[:, :, o:o + b_conv]),
                    bf(jnp.pad(bab, ((0, 0), (0, 0), (0, LANES - 2 * H_B)))), bf(ew[:, :, o + b_conv + 2 * H_B:])]
    ow = W['odd_w_in']
    kd, vd, rank = H_C * DK_C, H_C * DV_C, W['gla_a2'].shape[1]
    c_proj = 2 * kd + 2 * vd + rank
    dd = H_D * DK_D
    cad = ow[:, :, 2 * kd + vd:2 * kd + vd + rank]
    P['odd_in'] = [bf(ow[:, :, :2 * kd]), bf(ow[:, :, 2 * kd:2 * kd + vd]),
                   bf(jnp.pad(cad, ((0, 0), (0, 0), (0, LANES - rank)))), bf(ow[:, :, 2 * kd + vd + rank:c_proj])] \
        + [bf(ow[:, :, c_proj + j * dd:c_proj + (j + 1) * dd]) for j in range(4)]
    return P


def _trunk(x, pos0, mem_k, mem_v, shift, rwkv, conv, delta, gla, ret, W, P):
    b, t, d = x.shape
    n = b * t
    depth = W['norm_ffn1'].shape[0]
    L_delta = min(64, t)
    L_lin = min(64, t)
    flat = lambda z: z.reshape(n, z.shape[-1])
    unflat = lambda z: z.reshape(b, t, z.shape[-1])
    new = {k: [] for k in ('shift', 'rwkv', 'conv', 'delta', 'gla', 'ret')}
    x = flat(x)
    for l in range(depth):
        i = l // 2
        x = _ffn(x, W['norm_ffn1'][l], P['ffn1_wg'][l], P['ffn1_wu'][l], P['ffn1_wd'][l])
        if l % 2 == 0:
            pa, qkv, bab, z = _norm_proj(x, W['norm_mix'][l], [w[i] for w in P['even_in']])
            wa = dict(mu=W['rwkv_mu'][i], w0=W['rwkv_w0'][i], w2=W['rwkv_w2'][i], a0=W['rwkv_a0'][i],
                      a2=W['rwkv_a2'][i], g2=W['rwkv_g2'][i], kk=W['rwkv_kk'][i], ka=W['rwkv_ka'][i],
                      rk=W['rwkv_rk'][i], ln_w=W['rwkv_ln_w'][i], ln_b=W['rwkv_ln_b'][i])
            y_a, s1, s2 = _rwkv(unflat(pa), shift[i], rwkv[i], wa, L_delta)
            wb = dict(conv_w=W['delta_conv_w'][i], A_log=W['delta_A_log'][i], dt_bias=W['delta_dt_bias'][i],
                      norm_w=W['delta_norm_w'][i])
            y_b, s3, s4 = _delta(unflat(qkv), unflat(bab), unflat(z), conv[i], delta[i], wb, L_delta)
            new['shift'].append(s1)
            new['rwkv'].append(s2)
            new['conv'].append(s3)
            new['delta'].append(s4)
            split = H_A * N_A
            halves, w_out = (y_a, y_b), P['even_w_out'][i]
        else:
            cqk, cv, cad, cg, dq, dk, dv, dg = _norm_proj(x, W['norm_mix'][l], [w[i] for w in P['odd_in']])
            wc = dict(a2=W['gla_a2'][i], a_bias=W['gla_a_bias'][i], norm_w=W['gla_norm_w'][i])
            y_c, s5 = _gla(unflat(cqk), unflat(cv), unflat(cad), unflat(cg), gla[i], wc, L_lin)
            y_d, s6 = _retention(unflat(dq), unflat(dk), unflat(dv), unflat(dg), ret[i], pos0, L_lin)
            new['gla'].append(s5)
            new['ret'].append(s6)
            split = H_C * DV_C
            halves, w_out = (y_c, y_d), P['odd_w_out'][i]
        mk = mem_k[l].reshape(b, -1, d).astype(BF16)
        mv = mem_v[l].reshape(b, -1, d).astype(BF16)
        heads = mem_k[l].shape[2]
        x = flat(_mix_out_mem_attn(unflat(x), halves[0], halves[1], w_out[:split], w_out[split:], W['norm_mem'][l],
                                   P['mem_wq'][l], mk, mv, P['mem_wo'][l], heads))
        fw = W['final_norm'] if l == depth - 1 else None
        x = _ffn(x, W['norm_ffn2'][l], P['ffn2_wg'][l], P['ffn2_wu'][l], P['ffn2_wd'][l], fw)
    return unflat(x), new


def kernel(x_prompt, x_sample, mem_prompt, state_rwkv_shift, state_rwkv, state_delta_conv, state_delta, state_gla, state_ret, cache_mem_k, cache_mem_v, norm_ffn1, ffn1_wg, ffn1_wu, ffn1_wd, norm_mix, even_w_in, even_w_out, rwkv_mu, rwkv_w0, rwkv_w2, rwkv_a0, rwkv_a2, rwkv_g2, rwkv_kk, rwkv_ka, rwkv_rk, rwkv_ln_w, rwkv_ln_b, delta_conv_w, delta_A_log, delta_dt_bias, delta_norm_w, odd_w_in, odd_w_out, gla_a2, gla_a_bias, gla_norm_w, norm_mem, mem_norm_kv, mem_wq, mem_wk, mem_wv, mem_wo, norm_ffn2, ffn2_wg, ffn2_wu, ffn2_wd, final_norm):
    W = dict(norm_ffn1=norm_ffn1, ffn1_wg=ffn1_wg, ffn1_wu=ffn1_wu, ffn1_wd=ffn1_wd, norm_mix=norm_mix,
             even_w_in=even_w_in, even_w_out=even_w_out, rwkv_mu=rwkv_mu, rwkv_w0=rwkv_w0, rwkv_w2=rwkv_w2,
             rwkv_a0=rwkv_a0, rwkv_a2=rwkv_a2, rwkv_g2=rwkv_g2, rwkv_kk=rwkv_kk, rwkv_ka=rwkv_ka,
             rwkv_rk=rwkv_rk, rwkv_ln_w=rwkv_ln_w, rwkv_ln_b=rwkv_ln_b, delta_conv_w=delta_conv_w,
             delta_A_log=delta_A_log, delta_dt_bias=delta_dt_bias, delta_norm_w=delta_norm_w,
             odd_w_in=odd_w_in, odd_w_out=odd_w_out, gla_a2=gla_a2, gla_a_bias=gla_a_bias,
             gla_norm_w=gla_norm_w, norm_mem=norm_mem, mem_wq=mem_wq, mem_wk=mem_wk, mem_wv=mem_wv,
             mem_wo=mem_wo, norm_ffn2=norm_ffn2, ffn2_wg=ffn2_wg, ffn2_wu=ffn2_wu, ffn2_wd=ffn2_wd,
             final_norm=final_norm)
    P = _prep_weights(W)
    dt = x_prompt.dtype
    bp, _, d = x_prompt.shape
    depth = norm_ffn1.shape[0]
    n_even, n_odd = (depth + 1) // 2, depth // 2
    heads, hd = cache_mem_k.shape[3], cache_mem_k.shape[4]
    n_mem = mem_prompt.shape[1]
    pk, pv = [], []
    mem_flat = mem_prompt.reshape(bp * n_mem, d)
    for l in range(depth):
        mk, mv = _norm_proj(mem_flat, mem_norm_kv[l], [P['mem_wk'][l], P['mem_wv'][l]])
        pk.append(mk.reshape(bp, n_mem, heads, hd))
        pv.append(mv.reshape(bp, n_mem, heads, hd))
    zeros = lambda ref, cnt: [jnp.zeros((bp,) + ref.shape[2:], F32)] * cnt
    y_prompt, ps = _trunk(x_prompt, 0, pk, pv, zeros(state_rwkv_shift, n_even), zeros(state_rwkv, n_even),
                          zeros(state_delta_conv, n_even), zeros(state_delta, n_even), zeros(state_gla, n_odd),
                          zeros(state_ret, n_odd), W, P)
    y_sample, ss = _trunk(x_sample, PAST_LEN, [cache_mem_k[l] for l in range(depth)],
                          [cache_mem_v[l] for l in range(depth)],
                          [state_rwkv_shift[i] for i in range(n_even)], [state_rwkv[i] for i in range(n_even)],
                          [state_delta_conv[i] for i in range(n_even)], [state_delta[i] for i in range(n_even)],
                          [state_gla[i] for i in range(n_odd)], [state_ret[i] for i in range(n_odd)], W, P)
    st = lambda xs: jnp.stack(xs).astype(dt)
    order = ('shift', 'rwkv', 'conv', 'delta', 'gla', 'ret')
    return ((y_prompt, y_sample) + tuple(st(ps[k]) for k in order) + (st(pk), st(pv))
            + tuple(st(ss[k]) for k in order))
```

```python
import functools
import math

import jax
import jax.numpy as jnp
from jax import lax
from jax.experimental import pallas as pl
from jax.experimental.pallas import tpu as pltpu

F32 = jnp.float32
BF16 = jnp.bfloat16
EPS = 1e-6
RWKV_LN_EPS = 64e-5
GLA_NORMALIZER = 16.0
ROPE_BASE = 10000.0
PAST_LEN = 1024

LANES = 128
MXU_DIM = 256
VMEM_LIMIT = 56 * 1024 * 1024

H_A, N_A = 8, 64
H_B, DK_B = 4, 128
H_C, DK_C, DV_C = 4, 64, 128
H_D, DK_D = 4, 128
CONV_W = 4
CONV_CARRY = 8


def _cparams(sem):
    return pltpu.CompilerParams(dimension_semantics=sem, vmem_limit_bytes=VMEM_LIMIT)


def _mm(a, b):
    return jnp.dot(a.astype(BF16), b.astype(BF16), preferred_element_type=F32)


def _mm_nt(a, b):
    return lax.dot_general(a.astype(BF16), b.astype(BF16), (((1,), (1,)), ((), ())),
                           preferred_element_type=F32)


def _mm_tn(a, b):
    return lax.dot_general(a.astype(BF16), b.astype(BF16), (((0,), (0,)), ((), ())),
                           preferred_element_type=F32)


def _split3(x):
    hi = x.astype(BF16)
    r = x - hi.astype(F32)
    mid = r.astype(BF16)
    lo = (r - mid.astype(F32)).astype(BF16)
    return hi, mid, lo


def _mm_exact_lhs01(a01, x):
    hi, mid, lo = _split3(x)
    a = a01.astype(BF16)
    return (jnp.dot(a, hi, preferred_element_type=F32) + jnp.dot(a, mid, preferred_element_type=F32)
            + jnp.dot(a, lo, preferred_element_type=F32))


def _iota(shape, dim):
    return lax.broadcasted_iota(jnp.int32, shape, dim)


def _tri_incl(n):
    return (_iota((n, n), 1) <= _iota((n, n), 0)).astype(F32)


def _chunk_tri(tb, L):
    shift = L.bit_length() - 1
    assert 1 << shift == L
    rt, ct = _iota((tb, tb), 0), _iota((tb, tb), 1)
    same = jnp.right_shift(rt, shift) == jnp.right_shift(ct, shift)
    return jnp.logical_and(same, ct <= rt).astype(F32)


def _chunk_last(cum, L):
    tb, w = cum.shape
    return jnp.concatenate([jnp.broadcast_to(cum[c * L + L - 1:c * L + L, :], (L, w)) for c in range(tb // L)], axis=0)


def _sigmoid(x):
    return 0.5 * (jnp.tanh(0.5 * x) + 1.0)


def _silu(x):
    return x * _sigmoid(x)


def _softplus(x):
    return jnp.maximum(x, 0.0) + jnp.log1p(jnp.exp(-jnp.abs(x)))


def _rms(x, w):
    return x * lax.rsqrt(jnp.mean(x * x, axis=-1, keepdims=True) + EPS) * w


def _inv_unit_lower(n_mat, size):
    dim = n_mat.shape[0]
    eye = (_iota((dim, dim), 0) == _iota((dim, dim), 1)).astype(F32)
    t = eye + n_mat
    m = n_mat
    power = 2
    while power < size:
        m = _mm(m, m)
        t = t + _mm(t, m)
        power *= 2
    return t


def _row_tile(n, target):
    t = min(n, target)
    while n % t:
        t //= 2
    return t


def _resident(shape):
    nd = len(shape)
    return pl.BlockSpec(shape, lambda *_: (0,) * nd, pipeline_mode=pl.Buffered(1))


def _ffn_body(x_ref, nw_ref, wg_ref, wu_ref, wd_ref, fw_ref, o_ref, *, chunk, final):
    x = x_ref[...]
    h = _rms(x, nw_ref[...]).astype(BF16)
    d_ff = wg_ref.shape[1]
    n_chunks = d_ff // chunk

    def gate_up(c):
        sl = slice(c * chunk, (c + 1) * chunk)
        return (jnp.dot(h, wg_ref[:, sl], preferred_element_type=F32),
                jnp.dot(h, wu_ref[:, sl], preferred_element_type=F32))

    acc = None
    nxt = gate_up(0)
    for c in range(n_chunks):
        g, u = nxt
        if c + 1 < n_chunks:
            nxt = gate_up(c + 1)
        a = (_silu(g) * u).astype(BF16)
        d = jnp.dot(a, wd_ref[c * chunk:(c + 1) * chunk, :], preferred_element_type=F32)
        acc = d if acc is None else acc + d
    y = x + 0.5 * acc
    if final:
        y = _rms(y, fw_ref[...])
    o_ref[...] = y


def _ffn(x, nw, wg, wu, wd, fw=None):
    n, d = x.shape
    d_ff = wg.shape[1]
    tm = _row_tile(n, 512)
    final = fw is not None
    if fw is None:
        fw = nw
    return pl.pallas_call(
        functools.partial(_ffn_body, chunk=MXU_DIM, final=final),
        out_shape=jax.ShapeDtypeStruct((n, d), F32),
        grid=(n // tm,),
        in_specs=[pl.BlockSpec((tm, d), lambda i: (i, 0)), _resident((1, d)), _resident((d, d_ff)),
                  _resident((d, d_ff)), _resident((d_ff, d)), _resident((1, d))],
        out_specs=pl.BlockSpec((tm, d), lambda i: (i, 0)),
        compiler_params=_cparams(("parallel",)),
        name="ffn",
    )(x, nw.reshape(1, d), wg, wu, wd, fw.reshape(1, d))


def _norm_proj_body(*refs, n_w):
    x_ref, nw_ref = refs[0], refs[1]
    w_refs = refs[2:2 + n_w]
    o_refs = refs[2 + n_w:]
    h = _rms(x_ref[...], nw_ref[...]).astype(BF16)
    for w_ref, o_ref in zip(w_refs, o_refs):
        o_ref[...] = jnp.dot(h, w_ref[...], preferred_element_type=F32).astype(o_ref.dtype)


def _norm_proj(x, nw, ws, bf16_out=()):
    n, d = x.shape
    tm = _row_tile(n, 512)
    return pl.pallas_call(
        functools.partial(_norm_proj_body, n_w=len(ws)),
        out_shape=[jax.ShapeDtypeStruct((n, w.shape[1]), BF16 if i in bf16_out else F32) for i, w in enumerate(ws)],
        grid=(n // tm,),
        in_specs=[pl.BlockSpec((tm, d), lambda i: (i, 0)), _resident((1, d))] + [_resident(w.shape) for w in ws],
        out_specs=[pl.BlockSpec((tm, w.shape[1]), lambda i: (i, 0)) for w in ws],
        compiler_params=_cparams(("parallel",)),
        name="norm_proj",
    )(x, nw.reshape(1, d), *ws)


def _mix_out_mem_attn_body(x_ref, ya_ref, yb_ref, wa_ref, wb_ref, nw_ref, wq_ref, k_ref, v_ref, wo_ref, o_ref, *,
                           heads):
    x = (x_ref[...] + jnp.dot(ya_ref[...], wa_ref[...], preferred_element_type=F32)
         + jnp.dot(yb_ref[...], wb_ref[...], preferred_element_type=F32))
    h = _rms(x, nw_ref[...]).astype(BF16)
    q = jnp.dot(h, wq_ref[...], preferred_element_type=F32).astype(BF16)
    d = q.shape[1]
    hd = d // heads

    def scores(i):
        sl = slice(i * hd, (i + 1) * hd)
        return _mm_nt(q[:, sl], k_ref[:, sl]) * hd ** -0.5

    outs = []
    nxt = scores(0)
    for i in range(heads):
        s = nxt
        if i + 1 < heads:
            nxt = scores(i + 1)
        e = jnp.exp(s - jnp.max(s, axis=-1, keepdims=True))
        pr = e / jnp.sum(e, axis=-1, keepdims=True)
        outs.append(_mm(pr, v_ref[:, i * hd:(i + 1) * hd]))
    o = jnp.concatenate(outs, axis=-1).astype(BF16)
    o_ref[...] = x + jnp.dot(o, wo_ref[...], preferred_element_type=F32)


def _mix_out_mem_attn(x, ya, yb, wa, wb, nw, wq, mk, mv, wo, heads):
    b, t, d = x.shape
    m = mk.shape[1]
    tm = _row_tile(t, 512)
    row = lambda w: pl.BlockSpec((None, tm, w), lambda i, j: (i, j, 0))
    mem = pl.BlockSpec((None, m, d), lambda i, j: (i, 0, 0))
    return pl.pallas_call(
        functools.partial(_mix_out_mem_attn_body, heads=heads),
        out_shape=jax.ShapeDtypeStruct((b, t, d), F32),
        grid=(b, t // tm),
        in_specs=[row(d), row(ya.shape[2]), row(yb.shape[2]), _resident(wa.shape), _resident(wb.shape),
                  _resident((1, d)), _resident((d, d)), mem, mem, _resident((d, d))],
        out_specs=row(d),
        compiler_params=_cparams(("parallel", "parallel")),
        name="mix_out_mem_attn",
    )(x, ya, yb, wa, wb, nw.reshape(1, d), wq, mk, mv, wo)


def _seg_sum(x, bd):
    g = bd.shape[0]
    hi = x.astype(BF16)
    lo = (x - hi.astype(F32)).astype(BF16)
    outs = []
    for j in range(x.shape[1] // g):
        sl = slice(j * g, (j + 1) * g)
        outs.append(jnp.dot(hi[:, sl], bd, preferred_element_type=F32) + jnp.dot(lo[:, sl], bd, preferred_element_type=F32))
    return jnp.concatenate(outs, axis=1)


def _rwkv_body(pa_ref, shift0_ref, s0_ref, mu_ref, w0_ref, w2_ref, a0_ref, a2_ref, g2_ref, kkw_ref, ka_ref,
               rk_ref, lnw_ref, lnb_ref, bd_ref, y_ref, shift_ref, s_ref, carry, y_scr, *, tb, L):
    t_idx = pl.program_id(1)
    a_dim = H_A * N_A
    n_pairs = a_dim // LANES

    @pl.when(t_idx == 0)
    def _():
        carry[...] = shift0_ref[...]
        s_ref[...] = s0_ref[...]

    pa = pa_ref[...]
    row = _iota(pa.shape, 0)
    prev = jnp.where(row == 0, carry[...], pltpu.roll(pa, 1, 0))
    carry[...] = pa[tb - 1:tb, :]
    shift_ref[...] = pa[tb - 1:tb, :]
    xa = pa + (prev - pa) * mu_ref[...]
    r = xa[:, 0:a_dim]
    k = xa[:, a_dim:2 * a_dim]
    v = xa[:, 2 * a_dim:3 * a_dim]
    xwa = xa[:, 3 * a_dim:3 * a_dim + LANES]
    xg = xa[:, 3 * a_dim + LANES:]
    wlog = -_softplus(-(w0_ref[...] + _mm(jnp.tanh(xwa), w2_ref[...]))) - 0.5
    logdec = -jnp.exp(wlog)
    a = _sigmoid(a0_ref[...] + _mm(xwa, a2_ref[...]))
    gate = _mm(_sigmoid(xg), g2_ref[...])
    bd = bd_ref[...]
    kq = k * kkw_ref[...]
    kk = kq * lax.rsqrt(_seg_sum(kq * kq, bd) + EPS)
    k2 = k * (1.0 + (a - 1.0) * ka_ref[...])
    av = -kk
    bv = kk * a

    lane = _iota((L, LANES), 1)
    m0 = lane < N_A
    r2 = _iota((2 * L, 2 * L), 0)
    c2 = _iota((2 * L, 2 * L), 1)
    bdm = jnp.logical_not(jnp.logical_xor(r2 >= L, c2 >= L))
    r_loc = jnp.where(r2 >= L, r2 - L, r2)
    c_loc = jnp.where(c2 >= L, c2 - L, c2)
    strict = jnp.logical_and(bdm, c_loc < r_loc)
    incl = jnp.logical_and(bdm, c_loc <= r_loc)
    strict_t = jnp.logical_and(bdm, c_loc > r_loc)
    r4 = _iota((2 * L, 4 * L), 0)
    c4 = _iota((2 * L, 4 * L), 1)
    c4 = jnp.where(c4 >= 2 * L, c4 - 2 * L, c4)
    incl_2 = jnp.logical_and(jnp.logical_not(jnp.logical_xor(r4 >= L, c4 >= L)),
                             jnp.where(c4 >= L, c4 - L, c4) <= jnp.where(r4 >= L, r4 - L, r4))
    bdl = jnp.logical_not(jnp.logical_xor(_iota((2 * L, LANES), 0) >= L, _iota((2 * L, LANES), 1) >= N_A))
    eye2 = (r2 == c2).astype(F32)

    def stack2(x):
        return jnp.concatenate([jnp.where(m0, x, 0.0), jnp.where(m0, 0.0, x)], axis=0).astype(BF16)

    def dup2(x):
        return jnp.concatenate([x, x], axis=0).astype(BF16)

    cum = _mm_exact_lhs01(_chunk_tri(tb, L), logdec)
    tot = _chunk_last(cum, L)
    e_neg = jnp.exp(-cum)
    e_out = jnp.exp(tot - cum)
    a_t = av * jnp.exp(cum - logdec)
    r_t = r * jnp.exp(cum)
    b_t = bv * e_neg
    k_t = k2 * e_neg
    b_o = bv * e_out
    k_o = k2 * e_out
    g_l = jnp.exp(tot)

    items = [(c, p) for c in range(tb // L) for p in range(n_pairs)]
    sl = {(c, p): (slice(c * L, (c + 1) * L), slice(p * LANES, (p + 1) * LANES)) for c, p in items}
    xa = {it: stack2(a_t[sl[it]]) for it in items}
    xr = {it: stack2(r_t[sl[it]]) for it in items}
    yb = {it: dup2(b_t[sl[it]]) for it in items}
    yk = {it: dup2(k_t[sl[it]]) for it in items}
    v2 = {it: jnp.where(bdl, jnp.concatenate([v[sl[it]], v[sl[it]]], axis=0), 0.0).astype(BF16) for it in items}
    wst = {it: jnp.concatenate([stack2(b_o[sl[it]]), stack2(k_o[sl[it]])], axis=0) for it in items}
    n_t = {it: jnp.where(strict_t, _mm_nt(yb[it], xa[it]), 0.0) for it in items}
    ak = {it: jnp.where(strict, _mm_nt(xa[it], yk[it]), 0.0).astype(BF16) for it in items}
    rbk = {it: jnp.where(incl_2, _mm_nt(xr[it], jnp.concatenate([yb[it], yk[it]], axis=0)), 0.0).astype(BF16)
           for it in items}
    akv = {it: _mm(ak[it], v2[it]) for it in items}
    t_t = {it: eye2 + n_t[it] for it in items}
    m_t = {it: _mm(n_t[it], n_t[it]) for it in items}
    power = 4
    while power < L:
        prod = {it: _mm(m_t[it], jnp.concatenate([m_t[it], t_t[it]], axis=1)) for it in items}
        m_t = {it: prod[it][:, :2 * L] for it in items}
        t_t = {it: t_t[it] + prod[it][:, 2 * L:] for it in items}
        power *= 2
    if L > 2:
        t_t = {it: t_t[it] + _mm(m_t[it], t_t[it]) for it in items}
    om_up = {it: _mm_tn(t_t[it], jnp.concatenate([xa[it], akv[it].astype(BF16)], axis=1)) for it in items}
    omega = {it: om_up[it][:, :LANES].astype(BF16) for it in items}
    state = [s_ref[p] for p in range(n_pairs)]
    s_prev, uv = {}, {}
    for c in range(tb // L):
        for p in range(n_pairs):
            it = (c, p)
            s_prev[it] = state[p].astype(BF16)
            u_st = _mm_nt(omega[it], s_prev[it]) + om_up[it][:, LANES:]
            uv[it] = jnp.concatenate([u_st.astype(BF16), v2[it]], axis=0)
        for p in range(n_pairs):
            it = (c, p)
            state[p] = state[p] * g_l[c * L:c * L + 1, sl[it][1]] + _mm_tn(uv[it], wst[it])
    for p in range(n_pairs):
        s_ref[p] = state[p]
    for it in items:
        y_st = _mm_nt(xr[it], s_prev[it]) + _mm(rbk[it], uv[it])
        y_scr[sl[it]] = y_st[0:L] + y_st[L:2 * L]

    y = y_scr[...]
    inv_n = 1.0 / N_A
    mu_y = _seg_sum(y, bd) * inv_n
    yc = y - mu_y
    var = _seg_sum(yc * yc, bd) * inv_n
    yn = yc * lax.rsqrt(var + RWKV_LN_EPS) * lnw_ref[...] + lnb_ref[...]
    bonus = _seg_sum(r * k2 * rk_ref[...], bd) * v
    y_ref[...] = ((yn + bonus) * gate).astype(y_ref.dtype)


def _pair_blockdiag(s):
    b, h, n, _ = s.shape
    s = s.reshape(b, h // 2, 2, n, n)
    z = jnp.zeros((b, h // 2, n, n), s.dtype)
    top = jnp.concatenate([s[:, :, 0], z], axis=-1)
    bot = jnp.concatenate([z, s[:, :, 1]], axis=-1)
    return jnp.concatenate([top, bot], axis=-2)


def _pair_unblock(s):
    b, p, n2, _ = s.shape
    n = n2 // 2
    return jnp.stack([s[:, :, :n, :n], s[:, :, n:, n:]], axis=2).reshape(b, 2 * p, n, n)


def _rwkv(pa, shift0, s0, wts, L):
    b, t, pw = pa.shape
    a_dim = H_A * N_A
    tb = min(t, 256)
    n_pairs = a_dim // LANES
    seg = _iota((MXU_DIM, MXU_DIM), 0) // N_A == _iota((MXU_DIM, MXU_DIM), 1) // N_A
    bd = seg.astype(BF16)
    z64 = jnp.zeros((N_A, a_dim), F32)
    w2p = jnp.concatenate([wts['w2'], z64], axis=0)
    a2p = jnp.concatenate([z64, wts['a2']], axis=0)
    vec = lambda x: x.reshape(1, -1).astype(F32)
    params = [vec(wts['mu']), vec(wts['w0']), w2p.astype(BF16), vec(wts['a0']), a2p.astype(BF16),
              wts['g2'].astype(BF16), vec(wts['kk']), vec(wts['ka']), vec(wts['rk']), vec(wts['ln_w']),
              vec(wts['ln_b']), bd]
    blk_t = pl.BlockSpec((None, tb, pw), lambda i, j: (i, j, 0))
    per_b = lambda shape: pl.BlockSpec((None,) + shape, lambda i, j: (i,) + (0,) * len(shape))
    y, shift, s_new = pl.pallas_call(
        functools.partial(_rwkv_body, tb=tb, L=L),
        out_shape=[jax.ShapeDtypeStruct((b, t, a_dim), BF16), jax.ShapeDtypeStruct((b, 1, pw), F32),
                   jax.ShapeDtypeStruct((b, n_pairs, LANES, LANES), F32)],
        grid=(b, t // tb),
        in_specs=[blk_t, per_b((1, pw)), per_b((n_pairs, LANES, LANES))] + [_resident(p.shape) for p in params],
        out_specs=[pl.BlockSpec((None, tb, a_dim), lambda i, j: (i, j, 0)), per_b((1, pw)),
                   per_b((n_pairs, LANES, LANES))],
        scratch_shapes=[pltpu.VMEM((1, pw), F32), pltpu.VMEM((tb, a_dim), F32)],
        compiler_params=_cparams(("parallel", "arbitrary")),
        name="rwkv7",
    )(pa, shift0, _pair_blockdiag(s0.astype(F32)), *params)
    return y, shift, _pair_unblock(s_new)


def _even_in_body(x_ref, xh_ref, conv0_ref, nw_ref, wa_ref, wqkv_ref, wbab_ref, wz_ref, cw_ref,
                  pa_ref, q_ref, k_ref, v_ref, bab_ref, z_ref, tail_ref, cat_scr, *, tm):
    j = pl.program_id(1)
    hd = DK_B
    inner = H_B * hd
    nw = nw_ref[...]
    h = _rms(x_ref[...], nw).astype(BF16)
    cat_scr[CONV_CARRY:, :] = jnp.dot(h, wqkv_ref[...], preferred_element_type=F32)
    halo = jnp.dot(_rms(xh_ref[...], nw).astype(BF16), wqkv_ref[...], preferred_element_type=F32)
    cat_scr[0:CONV_CARRY, :] = jnp.where(j == 0, conv0_ref[...], halo)
    tail_ref[...] = cat_scr[tm:, :]
    cw = cw_ref[...]

    def conv_strip(c0):
        cols = slice(c0, c0 + hd)
        cat = cat_scr[:, cols]
        acc = cat * cw[0:1, cols]
        for t in range(1, CONV_W):
            acc = pltpu.roll(acc, 1, 0) + cat * cw[t:t + 1, cols]
        return _silu(acc[CONV_CARRY:, :])

    pieces = ([(pa_ref, wa_ref, c) for c in range(0, wa_ref.shape[1], MXU_DIM)]
              + [(z_ref, wz_ref, c) for c in range(0, wz_ref.shape[1], MXU_DIM)] + [(bab_ref, wbab_ref, 0)])

    def project(n):
        for o_ref, w_ref, c in pieces[:n]:
            width = min(MXU_DIM, w_ref.shape[1] - c)
            o_ref[:, c:c + width] = jnp.dot(h, w_ref[:, c:c + width], preferred_element_type=F32)
        del pieces[:n]

    for i in range(H_B):
        sl = slice(i * hd, (i + 1) * hd)
        qs = conv_strip(i * hd)
        q_ref[:, sl] = (qs * lax.rsqrt(jnp.sum(qs * qs, axis=-1, keepdims=True) + EPS) * hd ** -0.5).astype(BF16)
        project(1)
        ks = conv_strip(inner + i * hd)
        k_ref[:, sl] = (ks * lax.rsqrt(jnp.sum(ks * ks, axis=-1, keepdims=True) + EPS)).astype(BF16)
        project(1)
        v_ref[:, sl] = conv_strip(2 * inner + i * hd).astype(BF16)
        project(1)
    project(len(pieces))


def _even_in_proj(x, nw, ws, conv0, conv_w):
    b, t, d = x.shape
    wa, wqkv, wbab, wz = ws
    cdim = wqkv.shape[1]
    inner = H_B * DK_B
    tm = _row_tile(t, 512)
    conv0p = jnp.pad(conv0.astype(F32), ((0, 0), (CONV_CARRY - (CONV_W - 1), 0), (0, 0)))
    row = lambda w: pl.BlockSpec((None, tm, w), lambda i, j: (i, j, 0))
    halo = pl.BlockSpec((None, CONV_CARRY, d), lambda i, j: (i, jnp.maximum(j * (tm // CONV_CARRY) - 1, 0), 0))
    per_b = pl.BlockSpec((None, CONV_CARRY, cdim), lambda i, j: (i, 0, 0))
    outs = pl.pallas_call(
        functools.partial(_even_in_body, tm=tm),
        out_shape=[jax.ShapeDtypeStruct((b, t, wa.shape[1]), F32)]
        + [jax.ShapeDtypeStruct((b, t, inner), BF16)] * 3
        + [jax.ShapeDtypeStruct((b, t, wbab.shape[1]), F32), jax.ShapeDtypeStruct((b, t, wz.shape[1]), F32),
           jax.ShapeDtypeStruct((b, CONV_CARRY, cdim), F32)],
        grid=(b, t // tm),
        in_specs=[row(d), halo, per_b, _resident((1, d)), _resident(wa.shape), _resident(wqkv.shape),
                  _resident(wbab.shape), _resident(wz.shape), _resident(conv_w.shape)],
        out_specs=[row(wa.shape[1]), row(inner), row(inner), row(inner), row(wbab.shape[1]), row(wz.shape[1]), per_b],
        scratch_shapes=[pltpu.VMEM((tm + CONV_CARRY, cdim), F32)],
        compiler_params=_cparams(("parallel", "arbitrary")),
        name="even_in_proj",
    )(x, x, conv0p, nw.reshape(1, d), wa, wqkv, wbab, wz, conv_w.astype(F32))
    return outs[:6], outs[6][:, CONV_CARRY - (CONV_W - 1):]


def _delta_body(q_ref, k_ref, v_ref, bab_ref, z_ref, s0_ref, alog_ref, dtb_ref, nw_ref,
                y_ref, s_ref, *, tb, L):
    t_idx = pl.program_id(1)
    hd = DK_B

    @pl.when(t_idx == 0)
    def _():
        s_ref[...] = s0_ref[...]

    bab = bab_ref[...]
    g_all = -jnp.exp(alog_ref[...]) * _softplus(bab + dtb_ref[...])
    beta_all = _sigmoid(bab)

    tri = _tri_incl(L).astype(BF16)
    rw = _iota((L, L), 0)
    cl_ = _iota((L, L), 1)
    strict = cl_ < rw
    incl = cl_ <= rw
    eye = (rw == cl_).astype(F32)
    nw = nw_ref[...]
    z = z_ref[...]
    n_chunks = tb // L

    g_cum_all = _mm_exact_lhs01(_chunk_tri(tb, L), g_all)
    g_tot_all = _chunk_last(g_cum_all, L)

    qh, kh, vh, kdec, qg, bv_, bk_, gb, beta_b, e_last = {}, {}, {}, {}, {}, {}, {}, {}, {}, {}
    for h in range(H_B):
        qh[h] = q_ref[:, h * hd:(h + 1) * hd]
        kh[h] = k_ref[:, h * hd:(h + 1) * hd]
        qs, ks = qh[h].astype(F32), kh[h].astype(F32)
        vs = v_ref[:, h * hd:(h + 1) * hd].astype(F32)
        gb[h] = jnp.broadcast_to(g_all[:, h:h + 1], (tb, LANES))
        g_cum = jnp.broadcast_to(g_cum_all[:, h:h + 1], (tb, LANES))
        g_tot = jnp.broadcast_to(g_tot_all[:, h:h + 1], (tb, LANES))
        bb = jnp.broadcast_to(beta_all[:, H_B + h:H_B + h + 1], (tb, LANES))
        e_g = jnp.exp(g_cum)
        kdec[h] = (ks * jnp.exp(g_tot - g_cum)).astype(BF16)
        qg[h] = qs * e_g
        bv_[h] = bb * vs
        bk_[h] = bb * e_g * ks
        beta_b[h] = bb
        e_last[h] = jnp.exp(g_tot)

    items = [(c, h) for c in range(n_chunks) for h in range(H_B)]
    rs = {c: slice(c * L, (c + 1) * L) for c in range(n_chunks)}
    diff = {}
    for c, h in items:
        hi, mid, lo = _split3(jnp.where(strict, gb[h][rs[c], :L], 0.0))
        diff[(c, h)] = (jnp.dot(tri, hi, preferred_element_type=F32) + jnp.dot(tri, mid, preferred_element_type=F32)
                        + jnp.dot(tri, lo, preferred_element_type=F32))
    kk_ = {(c, h): _mm_nt(kh[h][rs[c]], kh[h][rs[c]]) for c, h in items}
    qk_ = {(c, h): _mm_nt(qh[h][rs[c]], kh[h][rs[c]]) for c, h in items}
    n_mat, qk_m = {}, {}
    for c, h in items:
        e_diff = jnp.exp(jnp.where(incl, diff[(c, h)], 0.0))
        n_mat[(c, h)] = -(beta_b[h][rs[c], :L] * jnp.where(strict, kk_[(c, h)] * e_diff, 0.0))
        qk_m[(c, h)] = jnp.where(incl, qk_[(c, h)] * e_diff, 0.0).astype(BF16)
    t_inv = {it: eye + n_mat[it] for it in items}
    m_pow = n_mat
    power = 2
    while power < L:
        m_pow = {it: _mm(m_pow[it], m_pow[it]) for it in items}
        t_inv = {it: t_inv[it] + _mm(t_inv[it], m_pow[it]) for it in items}
        power *= 2
    sol = {(c, h): _mm(t_inv[(c, h)], jnp.concatenate([bv_[h][rs[c]], bk_[h][rs[c]]], axis=1)) for c, h in items}
    lhs = {(c, h): jnp.concatenate([sol[(c, h)][:, hd:], qg[h][rs[c]]], axis=0).astype(BF16) for c, h in items}
    state = [s_ref[h] for h in range(H_B)]
    u_, o_inter = {}, {}
    for c in range(n_chunks):
        for h in range(H_B):
            ws = _mm(lhs[(c, h)], state[h])
            u_[(c, h)] = (sol[(c, h)][:, :hd] - ws[:L]).astype(BF16)
            o_inter[(c, h)] = ws[L:]
        for h in range(H_B):
            state[h] = e_last[h][c * L:c * L + 1, :] * state[h] + _mm_tn(kdec[h][rs[c]], u_[(c, h)])
    for h in range(H_B):
        s_ref[h] = state[h]
    for c, h in items:
        o = o_inter[(c, h)] + _mm(qk_m[(c, h)], u_[(c, h)])
        on = o * lax.rsqrt(jnp.mean(o * o, axis=-1, keepdims=True) + EPS) * nw
        y_ref[rs[c], h * hd:(h + 1) * hd] = (on * _silu(z[rs[c], h * hd:(h + 1) * hd])).astype(y_ref.dtype)


def _delta(q, k, v, bab, z, s0, wts, L):
    b, t, inner = q.shape
    tb = min(t, 256)
    pad = lambda x: jnp.pad(x.reshape(1, -1).astype(F32), ((0, 0), (0, LANES - x.size)))
    params = [pad(wts['A_log']), pad(wts['dt_bias']), wts['norm_w'].reshape(1, -1).astype(F32)]
    blk = lambda w: pl.BlockSpec((None, tb, w), lambda i, j: (i, j, 0))
    st = pl.BlockSpec((None, H_B, DK_B, DK_B), lambda i, j: (i, 0, 0, 0))
    return pl.pallas_call(
        functools.partial(_delta_body, tb=tb, L=L),
        out_shape=[jax.ShapeDtypeStruct((b, t, inner), BF16), jax.ShapeDtypeStruct((b, H_B, DK_B, DK_B), F32)],
        grid=(b, t // tb),
        in_specs=[blk(inner), blk(inner), blk(inner), blk(LANES), blk(inner), st] + [_resident(p.shape) for p in params],
        out_specs=[blk(inner), st],
        compiler_params=_cparams(("parallel", "arbitrary")),
        name="gated_delta",
    )(q, k, v, bab, z, s0.astype(F32), *params)


def _gla_body(qk_ref, v_ref, ad_ref, g_ref, s0_ref, a2_ref, ab_ref, nw_ref, y_ref, s_ref, *, tb, L):
    t_idx = pl.program_id(1)
    kdim = H_C * DK_C
    n_pairs = kdim // LANES

    @pl.when(t_idx == 0)
    def _():
        s_ref[...] = s0_ref[...]

    qk = qk_ref[...]
    log_a = -_softplus(-(_mm(ad_ref[...], a2_ref[...]) + ab_ref[...])) * (1.0 / GLA_NORMALIZER)
    vv = v_ref[...]
    gg = g_ref[...]
    nw = nw_ref[...]
    incl = _iota((L, L), 1) <= _iota((L, L), 0)
    m0 = _iota((tb, LANES), 1) < DK_C
    n_chunks = tb // L

    cum = _mm_exact_lhs01(_chunk_tri(tb, L), log_a)
    tot = _chunk_last(cum, L)
    q_in = qk[:, :kdim] * DK_C ** -0.5 * jnp.exp(cum)
    k_in = (qk[:, kdim:] * jnp.exp(-cum)).astype(BF16)
    k_out = qk[:, kdim:] * jnp.exp(tot - cum)
    dec = jnp.exp(tot)
    vb16 = vv.astype(BF16)

    rs = {c: slice(c * L, (c + 1) * L) for c in range(n_chunks)}
    items = [(c, h) for c in range(n_chunks) for h in range(H_C)]
    qm, km = {}, {}
    for h in range(H_C):
        ls = slice((h // 2) * LANES, (h // 2 + 1) * LANES)
        mask = m0 if h % 2 == 0 else jnp.logical_not(m0)
        qm[h] = jnp.where(mask, q_in[:, ls], 0.0).astype(BF16)
        km[h] = jnp.where(mask, k_out[:, ls], 0.0).astype(BF16)
    scores = {(c, h): jnp.where(incl, _mm_nt(qm[h][rs[c]], k_in[rs[c], (h // 2) * LANES:(h // 2 + 1) * LANES]), 0.0)
              .astype(BF16) for c, h in items}
    upd = {(c, h): _mm_tn(vb16[rs[c], h * DV_C:(h + 1) * DV_C], km[h][rs[c]]) for c, h in items}
    state = [s_ref[p] for p in range(n_pairs)]
    s_prev = {}
    for c in range(n_chunks):
        for p in range(n_pairs):
            s_prev[(c, p)] = state[p].astype(BF16)
            state[p] = (state[p] * dec[c * L:c * L + 1, p * LANES:(p + 1) * LANES]
                        + upd[(c, 2 * p)] + upd[(c, 2 * p + 1)])
    for p in range(n_pairs):
        s_ref[p] = state[p]
    for c, h in items:
        hs = slice(h * DV_C, (h + 1) * DV_C)
        o = _mm_nt(qm[h][rs[c]], s_prev[(c, h // 2)]) + _mm(scores[(c, h)], vb16[rs[c], hs])
        on = o * lax.rsqrt(jnp.mean(o * o, axis=-1, keepdims=True) + EPS) * nw
        y_ref[rs[c], hs] = (on * _silu(gg[rs[c], hs])).astype(y_ref.dtype)


def _gla(cqk, cv, cad, cg, s0, wts, L):
    b, t, _ = cqk.shape
    kdim = H_C * DK_C
    vdim = H_C * DV_C
    n_pairs = kdim // LANES
    tb = min(t, 256)
    rank = wts['a2'].shape[0]
    a2p = jnp.pad(wts['a2'].astype(F32), ((0, LANES - rank), (0, 0))).astype(BF16)
    params = [a2p, wts['a_bias'].reshape(1, -1).astype(F32), wts['norm_w'].reshape(1, -1).astype(F32)]
    st0 = jnp.swapaxes(s0.astype(F32), -1, -2).reshape(b, n_pairs, 2, DV_C, DK_C)
    st0 = jnp.concatenate([st0[:, :, 0], st0[:, :, 1]], axis=-1)
    blk = lambda w: pl.BlockSpec((None, tb, w), lambda i, j: (i, j, 0))
    per_b = lambda shape: pl.BlockSpec((None,) + shape, lambda i, j: (i,) + (0,) * len(shape))
    y, st = pl.pallas_call(
        functools.partial(_gla_body, tb=tb, L=L),
        out_shape=[jax.ShapeDtypeStruct((b, t, vdim), BF16), jax.ShapeDtypeStruct((b, n_pairs, DV_C, LANES), F32)],
        grid=(b, t // tb),
        in_specs=[blk(2 * kdim), blk(vdim), blk(LANES), blk(vdim), per_b((n_pairs, DV_C, LANES))]
        + [_resident(p.shape) for p in params],
        out_specs=[blk(vdim), per_b((n_pairs, DV_C, LANES))],
        compiler_params=_cparams(("parallel", "arbitrary")),
        name="gla",
    )(cqk, cv, cad, cg, st0, *params)
    st = jnp.stack([st[..., :DK_C], st[..., DK_C:]], axis=2).reshape(b, H_C, DV_C, DK_C)
    return y, jnp.swapaxes(st, -1, -2)


def _ret_body(q_ref, k_ref, v_ref, g_ref, cos_ref, sin_ref, s0_ref, y_ref, s_ref, *, tb, L):
    t_idx = pl.program_id(1)
    hd = DK_D

    @pl.when(t_idx == 0)
    def _():
        s_ref[...] = s0_ref[...]

    cos = cos_ref[...]
    sin = sin_ref[...]
    qq, kk, vv, gg = q_ref[...], k_ref[...], v_ref[...], g_ref[...]
    rw = _iota((L, L), 0)
    cl_ = _iota((L, L), 1)
    incl = cl_ <= rw
    dist = jnp.where(incl, rw - cl_, 0).astype(F32)
    pos = _iota((L, LANES), 0).astype(F32)

    n_chunks = tb // L
    rs = {c: slice(c * L, (c + 1) * L) for c in range(n_chunks)}
    items = [(c, h) for c in range(n_chunks) for h in range(H_D)]
    lg = [math.log(1.0 - 2.0 ** (-5.0 - h)) for h in range(H_D)]
    vb16 = vv.astype(BF16)
    qh, kh, q_in, k_out, dmat = {}, {}, {}, {}, {}
    for h in range(H_D):
        hs = slice(h * hd, (h + 1) * hd)
        q_rot = qq[:, hs] * cos + pltpu.roll(qq[:, hs], hd // 2, 1) * sin
        k_rot = (kk[:, hs] * cos + pltpu.roll(kk[:, hs], hd // 2, 1) * sin) * hd ** -0.5
        dmat[h] = jnp.where(incl, jnp.exp(dist * lg[h]), 0.0)
        e_in = jnp.exp((pos + 1.0) * lg[h])
        e_out = jnp.exp((L - 1.0 - pos) * lg[h])
        qh[h], kh[h] = q_rot.astype(BF16), k_rot.astype(BF16)
        for c in range(n_chunks):
            q_in[(c, h)] = (q_rot[rs[c]] * e_in).astype(BF16)
            k_out[(c, h)] = (k_rot[rs[c]] * e_out).astype(BF16)
    scores = {(c, h): (_mm_nt(qh[h][rs[c]], kh[h][rs[c]]) * dmat[h]).astype(BF16) for c, h in items}
    upd = {(c, h): _mm_tn(k_out[(c, h)], vb16[rs[c], h * hd:(h + 1) * hd]) for c, h in items}
    state = [s_ref[h] for h in range(H_D)]
    s_prev = {}
    for c in range(n_chunks):
        for h in range(H_D):
            s_prev[(c, h)] = state[h].astype(BF16)
            state[h] = state[h] * math.exp(lg[h] * L) + upd[(c, h)]
    for h in range(H_D):
        s_ref[h] = state[h]
    for c, h in items:
        hs = slice(h * hd, (h + 1) * hd)
        o = _mm(q_in[(c, h)], s_prev[(c, h)]) + _mm(scores[(c, h)], vb16[rs[c], hs])
        mu = jnp.mean(o, axis=-1, keepdims=True)
        oc = o - mu
        on = oc * lax.rsqrt(jnp.mean(oc * oc, axis=-1, keepdims=True) + EPS)
        y_ref[rs[c], hs] = (on * _silu(gg[rs[c], hs])).astype(y_ref.dtype)


def _retention(dq, dk, dv, dg, s0, pos0, L):
    b, t, dim = dq.shape
    tb = min(t, 256)
    half = DK_D // 2
    inv = ROPE_BASE ** (-jnp.arange(0, DK_D, 2, dtype=F32) / DK_D)
    ang = (jnp.arange(t) + pos0).astype(F32)[:, None] * inv[None, :]
    cos = jnp.concatenate([jnp.cos(ang), jnp.cos(ang)], axis=-1)
    sin = jnp.concatenate([-jnp.sin(ang), jnp.sin(ang)], axis=-1)
    del half
    blk = pl.BlockSpec((None, tb, dim), lambda i, j: (i, j, 0))
    tab = pl.BlockSpec((tb, DK_D), lambda i, j: (j, 0))
    st = pl.BlockSpec((None, H_D, DK_D, DK_D), lambda i, j: (i, 0, 0, 0))
    return pl.pallas_call(
        functools.partial(_ret_body, tb=tb, L=L),
        out_shape=[jax.ShapeDtypeStruct((b, t, dim), BF16), jax.ShapeDtypeStruct((b, H_D, DK_D, DK_D), F32)],
        grid=(b, t // tb),
        in_specs=[blk, blk, blk, blk, tab, tab, st],
        out_specs=[blk, st],
        compiler_params=_cparams(("parallel", "arbitrary")),
        name="retention",
    )(dq, dk, dv, dg, cos, sin, s0.astype(F32))


def _prep_weights(W):
    bf = lambda x: x.astype(BF16)
    a_proj = 3 * H_A * N_A + 64 + 64 + 128
    b_conv = 3 * H_B * DK_B
    P = {}
    for name in ('ffn1_wg', 'ffn1_wu', 'ffn1_wd', 'ffn2_wg', 'ffn2_wu', 'ffn2_wd', 'mem_wq', 'mem_wo', 'mem_wk',
                 'mem_wv', 'even_w_out', 'odd_w_out'):
        P[name] = bf(W[name])
    ew = W['even_w_in']
    o = a_proj
    bab = ew[:, :, o + b_conv:o + b_conv + 2 * H_B]
    P['even_in'] = [bf(ew[:, :, :a_proj]), bf(ew[:, :, o:o + b_conv]),
                    bf(jnp.pad(bab, ((0, 0), (0, 0), (0, LANES - 2 * H_B)))), bf(ew[:, :, o + b_conv + 2 * H_B:])]
    ow = W['odd_w_in']
    kd, vd, rank = H_C * DK_C, H_C * DV_C, W['gla_a2'].shape[1]
    c_proj = 2 * kd + 2 * vd + rank
    dd = H_D * DK_D
    cad = ow[:, :, 2 * kd + vd:2 * kd + vd + rank]
    P['odd_in'] = [bf(ow[:, :, :2 * kd]), bf(ow[:, :, 2 * kd:2 * kd + vd]),
                   bf(jnp.pad(cad, ((0, 0), (0, 0), (0, LANES - rank)))), bf(ow[:, :, 2 * kd + vd + rank:c_proj])] \
        + [bf(ow[:, :, c_proj + j * dd:c_proj + (j + 1) * dd]) for j in range(4)]
    return P


def _trunk(x, pos0, mem_k, mem_v, shift, rwkv, conv, delta, gla, ret, W, P):
    b, t, d = x.shape
    n = b * t
    depth = W['norm_ffn1'].shape[0]
    L_delta = min(64, t)
    L_lin = min(64, t)
    flat = lambda z: z.reshape(n, z.shape[-1])
    unflat = lambda z: z.reshape(b, t, z.shape[-1])
    new = {k: [] for k in ('shift', 'rwkv', 'conv', 'delta', 'gla', 'ret')}
    x = flat(x)
    for l in range(depth):
        i = l // 2
        x = _ffn(x, W['norm_ffn1'][l], P['ffn1_wg'][l], P['ffn1_wu'][l], P['ffn1_wd'][l])
        if l % 2 == 0:
            (pa, dq_, dk_, dv_, bab, z), s3 = _even_in_proj(unflat(x), W['norm_mix'][l], [w[i] for w in P['even_in']],
                                                           conv[i], W['delta_conv_w'][i])
            wa = dict(mu=W['rwkv_mu'][i], w0=W['rwkv_w0'][i], w2=W['rwkv_w2'][i], a0=W['rwkv_a0'][i],
                      a2=W['rwkv_a2'][i], g2=W['rwkv_g2'][i], kk=W['rwkv_kk'][i], ka=W['rwkv_ka'][i],
                      rk=W['rwkv_rk'][i], ln_w=W['rwkv_ln_w'][i], ln_b=W['rwkv_ln_b'][i])
            y_a, s1, s2 = _rwkv(pa, shift[i], rwkv[i], wa, L_delta)
            wb = dict(A_log=W['delta_A_log'][i], dt_bias=W['delta_dt_bias'][i], norm_w=W['delta_norm_w'][i])
            y_b, s4 = _delta(dq_, dk_, dv_, bab, z, delta[i], wb, L_delta)
            new['shift'].append(s1)
            new['rwkv'].append(s2)
            new['conv'].append(s3)
            new['delta'].append(s4)
            split = H_A * N_A
            halves, w_out = (y_a, y_b), P['even_w_out'][i]
        else:
            cqk, cv, cad, cg, dq, dk, dv, dg = _norm_proj(x, W['norm_mix'][l], [w[i] for w in P['odd_in']],
                                                          bf16_out=(1, 6))
            wc = dict(a2=W['gla_a2'][i], a_bias=W['gla_a_bias'][i], norm_w=W['gla_norm_w'][i])
            y_c, s5 = _gla(unflat(cqk), unflat(cv), unflat(cad), unflat(cg), gla[i], wc, L_lin)
            y_d, s6 = _retention(unflat(dq), unflat(dk), unflat(dv), unflat(dg), ret[i], pos0, L_lin)
            new['gla'].append(s5)
            new['ret'].append(s6)
            split = H_C * DV_C
            halves, w_out = (y_c, y_d), P['odd_w_out'][i]
        mk = mem_k[l].reshape(b, -1, d).astype(BF16)
        mv = mem_v[l].reshape(b, -1, d).astype(BF16)
        heads = mem_k[l].shape[2]
        x = flat(_mix_out_mem_attn(unflat(x), halves[0], halves[1], w_out[:split], w_out[split:], W['norm_mem'][l],
                                   P['mem_wq'][l], mk, mv, P['mem_wo'][l], heads))
        fw = W['final_norm'] if l == depth - 1 else None
        x = _ffn(x, W['norm_ffn2'][l], P['ffn2_wg'][l], P['ffn2_wu'][l], P['ffn2_wd'][l], fw)
    return unflat(x), new


def kernel(x_prompt, x_sample, mem_prompt, state_rwkv_shift, state_rwkv, state_delta_conv, state_delta, state_gla, state_ret, cache_mem_k, cache_mem_v, norm_ffn1, ffn1_wg, ffn1_wu, ffn1_wd, norm_mix, even_w_in, even_w_out, rwkv_mu, rwkv_w0, rwkv_w2, rwkv_a0, rwkv_a2, rwkv_g2, rwkv_kk, rwkv_ka, rwkv_rk, rwkv_ln_w, rwkv_ln_b, delta_conv_w, delta_A_log, delta_dt_bias, delta_norm_w, odd_w_in, odd_w_out, gla_a2, gla_a_bias, gla_norm_w, norm_mem, mem_norm_kv, mem_wq, mem_wk, mem_wv, mem_wo, norm_ffn2, ffn2_wg, ffn2_wu, ffn2_wd, final_norm):
    W = dict(norm_ffn1=norm_ffn1, ffn1_wg=ffn1_wg, ffn1_wu=ffn1_wu, ffn1_wd=ffn1_wd, norm_mix=norm_mix,
             even_w_in=even_w_in, even_w_out=even_w_out, rwkv_mu=rwkv_mu, rwkv_w0=rwkv_w0, rwkv_w2=rwkv_w2,
             rwkv_a0=rwkv_a0, rwkv_a2=rwkv_a2, rwkv_g2=rwkv_g2, rwkv_kk=rwkv_kk, rwkv_ka=rwkv_ka,
             rwkv_rk=rwkv_rk, rwkv_ln_w=rwkv_ln_w, rwkv_ln_b=rwkv_ln_b, delta_conv_w=delta_conv_w,
             delta_A_log=delta_A_log, delta_dt_bias=delta_dt_bias, delta_norm_w=delta_norm_w,
             odd_w_in=odd_w_in, odd_w_out=odd_w_out, gla_a2=gla_a2, gla_a_bias=gla_a_bias,
             gla_norm_w=gla_norm_w, norm_mem=norm_mem, mem_wq=mem_wq, mem_wk=mem_wk, mem_wv=mem_wv,
             mem_wo=mem_wo, norm_ffn2=norm_ffn2, ffn2_wg=ffn2_wg, ffn2_wu=ffn2_wu, ffn2_wd=ffn2_wd,
             final_norm=final_norm)
    P = _prep_weights(W)
    dt = x_prompt.dtype
    bp, _, d = x_prompt.shape
    depth = norm_ffn1.shape[0]
    n_even, n_odd = (depth + 1) // 2, depth // 2
    heads, hd = cache_mem_k.shape[3], cache_mem_k.shape[4]
    n_mem = mem_prompt.shape[1]
    pk, pv = [], []
    mem_flat = mem_prompt.reshape(bp * n_mem, d)
    for l in range(depth):
        mk, mv = _norm_proj(mem_flat, mem_norm_kv[l], [P['mem_wk'][l], P['mem_wv'][l]])
        pk.append(mk.reshape(bp, n_mem, heads, hd))
        pv.append(mv.reshape(bp, n_mem, heads, hd))
    zeros = lambda ref, cnt: [jnp.zeros((bp,) + ref.shape[2:], F32)] * cnt
    y_prompt, ps = _trunk(x_prompt, 0, pk, pv, zeros(state_rwkv_shift, n_even), zeros(state_rwkv, n_even),
                          zeros(state_delta_conv, n_even), zeros(state_delta, n_even), zeros(state_gla, n_odd),
                          zeros(state_ret, n_odd), W, P)
    y_sample, ss = _trunk(x_sample, PAST_LEN, [cache_mem_k[l] for l in range(depth)],
                          [cache_mem_v[l] for l in range(depth)],
                          [state_rwkv_shift[i] for i in range(n_even)], [state_rwkv[i] for i in range(n_even)],
                          [state_delta_conv[i] for i in range(n_even)], [state_delta[i] for i in range(n_even)],
                          [state_gla[i] for i in range(n_odd)], [state_ret[i] for i in range(n_odd)], W, P)
    st = lambda xs: jnp.stack(xs).astype(dt)
    order = ('shift', 'rwkv', 'conv', 'delta', 'gla', 'ret')
    return ((y_prompt, y_sample) + tuple(st(ps[k]) for k in order) + (st(pk), st(pv))
            + tuple(st(ss[k]) for k in order))
```

```python
import functools
import math

import jax
import jax.numpy as jnp
from jax import lax
from jax.experimental import pallas as pl
from jax.experimental.pallas import tpu as pltpu

F32 = jnp.float32
BF16 = jnp.bfloat16
EPS = 1e-6
RWKV_LN_EPS = 64e-5
GLA_NORMALIZER = 16.0
ROPE_BASE = 10000.0
PAST_LEN = 1024

LANES = 128
MXU_DIM = 256
VMEM_LIMIT = 56 * 1024 * 1024

H_A, N_A = 8, 64
H_B, DK_B = 4, 128
H_C, DK_C, DV_C = 4, 64, 128
H_D, DK_D = 4, 128
CONV_W = 4
CONV_CARRY = 8
MIXER_ROWS = 256
CONV_ROWS = 128


def _cparams(sem):
    return pltpu.CompilerParams(dimension_semantics=sem, vmem_limit_bytes=VMEM_LIMIT)


def _mm(a, b):
    return jnp.dot(a.astype(BF16), b.astype(BF16), preferred_element_type=F32)


def _mm_nt(a, b):
    return lax.dot_general(a.astype(BF16), b.astype(BF16), (((1,), (1,)), ((), ())),
                           preferred_element_type=F32)


def _mm_tn(a, b):
    return lax.dot_general(a.astype(BF16), b.astype(BF16), (((0,), (0,)), ((), ())),
                           preferred_element_type=F32)


def _split3(x):
    hi = x.astype(BF16)
    r = x - hi.astype(F32)
    mid = r.astype(BF16)
    lo = (r - mid.astype(F32)).astype(BF16)
    return hi, mid, lo


def _mm_exact_lhs01(a01, x):
    hi, mid, lo = _split3(x)
    a = a01.astype(BF16)
    return (jnp.dot(a, hi, preferred_element_type=F32) + jnp.dot(a, mid, preferred_element_type=F32)
            + jnp.dot(a, lo, preferred_element_type=F32))


def _iota(shape, dim):
    return lax.broadcasted_iota(jnp.int32, shape, dim)


def _tri_incl(n):
    return (_iota((n, n), 1) <= _iota((n, n), 0)).astype(F32)


def _chunk_tri(tb, L):
    shift = L.bit_length() - 1
    assert 1 << shift == L
    rt, ct = _iota((tb, tb), 0), _iota((tb, tb), 1)
    same = jnp.right_shift(rt, shift) == jnp.right_shift(ct, shift)
    return jnp.logical_and(same, ct <= rt).astype(F32)


def _chunk_last(cum, L):
    tb, w = cum.shape
    return jnp.concatenate([jnp.broadcast_to(cum[c * L + L - 1:c * L + L, :], (L, w)) for c in range(tb // L)], axis=0)


def _sigmoid(x):
    return 0.5 * (jnp.tanh(0.5 * x) + 1.0)


def _silu(x):
    return x * _sigmoid(x)


def _softplus(x):
    return jnp.maximum(x, 0.0) + jnp.log1p(jnp.exp(-jnp.abs(x)))


def _rms(x, w):
    return x * lax.rsqrt(jnp.mean(x * x, axis=-1, keepdims=True) + EPS) * w


def _inv_unit_lower(n_mat, size):
    dim = n_mat.shape[0]
    eye = (_iota((dim, dim), 0) == _iota((dim, dim), 1)).astype(F32)
    t = eye + n_mat
    m = n_mat
    power = 2
    while power < size:
        m = _mm(m, m)
        t = t + _mm(t, m)
        power *= 2
    return t


def _row_tile(n, target):
    t = min(n, target)
    while n % t:
        t //= 2
    return t


def _mixer_block(b, t, L):
    tb = min(t, 2 * L)
    nb = max(1, min(b, MIXER_ROWS // tb))
    while b % nb:
        nb -= 1
    return nb, tb


def _resident(shape):
    nd = len(shape)
    return pl.BlockSpec(shape, lambda *_: (0,) * nd, pipeline_mode=pl.Buffered(1))


def _ffn_body(x_ref, nw_ref, wg_ref, wu_ref, wd_ref, fw_ref, o_ref, *, chunk, final):
    x = x_ref[...]
    h = _rms(x, nw_ref[...]).astype(BF16)
    d_ff = wg_ref.shape[1]
    n_chunks = d_ff // chunk

    def gate_up(c):
        sl = slice(c * chunk, (c + 1) * chunk)
        return (jnp.dot(h, wg_ref[:, sl], preferred_element_type=F32),
                jnp.dot(h, wu_ref[:, sl], preferred_element_type=F32))

    acc = None
    nxt = gate_up(0)
    for c in range(n_chunks):
        g, u = nxt
        if c + 1 < n_chunks:
            nxt = gate_up(c + 1)
        a = (_silu(g) * u).astype(BF16)
        d = jnp.dot(a, wd_ref[c * chunk:(c + 1) * chunk, :], preferred_element_type=F32)
        acc = d if acc is None else acc + d
    y = x + 0.5 * acc
    if final:
        y = _rms(y, fw_ref[...])
    o_ref[...] = y


def _ffn(x, nw, wg, wu, wd, fw=None):
    n, d = x.shape
    d_ff = wg.shape[1]
    tm = _row_tile(n, 512)
    final = fw is not None
    if fw is None:
        fw = nw
    return pl.pallas_call(
        functools.partial(_ffn_body, chunk=MXU_DIM, final=final),
        out_shape=jax.ShapeDtypeStruct((n, d), F32),
        grid=(n // tm,),
        in_specs=[pl.BlockSpec((tm, d), lambda i: (i, 0)), _resident((1, d)), _resident((d, d_ff)),
                  _resident((d, d_ff)), _resident((d_ff, d)), _resident((1, d))],
        out_specs=pl.BlockSpec((tm, d), lambda i: (i, 0)),
        compiler_params=_cparams(("parallel",)),
        name="ffn",
    )(x, nw.reshape(1, d), wg, wu, wd, fw.reshape(1, d))


def _norm_proj_body(*refs, n_w):
    x_ref, nw_ref = refs[0], refs[1]
    w_refs = refs[2:2 + n_w]
    o_refs = refs[2 + n_w:]
    h = _rms(x_ref[...], nw_ref[...]).astype(BF16)
    for w_ref, o_ref in zip(w_refs, o_refs):
        o_ref[...] = jnp.dot(h, w_ref[...], preferred_element_type=F32).astype(o_ref.dtype)


def _norm_proj(x, nw, ws, bf16_out=()):
    n, d = x.shape
    tm = _row_tile(n, 512)
    return pl.pallas_call(
        functools.partial(_norm_proj_body, n_w=len(ws)),
        out_shape=[jax.ShapeDtypeStruct((n, w.shape[1]), BF16 if i in bf16_out else F32) for i, w in enumerate(ws)],
        grid=(n // tm,),
        in_specs=[pl.BlockSpec((tm, d), lambda i: (i, 0)), _resident((1, d))] + [_resident(w.shape) for w in ws],
        out_specs=[pl.BlockSpec((tm, w.shape[1]), lambda i: (i, 0)) for w in ws],
        compiler_params=_cparams(("parallel",)),
        name="norm_proj",
    )(x, nw.reshape(1, d), *ws)


def _mix_out_mem_attn_body(x_ref, ya_ref, yb_ref, wa_ref, wb_ref, nw_ref, wq_ref, k_ref, v_ref, wo_ref, o_ref, *,
                           heads):
    x = (x_ref[...] + jnp.dot(ya_ref[...], wa_ref[...], preferred_element_type=F32)
         + jnp.dot(yb_ref[...], wb_ref[...], preferred_element_type=F32))
    h = _rms(x, nw_ref[...]).astype(BF16)
    q = jnp.dot(h, wq_ref[...], preferred_element_type=F32).astype(BF16)
    d = q.shape[1]
    hd = d // heads

    def scores(i):
        sl = slice(i * hd, (i + 1) * hd)
        return _mm_nt(q[:, sl], k_ref[:, sl]) * hd ** -0.5

    outs = []
    nxt = scores(0)
    for i in range(heads):
        s = nxt
        if i + 1 < heads:
            nxt = scores(i + 1)
        e = jnp.exp(s - jnp.max(s, axis=-1, keepdims=True))
        pr = e / jnp.sum(e, axis=-1, keepdims=True)
        outs.append(_mm(pr, v_ref[:, i * hd:(i + 1) * hd]))
    o = jnp.concatenate(outs, axis=-1).astype(BF16)
    o_ref[...] = x + jnp.dot(o, wo_ref[...], preferred_element_type=F32)


def _mix_out_mem_attn(x, ya, yb, wa, wb, nw, wq, mk, mv, wo, heads):
    b, t, d = x.shape
    m = mk.shape[1]
    tm = _row_tile(t, 512)
    row = lambda w: pl.BlockSpec((None, tm, w), lambda i, j: (i, j, 0))
    mem = pl.BlockSpec((None, m, d), lambda i, j: (i, 0, 0))
    return pl.pallas_call(
        functools.partial(_mix_out_mem_attn_body, heads=heads),
        out_shape=jax.ShapeDtypeStruct((b, t, d), F32),
        grid=(b, t // tm),
        in_specs=[row(d), row(ya.shape[2]), row(yb.shape[2]), _resident(wa.shape), _resident(wb.shape),
                  _resident((1, d)), _resident((d, d)), mem, mem, _resident((d, d))],
        out_specs=row(d),
        compiler_params=_cparams(("parallel", "parallel")),
        name="mix_out_mem_attn",
    )(x, ya, yb, wa, wb, nw.reshape(1, d), wq, mk, mv, wo)


def _seg_sum(x, bd):
    g = bd.shape[0]
    hi = x.astype(BF16)
    lo = (x - hi.astype(F32)).astype(BF16)
    outs = []
    for j in range(x.shape[1] // g):
        sl = slice(j * g, (j + 1) * g)
        outs.append(jnp.dot(hi[:, sl], bd, preferred_element_type=F32) + jnp.dot(lo[:, sl], bd, preferred_element_type=F32))
    return jnp.concatenate(outs, axis=1)


def _rwkv_body(pa_ref, shift0_ref, s0_ref, mu_ref, w0_ref, w2_ref, a0_ref, a2_ref, g2_ref, kkw_ref, ka_ref,
               rk_ref, lnw_ref, lnb_ref, bd_ref, y_ref, shift_ref, s_ref, y_scr, *, nb, tb, L):
    t_idx = pl.program_id(1)
    a_dim = H_A * N_A
    n_pairs = a_dim // LANES

    @pl.when(t_idx == 0)
    def _():
        shift_ref[...] = shift0_ref[...]
        s_ref[...] = s0_ref[...]

    row = _iota((tb, pa_ref.shape[2]), 0)
    segs, prevs = [], []
    for b in range(nb):
        seg = pa_ref[b]
        segs.append(seg)
        prevs.append(jnp.where(row == 0, shift_ref[b], pltpu.roll(seg, 1, 0)))
        shift_ref[b] = seg[tb - 1:tb, :]
    pa = jnp.concatenate(segs, axis=0)
    prev = jnp.concatenate(prevs, axis=0)
    rows = nb * tb
    xa = pa + (prev - pa) * mu_ref[...]
    r = xa[:, 0:a_dim]
    k = xa[:, a_dim:2 * a_dim]
    v = xa[:, 2 * a_dim:3 * a_dim]
    xwa = xa[:, 3 * a_dim:3 * a_dim + LANES]
    xg = xa[:, 3 * a_dim + LANES:]
    wlog = -_softplus(-(w0_ref[...] + _mm(jnp.tanh(xwa), w2_ref[...]))) - 0.5
    logdec = -jnp.exp(wlog)
    a = _sigmoid(a0_ref[...] + _mm(xwa, a2_ref[...]))
    gate = _mm(_sigmoid(xg), g2_ref[...])
    bd = bd_ref[...]
    kq = k * kkw_ref[...]
    kk = kq * lax.rsqrt(_seg_sum(kq * kq, bd) + EPS)
    k2 = k * (1.0 + (a - 1.0) * ka_ref[...])
    av = -kk
    bv = kk * a

    lane = _iota((L, LANES), 1)
    m0 = lane < N_A
    r2 = _iota((2 * L, 2 * L), 0)
    c2 = _iota((2 * L, 2 * L), 1)
    bdm = jnp.logical_not(jnp.logical_xor(r2 >= L, c2 >= L))
    r_loc = jnp.where(r2 >= L, r2 - L, r2)
    c_loc = jnp.where(c2 >= L, c2 - L, c2)
    strict = jnp.logical_and(bdm, c_loc < r_loc)
    incl = jnp.logical_and(bdm, c_loc <= r_loc)
    strict_t = jnp.logical_and(bdm, c_loc > r_loc)
    r4 = _iota((2 * L, 4 * L), 0)
    c4 = _iota((2 * L, 4 * L), 1)
    c4 = jnp.where(c4 >= 2 * L, c4 - 2 * L, c4)
    incl_2 = jnp.logical_and(jnp.logical_not(jnp.logical_xor(r4 >= L, c4 >= L)),
                             jnp.where(c4 >= L, c4 - L, c4) <= jnp.where(r4 >= L, r4 - L, r4))
    bdl = jnp.logical_not(jnp.logical_xor(_iota((2 * L, LANES), 0) >= L, _iota((2 * L, LANES), 1) >= N_A))
    eye2 = (r2 == c2).astype(F32)

    def stack2(x):
        return jnp.concatenate([jnp.where(m0, x, 0.0), jnp.where(m0, 0.0, x)], axis=0).astype(BF16)

    def dup2(x):
        return jnp.concatenate([x, x], axis=0).astype(BF16)

    cum = _mm_exact_lhs01(_chunk_tri(rows, L), logdec)
    tot = _chunk_last(cum, L)
    e_neg = jnp.exp(-cum)
    e_out = jnp.exp(tot - cum)
    a_t = av * jnp.exp(cum - logdec)
    r_t = r * jnp.exp(cum)
    b_t = bv * e_neg
    k_t = k2 * e_neg
    b_o = bv * e_out
    k_o = k2 * e_out
    g_l = jnp.exp(tot)

    per_seg = tb // L
    items = [(c, p) for c in range(rows // L) for p in range(n_pairs)]
    sl = {(c, p): (slice(c * L, (c + 1) * L), slice(p * LANES, (p + 1) * LANES)) for c, p in items}
    xa = {it: stack2(a_t[sl[it]]) for it in items}
    xr = {it: stack2(r_t[sl[it]]) for it in items}
    yb = {it: dup2(b_t[sl[it]]) for it in items}
    yk = {it: dup2(k_t[sl[it]]) for it in items}
    v2 = {it: jnp.where(bdl, jnp.concatenate([v[sl[it]], v[sl[it]]], axis=0), 0.0).astype(BF16) for it in items}
    wst = {it: jnp.concatenate([stack2(b_o[sl[it]]), stack2(k_o[sl[it]])], axis=0) for it in items}
    n_t = {it: jnp.where(strict_t, _mm_nt(yb[it], xa[it]), 0.0) for it in items}
    ak = {it: jnp.where(strict, _mm_nt(xa[it], yk[it]), 0.0).astype(BF16) for it in items}
    rbk = {it: jnp.where(incl_2, _mm_nt(xr[it], jnp.concatenate([yb[it], yk[it]], axis=0)), 0.0).astype(BF16)
           for it in items}
    akv = {it: _mm(ak[it], v2[it]) for it in items}
    t_t = {it: eye2 + n_t[it] for it in items}
    m_t = {it: _mm(n_t[it], n_t[it]) for it in items}
    power = 4
    while power < L:
        prod = {it: _mm(m_t[it], jnp.concatenate([m_t[it], t_t[it]], axis=1)) for it in items}
        m_t = {it: prod[it][:, :2 * L] for it in items}
        t_t = {it: t_t[it] + prod[it][:, 2 * L:] for it in items}
        power *= 2
    if L > 2:
        t_t = {it: t_t[it] + _mm(m_t[it], t_t[it]) for it in items}
    om_up = {it: _mm_tn(t_t[it], jnp.concatenate([xa[it], akv[it].astype(BF16)], axis=1)) for it in items}
    omega = {it: om_up[it][:, :LANES].astype(BF16) for it in items}
    chains = [(b, p) for b in range(nb) for p in range(n_pairs)]
    state = {bp: s_ref[bp[0], bp[1]] for bp in chains}
    s_prev, uv = {}, {}
    for k_pos in range(per_seg):
        for b, p in chains:
            it = (b * per_seg + k_pos, p)
            s_prev[it] = state[(b, p)].astype(BF16)
            u_st = _mm_nt(omega[it], s_prev[it]) + om_up[it][:, LANES:]
            uv[it] = jnp.concatenate([u_st.astype(BF16), v2[it]], axis=0)
        for b, p in chains:
            it = (b * per_seg + k_pos, p)
            state[(b, p)] = state[(b, p)] * g_l[it[0] * L:it[0] * L + 1, sl[it][1]] + _mm_tn(uv[it], wst[it])
    for b, p in chains:
        s_ref[b, p] = state[(b, p)]
    for it in items:
        y_st = _mm_nt(xr[it], s_prev[it]) + _mm(rbk[it], uv[it])
        y_scr[sl[it]] = y_st[0:L] + y_st[L:2 * L]

    y = y_scr[...]
    inv_n = 1.0 / N_A
    mu_y = _seg_sum(y, bd) * inv_n
    yc = y - mu_y
    var = _seg_sum(yc * yc, bd) * inv_n
    yn = yc * lax.rsqrt(var + RWKV_LN_EPS) * lnw_ref[...] + lnb_ref[...]
    bonus = _seg_sum(r * k2 * rk_ref[...], bd) * v
    out = ((yn + bonus) * gate).astype(y_ref.dtype)
    for b in range(nb):
        y_ref[b] = out[b * tb:(b + 1) * tb, :]


def _pair_blockdiag(s):
    b, h, n, _ = s.shape
    s = s.reshape(b, h // 2, 2, n, n)
    z = jnp.zeros((b, h // 2, n, n), s.dtype)
    top = jnp.concatenate([s[:, :, 0], z], axis=-1)
    bot = jnp.concatenate([z, s[:, :, 1]], axis=-1)
    return jnp.concatenate([top, bot], axis=-2)


def _pair_unblock(s):
    b, p, n2, _ = s.shape
    n = n2 // 2
    return jnp.stack([s[:, :, :n, :n], s[:, :, n:, n:]], axis=2).reshape(b, 2 * p, n, n)


def _rwkv(pa, shift0, s0, wts, L):
    b, t, pw = pa.shape
    a_dim = H_A * N_A
    nb, tb = _mixer_block(b, t, L)
    n_pairs = a_dim // LANES
    seg = _iota((MXU_DIM, MXU_DIM), 0) // N_A == _iota((MXU_DIM, MXU_DIM), 1) // N_A
    bd = seg.astype(BF16)
    z64 = jnp.zeros((N_A, a_dim), F32)
    w2p = jnp.concatenate([wts['w2'], z64], axis=0)
    a2p = jnp.concatenate([z64, wts['a2']], axis=0)
    vec = lambda x: x.reshape(1, -1).astype(F32)
    params = [vec(wts['mu']), vec(wts['w0']), w2p.astype(BF16), vec(wts['a0']), a2p.astype(BF16),
              wts['g2'].astype(BF16), vec(wts['kk']), vec(wts['ka']), vec(wts['rk']), vec(wts['ln_w']),
              vec(wts['ln_b']), bd]
    blk_t = lambda w: pl.BlockSpec((nb, tb, w), lambda i, j: (i, j, 0))
    per_b = lambda shape: pl.BlockSpec((nb,) + shape, lambda i, j: (i,) + (0,) * len(shape))
    y, shift, s_new = pl.pallas_call(
        functools.partial(_rwkv_body, nb=nb, tb=tb, L=L),
        out_shape=[jax.ShapeDtypeStruct((b, t, a_dim), BF16), jax.ShapeDtypeStruct((b, 1, pw), F32),
                   jax.ShapeDtypeStruct((b, n_pairs, LANES, LANES), F32)],
        grid=(b // nb, t // tb),
        in_specs=[blk_t(pw), per_b((1, pw)), per_b((n_pairs, LANES, LANES))] + [_resident(p.shape) for p in params],
        out_specs=[blk_t(a_dim), per_b((1, pw)), per_b((n_pairs, LANES, LANES))],
        scratch_shapes=[pltpu.VMEM((nb * tb, a_dim), F32)],
        compiler_params=_cparams(("parallel", "arbitrary")),
        name="rwkv7",
    )(pa, shift0, _pair_blockdiag(s0.astype(F32)), *params)
    return y, shift, _pair_unblock(s_new)


def _even_in_body(x_ref, xh_ref, conv0_ref, nw_ref, wa_ref, wqkv_ref, wbab_ref, wz_ref, cw_ref,
                  pa_ref, q_ref, k_ref, v_ref, bab_ref, z_ref, tail_ref, cat_scr, *, tm):
    j = pl.program_id(1)
    hd = DK_B
    inner = H_B * hd
    nw = nw_ref[...]
    h = _rms(x_ref[...], nw).astype(BF16)
    cat_scr[CONV_CARRY:, :] = jnp.dot(h, wqkv_ref[...], preferred_element_type=F32)
    halo = jnp.dot(_rms(xh_ref[...], nw).astype(BF16), wqkv_ref[...], preferred_element_type=F32)
    cat_scr[0:CONV_CARRY, :] = jnp.where(j == 0, conv0_ref[...], halo)
    tail_ref[...] = cat_scr[tm:, :]
    cw = cw_ref[...]

    rows = min(tm, CONV_ROWS)

    def conv_piece(r0, c0):
        cols = slice(c0, c0 + hd)
        cat = cat_scr[r0:r0 + rows + CONV_CARRY, cols]
        acc = cat * cw[0:1, cols]
        for t in range(1, CONV_W):
            acc = pltpu.roll(acc, 1, 0) + cat * cw[t:t + 1, cols]
        return _silu(acc[CONV_CARRY:, :])

    def l2n(z):
        return z * lax.rsqrt(jnp.sum(z * z, axis=-1, keepdims=True) + EPS)

    pa_ref[...] = jnp.dot(h, wa_ref[...], preferred_element_type=F32)
    z_ref[...] = jnp.dot(h, wz_ref[...], preferred_element_type=F32)
    bab_ref[...] = jnp.dot(h, wbab_ref[...], preferred_element_type=F32)
    for r0 in range(0, tm, rows):
        rs = slice(r0, r0 + rows)
        for i in range(H_B):
            sl = slice(i * hd, (i + 1) * hd)
            q_ref[rs, sl] = (l2n(conv_piece(r0, i * hd)) * hd ** -0.5).astype(BF16)
            k_ref[rs, sl] = l2n(conv_piece(r0, inner + i * hd)).astype(BF16)
            v_ref[rs, sl] = conv_piece(r0, 2 * inner + i * hd).astype(BF16)


def _even_in_proj(x, nw, ws, conv0, conv_w):
    b, t, d = x.shape
    wa, wqkv, wbab, wz = ws
    cdim = wqkv.shape[1]
    inner = H_B * DK_B
    tm = _row_tile(t, 512)
    conv0p = jnp.pad(conv0.astype(F32), ((0, 0), (CONV_CARRY - (CONV_W - 1), 0), (0, 0)))
    row = lambda w: pl.BlockSpec((None, tm, w), lambda i, j: (i, j, 0))
    halo = pl.BlockSpec((None, CONV_CARRY, d), lambda i, j: (i, jnp.maximum(j * (tm // CONV_CARRY) - 1, 0), 0))
    per_b = pl.BlockSpec((None, CONV_CARRY, cdim), lambda i, j: (i, 0, 0))
    outs = pl.pallas_call(
        functools.partial(_even_in_body, tm=tm),
        out_shape=[jax.ShapeDtypeStruct((b, t, wa.shape[1]), F32)]
        + [jax.ShapeDtypeStruct((b, t, inner), BF16)] * 3
        + [jax.ShapeDtypeStruct((b, t, wbab.shape[1]), F32), jax.ShapeDtypeStruct((b, t, wz.shape[1]), F32),
           jax.ShapeDtypeStruct((b, CONV_CARRY, cdim), F32)],
        grid=(b, t // tm),
        in_specs=[row(d), halo, per_b, _resident((1, d)), _resident(wa.shape), _resident(wqkv.shape),
                  _resident(wbab.shape), _resident(wz.shape), _resident(conv_w.shape)],
        out_specs=[row(wa.shape[1]), row(inner), row(inner), row(inner), row(wbab.shape[1]), row(wz.shape[1]), per_b],
        scratch_shapes=[pltpu.VMEM((tm + CONV_CARRY, cdim), F32)],
        compiler_params=_cparams(("parallel", "arbitrary")),
        name="even_in_proj",
    )(x, x, conv0p, nw.reshape(1, d), wa, wqkv, wbab, wz, conv_w.astype(F32))
    return outs[:6], outs[6][:, CONV_CARRY - (CONV_W - 1):]


def _delta_body(q_ref, k_ref, v_ref, bab_ref, z_ref, s0_ref, alog_ref, dtb_ref, nw_ref,
                y_ref, s_ref, *, nb, tb, L):
    t_idx = pl.program_id(1)
    hd = DK_B
    rows = nb * tb
    per_seg = tb // L

    @pl.when(t_idx == 0)
    def _():
        s_ref[...] = s0_ref[...]

    flat = lambda ref: jnp.concatenate([ref[b] for b in range(nb)], axis=0)
    bab = flat(bab_ref)
    q_all, k_all, v_all = flat(q_ref), flat(k_ref), flat(v_ref)
    g_all = -jnp.exp(alog_ref[...]) * _softplus(bab + dtb_ref[...])
    beta_all = _sigmoid(bab)

    tri = _tri_incl(L).astype(BF16)
    rw = _iota((L, L), 0)
    cl_ = _iota((L, L), 1)
    strict = cl_ < rw
    incl = cl_ <= rw
    eye = (rw == cl_).astype(F32)
    nw = nw_ref[...]
    z = flat(z_ref)
    n_chunks = rows // L

    g_cum_all = _mm_exact_lhs01(_chunk_tri(rows, L), g_all)
    g_tot_all = _chunk_last(g_cum_all, L)

    qh, kh, vh, kdec, qg, bv_, bk_, gb, beta_b, e_last = {}, {}, {}, {}, {}, {}, {}, {}, {}, {}
    for h in range(H_B):
        qh[h] = q_all[:, h * hd:(h + 1) * hd]
        kh[h] = k_all[:, h * hd:(h + 1) * hd]
        qs, ks = qh[h].astype(F32), kh[h].astype(F32)
        vs = v_all[:, h * hd:(h + 1) * hd].astype(F32)
        gb[h] = jnp.broadcast_to(g_all[:, h:h + 1], (rows, LANES))
        g_cum = jnp.broadcast_to(g_cum_all[:, h:h + 1], (rows, LANES))
        g_tot = jnp.broadcast_to(g_tot_all[:, h:h + 1], (rows, LANES))
        bb = jnp.broadcast_to(beta_all[:, H_B + h:H_B + h + 1], (rows, LANES))
        e_g = jnp.exp(g_cum)
        kdec[h] = (ks * jnp.exp(g_tot - g_cum)).astype(BF16)
        qg[h] = qs * e_g
        bv_[h] = bb * vs
        bk_[h] = bb * e_g * ks
        beta_b[h] = bb
        e_last[h] = jnp.exp(g_tot)

    items = [(c, h) for c in range(n_chunks) for h in range(H_B)]
    rs = {c: slice(c * L, (c + 1) * L) for c in range(n_chunks)}
    diff = {}
    for c, h in items:
        hi, mid, lo = _split3(jnp.where(strict, gb[h][rs[c], :L], 0.0))
        diff[(c, h)] = (jnp.dot(tri, hi, preferred_element_type=F32) + jnp.dot(tri, mid, preferred_element_type=F32)
                        + jnp.dot(tri, lo, preferred_element_type=F32))
    kk_ = {(c, h): _mm_nt(kh[h][rs[c]], kh[h][rs[c]]) for c, h in items}
    qk_ = {(c, h): _mm_nt(qh[h][rs[c]], kh[h][rs[c]]) for c, h in items}
    n_mat, qk_m = {}, {}
    for c, h in items:
        e_diff = jnp.exp(jnp.where(incl, diff[(c, h)], 0.0))
        n_mat[(c, h)] = -(beta_b[h][rs[c], :L] * jnp.where(strict, kk_[(c, h)] * e_diff, 0.0))
        qk_m[(c, h)] = jnp.where(incl, qk_[(c, h)] * e_diff, 0.0).astype(BF16)
    t_inv = {it: eye + n_mat[it] for it in items}
    m_pow = n_mat
    power = 2
    while power < L:
        m_pow = {it: _mm(m_pow[it], m_pow[it]) for it in items}
        t_inv = {it: t_inv[it] + _mm(t_inv[it], m_pow[it]) for it in items}
        power *= 2
    sol = {(c, h): _mm(t_inv[(c, h)], jnp.concatenate([bv_[h][rs[c]], bk_[h][rs[c]]], axis=1)) for c, h in items}
    lhs = {(c, h): jnp.concatenate([sol[(c, h)][:, hd:], qg[h][rs[c]]], axis=0).astype(BF16) for c, h in items}
    chains = [(b, h) for b in range(nb) for h in range(H_B)]
    state = {bh: s_ref[bh[0], bh[1]] for bh in chains}
    u_, o_inter = {}, {}
    for k_pos in range(per_seg):
        for b, h in chains:
            c = b * per_seg + k_pos
            ws = _mm(lhs[(c, h)], state[(b, h)])
            u_[(c, h)] = (sol[(c, h)][:, :hd] - ws[:L]).astype(BF16)
            o_inter[(c, h)] = ws[L:]
        for b, h in chains:
            c = b * per_seg + k_pos
            state[(b, h)] = e_last[h][c * L:c * L + 1, :] * state[(b, h)] + _mm_tn(kdec[h][rs[c]], u_[(c, h)])
    for b, h in chains:
        s_ref[b, h] = state[(b, h)]
    for c, h in items:
        o = o_inter[(c, h)] + _mm(qk_m[(c, h)], u_[(c, h)])
        on = o * lax.rsqrt(jnp.mean(o * o, axis=-1, keepdims=True) + EPS) * nw
        k_pos = c % per_seg
        y_ref[c // per_seg, k_pos * L:(k_pos + 1) * L, h * hd:(h + 1) * hd] = (
            on * _silu(z[rs[c], h * hd:(h + 1) * hd])).astype(y_ref.dtype)


def _delta(q, k, v, bab, z, s0, wts, L):
    b, t, inner = q.shape
    nb, tb = _mixer_block(b, t, L)
    pad = lambda x: jnp.pad(x.reshape(1, -1).astype(F32), ((0, 0), (0, LANES - x.size)))
    params = [pad(wts['A_log']), pad(wts['dt_bias']), wts['norm_w'].reshape(1, -1).astype(F32)]
    blk = lambda w: pl.BlockSpec((nb, tb, w), lambda i, j: (i, j, 0))
    st = pl.BlockSpec((nb, H_B, DK_B, DK_B), lambda i, j: (i, 0, 0, 0))
    return pl.pallas_call(
        functools.partial(_delta_body, nb=nb, tb=tb, L=L),
        out_shape=[jax.ShapeDtypeStruct((b, t, inner), BF16), jax.ShapeDtypeStruct((b, H_B, DK_B, DK_B), F32)],
        grid=(b // nb, t // tb),
        in_specs=[blk(inner), blk(inner), blk(inner), blk(LANES), blk(inner), st] + [_resident(p.shape) for p in params],
        out_specs=[blk(inner), st],
        compiler_params=_cparams(("parallel", "arbitrary")),
        name="gated_delta",
    )(q, k, v, bab, z, s0.astype(F32), *params)


def _gla_body(qk_ref, v_ref, ad_ref, g_ref, s0_ref, a2_ref, ab_ref, nw_ref, y_ref, s_ref, *, tb, L):
    t_idx = pl.program_id(1)
    kdim = H_C * DK_C
    n_pairs = kdim // LANES

    @pl.when(t_idx == 0)
    def _():
        s_ref[...] = s0_ref[...]

    qk = qk_ref[...]
    log_a = -_softplus(-(_mm(ad_ref[...], a2_ref[...]) + ab_ref[...])) * (1.0 / GLA_NORMALIZER)
    vv = v_ref[...]
    gg = g_ref[...]
    nw = nw_ref[...]
    incl = _iota((L, L), 1) <= _iota((L, L), 0)
    m0 = _iota((tb, LANES), 1) < DK_C
    n_chunks = tb // L

    cum = _mm_exact_lhs01(_chunk_tri(tb, L), log_a)
    tot = _chunk_last(cum, L)
    q_in = qk[:, :kdim] * DK_C ** -0.5 * jnp.exp(cum)
    k_in = (qk[:, kdim:] * jnp.exp(-cum)).astype(BF16)
    k_out = qk[:, kdim:] * jnp.exp(tot - cum)
    dec = jnp.exp(tot)
    vb16 = vv.astype(BF16)

    rs = {c: slice(c * L, (c + 1) * L) for c in range(n_chunks)}
    items = [(c, h) for c in range(n_chunks) for h in range(H_C)]
    qm, km = {}, {}
    for h in range(H_C):
        ls = slice((h // 2) * LANES, (h // 2 + 1) * LANES)
        mask = m0 if h % 2 == 0 else jnp.logical_not(m0)
        qm[h] = jnp.where(mask, q_in[:, ls], 0.0).astype(BF16)
        km[h] = jnp.where(mask, k_out[:, ls], 0.0).astype(BF16)
    scores = {(c, h): jnp.where(incl, _mm_nt(qm[h][rs[c]], k_in[rs[c], (h // 2) * LANES:(h // 2 + 1) * LANES]), 0.0)
              .astype(BF16) for c, h in items}
    upd = {(c, h): _mm_tn(vb16[rs[c], h * DV_C:(h + 1) * DV_C], km[h][rs[c]]) for c, h in items}
    state = [s_ref[p] for p in range(n_pairs)]
    s_prev = {}
    for c in range(n_chunks):
        for p in range(n_pairs):
            s_prev[(c, p)] = state[p].astype(BF16)
            state[p] = (state[p] * dec[c * L:c * L + 1, p * LANES:(p + 1) * LANES]
                        + upd[(c, 2 * p)] + upd[(c, 2 * p + 1)])
    for p in range(n_pairs):
        s_ref[p] = state[p]
    for c, h in items:
        hs = slice(h * DV_C, (h + 1) * DV_C)
        o = _mm_nt(qm[h][rs[c]], s_prev[(c, h // 2)]) + _mm(scores[(c, h)], vb16[rs[c], hs])
        on = o * lax.rsqrt(jnp.mean(o * o, axis=-1, keepdims=True) + EPS) * nw
        y_ref[rs[c], hs] = (on * _silu(gg[rs[c], hs])).astype(y_ref.dtype)


def _gla(cqk, cv, cad, cg, s0, wts, L):
    b, t, _ = cqk.shape
    kdim = H_C * DK_C
    vdim = H_C * DV_C
    n_pairs = kdim // LANES
    tb = min(t, 256)
    rank = wts['a2'].shape[0]
    a2p = jnp.pad(wts['a2'].astype(F32), ((0, LANES - rank), (0, 0))).astype(BF16)
    params = [a2p, wts['a_bias'].reshape(1, -1).astype(F32), wts['norm_w'].reshape(1, -1).astype(F32)]
    st0 = jnp.swapaxes(s0.astype(F32), -1, -2).reshape(b, n_pairs, 2, DV_C, DK_C)
    st0 = jnp.concatenate([st0[:, :, 0], st0[:, :, 1]], axis=-1)
    blk = lambda w: pl.BlockSpec((None, tb, w), lambda i, j: (i, j, 0))
    per_b = lambda shape: pl.BlockSpec((None,) + shape, lambda i, j: (i,) + (0,) * len(shape))
    y, st = pl.pallas_call(
        functools.partial(_gla_body, tb=tb, L=L),
        out_shape=[jax.ShapeDtypeStruct((b, t, vdim), BF16), jax.ShapeDtypeStruct((b, n_pairs, DV_C, LANES), F32)],
        grid=(b, t // tb),
        in_specs=[blk(2 * kdim), blk(vdim), blk(LANES), blk(vdim), per_b((n_pairs, DV_C, LANES))]
        + [_resident(p.shape) for p in params],
        out_specs=[blk(vdim), per_b((n_pairs, DV_C, LANES))],
        compiler_params=_cparams(("parallel", "arbitrary")),
        name="gla",
    )(cqk, cv, cad, cg, st0, *params)
    st = jnp.stack([st[..., :DK_C], st[..., DK_C:]], axis=2).reshape(b, H_C, DV_C, DK_C)
    return y, jnp.swapaxes(st, -1, -2)


def _ret_body(q_ref, k_ref, v_ref, g_ref, cos_ref, sin_ref, s0_ref, y_ref, s_ref, *, tb, L):
    t_idx = pl.program_id(1)
    hd = DK_D

    @pl.when(t_idx == 0)
    def _():
        s_ref[...] = s0_ref[...]

    cos = cos_ref[...]
    sin = sin_ref[...]
    qq, kk, vv, gg = q_ref[...], k_ref[...], v_ref[...], g_ref[...]
    rw = _iota((L, L), 0)
    cl_ = _iota((L, L), 1)
    incl = cl_ <= rw
    dist = jnp.where(incl, rw - cl_, 0).astype(F32)
    pos = _iota((L, LANES), 0).astype(F32)

    n_chunks = tb // L
    rs = {c: slice(c * L, (c + 1) * L) for c in range(n_chunks)}
    items = [(c, h) for c in range(n_chunks) for h in range(H_D)]
    lg = [math.log(1.0 - 2.0 ** (-5.0 - h)) for h in range(H_D)]
    vb16 = vv.astype(BF16)
    qh, kh, q_in, k_out, dmat = {}, {}, {}, {}, {}
    for h in range(H_D):
        hs = slice(h * hd, (h + 1) * hd)
        q_rot = qq[:, hs] * cos + pltpu.roll(qq[:, hs], hd // 2, 1) * sin
        k_rot = (kk[:, hs] * cos + pltpu.roll(kk[:, hs], hd // 2, 1) * sin) * hd ** -0.5
        dmat[h] = jnp.where(incl, jnp.exp(dist * lg[h]), 0.0)
        e_in = jnp.exp((pos + 1.0) * lg[h])
        e_out = jnp.exp((L - 1.0 - pos) * lg[h])
        qh[h], kh[h] = q_rot.astype(BF16), k_rot.astype(BF16)
        for c in range(n_chunks):
            q_in[(c, h)] = (q_rot[rs[c]] * e_in).astype(BF16)
            k_out[(c, h)] = (k_rot[rs[c]] * e_out).astype(BF16)
    scores = {(c, h): (_mm_nt(qh[h][rs[c]], kh[h][rs[c]]) * dmat[h]).astype(BF16) for c, h in items}
    upd = {(c, h): _mm_tn(k_out[(c, h)], vb16[rs[c], h * hd:(h + 1) * hd]) for c, h in items}
    state = [s_ref[h] for h in range(H_D)]
    s_prev = {}
    for c in range(n_chunks):
        for h in range(H_D):
            s_prev[(c, h)] = state[h].astype(BF16)
            state[h] = state[h] * math.exp(lg[h] * L) + upd[(c, h)]
    for h in range(H_D):
        s_ref[h] = state[h]
    for c, h in items:
        hs = slice(h * hd, (h + 1) * hd)
        o = _mm(q_in[(c, h)], s_prev[(c, h)]) + _mm(scores[(c, h)], vb16[rs[c], hs])
        mu = jnp.mean(o, axis=-1, keepdims=True)
        oc = o - mu
        on = oc * lax.rsqrt(jnp.mean(oc * oc, axis=-1, keepdims=True) + EPS)
        y_ref[rs[c], hs] = (on * _silu(gg[rs[c], hs])).astype(y_ref.dtype)


def _retention(dq, dk, dv, dg, s0, pos0, L):
    b, t, dim = dq.shape
    tb = min(t, 256)
    half = DK_D // 2
    inv = ROPE_BASE ** (-jnp.arange(0, DK_D, 2, dtype=F32) / DK_D)
    ang = (jnp.arange(t) + pos0).astype(F32)[:, None] * inv[None, :]
    cos = jnp.concatenate([jnp.cos(ang), jnp.cos(ang)], axis=-1)
    sin = jnp.concatenate([-jnp.sin(ang), jnp.sin(ang)], axis=-1)
    del half
    blk = pl.BlockSpec((None, tb, dim), lambda i, j: (i, j, 0))
    tab = pl.BlockSpec((tb, DK_D), lambda i, j: (j, 0))
    st = pl.BlockSpec((None, H_D, DK_D, DK_D), lambda i, j: (i, 0, 0, 0))
    return pl.pallas_call(
        functools.partial(_ret_body, tb=tb, L=L),
        out_shape=[jax.ShapeDtypeStruct((b, t, dim), BF16), jax.ShapeDtypeStruct((b, H_D, DK_D, DK_D), F32)],
        grid=(b, t // tb),
        in_specs=[blk, blk, blk, blk, tab, tab, st],
        out_specs=[blk, st],
        compiler_params=_cparams(("parallel", "arbitrary")),
        name="retention",
    )(dq, dk, dv, dg, cos, sin, s0.astype(F32))


def _prep_weights(W):
    bf = lambda x: x.astype(BF16)
    a_proj = 3 * H_A * N_A + 64 + 64 + 128
    b_conv = 3 * H_B * DK_B
    P = {}
    for name in ('ffn1_wg', 'ffn1_wu', 'ffn1_wd', 'ffn2_wg', 'ffn2_wu', 'ffn2_wd', 'mem_wq', 'mem_wo', 'mem_wk',
                 'mem_wv', 'even_w_out', 'odd_w_out'):
        P[name] = bf(W[name])
    ew = W['even_w_in']
    o = a_proj
    bab = ew[:, :, o + b_conv:o + b_conv + 2 * H_B]
    P['even_in'] = [bf(ew[:, :, :a_proj]), bf(ew[:, :, o:o + b_conv]),
                    bf(jnp.pad(bab, ((0, 0), (0, 0), (0, LANES - 2 * H_B)))), bf(ew[:, :, o + b_conv + 2 * H_B:])]
    ow = W['odd_w_in']
    kd, vd, rank = H_C * DK_C, H_C * DV_C, W['gla_a2'].shape[1]
    c_proj = 2 * kd + 2 * vd + rank
    dd = H_D * DK_D
    cad = ow[:, :, 2 * kd + vd:2 * kd + vd + rank]
    P['odd_in'] = [bf(ow[:, :, :2 * kd]), bf(ow[:, :, 2 * kd:2 * kd + vd]),
                   bf(jnp.pad(cad, ((0, 0), (0, 0), (0, LANES - rank)))), bf(ow[:, :, 2 * kd + vd + rank:c_proj])] \
        + [bf(ow[:, :, c_proj + j * dd:c_proj + (j + 1) * dd]) for j in range(4)]
    return P


def _trunk(x, pos0, mem_k, mem_v, heads, shift, rwkv, conv, delta, gla, ret, W, P):
    b, t, d = x.shape
    n = b * t
    depth = W['norm_ffn1'].shape[0]
    L_delta = min(64, t)
    L_lin = min(64, t)
    flat = lambda z: z.reshape(n, z.shape[-1])
    unflat = lambda z: z.reshape(b, t, z.shape[-1])
    new = {k: [] for k in ('shift', 'rwkv', 'conv', 'delta', 'gla', 'ret')}
    x = flat(x)
    for l in range(depth):
        i = l // 2
        x = _ffn(x, W['norm_ffn1'][l], P['ffn1_wg'][l], P['ffn1_wu'][l], P['ffn1_wd'][l])
        if l % 2 == 0:
            (pa, dq_, dk_, dv_, bab, z), s3 = _even_in_proj(unflat(x), W['norm_mix'][l], [w[i] for w in P['even_in']],
                                                           conv[i], W['delta_conv_w'][i])
            wa = dict(mu=W['rwkv_mu'][i], w0=W['rwkv_w0'][i], w2=W['rwkv_w2'][i], a0=W['rwkv_a0'][i],
                      a2=W['rwkv_a2'][i], g2=W['rwkv_g2'][i], kk=W['rwkv_kk'][i], ka=W['rwkv_ka'][i],
                      rk=W['rwkv_rk'][i], ln_w=W['rwkv_ln_w'][i], ln_b=W['rwkv_ln_b'][i])
            y_a, s1, s2 = _rwkv(pa, shift[i], rwkv[i], wa, L_delta)
            wb = dict(A_log=W['delta_A_log'][i], dt_bias=W['delta_dt_bias'][i], norm_w=W['delta_norm_w'][i])
            y_b, s4 = _delta(dq_, dk_, dv_, bab, z, delta[i], wb, L_delta)
            new['shift'].append(s1)
            new['rwkv'].append(s2)
            new['conv'].append(s3)
            new['delta'].append(s4)
            split = H_A * N_A
            halves, w_out = (y_a, y_b), P['even_w_out'][i]
        else:
            cqk, cv, cad, cg, dq, dk, dv, dg = _norm_proj(x, W['norm_mix'][l], [w[i] for w in P['odd_in']],
                                                          bf16_out=(1, 6))
            wc = dict(a2=W['gla_a2'][i], a_bias=W['gla_a_bias'][i], norm_w=W['gla_norm_w'][i])
            y_c, s5 = _gla(unflat(cqk), unflat(cv), unflat(cad), unflat(cg), gla[i], wc, L_lin)
            y_d, s6 = _retention(unflat(dq), unflat(dk), unflat(dv), unflat(dg), ret[i], pos0, L_lin)
            new['gla'].append(s5)
            new['ret'].append(s6)
            split = H_C * DV_C
            halves, w_out = (y_c, y_d), P['odd_w_out'][i]
        mk = mem_k[l].astype(BF16)
        mv = mem_v[l].astype(BF16)
        x = flat(_mix_out_mem_attn(unflat(x), halves[0], halves[1], w_out[:split], w_out[split:], W['norm_mem'][l],
                                   P['mem_wq'][l], mk, mv, P['mem_wo'][l], heads))
        fw = W['final_norm'] if l == depth - 1 else None
        x = _ffn(x, W['norm_ffn2'][l], P['ffn2_wg'][l], P['ffn2_wu'][l], P['ffn2_wd'][l], fw)
    return unflat(x), new


def kernel(x_prompt, x_sample, mem_prompt, state_rwkv_shift, state_rwkv, state_delta_conv, state_delta, state_gla, state_ret, cache_mem_k, cache_mem_v, norm_ffn1, ffn1_wg, ffn1_wu, ffn1_wd, norm_mix, even_w_in, even_w_out, rwkv_mu, rwkv_w0, rwkv_w2, rwkv_a0, rwkv_a2, rwkv_g2, rwkv_kk, rwkv_ka, rwkv_rk, rwkv_ln_w, rwkv_ln_b, delta_conv_w, delta_A_log, delta_dt_bias, delta_norm_w, odd_w_in, odd_w_out, gla_a2, gla_a_bias, gla_norm_w, norm_mem, mem_norm_kv, mem_wq, mem_wk, mem_wv, mem_wo, norm_ffn2, ffn2_wg, ffn2_wu, ffn2_wd, final_norm):
    W = dict(norm_ffn1=norm_ffn1, ffn1_wg=ffn1_wg, ffn1_wu=ffn1_wu, ffn1_wd=ffn1_wd, norm_mix=norm_mix,
             even_w_in=even_w_in, even_w_out=even_w_out, rwkv_mu=rwkv_mu, rwkv_w0=rwkv_w0, rwkv_w2=rwkv_w2,
             rwkv_a0=rwkv_a0, rwkv_a2=rwkv_a2, rwkv_g2=rwkv_g2, rwkv_kk=rwkv_kk, rwkv_ka=rwkv_ka,
             rwkv_rk=rwkv_rk, rwkv_ln_w=rwkv_ln_w, rwkv_ln_b=rwkv_ln_b, delta_conv_w=delta_conv_w,
             delta_A_log=delta_A_log, delta_dt_bias=delta_dt_bias, delta_norm_w=delta_norm_w,
             odd_w_in=odd_w_in, odd_w_out=odd_w_out, gla_a2=gla_a2, gla_a_bias=gla_a_bias,
             gla_norm_w=gla_norm_w, norm_mem=norm_mem, mem_wq=mem_wq, mem_wk=mem_wk, mem_wv=mem_wv,
             mem_wo=mem_wo, norm_ffn2=norm_ffn2, ffn2_wg=ffn2_wg, ffn2_wu=ffn2_wu, ffn2_wd=ffn2_wd,
             final_norm=final_norm)
    P = _prep_weights(W)
    dt = x_prompt.dtype
    bp, _, d = x_prompt.shape
    depth = norm_ffn1.shape[0]
    n_even, n_odd = (depth + 1) // 2, depth // 2
    heads, hd = cache_mem_k.shape[3], cache_mem_k.shape[4]
    n_mem = mem_prompt.shape[1]
    pk, pv, pk_flat, pv_flat = [], [], [], []
    mem_flat = mem_prompt.reshape(bp * n_mem, d)
    for l in range(depth):
        mk, mv = _norm_proj(mem_flat, mem_norm_kv[l], [P['mem_wk'][l], P['mem_wv'][l]])
        pk_flat.append(mk.reshape(bp, n_mem, d))
        pv_flat.append(mv.reshape(bp, n_mem, d))
        pk.append(mk.reshape(bp, n_mem, heads, hd))
        pv.append(mv.reshape(bp, n_mem, heads, hd))
    zeros = lambda ref, cnt: [jnp.zeros((bp,) + ref.shape[2:], F32)] * cnt
    y_prompt, ps = _trunk(x_prompt, 0, pk_flat, pv_flat, heads, zeros(state_rwkv_shift, n_even), zeros(state_rwkv, n_even),
                          zeros(state_delta_conv, n_even), zeros(state_delta, n_even), zeros(state_gla, n_odd),
                          zeros(state_ret, n_odd), W, P)
    bs = x_sample.shape[0]
    y_sample, ss = _trunk(x_sample, PAST_LEN, [cache_mem_k[l].reshape(bs, n_mem, d) for l in range(depth)],
                          [cache_mem_v[l].reshape(bs, n_mem, d) for l in range(depth)], heads,
                          [state_rwkv_shift[i] for i in range(n_even)], [state_rwkv[i] for i in range(n_even)],
                          [state_delta_conv[i] for i in range(n_even)], [state_delta[i] for i in range(n_even)],
                          [state_gla[i] for i in range(n_odd)], [state_ret[i] for i in range(n_odd)], W, P)
    st = lambda xs: jnp.stack(xs).astype(dt)
    order = ('shift', 'rwkv', 'conv', 'delta', 'gla', 'ret')
    return ((y_prompt, y_sample) + tuple(st(ps[k]) for k in order) + (st(pk), st(pv))
            + tuple(st(ss[k]) for k in order))
```

```python
import functools
import math

import jax
import jax.numpy as jnp
from jax import lax
from jax.experimental import pallas as pl
from jax.experimental.pallas import tpu as pltpu

F32 = jnp.float32
BF16 = jnp.bfloat16
EPS = 1e-6
RWKV_LN_EPS = 64e-5
GLA_NORMALIZER = 16.0
ROPE_BASE = 10000.0
PAST_LEN = 1024

LANES = 128
MXU_DIM = 256
VMEM_LIMIT = 56 * 1024 * 1024

H_A, N_A = 8, 64
H_B, DK_B = 4, 128
H_C, DK_C, DV_C = 4, 64, 128
H_D, DK_D = 4, 128
CONV_W = 4
CONV_CARRY = 8
MIXER_ROWS = 512
CONV_ROWS = 128


def _cparams(sem):
    return pltpu.CompilerParams(dimension_semantics=sem, vmem_limit_bytes=VMEM_LIMIT)


def _mm(a, b):
    return jnp.dot(a.astype(BF16), b.astype(BF16), preferred_element_type=F32)


def _mm_nt(a, b):
    return lax.dot_general(a.astype(BF16), b.astype(BF16), (((1,), (1,)), ((), ())),
                           preferred_element_type=F32)


def _mm_tn(a, b):
    return lax.dot_general(a.astype(BF16), b.astype(BF16), (((0,), (0,)), ((), ())),
                           preferred_element_type=F32)


def _split3(x):
    hi = x.astype(BF16)
    r = x - hi.astype(F32)
    mid = r.astype(BF16)
    lo = (r - mid.astype(F32)).astype(BF16)
    return hi, mid, lo


def _mm_exact_lhs01(a01, x):
    hi, mid, lo = _split3(x)
    a = a01.astype(BF16)
    return (jnp.dot(a, hi, preferred_element_type=F32) + jnp.dot(a, mid, preferred_element_type=F32)
            + jnp.dot(a, lo, preferred_element_type=F32))


def _iota(shape, dim):
    return lax.broadcasted_iota(jnp.int32, shape, dim)


def _tri_incl(n):
    return (_iota((n, n), 1) <= _iota((n, n), 0)).astype(F32)


def _chunk_tri(tb, L):
    shift = L.bit_length() - 1
    assert 1 << shift == L
    rt, ct = _iota((tb, tb), 0), _iota((tb, tb), 1)
    same = jnp.right_shift(rt, shift) == jnp.right_shift(ct, shift)
    return jnp.logical_and(same, ct <= rt).astype(F32)


def _chunk_cumsum(x, L):
    n = x.shape[0]
    g = min(n, MXU_DIM)
    tri = _chunk_tri(g, L)
    return jnp.concatenate([_mm_exact_lhs01(tri, x[r:r + g]) for r in range(0, n, g)], axis=0)


def _chunk_last(cum, L):
    tb, w = cum.shape
    return jnp.concatenate([jnp.broadcast_to(cum[c * L + L - 1:c * L + L, :], (L, w)) for c in range(tb // L)], axis=0)


def _sigmoid(x):
    return 0.5 * (jnp.tanh(0.5 * x) + 1.0)


def _silu(x):
    return x * _sigmoid(x)


def _softplus(x):
    return jnp.maximum(x, 0.0) + jnp.log1p(jnp.exp(-jnp.abs(x)))


def _rms(x, w):
    return x * lax.rsqrt(jnp.mean(x * x, axis=-1, keepdims=True) + EPS) * w


def _inv_unit_lower(n_mat, size):
    dim = n_mat.shape[0]
    eye = (_iota((dim, dim), 0) == _iota((dim, dim), 1)).astype(F32)
    t = eye + n_mat
    m = n_mat
    power = 2
    while power < size:
        m = _mm(m, m)
        t = t + _mm(t, m)
        power *= 2
    return t


def _row_tile(n, target):
    t = min(n, target)
    while n % t:
        t //= 2
    return t


def _mixer_block(b, t, L):
    tb = min(t, 2 * L)
    nb = max(1, min(b, MIXER_ROWS // tb))
    while b % nb:
        nb -= 1
    return nb, tb


def _resident(shape):
    nd = len(shape)
    return pl.BlockSpec(shape, lambda *_: (0,) * nd, pipeline_mode=pl.Buffered(1))


def _ffn_body(x_ref, nw_ref, wg_ref, wu_ref, wd_ref, fw_ref, o_ref, *, chunk, final):
    x = x_ref[...]
    h = _rms(x, nw_ref[...]).astype(BF16)
    d_ff = wg_ref.shape[1]
    n_chunks = d_ff // chunk

    def gate_up(c):
        sl = slice(c * chunk, (c + 1) * chunk)
        return (jnp.dot(h, wg_ref[:, sl], preferred_element_type=F32),
                jnp.dot(h, wu_ref[:, sl], preferred_element_type=F32))

    acc = None
    nxt = gate_up(0)
    for c in range(n_chunks):
        g, u = nxt
        if c + 1 < n_chunks:
            nxt = gate_up(c + 1)
        a = (_silu(g) * u).astype(BF16)
        d = jnp.dot(a, wd_ref[c * chunk:(c + 1) * chunk, :], preferred_element_type=F32)
        acc = d if acc is None else acc + d
    y = x + 0.5 * acc
    if final:
        y = _rms(y, fw_ref[...])
    o_ref[...] = y


def _ffn(x, nw, wg, wu, wd, fw=None):
    n, d = x.shape
    d_ff = wg.shape[1]
    tm = _row_tile(n, 512)
    final = fw is not None
    if fw is None:
        fw = nw
    return pl.pallas_call(
        functools.partial(_ffn_body, chunk=MXU_DIM, final=final),
        out_shape=jax.ShapeDtypeStruct((n, d), F32),
        grid=(n // tm,),
        in_specs=[pl.BlockSpec((tm, d), lambda i: (i, 0)), _resident((1, d)), _resident((d, d_ff)),
                  _resident((d, d_ff)), _resident((d_ff, d)), _resident((1, d))],
        out_specs=pl.BlockSpec((tm, d), lambda i: (i, 0)),
        compiler_params=_cparams(("parallel",)),
        name="ffn",
    )(x, nw.reshape(1, d), wg, wu, wd, fw.reshape(1, d))


def _norm_proj_body(*refs, n_w):
    x_ref, nw_ref = refs[0], refs[1]
    w_refs = refs[2:2 + n_w]
    o_refs = refs[2 + n_w:]
    h = _rms(x_ref[...], nw_ref[...]).astype(BF16)
    for w_ref, o_ref in zip(w_refs, o_refs):
        o_ref[...] = jnp.dot(h, w_ref[...], preferred_element_type=F32).astype(o_ref.dtype)


def _norm_proj(x, nw, ws, bf16_out=()):
    n, d = x.shape
    tm = _row_tile(n, 512)
    return pl.pallas_call(
        functools.partial(_norm_proj_body, n_w=len(ws)),
        out_shape=[jax.ShapeDtypeStruct((n, w.shape[1]), BF16 if i in bf16_out else F32) for i, w in enumerate(ws)],
        grid=(n // tm,),
        in_specs=[pl.BlockSpec((tm, d), lambda i: (i, 0)), _resident((1, d))] + [_resident(w.shape) for w in ws],
        out_specs=[pl.BlockSpec((tm, w.shape[1]), lambda i: (i, 0)) for w in ws],
        compiler_params=_cparams(("parallel",)),
        name="norm_proj",
    )(x, nw.reshape(1, d), *ws)


def _mix_out_mem_attn_body(x_ref, ya_ref, yb_ref, wa_ref, wb_ref, nw_ref, wq_ref, k_ref, v_ref, wo_ref, o_ref, *,
                           heads):
    x = (x_ref[...] + jnp.dot(ya_ref[...], wa_ref[...], preferred_element_type=F32)
         + jnp.dot(yb_ref[...], wb_ref[...], preferred_element_type=F32))
    h = _rms(x, nw_ref[...]).astype(BF16)
    q = jnp.dot(h, wq_ref[...], preferred_element_type=F32).astype(BF16)
    d = q.shape[1]
    hd = d // heads

    def scores(i):
        sl = slice(i * hd, (i + 1) * hd)
        return _mm_nt(q[:, sl], k_ref[:, sl]) * hd ** -0.5

    outs = []
    nxt = scores(0)
    for i in range(heads):
        s = nxt
        if i + 1 < heads:
            nxt = scores(i + 1)
        e = jnp.exp(s - jnp.max(s, axis=-1, keepdims=True))
        pr = e / jnp.sum(e, axis=-1, keepdims=True)
        outs.append(_mm(pr, v_ref[:, i * hd:(i + 1) * hd]))
    o = jnp.concatenate(outs, axis=-1).astype(BF16)
    o_ref[...] = x + jnp.dot(o, wo_ref[...], preferred_element_type=F32)


def _mix_out_mem_attn(x, ya, yb, wa, wb, nw, wq, mk, mv, wo, heads):
    b, t, d = x.shape
    m = mk.shape[1]
    tm = _row_tile(t, 512)
    row = lambda w: pl.BlockSpec((None, tm, w), lambda i, j: (i, j, 0))
    mem = pl.BlockSpec((None, m, d), lambda i, j: (i, 0, 0))
    return pl.pallas_call(
        functools.partial(_mix_out_mem_attn_body, heads=heads),
        out_shape=jax.ShapeDtypeStruct((b, t, d), F32),
        grid=(b, t // tm),
        in_specs=[row(d), row(ya.shape[2]), row(yb.shape[2]), _resident(wa.shape), _resident(wb.shape),
                  _resident((1, d)), _resident((d, d)), mem, mem, _resident((d, d))],
        out_specs=row(d),
        compiler_params=_cparams(("parallel", "parallel")),
        name="mix_out_mem_attn",
    )(x, ya, yb, wa, wb, nw.reshape(1, d), wq, mk, mv, wo)


def _seg_sum(x, bd):
    g = bd.shape[0]
    hi = x.astype(BF16)
    lo = (x - hi.astype(F32)).astype(BF16)
    outs = []
    for j in range(x.shape[1] // g):
        sl = slice(j * g, (j + 1) * g)
        outs.append(jnp.dot(hi[:, sl], bd, preferred_element_type=F32) + jnp.dot(lo[:, sl], bd, preferred_element_type=F32))
    return jnp.concatenate(outs, axis=1)


def _rwkv_body(pa_ref, shift0_ref, s0_ref, mu_ref, w0_ref, w2_ref, a0_ref, a2_ref, g2_ref, kkw_ref, ka_ref,
               rk_ref, lnw_ref, lnb_ref, bd_ref, y_ref, shift_ref, s_ref, y_scr, *, nb, tb, L):
    t_idx = pl.program_id(1)
    a_dim = H_A * N_A
    n_pairs = a_dim // LANES

    @pl.when(t_idx == 0)
    def _():
        shift_ref[...] = shift0_ref[...]
        s_ref[...] = s0_ref[...]

    row = _iota((tb, pa_ref.shape[2]), 0)
    segs, prevs = [], []
    for b in range(nb):
        seg = pa_ref[b]
        segs.append(seg)
        prevs.append(jnp.where(row == 0, shift_ref[b], pltpu.roll(seg, 1, 0)))
        shift_ref[b] = seg[tb - 1:tb, :]
    pa = jnp.concatenate(segs, axis=0)
    prev = jnp.concatenate(prevs, axis=0)
    rows = nb * tb
    xa = pa + (prev - pa) * mu_ref[...]
    r = xa[:, 0:a_dim]
    k = xa[:, a_dim:2 * a_dim]
    v = xa[:, 2 * a_dim:3 * a_dim]
    xwa = xa[:, 3 * a_dim:3 * a_dim + LANES]
    xg = xa[:, 3 * a_dim + LANES:]
    wlog = -_softplus(-(w0_ref[...] + _mm(jnp.tanh(xwa), w2_ref[...]))) - 0.5
    logdec = -jnp.exp(wlog)
    a = _sigmoid(a0_ref[...] + _mm(xwa, a2_ref[...]))
    gate = _mm(_sigmoid(xg), g2_ref[...])
    bd = bd_ref[...]
    kq = k * kkw_ref[...]
    kk = kq * lax.rsqrt(_seg_sum(kq * kq, bd) + EPS)
    k2 = k * (1.0 + (a - 1.0) * ka_ref[...])
    av = -kk
    bv = kk * a

    lane = _iota((L, LANES), 1)
    m0 = lane < N_A
    r2 = _iota((2 * L, 2 * L), 0)
    c2 = _iota((2 * L, 2 * L), 1)
    bdm = jnp.logical_not(jnp.logical_xor(r2 >= L, c2 >= L))
    r_loc = jnp.where(r2 >= L, r2 - L, r2)
    c_loc = jnp.where(c2 >= L, c2 - L, c2)
    strict_t = jnp.logical_and(bdm, c_loc > r_loc)
    strict_l = jnp.bitwise_and(_iota((L, 2 * L), 1), L - 1) < _iota((L, 2 * L), 0)
    incl_l = jnp.bitwise_and(_iota((L, 4 * L), 1), L - 1) <= _iota((L, 4 * L), 0)
    bdl = jnp.logical_not(jnp.logical_xor(_iota((2 * L, LANES), 0) >= L, _iota((2 * L, LANES), 1) >= N_A))
    eye2 = (r2 == c2).astype(F32)

    def stack2(x):
        return jnp.concatenate([jnp.where(m0, x, 0.0), jnp.where(m0, 0.0, x)], axis=0).astype(BF16)

    def dup2(x):
        return jnp.concatenate([x, x], axis=0).astype(BF16)

    cum = _chunk_cumsum(logdec, L)
    tot = _chunk_last(cum, L)
    e_neg = jnp.exp(-cum)
    e_out = jnp.exp(tot - cum)
    a_t = av * jnp.exp(cum - logdec)
    r_t = r * jnp.exp(cum)
    b_t = bv * e_neg
    k_t = k2 * e_neg
    b_o = bv * e_out
    k_o = k2 * e_out
    g_l = jnp.exp(tot)

    per_seg = tb // L
    items = [(c, p) for c in range(rows // L) for p in range(n_pairs)]
    sl = {(c, p): (slice(c * L, (c + 1) * L), slice(p * LANES, (p + 1) * LANES)) for c, p in items}
    xa = {it: stack2(a_t[sl[it]]) for it in items}
    ra = {it: r_t[sl[it]].astype(BF16) for it in items}
    yb = {it: dup2(b_t[sl[it]]) for it in items}
    sbk = {it: jnp.concatenate([stack2(b_t[sl[it]]), stack2(k_t[sl[it]])], axis=0) for it in items}
    v2 = {it: jnp.where(bdl, jnp.concatenate([v[sl[it]], v[sl[it]]], axis=0), 0.0).astype(BF16) for it in items}
    wst = {it: jnp.concatenate([stack2(b_o[sl[it]]), stack2(k_o[sl[it]])], axis=0) for it in items}
    n_t = {it: jnp.where(strict_t, _mm_nt(yb[it], xa[it]), 0.0) for it in items}
    ak = {it: jnp.where(strict_l, _mm_nt(a_t[sl[it]], sbk[it][2 * L:]), 0.0).astype(BF16) for it in items}
    rbk = {it: jnp.where(incl_l, _mm_nt(ra[it], sbk[it]), 0.0).astype(BF16) for it in items}
    akv = {it: stack2(_mm(ak[it], v2[it])) for it in items}
    t_t = {it: eye2 + n_t[it] for it in items}
    m_t = {it: _mm(n_t[it], n_t[it]) for it in items}
    power = 4
    while power < L:
        prod = {it: _mm(m_t[it], jnp.concatenate([m_t[it], t_t[it]], axis=1)) for it in items}
        m_t = {it: prod[it][:, :2 * L] for it in items}
        t_t = {it: t_t[it] + prod[it][:, 2 * L:] for it in items}
        power *= 2
    if L > 2:
        t_t = {it: t_t[it] + _mm(m_t[it], t_t[it]) for it in items}
    om_up = {it: _mm_tn(t_t[it], jnp.concatenate([xa[it], akv[it]], axis=1)) for it in items}
    omega = {it: om_up[it][:, :LANES].astype(BF16) for it in items}
    chains = [(b, p) for b in range(nb) for p in range(n_pairs)]
    state = {bp: s_ref[bp[0], bp[1]] for bp in chains}
    s_prev, uv = {}, {}
    for k_pos in range(per_seg):
        for b, p in chains:
            it = (b * per_seg + k_pos, p)
            s_prev[it] = state[(b, p)].astype(BF16)
            u_st = _mm_nt(omega[it], s_prev[it]) + om_up[it][:, LANES:]
            uv[it] = jnp.concatenate([u_st.astype(BF16), v2[it]], axis=0)
        for b, p in chains:
            it = (b * per_seg + k_pos, p)
            state[(b, p)] = state[(b, p)] * g_l[it[0] * L:it[0] * L + 1, sl[it][1]] + _mm_tn(uv[it], wst[it])
    for b, p in chains:
        s_ref[b, p] = state[(b, p)]
    for it in items:
        y_scr[sl[it]] = _mm_nt(ra[it], s_prev[it]) + _mm(rbk[it], uv[it])

    y = y_scr[...]
    inv_n = 1.0 / N_A
    mu_y = _seg_sum(y, bd) * inv_n
    yc = y - mu_y
    var = _seg_sum(yc * yc, bd) * inv_n
    yn = yc * lax.rsqrt(var + RWKV_LN_EPS) * lnw_ref[...] + lnb_ref[...]
    bonus = _seg_sum(r * k2 * rk_ref[...], bd) * v
    out = ((yn + bonus) * gate).astype(y_ref.dtype)
    for b in range(nb):
        y_ref[b] = out[b * tb:(b + 1) * tb, :]


def _pair_blockdiag(s):
    b, h, n, _ = s.shape
    s = s.reshape(b, h // 2, 2, n, n)
    z = jnp.zeros((b, h // 2, n, n), s.dtype)
    top = jnp.concatenate([s[:, :, 0], z], axis=-1)
    bot = jnp.concatenate([z, s[:, :, 1]], axis=-1)
    return jnp.concatenate([top, bot], axis=-2)


def _pair_unblock(s):
    b, p, n2, _ = s.shape
    n = n2 // 2
    return jnp.stack([s[:, :, :n, :n], s[:, :, n:, n:]], axis=2).reshape(b, 2 * p, n, n)


def _rwkv(pa, shift0, s0, wts, L):
    b, t, pw = pa.shape
    a_dim = H_A * N_A
    nb, tb = _mixer_block(b, t, L)
    n_pairs = a_dim // LANES
    seg = _iota((MXU_DIM, MXU_DIM), 0) // N_A == _iota((MXU_DIM, MXU_DIM), 1) // N_A
    bd = seg.astype(BF16)
    z64 = jnp.zeros((N_A, a_dim), F32)
    w2p = jnp.concatenate([wts['w2'], z64], axis=0)
    a2p = jnp.concatenate([z64, wts['a2']], axis=0)
    vec = lambda x: x.reshape(1, -1).astype(F32)
    params = [vec(wts['mu']), vec(wts['w0']), w2p.astype(BF16), vec(wts['a0']), a2p.astype(BF16),
              wts['g2'].astype(BF16), vec(wts['kk']), vec(wts['ka']), vec(wts['rk']), vec(wts['ln_w']),
              vec(wts['ln_b']), bd]
    blk_t = lambda w: pl.BlockSpec((nb, tb, w), lambda i, j: (i, j, 0))
    per_b = lambda shape: pl.BlockSpec((nb,) + shape, lambda i, j: (i,) + (0,) * len(shape))
    y, shift, s_new = pl.pallas_call(
        functools.partial(_rwkv_body, nb=nb, tb=tb, L=L),
        out_shape=[jax.ShapeDtypeStruct((b, t, a_dim), BF16), jax.ShapeDtypeStruct((b, 1, pw), F32),
                   jax.ShapeDtypeStruct((b, n_pairs, LANES, LANES), F32)],
        grid=(b // nb, t // tb),
        in_specs=[blk_t(pw), per_b((1, pw)), per_b((n_pairs, LANES, LANES))] + [_resident(p.shape) for p in params],
        out_specs=[blk_t(a_dim), per_b((1, pw)), per_b((n_pairs, LANES, LANES))],
        scratch_shapes=[pltpu.VMEM((nb * tb, a_dim), F32)],
        compiler_params=_cparams(("parallel", "arbitrary")),
        name="rwkv7",
    )(pa, shift0, _pair_blockdiag(s0.astype(F32)), *params)
    return y, shift, _pair_unblock(s_new)


def _even_in_body(x_ref, xh_ref, conv0_ref, nw_ref, wa_ref, wqkv_ref, wbab_ref, wz_ref, cw_ref,
                  pa_ref, q_ref, k_ref, v_ref, bab_ref, z_ref, tail_ref, cat_scr, *, tm):
    j = pl.program_id(1)
    hd = DK_B
    inner = H_B * hd
    nw = nw_ref[...]
    h = _rms(x_ref[...], nw).astype(BF16)
    cat_scr[CONV_CARRY:, :] = jnp.dot(h, wqkv_ref[...], preferred_element_type=F32)
    halo = jnp.dot(_rms(xh_ref[...], nw).astype(BF16), wqkv_ref[...], preferred_element_type=F32)
    cat_scr[0:CONV_CARRY, :] = jnp.where(j == 0, conv0_ref[...], halo)
    tail_ref[...] = cat_scr[tm:, :]
    cw = cw_ref[...]

    rows = min(tm, CONV_ROWS)

    def conv_piece(r0, c0):
        cols = slice(c0, c0 + hd)
        cat = cat_scr[r0:r0 + rows + CONV_CARRY, cols]
        acc = cat * cw[0:1, cols]
        for t in range(1, CONV_W):
            acc = pltpu.roll(acc, 1, 0) + cat * cw[t:t + 1, cols]
        return _silu(acc[CONV_CARRY:, :])

    def l2n(z):
        return z * lax.rsqrt(jnp.sum(z * z, axis=-1, keepdims=True) + EPS)

    pa_ref[...] = jnp.dot(h, wa_ref[...], preferred_element_type=F32)
    z_ref[...] = jnp.dot(h, wz_ref[...], preferred_element_type=F32)
    bab_ref[...] = jnp.dot(h, wbab_ref[...], preferred_element_type=F32)
    for r0 in range(0, tm, rows):
        rs = slice(r0, r0 + rows)
        for i in range(H_B):
            sl = slice(i * hd, (i + 1) * hd)
            q_ref[rs, sl] = (l2n(conv_piece(r0, i * hd)) * hd ** -0.5).astype(BF16)
            k_ref[rs, sl] = l2n(conv_piece(r0, inner + i * hd)).astype(BF16)
            v_ref[rs, sl] = conv_piece(r0, 2 * inner + i * hd).astype(BF16)


def _even_in_proj(x, nw, ws, conv0, conv_w):
    b, t, d = x.shape
    wa, wqkv, wbab, wz = ws
    cdim = wqkv.shape[1]
    inner = H_B * DK_B
    tm = _row_tile(t, 512)
    conv0p = jnp.pad(conv0.astype(F32), ((0, 0), (CONV_CARRY - (CONV_W - 1), 0), (0, 0)))
    row = lambda w: pl.BlockSpec((None, tm, w), lambda i, j: (i, j, 0))
    halo = pl.BlockSpec((None, CONV_CARRY, d), lambda i, j: (i, jnp.maximum(j * (tm // CONV_CARRY) - 1, 0), 0))
    per_b = pl.BlockSpec((None, CONV_CARRY, cdim), lambda i, j: (i, 0, 0))
    outs = pl.pallas_call(
        functools.partial(_even_in_body, tm=tm),
        out_shape=[jax.ShapeDtypeStruct((b, t, wa.shape[1]), F32)]
        + [jax.ShapeDtypeStruct((b, t, inner), BF16)] * 3
        + [jax.ShapeDtypeStruct((b, t, wbab.shape[1]), F32), jax.ShapeDtypeStruct((b, t, wz.shape[1]), F32),
           jax.ShapeDtypeStruct((b, CONV_CARRY, cdim), F32)],
        grid=(b, t // tm),
        in_specs=[row(d), halo, per_b, _resident((1, d)), _resident(wa.shape), _resident(wqkv.shape),
                  _resident(wbab.shape), _resident(wz.shape), _resident(conv_w.shape)],
        out_specs=[row(wa.shape[1]), row(inner), row(inner), row(inner), row(wbab.shape[1]), row(wz.shape[1]), per_b],
        scratch_shapes=[pltpu.VMEM((tm + CONV_CARRY, cdim), F32)],
        compiler_params=_cparams(("parallel", "arbitrary")),
        name="even_in_proj",
    )(x, x, conv0p, nw.reshape(1, d), wa, wqkv, wbab, wz, conv_w.astype(F32))
    return outs[:6], outs[6][:, CONV_CARRY - (CONV_W - 1):]


def _delta_body(q_ref, k_ref, v_ref, bab_ref, z_ref, s0_ref, alog_ref, dtb_ref, nw_ref,
                y_ref, s_ref, *, nb, tb, L):
    t_idx = pl.program_id(1)
    hd = DK_B
    rows = nb * tb
    per_seg = tb // L

    @pl.when(t_idx == 0)
    def _():
        s_ref[...] = s0_ref[...]

    flat = lambda ref: jnp.concatenate([ref[b] for b in range(nb)], axis=0)
    bab = flat(bab_ref)
    q_all, k_all, v_all = flat(q_ref), flat(k_ref), flat(v_ref)
    g_all = -jnp.exp(alog_ref[...]) * _softplus(bab + dtb_ref[...])
    beta_all = _sigmoid(bab)

    tri = _tri_incl(L).astype(BF16)
    rw = _iota((L, L), 0)
    cl_ = _iota((L, L), 1)
    strict = cl_ < rw
    incl = cl_ <= rw
    eye = (rw == cl_).astype(F32)
    nw = nw_ref[...]
    z = flat(z_ref)
    n_chunks = rows // L

    g_cum_all = _chunk_cumsum(g_all, L)
    g_tot_all = _chunk_last(g_cum_all, L)

    qh, kh, vh, kdec, qg, bv_, bk_, gb, beta_b, e_last = {}, {}, {}, {}, {}, {}, {}, {}, {}, {}
    for h in range(H_B):
        qh[h] = q_all[:, h * hd:(h + 1) * hd]
        kh[h] = k_all[:, h * hd:(h + 1) * hd]
        qs, ks = qh[h].astype(F32), kh[h].astype(F32)
        vs = v_all[:, h * hd:(h + 1) * hd].astype(F32)
        gb[h] = jnp.broadcast_to(g_all[:, h:h + 1], (rows, LANES))
        g_cum = jnp.broadcast_to(g_cum_all[:, h:h + 1], (rows, LANES))
        g_tot = jnp.broadcast_to(g_tot_all[:, h:h + 1], (rows, LANES))
        bb = jnp.broadcast_to(beta_all[:, H_B + h:H_B + h + 1], (rows, LANES))
        e_g = jnp.exp(g_cum)
        kdec[h] = (ks * jnp.exp(g_tot - g_cum)).astype(BF16)
        qg[h] = qs * e_g
        bv_[h] = bb * vs
        bk_[h] = bb * e_g * ks
        beta_b[h] = bb
        e_last[h] = jnp.exp(g_tot)

    items = [(c, h) for c in range(n_chunks) for h in range(H_B)]
    rs = {c: slice(c * L, (c + 1) * L) for c in range(n_chunks)}
    diff = {}
    for c, h in items:
        hi, mid, lo = _split3(jnp.where(strict, gb[h][rs[c], :L], 0.0))
        diff[(c, h)] = (jnp.dot(tri, hi, preferred_element_type=F32) + jnp.dot(tri, mid, preferred_element_type=F32)
                        + jnp.dot(tri, lo, preferred_element_type=F32))
    kk_ = {(c, h): _mm_nt(kh[h][rs[c]], kh[h][rs[c]]) for c, h in items}
    qk_ = {(c, h): _mm_nt(qh[h][rs[c]], kh[h][rs[c]]) for c, h in items}
    n_mat, qk_m = {}, {}
    for c, h in items:
        e_diff = jnp.exp(jnp.where(incl, diff[(c, h)], 0.0))
        n_mat[(c, h)] = -(beta_b[h][rs[c], :L] * jnp.where(strict, kk_[(c, h)] * e_diff, 0.0))
        qk_m[(c, h)] = jnp.where(incl, qk_[(c, h)] * e_diff, 0.0).astype(BF16)
    t_inv = {it: eye + n_mat[it] for it in items}
    m_pow = n_mat
    power = 2
    while power < L:
        m_pow = {it: _mm(m_pow[it], m_pow[it]) for it in items}
        t_inv = {it: t_inv[it] + _mm(t_inv[it], m_pow[it]) for it in items}
        power *= 2
    sol = {(c, h): _mm(t_inv[(c, h)], jnp.concatenate([bv_[h][rs[c]], bk_[h][rs[c]]], axis=1)) for c, h in items}
    lhs = {(c, h): jnp.concatenate([sol[(c, h)][:, hd:], qg[h][rs[c]]], axis=0).astype(BF16) for c, h in items}
    chains = [(b, h) for b in range(nb) for h in range(H_B)]
    state = {bh: s_ref[bh[0], bh[1]] for bh in chains}
    u_, o_inter = {}, {}
    for k_pos in range(per_seg):
        for b, h in chains:
            c = b * per_seg + k_pos
            ws = _mm(lhs[(c, h)], state[(b, h)])
            u_[(c, h)] = (sol[(c, h)][:, :hd] - ws[:L]).astype(BF16)
            o_inter[(c, h)] = ws[L:]
        for b, h in chains:
            c = b * per_seg + k_pos
            state[(b, h)] = e_last[h][c * L:c * L + 1, :] * state[(b, h)] + _mm_tn(kdec[h][rs[c]], u_[(c, h)])
    for b, h in chains:
        s_ref[b, h] = state[(b, h)]
    for c, h in items:
        o = o_inter[(c, h)] + _mm(qk_m[(c, h)], u_[(c, h)])
        on = o * lax.rsqrt(jnp.mean(o * o, axis=-1, keepdims=True) + EPS) * nw
        k_pos = c % per_seg
        y_ref[c // per_seg, k_pos * L:(k_pos + 1) * L, h * hd:(h + 1) * hd] = (
            on * _silu(z[rs[c], h * hd:(h + 1) * hd])).astype(y_ref.dtype)


def _delta(q, k, v, bab, z, s0, wts, L):
    b, t, inner = q.shape
    nb, tb = _mixer_block(b, t, L)
    pad = lambda x: jnp.pad(x.reshape(1, -1).astype(F32), ((0, 0), (0, LANES - x.size)))
    params = [pad(wts['A_log']), pad(wts['dt_bias']), wts['norm_w'].reshape(1, -1).astype(F32)]
    blk = lambda w: pl.BlockSpec((nb, tb, w), lambda i, j: (i, j, 0))
    st = pl.BlockSpec((nb, H_B, DK_B, DK_B), lambda i, j: (i, 0, 0, 0))
    return pl.pallas_call(
        functools.partial(_delta_body, nb=nb, tb=tb, L=L),
        out_shape=[jax.ShapeDtypeStruct((b, t, inner), BF16), jax.ShapeDtypeStruct((b, H_B, DK_B, DK_B), F32)],
        grid=(b // nb, t // tb),
        in_specs=[blk(inner), blk(inner), blk(inner), blk(LANES), blk(inner), st] + [_resident(p.shape) for p in params],
        out_specs=[blk(inner), st],
        compiler_params=_cparams(("parallel", "arbitrary")),
        name="gated_delta",
    )(q, k, v, bab, z, s0.astype(F32), *params)


def _gla_body(qk_ref, v_ref, ad_ref, g_ref, s0_ref, a2_ref, ab_ref, nw_ref, y_ref, s_ref, *, tb, L):
    t_idx = pl.program_id(1)
    kdim = H_C * DK_C
    n_pairs = kdim // LANES

    @pl.when(t_idx == 0)
    def _():
        s_ref[...] = s0_ref[...]

    qk = qk_ref[...]
    log_a = -_softplus(-(_mm(ad_ref[...], a2_ref[...]) + ab_ref[...])) * (1.0 / GLA_NORMALIZER)
    vv = v_ref[...]
    gg = g_ref[...]
    nw = nw_ref[...]
    incl = _iota((L, L), 1) <= _iota((L, L), 0)
    m0 = _iota((tb, LANES), 1) < DK_C
    n_chunks = tb // L

    cum = _chunk_cumsum(log_a, L)
    tot = _chunk_last(cum, L)
    q_in = qk[:, :kdim] * DK_C ** -0.5 * jnp.exp(cum)
    k_in = (qk[:, kdim:] * jnp.exp(-cum)).astype(BF16)
    k_out = qk[:, kdim:] * jnp.exp(tot - cum)
    dec = jnp.exp(tot)
    vb16 = vv.astype(BF16)

    rs = {c: slice(c * L, (c + 1) * L) for c in range(n_chunks)}
    items = [(c, h) for c in range(n_chunks) for h in range(H_C)]
    qm, km = {}, {}
    for h in range(H_C):
        ls = slice((h // 2) * LANES, (h // 2 + 1) * LANES)
        mask = m0 if h % 2 == 0 else jnp.logical_not(m0)
        qm[h] = jnp.where(mask, q_in[:, ls], 0.0).astype(BF16)
        km[h] = jnp.where(mask, k_out[:, ls], 0.0).astype(BF16)
    scores = {(c, h): jnp.where(incl, _mm_nt(qm[h][rs[c]], k_in[rs[c], (h // 2) * LANES:(h // 2 + 1) * LANES]), 0.0)
              .astype(BF16) for c, h in items}
    upd = {(c, h): _mm_tn(vb16[rs[c], h * DV_C:(h + 1) * DV_C], km[h][rs[c]]) for c, h in items}
    state = [s_ref[p] for p in range(n_pairs)]
    s_prev = {}
    for c in range(n_chunks):
        for p in range(n_pairs):
            s_prev[(c, p)] = state[p].astype(BF16)
            state[p] = (state[p] * dec[c * L:c * L + 1, p * LANES:(p + 1) * LANES]
                        + upd[(c, 2 * p)] + upd[(c, 2 * p + 1)])
    for p in range(n_pairs):
        s_ref[p] = state[p]
    for c, h in items:
        hs = slice(h * DV_C, (h + 1) * DV_C)
        o = _mm_nt(qm[h][rs[c]], s_prev[(c, h // 2)]) + _mm(scores[(c, h)], vb16[rs[c], hs])
        on = o * lax.rsqrt(jnp.mean(o * o, axis=-1, keepdims=True) + EPS) * nw
        y_ref[rs[c], hs] = (on * _silu(gg[rs[c], hs])).astype(y_ref.dtype)


def _gla(cqk, cv, cad, cg, s0, wts, L):
    b, t, _ = cqk.shape
    kdim = H_C * DK_C
    vdim = H_C * DV_C
    n_pairs = kdim // LANES
    tb = min(t, 256)
    rank = wts['a2'].shape[0]
    a2p = jnp.pad(wts['a2'].astype(F32), ((0, LANES - rank), (0, 0))).astype(BF16)
    params = [a2p, wts['a_bias'].reshape(1, -1).astype(F32), wts['norm_w'].reshape(1, -1).astype(F32)]
    st0 = jnp.swapaxes(s0.astype(F32), -1, -2).reshape(b, n_pairs, 2, DV_C, DK_C)
    st0 = jnp.concatenate([st0[:, :, 0], st0[:, :, 1]], axis=-1)
    blk = lambda w: pl.BlockSpec((None, tb, w), lambda i, j: (i, j, 0))
    per_b = lambda shape: pl.BlockSpec((None,) + shape, lambda i, j: (i,) + (0,) * len(shape))
    y, st = pl.pallas_call(
        functools.partial(_gla_body, tb=tb, L=L),
        out_shape=[jax.ShapeDtypeStruct((b, t, vdim), BF16), jax.ShapeDtypeStruct((b, n_pairs, DV_C, LANES), F32)],
        grid=(b, t // tb),
        in_specs=[blk(2 * kdim), blk(vdim), blk(LANES), blk(vdim), per_b((n_pairs, DV_C, LANES))]
        + [_resident(p.shape) for p in params],
        out_specs=[blk(vdim), per_b((n_pairs, DV_C, LANES))],
        compiler_params=_cparams(("parallel", "arbitrary")),
        name="gla",
    )(cqk, cv, cad, cg, st0, *params)
    st = jnp.stack([st[..., :DK_C], st[..., DK_C:]], axis=2).reshape(b, H_C, DV_C, DK_C)
    return y, jnp.swapaxes(st, -1, -2)


def _ret_body(q_ref, k_ref, v_ref, g_ref, cos_ref, sin_ref, s0_ref, y_ref, s_ref, *, tb, L):
    t_idx = pl.program_id(1)
    hd = DK_D

    @pl.when(t_idx == 0)
    def _():
        s_ref[...] = s0_ref[...]

    cos = cos_ref[...]
    sin = sin_ref[...]
    qq, kk, vv, gg = q_ref[...], k_ref[...], v_ref[...], g_ref[...]
    rw = _iota((L, L), 0)
    cl_ = _iota((L, L), 1)
    incl = cl_ <= rw
    dist = jnp.where(incl, rw - cl_, 0).astype(F32)
    pos = _iota((L, LANES), 0).astype(F32)

    n_chunks = tb // L
    rs = {c: slice(c * L, (c + 1) * L) for c in range(n_chunks)}
    items = [(c, h) for c in range(n_chunks) for h in range(H_D)]
    lg = [math.log(1.0 - 2.0 ** (-5.0 - h)) for h in range(H_D)]
    vb16 = vv.astype(BF16)
    qh, kh, q_in, k_out, dmat = {}, {}, {}, {}, {}
    for h in range(H_D):
        hs = slice(h * hd, (h + 1) * hd)
        q_rot = qq[:, hs] * cos + pltpu.roll(qq[:, hs], hd // 2, 1) * sin
        k_rot = (kk[:, hs] * cos + pltpu.roll(kk[:, hs], hd // 2, 1) * sin) * hd ** -0.5
        dmat[h] = jnp.where(incl, jnp.exp(dist * lg[h]), 0.0)
        e_in = jnp.exp((pos + 1.0) * lg[h])
        e_out = jnp.exp((L - 1.0 - pos) * lg[h])
        qh[h], kh[h] = q_rot.astype(BF16), k_rot.astype(BF16)
        for c in range(n_chunks):
            q_in[(c, h)] = (q_rot[rs[c]] * e_in).astype(BF16)
            k_out[(c, h)] = (k_rot[rs[c]] * e_out).astype(BF16)
    scores = {(c, h): (_mm_nt(qh[h][rs[c]], kh[h][rs[c]]) * dmat[h]).astype(BF16) for c, h in items}
    upd = {(c, h): _mm_tn(k_out[(c, h)], vb16[rs[c], h * hd:(h + 1) * hd]) for c, h in items}
    state = [s_ref[h] for h in range(H_D)]
    s_prev = {}
    for c in range(n_chunks):
        for h in range(H_D):
            s_prev[(c, h)] = state[h].astype(BF16)
            state[h] = state[h] * math.exp(lg[h] * L) + upd[(c, h)]
    for h in range(H_D):
        s_ref[h] = state[h]
    for c, h in items:
        hs = slice(h * hd, (h + 1) * hd)
        o = _mm(q_in[(c, h)], s_prev[(c, h)]) + _mm(scores[(c, h)], vb16[rs[c], hs])
        mu = jnp.mean(o, axis=-1, keepdims=True)
        oc = o - mu
        on = oc * lax.rsqrt(jnp.mean(oc * oc, axis=-1, keepdims=True) + EPS)
        y_ref[rs[c], hs] = (on * _silu(gg[rs[c], hs])).astype(y_ref.dtype)


def _retention(dq, dk, dv, dg, s0, pos0, L):
    b, t, dim = dq.shape
    tb = min(t, 256)
    half = DK_D // 2
    inv = ROPE_BASE ** (-jnp.arange(0, DK_D, 2, dtype=F32) / DK_D)
    ang = (jnp.arange(t) + pos0).astype(F32)[:, None] * inv[None, :]
    cos = jnp.concatenate([jnp.cos(ang), jnp.cos(ang)], axis=-1)
    sin = jnp.concatenate([-jnp.sin(ang), jnp.sin(ang)], axis=-1)
    del half
    blk = pl.BlockSpec((None, tb, dim), lambda i, j: (i, j, 0))
    tab = pl.BlockSpec((tb, DK_D), lambda i, j: (j, 0))
    st = pl.BlockSpec((None, H_D, DK_D, DK_D), lambda i, j: (i, 0, 0, 0))
    return pl.pallas_call(
        functools.partial(_ret_body, tb=tb, L=L),
        out_shape=[jax.ShapeDtypeStruct((b, t, dim), BF16), jax.ShapeDtypeStruct((b, H_D, DK_D, DK_D), F32)],
        grid=(b, t // tb),
        in_specs=[blk, blk, blk, blk, tab, tab, st],
        out_specs=[blk, st],
        compiler_params=_cparams(("parallel", "arbitrary")),
        name="retention",
    )(dq, dk, dv, dg, cos, sin, s0.astype(F32))


def _prep_weights(W):
    bf = lambda x: x.astype(BF16)
    a_proj = 3 * H_A * N_A + 64 + 64 + 128
    b_conv = 3 * H_B * DK_B
    P = {}
    for name in ('ffn1_wg', 'ffn1_wu', 'ffn1_wd', 'ffn2_wg', 'ffn2_wu', 'ffn2_wd', 'mem_wq', 'mem_wo', 'mem_wk',
                 'mem_wv', 'even_w_out', 'odd_w_out'):
        P[name] = bf(W[name])
    ew = W['even_w_in']
    o = a_proj
    bab = ew[:, :, o + b_conv:o + b_conv + 2 * H_B]
    P['even_in'] = [bf(ew[:, :, :a_proj]), bf(ew[:, :, o:o + b_conv]),
                    bf(jnp.pad(bab, ((0, 0), (0, 0), (0, LANES - 2 * H_B)))), bf(ew[:, :, o + b_conv + 2 * H_B:])]
    ow = W['odd_w_in']
    kd, vd, rank = H_C * DK_C, H_C * DV_C, W['gla_a2'].shape[1]
    c_proj = 2 * kd + 2 * vd + rank
    dd = H_D * DK_D
    cad = ow[:, :, 2 * kd + vd:2 * kd + vd + rank]
    P['odd_in'] = [bf(ow[:, :, :2 * kd]), bf(ow[:, :, 2 * kd:2 * kd + vd]),
                   bf(jnp.pad(cad, ((0, 0), (0, 0), (0, LANES - rank)))), bf(ow[:, :, 2 * kd + vd + rank:c_proj])] \
        + [bf(ow[:, :, c_proj + j * dd:c_proj + (j + 1) * dd]) for j in range(4)]
    return P


def _trunk(x, pos0, mem_k, mem_v, heads, shift, rwkv, conv, delta, gla, ret, W, P):
    b, t, d = x.shape
    n = b * t
    depth = W['norm_ffn1'].shape[0]
    L_delta = min(64, t)
    L_lin = min(64, t)
    flat = lambda z: z.reshape(n, z.shape[-1])
    unflat = lambda z: z.reshape(b, t, z.shape[-1])
    new = {k: [] for k in ('shift', 'rwkv', 'conv', 'delta', 'gla', 'ret')}
    x = flat(x)
    for l in range(depth):
        i = l // 2
        x = _ffn(x, W['norm_ffn1'][l], P['ffn1_wg'][l], P['ffn1_wu'][l], P['ffn1_wd'][l])
        if l % 2 == 0:
            (pa, dq_, dk_, dv_, bab, z), s3 = _even_in_proj(unflat(x), W['norm_mix'][l], [w[i] for w in P['even_in']],
                                                           conv[i], W['delta_conv_w'][i])
            wa = dict(mu=W['rwkv_mu'][i], w0=W['rwkv_w0'][i], w2=W['rwkv_w2'][i], a0=W['rwkv_a0'][i],
                      a2=W['rwkv_a2'][i], g2=W['rwkv_g2'][i], kk=W['rwkv_kk'][i], ka=W['rwkv_ka'][i],
                      rk=W['rwkv_rk'][i], ln_w=W['rwkv_ln_w'][i], ln_b=W['rwkv_ln_b'][i])
            y_a, s1, s2 = _rwkv(pa, shift[i], rwkv[i], wa, L_delta)
            wb = dict(A_log=W['delta_A_log'][i], dt_bias=W['delta_dt_bias'][i], norm_w=W['delta_norm_w'][i])
            y_b, s4 = _delta(dq_, dk_, dv_, bab, z, delta[i], wb, L_delta)
            new['shift'].append(s1)
            new['rwkv'].append(s2)
            new['conv'].append(s3)
            new['delta'].append(s4)
            split = H_A * N_A
            halves, w_out = (y_a, y_b), P['even_w_out'][i]
        else:
            cqk, cv, cad, cg, dq, dk, dv, dg = _norm_proj(x, W['norm_mix'][l], [w[i] for w in P['odd_in']],
                                                          bf16_out=(1, 6))
            wc = dict(a2=W['gla_a2'][i], a_bias=W['gla_a_bias'][i], norm_w=W['gla_norm_w'][i])
            y_c, s5 = _gla(unflat(cqk), unflat(cv), unflat(cad), unflat(cg), gla[i], wc, L_lin)
            y_d, s6 = _retention(unflat(dq), unflat(dk), unflat(dv), unflat(dg), ret[i], pos0, L_lin)
            new['gla'].append(s5)
            new['ret'].append(s6)
            split = H_C * DV_C
            halves, w_out = (y_c, y_d), P['odd_w_out'][i]
        mk = mem_k[l].astype(BF16)
        mv = mem_v[l].astype(BF16)
        x = flat(_mix_out_mem_attn(unflat(x), halves[0], halves[1], w_out[:split], w_out[split:], W['norm_mem'][l],
                                   P['mem_wq'][l], mk, mv, P['mem_wo'][l], heads))
        fw = W['final_norm'] if l == depth - 1 else None
        x = _ffn(x, W['norm_ffn2'][l], P['ffn2_wg'][l], P['ffn2_wu'][l], P['ffn2_wd'][l], fw)
    return unflat(x), new


def kernel(x_prompt, x_sample, mem_prompt, state_rwkv_shift, state_rwkv, state_delta_conv, state_delta, state_gla, state_ret, cache_mem_k, cache_mem_v, norm_ffn1, ffn1_wg, ffn1_wu, ffn1_wd, norm_mix, even_w_in, even_w_out, rwkv_mu, rwkv_w0, rwkv_w2, rwkv_a0, rwkv_a2, rwkv_g2, rwkv_kk, rwkv_ka, rwkv_rk, rwkv_ln_w, rwkv_ln_b, delta_conv_w, delta_A_log, delta_dt_bias, delta_norm_w, odd_w_in, odd_w_out, gla_a2, gla_a_bias, gla_norm_w, norm_mem, mem_norm_kv, mem_wq, mem_wk, mem_wv, mem_wo, norm_ffn2, ffn2_wg, ffn2_wu, ffn2_wd, final_norm):
    W = dict(norm_ffn1=norm_ffn1, ffn1_wg=ffn1_wg, ffn1_wu=ffn1_wu, ffn1_wd=ffn1_wd, norm_mix=norm_mix,
             even_w_in=even_w_in, even_w_out=even_w_out, rwkv_mu=rwkv_mu, rwkv_w0=rwkv_w0, rwkv_w2=rwkv_w2,
             rwkv_a0=rwkv_a0, rwkv_a2=rwkv_a2, rwkv_g2=rwkv_g2, rwkv_kk=rwkv_kk, rwkv_ka=rwkv_ka,
             rwkv_rk=rwkv_rk, rwkv_ln_w=rwkv_ln_w, rwkv_ln_b=rwkv_ln_b, delta_conv_w=delta_conv_w,
             delta_A_log=delta_A_log, delta_dt_bias=delta_dt_bias, delta_norm_w=delta_norm_w,
             odd_w_in=odd_w_in, odd_w_out=odd_w_out, gla_a2=gla_a2, gla_a_bias=gla_a_bias,
             gla_norm_w=gla_norm_w, norm_mem=norm_mem, mem_wq=mem_wq, mem_wk=mem_wk, mem_wv=mem_wv,
             mem_wo=mem_wo, norm_ffn2=norm_ffn2, ffn2_wg=ffn2_wg, ffn2_wu=ffn2_wu, ffn2_wd=ffn2_wd,
             final_norm=final_norm)
    P = _prep_weights(W)
    dt = x_prompt.dtype
    bp, _, d = x_prompt.shape
    depth = norm_ffn1.shape[0]
    n_even, n_odd = (depth + 1) // 2, depth // 2
    heads, hd = cache_mem_k.shape[3], cache_mem_k.shape[4]
    n_mem = mem_prompt.shape[1]
    pk, pv, pk_flat, pv_flat = [], [], [], []
    mem_flat = mem_prompt.reshape(bp * n_mem, d)
    for l in range(depth):
        mk, mv = _norm_proj(mem_flat, mem_norm_kv[l], [P['mem_wk'][l], P['mem_wv'][l]])
        pk_flat.append(mk.reshape(bp, n_mem, d))
        pv_flat.append(mv.reshape(bp, n_mem, d))
        pk.append(mk.reshape(bp, n_mem, heads, hd))
        pv.append(mv.reshape(bp, n_mem, heads, hd))
    zeros = lambda ref, cnt: [jnp.zeros((bp,) + ref.shape[2:], F32)] * cnt
    y_prompt, ps = _trunk(x_prompt, 0, pk_flat, pv_flat, heads, zeros(state_rwkv_shift, n_even), zeros(state_rwkv, n_even),
                          zeros(state_delta_conv, n_even), zeros(state_delta, n_even), zeros(state_gla, n_odd),
                          zeros(state_ret, n_odd), W, P)
    bs = x_sample.shape[0]
    y_sample, ss = _trunk(x_sample, PAST_LEN, [cache_mem_k[l].reshape(bs, n_mem, d) for l in range(depth)],
                          [cache_mem_v[l].reshape(bs, n_mem, d) for l in range(depth)], heads,
                          [state_rwkv_shift[i] for i in range(n_even)], [state_rwkv[i] for i in range(n_even)],
                          [state_delta_conv[i] for i in range(n_even)], [state_delta[i] for i in range(n_even)],
                          [state_gla[i] for i in range(n_odd)], [state_ret[i] for i in range(n_odd)], W, P)
    st = lambda xs: jnp.stack(xs).astype(dt)
    order = ('shift', 'rwkv', 'conv', 'delta', 'gla', 'ret')
    return ((y_prompt, y_sample) + tuple(st(ps[k]) for k in order) + (st(pk), st(pv))
            + tuple(st(ss[k]) for k in order))
```

```python
import functools
import math

import jax
import jax.numpy as jnp
from jax import lax
from jax.experimental import pallas as pl
from jax.experimental.pallas import tpu as pltpu

F32 = jnp.float32
BF16 = jnp.bfloat16
EPS = 1e-6
RWKV_LN_EPS = 64e-5
GLA_NORMALIZER = 16.0
ROPE_BASE = 10000.0
PAST_LEN = 1024

LANES = 128
MXU_DIM = 256
VMEM_LIMIT = 56 * 1024 * 1024

H_A, N_A = 8, 64
H_B, DK_B = 4, 128
H_C, DK_C, DV_C = 4, 64, 128
H_D, DK_D = 4, 128
CONV_W = 4
CONV_CARRY = 8
MIXER_ROWS = 512
CONV_ROWS = 128


def _cparams(sem):
    return pltpu.CompilerParams(dimension_semantics=sem, vmem_limit_bytes=VMEM_LIMIT)


def _mm(a, b):
    return jnp.dot(a.astype(BF16), b.astype(BF16), preferred_element_type=F32)


def _mm_nt(a, b):
    return lax.dot_general(a.astype(BF16), b.astype(BF16), (((1,), (1,)), ((), ())),
                           preferred_element_type=F32)


def _mm_tn(a, b):
    return lax.dot_general(a.astype(BF16), b.astype(BF16), (((0,), (0,)), ((), ())),
                           preferred_element_type=F32)


def _split3(x):
    hi = x.astype(BF16)
    r = x - hi.astype(F32)
    mid = r.astype(BF16)
    lo = (r - mid.astype(F32)).astype(BF16)
    return hi, mid, lo


def _mm_exact_lhs01(a01, x):
    hi, mid, lo = _split3(x)
    a = a01.astype(BF16)
    return (jnp.dot(a, hi, preferred_element_type=F32) + jnp.dot(a, mid, preferred_element_type=F32)
            + jnp.dot(a, lo, preferred_element_type=F32))


def _iota(shape, dim):
    return lax.broadcasted_iota(jnp.int32, shape, dim)


def _tri_incl(n):
    return (_iota((n, n), 1) <= _iota((n, n), 0)).astype(F32)


def _chunk_tri(tb, L):
    shift = L.bit_length() - 1
    assert 1 << shift == L
    rt, ct = _iota((tb, tb), 0), _iota((tb, tb), 1)
    same = jnp.right_shift(rt, shift) == jnp.right_shift(ct, shift)
    return jnp.logical_and(same, ct <= rt).astype(F32)


def _chunk_cumsum(x, L):
    n = x.shape[0]
    g = min(n, MXU_DIM)
    tri = _chunk_tri(g, L)
    return jnp.concatenate([_mm_exact_lhs01(tri, x[r:r + g]) for r in range(0, n, g)], axis=0)


def _chunk_last(cum, L):
    tb, w = cum.shape
    return jnp.concatenate([jnp.broadcast_to(cum[c * L + L - 1:c * L + L, :], (L, w)) for c in range(tb // L)], axis=0)


def _sigmoid(x):
    return 0.5 * (jnp.tanh(0.5 * x) + 1.0)


def _silu(x):
    return x * _sigmoid(x)


def _softplus(x):
    return jnp.maximum(x, 0.0) + jnp.log(1.0 + jnp.exp(-jnp.abs(x)))


def _rms(x, w):
    return x * lax.rsqrt(jnp.mean(x * x, axis=-1, keepdims=True) + EPS) * w


def _inv_unit_lower(n_mat, size):
    dim = n_mat.shape[0]
    eye = (_iota((dim, dim), 0) == _iota((dim, dim), 1)).astype(F32)
    t = eye + n_mat
    m = n_mat
    power = 2
    while power < size:
        m = _mm(m, m)
        t = t + _mm(t, m)
        power *= 2
    return t


def _row_tile(n, target):
    t = min(n, target)
    while n % t:
        t //= 2
    return t


def _mixer_block(b, t, L):
    tb = min(t, 2 * L)
    nb = max(1, min(b, MIXER_ROWS // tb))
    while b % nb:
        nb -= 1
    return nb, tb


def _resident(shape):
    nd = len(shape)
    return pl.BlockSpec(shape, lambda *_: (0,) * nd, pipeline_mode=pl.Buffered(1))


def _ffn_body(x_ref, nw_ref, wg_ref, wu_ref, wd_ref, fw_ref, o_ref, *, chunk, final):
    x = x_ref[...]
    h = _rms(x, nw_ref[...]).astype(BF16)
    d_ff = wg_ref.shape[1]
    n_chunks = d_ff // chunk

    def gate_up(c):
        sl = slice(c * chunk, (c + 1) * chunk)
        return (jnp.dot(h, wg_ref[:, sl], preferred_element_type=F32),
                jnp.dot(h, wu_ref[:, sl], preferred_element_type=F32))

    acc = None
    nxt = gate_up(0)
    for c in range(n_chunks):
        g, u = nxt
        if c + 1 < n_chunks:
            nxt = gate_up(c + 1)
        a = (_silu(g) * u).astype(BF16)
        d = jnp.dot(a, wd_ref[c * chunk:(c + 1) * chunk, :], preferred_element_type=F32)
        acc = d if acc is None else acc + d
    y = x + 0.5 * acc
    if final:
        y = _rms(y, fw_ref[...])
    o_ref[...] = y


def _ffn(x, nw, wg, wu, wd, fw=None):
    n, d = x.shape
    d_ff = wg.shape[1]
    tm = _row_tile(n, 1024)
    final = fw is not None
    if fw is None:
        fw = nw
    return pl.pallas_call(
        functools.partial(_ffn_body, chunk=MXU_DIM, final=final),
        out_shape=jax.ShapeDtypeStruct((n, d), F32),
        grid=(n // tm,),
        in_specs=[pl.BlockSpec((tm, d), lambda i: (i, 0)), _resident((1, d)), _resident((d, d_ff)),
                  _resident((d, d_ff)), _resident((d_ff, d)), _resident((1, d))],
        out_specs=pl.BlockSpec((tm, d), lambda i: (i, 0)),
        compiler_params=_cparams(("parallel",)),
        name="ffn",
    )(x, nw.reshape(1, d), wg, wu, wd, fw.reshape(1, d))


def _norm_proj_body(*refs, n_w):
    x_ref, nw_ref = refs[0], refs[1]
    w_refs = refs[2:2 + n_w]
    o_refs = refs[2 + n_w:]
    h = _rms(x_ref[...], nw_ref[...]).astype(BF16)
    for w_ref, o_ref in zip(w_refs, o_refs):
        o_ref[...] = jnp.dot(h, w_ref[...], preferred_element_type=F32).astype(o_ref.dtype)


def _norm_proj(x, nw, ws, bf16_out=()):
    n, d = x.shape
    tm = _row_tile(n, 512)
    return pl.pallas_call(
        functools.partial(_norm_proj_body, n_w=len(ws)),
        out_shape=[jax.ShapeDtypeStruct((n, w.shape[1]), BF16 if i in bf16_out else F32) for i, w in enumerate(ws)],
        grid=(n // tm,),
        in_specs=[pl.BlockSpec((tm, d), lambda i: (i, 0)), _resident((1, d))] + [_resident(w.shape) for w in ws],
        out_specs=[pl.BlockSpec((tm, w.shape[1]), lambda i: (i, 0)) for w in ws],
        compiler_params=_cparams(("parallel",)),
        name="norm_proj",
    )(x, nw.reshape(1, d), *ws)


def _mix_out_mem_attn_body(x_ref, ya_ref, yb_ref, wa_ref, wb_ref, nw_ref, wq_ref, k_ref, v_ref, wo_ref, o_ref, *,
                           heads):
    x = (x_ref[...] + jnp.dot(ya_ref[...], wa_ref[...], preferred_element_type=F32)
         + jnp.dot(yb_ref[...], wb_ref[...], preferred_element_type=F32))
    h = _rms(x, nw_ref[...]).astype(BF16)
    q = jnp.dot(h, wq_ref[...], preferred_element_type=F32).astype(BF16)
    d = q.shape[1]
    hd = d // heads

    def scores(i):
        sl = slice(i * hd, (i + 1) * hd)
        return _mm_nt(q[:, sl], k_ref[:, sl]) * hd ** -0.5

    outs = []
    nxt = scores(0)
    for i in range(heads):
        s = nxt
        if i + 1 < heads:
            nxt = scores(i + 1)
        e = jnp.exp(s - jnp.max(s, axis=-1, keepdims=True))
        pr = e / jnp.sum(e, axis=-1, keepdims=True)
        outs.append(_mm(pr, v_ref[:, i * hd:(i + 1) * hd]))
    o = jnp.concatenate(outs, axis=-1).astype(BF16)
    o_ref[...] = x + jnp.dot(o, wo_ref[...], preferred_element_type=F32)


def _mix_out_mem_attn(x, ya, yb, wa, wb, nw, wq, mk, mv, wo, heads):
    b, t, d = x.shape
    m = mk.shape[1]
    tm = _row_tile(t, 512)
    row = lambda w: pl.BlockSpec((None, tm, w), lambda i, j: (i, j, 0))
    mem = pl.BlockSpec((None, m, d), lambda i, j: (i, 0, 0))
    return pl.pallas_call(
        functools.partial(_mix_out_mem_attn_body, heads=heads),
        out_shape=jax.ShapeDtypeStruct((b, t, d), F32),
        grid=(b, t // tm),
        in_specs=[row(d), row(ya.shape[2]), row(yb.shape[2]), _resident(wa.shape), _resident(wb.shape),
                  _resident((1, d)), _resident((d, d)), mem, mem, _resident((d, d))],
        out_specs=row(d),
        compiler_params=_cparams(("parallel", "parallel")),
        name="mix_out_mem_attn",
    )(x, ya, yb, wa, wb, nw.reshape(1, d), wq, mk, mv, wo)


def _seg_sum(x, bd):
    g = bd.shape[0]
    xb = x.astype(BF16)
    return jnp.concatenate([jnp.dot(xb[:, j:j + g], bd, preferred_element_type=F32)
                            for j in range(0, x.shape[1], g)], axis=1)


def _rwkv_body(pa_ref, shift0_ref, s0_ref, mu_ref, w0_ref, w2_ref, a0_ref, a2_ref, g2_ref, kkw_ref, ka_ref,
               rk_ref, lnw_ref, lnb_ref, bd_ref, y_ref, shift_ref, s_ref, y_scr, *, nb, tb, L):
    t_idx = pl.program_id(1)
    a_dim = H_A * N_A
    n_pairs = a_dim // LANES

    @pl.when(t_idx == 0)
    def _():
        shift_ref[...] = shift0_ref[...]
        s_ref[...] = s0_ref[...]

    row = _iota((tb, pa_ref.shape[2]), 0)
    segs, prevs = [], []
    for b in range(nb):
        seg = pa_ref[b]
        segs.append(seg)
        prevs.append(jnp.where(row == 0, shift_ref[b], pltpu.roll(seg, 1, 0)))
        shift_ref[b] = seg[tb - 1:tb, :]
    pa = jnp.concatenate(segs, axis=0)
    prev = jnp.concatenate(prevs, axis=0)
    rows = nb * tb
    xa = pa + (prev - pa) * mu_ref[...]
    r = xa[:, 0:a_dim]
    k = xa[:, a_dim:2 * a_dim]
    v = xa[:, 2 * a_dim:3 * a_dim]
    xwa = xa[:, 3 * a_dim:3 * a_dim + LANES]
    xg = xa[:, 3 * a_dim + LANES:]
    wlog = -_softplus(-(w0_ref[...] + _mm(jnp.tanh(xwa), w2_ref[...]))) - 0.5
    logdec = -jnp.exp(wlog)
    a = _sigmoid(a0_ref[...] + _mm(xwa, a2_ref[...]))
    gate = _mm(_sigmoid(xg), g2_ref[...])
    bd = bd_ref[...]
    kq = k * kkw_ref[...]
    kk = kq * lax.rsqrt(_seg_sum(kq * kq, bd) + EPS)
    k2 = k * (1.0 + (a - 1.0) * ka_ref[...])
    av = -kk
    bv = kk * a

    lane = _iota((L, LANES), 1)
    m0 = lane < N_A
    r2 = _iota((2 * L, 2 * L), 0)
    c2 = _iota((2 * L, 2 * L), 1)
    bdm = jnp.logical_not(jnp.logical_xor(r2 >= L, c2 >= L))
    r_loc = jnp.where(r2 >= L, r2 - L, r2)
    c_loc = jnp.where(c2 >= L, c2 - L, c2)
    strict_t = jnp.logical_and(bdm, c_loc > r_loc)
    strict_l = jnp.bitwise_and(_iota((L, 2 * L), 1), L - 1) < _iota((L, 2 * L), 0)
    incl_l = jnp.bitwise_and(_iota((L, 4 * L), 1), L - 1) <= _iota((L, 4 * L), 0)
    bdl = jnp.logical_not(jnp.logical_xor(_iota((2 * L, LANES), 0) >= L, _iota((2 * L, LANES), 1) >= N_A))
    eye2 = (r2 == c2).astype(F32)
    zero_b = jnp.zeros((L, LANES), BF16)

    def stack2(x):
        xb = x.astype(BF16)
        return jnp.concatenate([jnp.where(m0, xb, zero_b), jnp.where(m0, zero_b, xb)], axis=0)

    def dup2(x):
        xb = x.astype(BF16)
        return jnp.concatenate([xb, xb], axis=0)

    cum = _chunk_cumsum(logdec, L)
    tot = _chunk_last(cum, L)
    e_neg = jnp.exp(-cum)
    e_out = jnp.exp(tot - cum)
    a_t = av * jnp.exp(cum - logdec)
    r_t = r * jnp.exp(cum)
    b_t = bv * e_neg
    k_t = k2 * e_neg
    b_o = bv * e_out
    k_o = k2 * e_out
    g_l = jnp.exp(tot)

    per_seg = tb // L
    items = [(c, p) for c in range(rows // L) for p in range(n_pairs)]
    sl = {(c, p): (slice(c * L, (c + 1) * L), slice(p * LANES, (p + 1) * LANES)) for c, p in items}
    xa = {it: stack2(a_t[sl[it]]) for it in items}
    ra = {it: r_t[sl[it]].astype(BF16) for it in items}
    yb = {it: dup2(b_t[sl[it]]) for it in items}
    sbk = {it: jnp.concatenate([stack2(b_t[sl[it]]), stack2(k_t[sl[it]])], axis=0) for it in items}
    v2 = {it: jnp.where(bdl, dup2(v[sl[it]]), jnp.zeros((2 * L, LANES), BF16)) for it in items}
    wst = {it: jnp.concatenate([stack2(b_o[sl[it]]), stack2(k_o[sl[it]])], axis=0) for it in items}
    n_t = {it: jnp.where(strict_t, _mm_nt(yb[it], xa[it]), 0.0) for it in items}
    ak = {it: jnp.where(strict_l, _mm_nt(a_t[sl[it]], sbk[it][2 * L:]), 0.0).astype(BF16) for it in items}
    rbk = {it: jnp.where(incl_l, _mm_nt(ra[it], sbk[it]), 0.0).astype(BF16) for it in items}
    akv = {it: stack2(_mm(ak[it], v2[it])) for it in items}
    t_t = {it: eye2 + n_t[it] for it in items}
    n_b = {it: n_t[it].astype(BF16) for it in items}
    m_t = {it: _mm(n_b[it], n_b[it]).astype(BF16) for it in items}
    power = 4
    while power < L:
        prod = {it: _mm(m_t[it], jnp.concatenate([m_t[it], t_t[it].astype(BF16)], axis=1)) for it in items}
        m_t = {it: prod[it][:, :2 * L].astype(BF16) for it in items}
        t_t = {it: t_t[it] + prod[it][:, 2 * L:] for it in items}
        power *= 2
    if L > 2:
        t_t = {it: t_t[it] + _mm(m_t[it], t_t[it]) for it in items}
    om_up = {it: _mm_tn(t_t[it], jnp.concatenate([xa[it], akv[it]], axis=1)) for it in items}
    omega = {it: om_up[it][:, :LANES].astype(BF16) for it in items}
    chains = [(b, p) for b in range(nb) for p in range(n_pairs)]
    state = {bp: s_ref[bp[0], bp[1]] for bp in chains}
    s_prev, uv = {}, {}
    for k_pos in range(per_seg):
        for b, p in chains:
            it = (b * per_seg + k_pos, p)
            s_prev[it] = state[(b, p)].astype(BF16)
            u_st = _mm_nt(omega[it], s_prev[it]) + om_up[it][:, LANES:]
            uv[it] = jnp.concatenate([u_st.astype(BF16), v2[it]], axis=0)
        for b, p in chains:
            it = (b * per_seg + k_pos, p)
            state[(b, p)] = state[(b, p)] * g_l[it[0] * L:it[0] * L + 1, sl[it][1]] + _mm_tn(uv[it], wst[it])
    for b, p in chains:
        s_ref[b, p] = state[(b, p)]
    for it in items:
        y_scr[sl[it]] = _mm_nt(ra[it], s_prev[it]) + _mm(rbk[it], uv[it])

    y = y_scr[...]
    inv_n = 1.0 / N_A
    mu_y = _seg_sum(y, bd) * inv_n
    yc = y - mu_y
    var = _seg_sum(yc * yc, bd) * inv_n
    yn = yc * lax.rsqrt(var + RWKV_LN_EPS) * lnw_ref[...] + lnb_ref[...]
    bonus = _seg_sum(r * k2 * rk_ref[...], bd) * v
    out = ((yn + bonus) * gate).astype(y_ref.dtype)
    for b in range(nb):
        y_ref[b] = out[b * tb:(b + 1) * tb, :]


def _pair_blockdiag(s):
    b, h, n, _ = s.shape
    s = s.reshape(b, h // 2, 2, n, n)
    z = jnp.zeros((b, h // 2, n, n), s.dtype)
    top = jnp.concatenate([s[:, :, 0], z], axis=-1)
    bot = jnp.concatenate([z, s[:, :, 1]], axis=-1)
    return jnp.concatenate([top, bot], axis=-2)


def _pair_unblock(s):
    b, p, n2, _ = s.shape
    n = n2 // 2
    return jnp.stack([s[:, :, :n, :n], s[:, :, n:, n:]], axis=2).reshape(b, 2 * p, n, n)


def _rwkv(pa, shift0, s0, wts, L):
    b, t, pw = pa.shape
    a_dim = H_A * N_A
    nb, tb = _mixer_block(b, t, L)
    n_pairs = a_dim // LANES
    seg = _iota((MXU_DIM, MXU_DIM), 0) // N_A == _iota((MXU_DIM, MXU_DIM), 1) // N_A
    bd = seg.astype(BF16)
    z64 = jnp.zeros((N_A, a_dim), F32)
    w2p = jnp.concatenate([wts['w2'], z64], axis=0)
    a2p = jnp.concatenate([z64, wts['a2']], axis=0)
    vec = lambda x: x.reshape(1, -1).astype(F32)
    params = [vec(wts['mu']), vec(wts['w0']), w2p.astype(BF16), vec(wts['a0']), a2p.astype(BF16),
              wts['g2'].astype(BF16), vec(wts['kk']), vec(wts['ka']), vec(wts['rk']), vec(wts['ln_w']),
              vec(wts['ln_b']), bd]
    blk_t = lambda w: pl.BlockSpec((nb, tb, w), lambda i, j: (i, j, 0))
    per_b = lambda shape: pl.BlockSpec((nb,) + shape, lambda i, j: (i,) + (0,) * len(shape))
    y, shift, s_new = pl.pallas_call(
        functools.partial(_rwkv_body, nb=nb, tb=tb, L=L),
        out_shape=[jax.ShapeDtypeStruct((b, t, a_dim), BF16), jax.ShapeDtypeStruct((b, 1, pw), F32),
                   jax.ShapeDtypeStruct((b, n_pairs, LANES, LANES), F32)],
        grid=(b // nb, t // tb),
        in_specs=[blk_t(pw), per_b((1, pw)), per_b((n_pairs, LANES, LANES))] + [_resident(p.shape) for p in params],
        out_specs=[blk_t(a_dim), per_b((1, pw)), per_b((n_pairs, LANES, LANES))],
        scratch_shapes=[pltpu.VMEM((nb * tb, a_dim), F32)],
        compiler_params=_cparams(("parallel", "arbitrary")),
        name="rwkv7",
    )(pa, shift0, _pair_blockdiag(s0.astype(F32)), *params)
    return y, shift, _pair_unblock(s_new)


def _even_in_body(x_ref, xh_ref, conv0_ref, nw_ref, wa_ref, wqkv_ref, wbab_ref, wz_ref, cw_ref,
                  pa_ref, q_ref, k_ref, v_ref, bab_ref, z_ref, tail_ref, cat_scr, *, tm):
    j = pl.program_id(1)
    hd = DK_B
    inner = H_B * hd
    nw = nw_ref[...]
    halo = jnp.dot(_rms(xh_ref[...], nw).astype(BF16), wqkv_ref[...], preferred_element_type=F32)
    cat_scr[0:CONV_CARRY, :] = jnp.where(j == 0, conv0_ref[...], halo)
    h = _rms(x_ref[...], nw).astype(BF16)
    cat_scr[CONV_CARRY:, :] = jnp.dot(h, wqkv_ref[...], preferred_element_type=F32)
    tail_ref[...] = cat_scr[tm:, :]
    cw = cw_ref[...]

    rows = min(tm, CONV_ROWS)

    def conv_piece(r0, c0):
        cols = slice(c0, c0 + hd)
        cat = cat_scr[r0:r0 + rows + CONV_CARRY, cols]
        acc = cat * cw[0:1, cols]
        for t in range(1, CONV_W):
            acc = pltpu.roll(acc, 1, 0) + cat * cw[t:t + 1, cols]
        return _silu(acc[CONV_CARRY:, :])

    def l2n(z):
        return z * lax.rsqrt(jnp.sum(z * z, axis=-1, keepdims=True) + EPS)

    pa_ref[...] = jnp.dot(h, wa_ref[...], preferred_element_type=F32)
    z_ref[...] = jnp.dot(h, wz_ref[...], preferred_element_type=F32)
    bab_ref[...] = jnp.dot(h, wbab_ref[...], preferred_element_type=F32)
    for r0 in range(0, tm, rows):
        rs = slice(r0, r0 + rows)
        for i in range(H_B):
            sl = slice(i * hd, (i + 1) * hd)
            q_ref[rs, sl] = (l2n(conv_piece(r0, i * hd)) * hd ** -0.5).astype(BF16)
            k_ref[rs, sl] = l2n(conv_piece(r0, inner + i * hd)).astype(BF16)
            v_ref[rs, sl] = conv_piece(r0, 2 * inner + i * hd).astype(BF16)


def _even_in_proj(x, nw, ws, conv0, conv_w):
    b, t, d = x.shape
    wa, wqkv, wbab, wz = ws
    cdim = wqkv.shape[1]
    inner = H_B * DK_B
    tm = _row_tile(t, 512)
    conv0p = jnp.pad(conv0.astype(F32), ((0, 0), (CONV_CARRY - (CONV_W - 1), 0), (0, 0)))
    row = lambda w: pl.BlockSpec((None, tm, w), lambda i, j: (i, j, 0))
    halo = pl.BlockSpec((None, CONV_CARRY, d), lambda i, j: (i, jnp.maximum(j * (tm // CONV_CARRY) - 1, 0), 0))
    per_b = pl.BlockSpec((None, CONV_CARRY, cdim), lambda i, j: (i, 0, 0))
    outs = pl.pallas_call(
        functools.partial(_even_in_body, tm=tm),
        out_shape=[jax.ShapeDtypeStruct((b, t, wa.shape[1]), F32)]
        + [jax.ShapeDtypeStruct((b, t, inner), BF16)] * 3
        + [jax.ShapeDtypeStruct((b, t, wbab.shape[1]), F32), jax.ShapeDtypeStruct((b, t, wz.shape[1]), F32),
           jax.ShapeDtypeStruct((b, CONV_CARRY, cdim), F32)],
        grid=(b, t // tm),
        in_specs=[row(d), halo, per_b, _resident((1, d)), _resident(wa.shape), _resident(wqkv.shape),
                  _resident(wbab.shape), _resident(wz.shape), _resident(conv_w.shape)],
        out_specs=[row(wa.shape[1]), row(inner), row(inner), row(inner), row(wbab.shape[1]), row(wz.shape[1]), per_b],
        scratch_shapes=[pltpu.VMEM((tm + CONV_CARRY, cdim), F32)],
        compiler_params=_cparams(("parallel", "arbitrary")),
        name="even_in_proj",
    )(x, x, conv0p, nw.reshape(1, d), wa, wqkv, wbab, wz, conv_w.astype(F32))
    return outs[:6], outs[6][:, CONV_CARRY - (CONV_W - 1):]


def _delta_body(q_ref, k_ref, v_ref, bab_ref, z_ref, s0_ref, alog_ref, dtb_ref, nw_ref,
                y_ref, s_ref, *, nb, tb, L):
    t_idx = pl.program_id(1)
    hd = DK_B
    rows = nb * tb
    per_seg = tb // L

    @pl.when(t_idx == 0)
    def _():
        s_ref[...] = s0_ref[...]

    flat = lambda ref: jnp.concatenate([ref[b] for b in range(nb)], axis=0)
    bab = flat(bab_ref)
    q_all, k_all, v_all = flat(q_ref), flat(k_ref), flat(v_ref)
    g_all = -jnp.exp(alog_ref[...]) * _softplus(bab + dtb_ref[...])
    beta_all = _sigmoid(bab)

    tri = _tri_incl(L).astype(BF16)
    rw = _iota((L, L), 0)
    cl_ = _iota((L, L), 1)
    strict = cl_ < rw
    incl = cl_ <= rw
    eye = (rw == cl_).astype(F32)
    nw = nw_ref[...]
    z = flat(z_ref)
    n_chunks = rows // L

    g_cum_all = _chunk_cumsum(g_all, L)
    g_tot_all = _chunk_last(g_cum_all, L)

    qh, kh, vh, kdec, qg, bv_, bk_, gb, beta_b, e_last = {}, {}, {}, {}, {}, {}, {}, {}, {}, {}
    for h in range(H_B):
        qh[h] = q_all[:, h * hd:(h + 1) * hd]
        kh[h] = k_all[:, h * hd:(h + 1) * hd]
        qs, ks = qh[h].astype(F32), kh[h].astype(F32)
        vs = v_all[:, h * hd:(h + 1) * hd].astype(F32)
        gb[h] = jnp.broadcast_to(g_all[:, h:h + 1], (rows, LANES))
        g_cum = jnp.broadcast_to(g_cum_all[:, h:h + 1], (rows, LANES))
        g_tot = jnp.broadcast_to(g_tot_all[:, h:h + 1], (rows, LANES))
        bb = jnp.broadcast_to(beta_all[:, H_B + h:H_B + h + 1], (rows, LANES))
        e_g = jnp.exp(g_cum)
        kdec[h] = (ks * jnp.exp(g_tot - g_cum)).astype(BF16)
        qg[h] = qs * e_g
        bv_[h] = bb * vs
        bk_[h] = bb * e_g * ks
        beta_b[h] = bb
        e_last[h] = jnp.exp(g_tot)

    items = [(c, h) for c in range(n_chunks) for h in range(H_B)]
    rs = {c: slice(c * L, (c + 1) * L) for c in range(n_chunks)}
    diff = {}
    for c, h in items:
        hi, mid, lo = _split3(jnp.where(strict, gb[h][rs[c], :L], 0.0))
        diff[(c, h)] = (jnp.dot(tri, hi, preferred_element_type=F32) + jnp.dot(tri, mid, preferred_element_type=F32)
                        + jnp.dot(tri, lo, preferred_element_type=F32))
    kk_ = {(c, h): _mm_nt(kh[h][rs[c]], kh[h][rs[c]]) for c, h in items}
    qk_ = {(c, h): _mm_nt(qh[h][rs[c]], kh[h][rs[c]]) for c, h in items}
    n_mat, qk_m = {}, {}
    for c, h in items:
        e_diff = jnp.exp(jnp.where(incl, diff[(c, h)], 0.0))
        n_mat[(c, h)] = -(beta_b[h][rs[c], :L] * jnp.where(strict, kk_[(c, h)] * e_diff, 0.0))
        qk_m[(c, h)] = jnp.where(incl, qk_[(c, h)] * e_diff, 0.0).astype(BF16)
    t_inv = {it: eye + n_mat[it] for it in items}
    m_pow = n_mat
    power = 2
    while power < L:
        m_pow = {it: _mm(m_pow[it], m_pow[it]) for it in items}
        t_inv = {it: t_inv[it] + _mm(t_inv[it], m_pow[it]) for it in items}
        power *= 2
    sol = {(c, h): _mm(t_inv[(c, h)], jnp.concatenate([bv_[h][rs[c]], bk_[h][rs[c]]], axis=1)) for c, h in items}
    lhs = {(c, h): jnp.concatenate([sol[(c, h)][:, hd:], qg[h][rs[c]]], axis=0).astype(BF16) for c, h in items}
    chains = [(b, h) for b in range(nb) for h in range(H_B)]
    state = {bh: s_ref[bh[0], bh[1]] for bh in chains}
    u_, o_inter = {}, {}
    for k_pos in range(per_seg):
        for b, h in chains:
            c = b * per_seg + k_pos
            ws = _mm(lhs[(c, h)], state[(b, h)])
            u_[(c, h)] = (sol[(c, h)][:, :hd] - ws[:L]).astype(BF16)
            o_inter[(c, h)] = ws[L:]
        for b, h in chains:
            c = b * per_seg + k_pos
            state[(b, h)] = e_last[h][c * L:c * L + 1, :] * state[(b, h)] + _mm_tn(kdec[h][rs[c]], u_[(c, h)])
    for b, h in chains:
        s_ref[b, h] = state[(b, h)]
    for c, h in items:
        o = o_inter[(c, h)] + _mm(qk_m[(c, h)], u_[(c, h)])
        on = o * lax.rsqrt(jnp.mean(o * o, axis=-1, keepdims=True) + EPS) * nw
        k_pos = c % per_seg
        y_ref[c // per_seg, k_pos * L:(k_pos + 1) * L, h * hd:(h + 1) * hd] = (
            on * _silu(z[rs[c], h * hd:(h + 1) * hd])).astype(y_ref.dtype)


def _delta(q, k, v, bab, z, s0, wts, L):
    b, t, inner = q.shape
    nb, tb = _mixer_block(b, t, L)
    pad = lambda x: jnp.pad(x.reshape(1, -1).astype(F32), ((0, 0), (0, LANES - x.size)))
    params = [pad(wts['A_log']), pad(wts['dt_bias']), wts['norm_w'].reshape(1, -1).astype(F32)]
    blk = lambda w: pl.BlockSpec((nb, tb, w), lambda i, j: (i, j, 0))
    st = pl.BlockSpec((nb, H_B, DK_B, DK_B), lambda i, j: (i, 0, 0, 0))
    return pl.pallas_call(
        functools.partial(_delta_body, nb=nb, tb=tb, L=L),
        out_shape=[jax.ShapeDtypeStruct((b, t, inner), BF16), jax.ShapeDtypeStruct((b, H_B, DK_B, DK_B), F32)],
        grid=(b // nb, t // tb),
        in_specs=[blk(inner), blk(inner), blk(inner), blk(LANES), blk(inner), st] + [_resident(p.shape) for p in params],
        out_specs=[blk(inner), st],
        compiler_params=_cparams(("parallel", "arbitrary")),
        name="gated_delta",
    )(q, k, v, bab, z, s0.astype(F32), *params)


def _gla_body(qk_ref, v_ref, ad_ref, g_ref, s0_ref, a2_ref, ab_ref, nw_ref, y_ref, s_ref, *, tb, L):
    t_idx = pl.program_id(1)
    kdim = H_C * DK_C
    n_pairs = kdim // LANES

    @pl.when(t_idx == 0)
    def _():
        s_ref[...] = s0_ref[...]

    qk = qk_ref[...]
    log_a = -_softplus(-(_mm(ad_ref[...], a2_ref[...]) + ab_ref[...])) * (1.0 / GLA_NORMALIZER)
    vv = v_ref[...]
    gg = g_ref[...]
    nw = nw_ref[...]
    incl = _iota((L, L), 1) <= _iota((L, L), 0)
    m0 = _iota((tb, LANES), 1) < DK_C
    n_chunks = tb // L

    cum = _chunk_cumsum(log_a, L)
    tot = _chunk_last(cum, L)
    q_in = qk[:, :kdim] * DK_C ** -0.5 * jnp.exp(cum)
    k_in = (qk[:, kdim:] * jnp.exp(-cum)).astype(BF16)
    k_out = qk[:, kdim:] * jnp.exp(tot - cum)
    dec = jnp.exp(tot)
    vb16 = vv.astype(BF16)

    rs = {c: slice(c * L, (c + 1) * L) for c in range(n_chunks)}
    items = [(c, h) for c in range(n_chunks) for h in range(H_C)]
    qm, km = {}, {}
    for h in range(H_C):
        ls = slice((h // 2) * LANES, (h // 2 + 1) * LANES)
        mask = m0 if h % 2 == 0 else jnp.logical_not(m0)
        qm[h] = jnp.where(mask, q_in[:, ls], 0.0).astype(BF16)
        km[h] = jnp.where(mask, k_out[:, ls], 0.0).astype(BF16)
    scores = {(c, h): jnp.where(incl, _mm_nt(qm[h][rs[c]], k_in[rs[c], (h // 2) * LANES:(h // 2 + 1) * LANES]), 0.0)
              .astype(BF16) for c, h in items}
    upd = {(c, h): _mm_tn(vb16[rs[c], h * DV_C:(h + 1) * DV_C], km[h][rs[c]]) for c, h in items}
    state = [s_ref[p] for p in range(n_pairs)]
    s_prev = {}
    for c in range(n_chunks):
        for p in range(n_pairs):
            s_prev[(c, p)] = state[p].astype(BF16)
            state[p] = (state[p] * dec[c * L:c * L + 1, p * LANES:(p + 1) * LANES]
                        + upd[(c, 2 * p)] + upd[(c, 2 * p + 1)])
    for p in range(n_pairs):
        s_ref[p] = state[p]
    for c, h in items:
        hs = slice(h * DV_C, (h + 1) * DV_C)
        o = _mm_nt(qm[h][rs[c]], s_prev[(c, h // 2)]) + _mm(scores[(c, h)], vb16[rs[c], hs])
        on = o * lax.rsqrt(jnp.mean(o * o, axis=-1, keepdims=True) + EPS) * nw
        y_ref[rs[c], hs] = (on * _silu(gg[rs[c], hs])).astype(y_ref.dtype)


def _gla(cqk, cv, cad, cg, s0, wts, L):
    b, t, _ = cqk.shape
    kdim = H_C * DK_C
    vdim = H_C * DV_C
    n_pairs = kdim // LANES
    tb = min(t, 256)
    rank = wts['a2'].shape[0]
    a2p = jnp.pad(wts['a2'].astype(F32), ((0, LANES - rank), (0, 0))).astype(BF16)
    params = [a2p, wts['a_bias'].reshape(1, -1).astype(F32), wts['norm_w'].reshape(1, -1).astype(F32)]
    st0 = jnp.swapaxes(s0.astype(F32), -1, -2).reshape(b, n_pairs, 2, DV_C, DK_C)
    st0 = jnp.concatenate([st0[:, :, 0], st0[:, :, 1]], axis=-1)
    blk = lambda w: pl.BlockSpec((None, tb, w), lambda i, j: (i, j, 0))
    per_b = lambda shape: pl.BlockSpec((None,) + shape, lambda i, j: (i,) + (0,) * len(shape))
    y, st = pl.pallas_call(
        functools.partial(_gla_body, tb=tb, L=L),
        out_shape=[jax.ShapeDtypeStruct((b, t, vdim), BF16), jax.ShapeDtypeStruct((b, n_pairs, DV_C, LANES), F32)],
        grid=(b, t // tb),
        in_specs=[blk(2 * kdim), blk(vdim), blk(LANES), blk(vdim), per_b((n_pairs, DV_C, LANES))]
        + [_resident(p.shape) for p in params],
        out_specs=[blk(vdim), per_b((n_pairs, DV_C, LANES))],
        compiler_params=_cparams(("parallel", "arbitrary")),
        name="gla",
    )(cqk, cv, cad, cg, st0, *params)
    st = jnp.stack([st[..., :DK_C], st[..., DK_C:]], axis=2).reshape(b, H_C, DV_C, DK_C)
    return y, jnp.swapaxes(st, -1, -2)


def _ret_body(q_ref, k_ref, v_ref, g_ref, cos_ref, sin_ref, s0_ref, y_ref, s_ref, *, tb, L):
    t_idx = pl.program_id(1)
    hd = DK_D

    @pl.when(t_idx == 0)
    def _():
        s_ref[...] = s0_ref[...]

    cos = cos_ref[...]
    sin = sin_ref[...]
    qq, kk, vv, gg = q_ref[...], k_ref[...], v_ref[...], g_ref[...]
    rw = _iota((L, L), 0)
    cl_ = _iota((L, L), 1)
    incl = cl_ <= rw
    dist = jnp.where(incl, rw - cl_, 0).astype(F32)
    pos = _iota((L, LANES), 0).astype(F32)

    n_chunks = tb // L
    rs = {c: slice(c * L, (c + 1) * L) for c in range(n_chunks)}
    items = [(c, h) for c in range(n_chunks) for h in range(H_D)]
    lg = [math.log(1.0 - 2.0 ** (-5.0 - h)) for h in range(H_D)]
    vb16 = vv.astype(BF16)
    qh, kh, q_in, k_out, dmat = {}, {}, {}, {}, {}
    for h in range(H_D):
        hs = slice(h * hd, (h + 1) * hd)
        q_rot = qq[:, hs] * cos + pltpu.roll(qq[:, hs], hd // 2, 1) * sin
        k_rot = (kk[:, hs] * cos + pltpu.roll(kk[:, hs], hd // 2, 1) * sin) * hd ** -0.5
        dmat[h] = jnp.where(incl, jnp.exp(dist * lg[h]), 0.0)
        e_in = jnp.exp((pos + 1.0) * lg[h])
        e_out = jnp.exp((L - 1.0 - pos) * lg[h])
        qh[h], kh[h] = q_rot.astype(BF16), k_rot.astype(BF16)
        for c in range(n_chunks):
            q_in[(c, h)] = (q_rot[rs[c]] * e_in).astype(BF16)
            k_out[(c, h)] = (k_rot[rs[c]] * e_out).astype(BF16)
    scores = {(c, h): (_mm_nt(qh[h][rs[c]], kh[h][rs[c]]) * dmat[h]).astype(BF16) for c, h in items}
    upd = {(c, h): _mm_tn(k_out[(c, h)], vb16[rs[c], h * hd:(h + 1) * hd]) for c, h in items}
    state = [s_ref[h] for h in range(H_D)]
    s_prev = {}
    for c in range(n_chunks):
        for h in range(H_D):
            s_prev[(c, h)] = state[h].astype(BF16)
            state[h] = state[h] * math.exp(lg[h] * L) + upd[(c, h)]
    for h in range(H_D):
        s_ref[h] = state[h]
    for c, h in items:
        hs = slice(h * hd, (h + 1) * hd)
        o = _mm(q_in[(c, h)], s_prev[(c, h)]) + _mm(scores[(c, h)], vb16[rs[c], hs])
        mu = jnp.mean(o, axis=-1, keepdims=True)
        oc = o - mu
        on = oc * lax.rsqrt(jnp.mean(oc * oc, axis=-1, keepdims=True) + EPS)
        y_ref[rs[c], hs] = (on * _silu(gg[rs[c], hs])).astype(y_ref.dtype)


def _retention(dq, dk, dv, dg, s0, pos0, L):
    b, t, dim = dq.shape
    tb = min(t, 256)
    half = DK_D // 2
    inv = ROPE_BASE ** (-jnp.arange(0, DK_D, 2, dtype=F32) / DK_D)
    ang = (jnp.arange(t) + pos0).astype(F32)[:, None] * inv[None, :]
    cos = jnp.concatenate([jnp.cos(ang), jnp.cos(ang)], axis=-1)
    sin = jnp.concatenate([-jnp.sin(ang), jnp.sin(ang)], axis=-1)
    del half
    blk = pl.BlockSpec((None, tb, dim), lambda i, j: (i, j, 0))
    tab = pl.BlockSpec((tb, DK_D), lambda i, j: (j, 0))
    st = pl.BlockSpec((None, H_D, DK_D, DK_D), lambda i, j: (i, 0, 0, 0))
    return pl.pallas_call(
        functools.partial(_ret_body, tb=tb, L=L),
        out_shape=[jax.ShapeDtypeStruct((b, t, dim), BF16), jax.ShapeDtypeStruct((b, H_D, DK_D, DK_D), F32)],
        grid=(b, t // tb),
        in_specs=[blk, blk, blk, blk, tab, tab, st],
        out_specs=[blk, st],
        compiler_params=_cparams(("parallel", "arbitrary")),
        name="retention",
    )(dq, dk, dv, dg, cos, sin, s0.astype(F32))


def _prep_weights(W):
    bf = lambda x: x.astype(BF16)
    a_proj = 3 * H_A * N_A + 64 + 64 + 128
    b_conv = 3 * H_B * DK_B
    P = {}
    for name in ('ffn1_wg', 'ffn1_wu', 'ffn1_wd', 'ffn2_wg', 'ffn2_wu', 'ffn2_wd', 'mem_wq', 'mem_wo', 'mem_wk',
                 'mem_wv', 'even_w_out', 'odd_w_out'):
        P[name] = bf(W[name])
    ew = W['even_w_in']
    o = a_proj
    bab = ew[:, :, o + b_conv:o + b_conv + 2 * H_B]
    P['even_in'] = [bf(ew[:, :, :a_proj]), bf(ew[:, :, o:o + b_conv]),
                    bf(jnp.pad(bab, ((0, 0), (0, 0), (0, LANES - 2 * H_B)))), bf(ew[:, :, o + b_conv + 2 * H_B:])]
    ow = W['odd_w_in']
    kd, vd, rank = H_C * DK_C, H_C * DV_C, W['gla_a2'].shape[1]
    c_proj = 2 * kd + 2 * vd + rank
    dd = H_D * DK_D
    cad = ow[:, :, 2 * kd + vd:2 * kd + vd + rank]
    P['odd_in'] = [bf(ow[:, :, :2 * kd]), bf(ow[:, :, 2 * kd:2 * kd + vd]),
                   bf(jnp.pad(cad, ((0, 0), (0, 0), (0, LANES - rank)))), bf(ow[:, :, 2 * kd + vd + rank:c_proj])] \
        + [bf(ow[:, :, c_proj + j * dd:c_proj + (j + 1) * dd]) for j in range(4)]
    return P


def _trunk(x, pos0, mem_k, mem_v, heads, shift, rwkv, conv, delta, gla, ret, W, P):
    b, t, d = x.shape
    n = b * t
    depth = W['norm_ffn1'].shape[0]
    L_delta = min(64, t)
    L_lin = min(64, t)
    flat = lambda z: z.reshape(n, z.shape[-1])
    unflat = lambda z: z.reshape(b, t, z.shape[-1])
    new = {k: [] for k in ('shift', 'rwkv', 'conv', 'delta', 'gla', 'ret')}
    x = flat(x)
    for l in range(depth):
        i = l // 2
        x = _ffn(x, W['norm_ffn1'][l], P['ffn1_wg'][l], P['ffn1_wu'][l], P['ffn1_wd'][l])
        if l % 2 == 0:
            (pa, dq_, dk_, dv_, bab, z), s3 = _even_in_proj(unflat(x), W['norm_mix'][l], [w[i] for w in P['even_in']],
                                                           conv[i], W['delta_conv_w'][i])
            wa = dict(mu=W['rwkv_mu'][i], w0=W['rwkv_w0'][i], w2=W['rwkv_w2'][i], a0=W['rwkv_a0'][i],
                      a2=W['rwkv_a2'][i], g2=W['rwkv_g2'][i], kk=W['rwkv_kk'][i], ka=W['rwkv_ka'][i],
                      rk=W['rwkv_rk'][i], ln_w=W['rwkv_ln_w'][i], ln_b=W['rwkv_ln_b'][i])
            y_a, s1, s2 = _rwkv(pa, shift[i], rwkv[i], wa, L_delta)
            wb = dict(A_log=W['delta_A_log'][i], dt_bias=W['delta_dt_bias'][i], norm_w=W['delta_norm_w'][i])
            y_b, s4 = _delta(dq_, dk_, dv_, bab, z, delta[i], wb, L_delta)
            new['shift'].append(s1)
            new['rwkv'].append(s2)
            new['conv'].append(s3)
            new['delta'].append(s4)
            split = H_A * N_A
            halves, w_out = (y_a, y_b), P['even_w_out'][i]
        else:
            cqk, cv, cad, cg, dq, dk, dv, dg = _norm_proj(x, W['norm_mix'][l], [w[i] for w in P['odd_in']],
                                                          bf16_out=(1, 6))
            wc = dict(a2=W['gla_a2'][i], a_bias=W['gla_a_bias'][i], norm_w=W['gla_norm_w'][i])
            y_c, s5 = _gla(unflat(cqk), unflat(cv), unflat(cad), unflat(cg), gla[i], wc, L_lin)
            y_d, s6 = _retention(unflat(dq), unflat(dk), unflat(dv), unflat(dg), ret[i], pos0, L_lin)
            new['gla'].append(s5)
            new['ret'].append(s6)
            split = H_C * DV_C
            halves, w_out = (y_c, y_d), P['odd_w_out'][i]
        mk = mem_k[l].astype(BF16)
        mv = mem_v[l].astype(BF16)
        x = flat(_mix_out_mem_attn(unflat(x), halves[0], halves[1], w_out[:split], w_out[split:], W['norm_mem'][l],
                                   P['mem_wq'][l], mk, mv, P['mem_wo'][l], heads))
        fw = W['final_norm'] if l == depth - 1 else None
        x = _ffn(x, W['norm_ffn2'][l], P['ffn2_wg'][l], P['ffn2_wu'][l], P['ffn2_wd'][l], fw)
    return unflat(x), new


def kernel(x_prompt, x_sample, mem_prompt, state_rwkv_shift, state_rwkv, state_delta_conv, state_delta, state_gla, state_ret, cache_mem_k, cache_mem_v, norm_ffn1, ffn1_wg, ffn1_wu, ffn1_wd, norm_mix, even_w_in, even_w_out, rwkv_mu, rwkv_w0, rwkv_w2, rwkv_a0, rwkv_a2, rwkv_g2, rwkv_kk, rwkv_ka, rwkv_rk, rwkv_ln_w, rwkv_ln_b, delta_conv_w, delta_A_log, delta_dt_bias, delta_norm_w, odd_w_in, odd_w_out, gla_a2, gla_a_bias, gla_norm_w, norm_mem, mem_norm_kv, mem_wq, mem_wk, mem_wv, mem_wo, norm_ffn2, ffn2_wg, ffn2_wu, ffn2_wd, final_norm):
    W = dict(norm_ffn1=norm_ffn1, ffn1_wg=ffn1_wg, ffn1_wu=ffn1_wu, ffn1_wd=ffn1_wd, norm_mix=norm_mix,
             even_w_in=even_w_in, even_w_out=even_w_out, rwkv_mu=rwkv_mu, rwkv_w0=rwkv_w0, rwkv_w2=rwkv_w2,
             rwkv_a0=rwkv_a0, rwkv_a2=rwkv_a2, rwkv_g2=rwkv_g2, rwkv_kk=rwkv_kk, rwkv_ka=rwkv_ka,
             rwkv_rk=rwkv_rk, rwkv_ln_w=rwkv_ln_w, rwkv_ln_b=rwkv_ln_b, delta_conv_w=delta_conv_w,
             delta_A_log=delta_A_log, delta_dt_bias=delta_dt_bias, delta_norm_w=delta_norm_w,
             odd_w_in=odd_w_in, odd_w_out=odd_w_out, gla_a2=gla_a2, gla_a_bias=gla_a_bias,
             gla_norm_w=gla_norm_w, norm_mem=norm_mem, mem_wq=mem_wq, mem_wk=mem_wk, mem_wv=mem_wv,
             mem_wo=mem_wo, norm_ffn2=norm_ffn2, ffn2_wg=ffn2_wg, ffn2_wu=ffn2_wu, ffn2_wd=ffn2_wd,
             final_norm=final_norm)
    P = _prep_weights(W)
    dt = x_prompt.dtype
    bp, _, d = x_prompt.shape
    depth = norm_ffn1.shape[0]
    n_even, n_odd = (depth + 1) // 2, depth // 2
    heads, hd = cache_mem_k.shape[3], cache_mem_k.shape[4]
    n_mem = mem_prompt.shape[1]
    pk, pv, pk_flat, pv_flat = [], [], [], []
    mem_flat = mem_prompt.reshape(bp * n_mem, d)
    for l in range(depth):
        mk, mv = _norm_proj(mem_flat, mem_norm_kv[l], [P['mem_wk'][l], P['mem_wv'][l]])
        pk_flat.append(mk.reshape(bp, n_mem, d))
        pv_flat.append(mv.reshape(bp, n_mem, d))
        pk.append(mk.reshape(bp, n_mem, heads, hd))
        pv.append(mv.reshape(bp, n_mem, heads, hd))
    zeros = lambda ref, cnt: [jnp.zeros((bp,) + ref.shape[2:], F32)] * cnt
    y_prompt, ps = _trunk(x_prompt, 0, pk_flat, pv_flat, heads, zeros(state_rwkv_shift, n_even), zeros(state_rwkv, n_even),
                          zeros(state_delta_conv, n_even), zeros(state_delta, n_even), zeros(state_gla, n_odd),
                          zeros(state_ret, n_odd), W, P)
    bs = x_sample.shape[0]
    y_sample, ss = _trunk(x_sample, PAST_LEN, [cache_mem_k[l].reshape(bs, n_mem, d) for l in range(depth)],
                          [cache_mem_v[l].reshape(bs, n_mem, d) for l in range(depth)], heads,
                          [state_rwkv_shift[i] for i in range(n_even)], [state_rwkv[i] for i in range(n_even)],
                          [state_delta_conv[i] for i in range(n_even)], [state_delta[i] for i in range(n_even)],
                          [state_gla[i] for i in range(n_odd)], [state_ret[i] for i in range(n_odd)], W, P)
    st = lambda xs: jnp.stack(xs).astype(dt)
    order = ('shift', 'rwkv', 'conv', 'delta', 'gla', 'ret')
    return ((y_prompt, y_sample) + tuple(st(ps[k]) for k in order) + (st(pk), st(pv))
            + tuple(st(ss[k]) for k in order))
```

```python
import functools
import math

import jax
import jax.numpy as jnp
from jax import lax
from jax.experimental import pallas as pl
from jax.experimental.pallas import tpu as pltpu

F32 = jnp.float32
BF16 = jnp.bfloat16
EPS = 1e-6
RWKV_LN_EPS = 64e-5
GLA_NORMALIZER = 16.0
ROPE_BASE = 10000.0
PAST_LEN = 1024

LANES = 128
MXU_DIM = 256
VMEM_LIMIT = 56 * 1024 * 1024

H_A, N_A = 8, 64
H_B, DK_B = 4, 128
H_C, DK_C, DV_C = 4, 64, 128
H_D, DK_D = 4, 128
CONV_W = 4
CONV_CARRY = 8
MIXER_ROWS = 512
CONV_ROWS = 128


def _cparams(sem):
    return pltpu.CompilerParams(dimension_semantics=sem, vmem_limit_bytes=VMEM_LIMIT)


def _mm(a, b):
    return jnp.dot(a.astype(BF16), b.astype(BF16), preferred_element_type=F32)


def _mm_nt(a, b):
    return lax.dot_general(a.astype(BF16), b.astype(BF16), (((1,), (1,)), ((), ())),
                           preferred_element_type=F32)


def _mm_tn(a, b):
    return lax.dot_general(a.astype(BF16), b.astype(BF16), (((0,), (0,)), ((), ())),
                           preferred_element_type=F32)


def _split3(x):
    hi = x.astype(BF16)
    r = x - hi.astype(F32)
    mid = r.astype(BF16)
    lo = (r - mid.astype(F32)).astype(BF16)
    return hi, mid, lo


def _mm_exact_lhs01(a01, x):
    hi, mid, lo = _split3(x)
    a = a01.astype(BF16)
    return (jnp.dot(a, hi, preferred_element_type=F32) + jnp.dot(a, mid, preferred_element_type=F32)
            + jnp.dot(a, lo, preferred_element_type=F32))


def _iota(shape, dim):
    return lax.broadcasted_iota(jnp.int32, shape, dim)


def _tri_incl(n):
    return (_iota((n, n), 1) <= _iota((n, n), 0)).astype(F32)


def _chunk_tri(tb, L):
    shift = L.bit_length() - 1
    assert 1 << shift == L
    rt, ct = _iota((tb, tb), 0), _iota((tb, tb), 1)
    same = jnp.right_shift(rt, shift) == jnp.right_shift(ct, shift)
    return jnp.logical_and(same, ct <= rt).astype(F32)


def _chunk_cumsum(x, L):
    n = x.shape[0]
    g = min(n, MXU_DIM)
    tri = _chunk_tri(g, L)
    return jnp.concatenate([_mm_exact_lhs01(tri, x[r:r + g]) for r in range(0, n, g)], axis=0)


def _chunk_last(cum, L):
    tb, w = cum.shape
    return jnp.concatenate([jnp.broadcast_to(cum[c * L + L - 1:c * L + L, :], (L, w)) for c in range(tb // L)], axis=0)


def _sigmoid(x):
    return 0.5 * (jnp.tanh(0.5 * x) + 1.0)


def _silu(x):
    return x * _sigmoid(x)


def _softplus(x):
    return jnp.maximum(x, 0.0) + jnp.log(1.0 + jnp.exp(-jnp.abs(x)))


def _rms(x, w):
    return x * lax.rsqrt(jnp.mean(x * x, axis=-1, keepdims=True) + EPS) * w


def _inv_unit_lower(n_mat, size):
    dim = n_mat.shape[0]
    eye = (_iota((dim, dim), 0) == _iota((dim, dim), 1)).astype(F32)
    t = eye + n_mat
    m = n_mat
    power = 2
    while power < size:
        m = _mm(m, m)
        t = t + _mm(t, m)
        power *= 2
    return t


def _row_tile(n, target):
    t = min(n, target)
    while n % t:
        t //= 2
    return t


def _mixer_block(b, t, L):
    tb = min(t, 2 * L)
    nb = max(1, min(b, MIXER_ROWS // tb))
    while b % nb:
        nb -= 1
    return nb, tb


def _resident(shape):
    nd = len(shape)
    return pl.BlockSpec(shape, lambda *_: (0,) * nd, pipeline_mode=pl.Buffered(1))


def _ffn_body(x_ref, nw_ref, wg_ref, wu_ref, wd_ref, fw_ref, o_ref, *, chunk, final):
    x = x_ref[...]
    h = _rms(x, nw_ref[...]).astype(BF16)
    d_ff = wg_ref.shape[1]
    n_chunks = d_ff // chunk

    def gate_up(c):
        sl = slice(c * chunk, (c + 1) * chunk)
        return (jnp.dot(h, wg_ref[:, sl], preferred_element_type=F32),
                jnp.dot(h, wu_ref[:, sl], preferred_element_type=F32))

    acc = None
    nxt = gate_up(0)
    for c in range(n_chunks):
        g, u = nxt
        if c + 1 < n_chunks:
            nxt = gate_up(c + 1)
        a = (_silu(g) * u).astype(BF16)
        d = jnp.dot(a, wd_ref[c * chunk:(c + 1) * chunk, :], preferred_element_type=F32)
        acc = d if acc is None else acc + d
    y = x + 0.5 * acc
    if final:
        y = _rms(y, fw_ref[...])
    o_ref[...] = y


def _ffn(x, nw, wg, wu, wd, fw=None):
    n, d = x.shape
    d_ff = wg.shape[1]
    tm = _row_tile(n, 1024)
    final = fw is not None
    if fw is None:
        fw = nw
    return pl.pallas_call(
        functools.partial(_ffn_body, chunk=MXU_DIM, final=final),
        out_shape=jax.ShapeDtypeStruct((n, d), F32),
        grid=(n // tm,),
        in_specs=[pl.BlockSpec((tm, d), lambda i: (i, 0)), _resident((1, d)), _resident((d, d_ff)),
                  _resident((d, d_ff)), _resident((d_ff, d)), _resident((1, d))],
        out_specs=pl.BlockSpec((tm, d), lambda i: (i, 0)),
        compiler_params=_cparams(("parallel",)),
        name="ffn",
    )(x, nw.reshape(1, d), wg, wu, wd, fw.reshape(1, d))


def _norm_proj_body(*refs, n_w):
    x_ref, nw_ref = refs[0], refs[1]
    w_refs = refs[2:2 + n_w]
    o_refs = refs[2 + n_w:]
    h = _rms(x_ref[...], nw_ref[...]).astype(BF16)
    for w_ref, o_ref in zip(w_refs, o_refs):
        o_ref[...] = jnp.dot(h, w_ref[...], preferred_element_type=F32).astype(o_ref.dtype)


def _norm_proj(x, nw, ws, bf16_out=()):
    n, d = x.shape
    tm = _row_tile(n, 512)
    return pl.pallas_call(
        functools.partial(_norm_proj_body, n_w=len(ws)),
        out_shape=[jax.ShapeDtypeStruct((n, w.shape[1]), BF16 if i in bf16_out else F32) for i, w in enumerate(ws)],
        grid=(n // tm,),
        in_specs=[pl.BlockSpec((tm, d), lambda i: (i, 0)), _resident((1, d))] + [_resident(w.shape) for w in ws],
        out_specs=[pl.BlockSpec((tm, w.shape[1]), lambda i: (i, 0)) for w in ws],
        compiler_params=_cparams(("parallel",)),
        name="norm_proj",
    )(x, nw.reshape(1, d), *ws)


def _mix_out_mem_attn_body(x_ref, ya_ref, yb_ref, wa_ref, wb_ref, nw_ref, wq_ref, k_ref, v_ref, wo_ref, o_ref, *,
                           heads):
    x = (x_ref[...] + jnp.dot(ya_ref[...], wa_ref[...], preferred_element_type=F32)
         + jnp.dot(yb_ref[...], wb_ref[...], preferred_element_type=F32))
    h = _rms(x, nw_ref[...]).astype(BF16)
    q = jnp.dot(h, wq_ref[...], preferred_element_type=F32).astype(BF16)
    d = q.shape[1]
    hd = d // heads

    def scores(i):
        sl = slice(i * hd, (i + 1) * hd)
        return _mm_nt(q[:, sl], k_ref[:, sl]) * hd ** -0.5

    outs = []
    nxt = scores(0)
    for i in range(heads):
        s = nxt
        if i + 1 < heads:
            nxt = scores(i + 1)
        e = jnp.exp(s - jnp.max(s, axis=-1, keepdims=True))
        pr = e / jnp.sum(e, axis=-1, keepdims=True)
        outs.append(_mm(pr, v_ref[:, i * hd:(i + 1) * hd]))
    o = jnp.concatenate(outs, axis=-1).astype(BF16)
    o_ref[...] = x + jnp.dot(o, wo_ref[...], preferred_element_type=F32)


def _mix_out_mem_attn(x, ya, yb, wa, wb, nw, wq, mk, mv, wo, heads):
    b, t, d = x.shape
    m = mk.shape[1]
    tm = _row_tile(t, 512)
    row = lambda w: pl.BlockSpec((None, tm, w), lambda i, j: (i, j, 0))
    mem = pl.BlockSpec((None, m, d), lambda i, j: (i, 0, 0))
    return pl.pallas_call(
        functools.partial(_mix_out_mem_attn_body, heads=heads),
        out_shape=jax.ShapeDtypeStruct((b, t, d), F32),
        grid=(b, t // tm),
        in_specs=[row(d), row(ya.shape[2]), row(yb.shape[2]), _resident(wa.shape), _resident(wb.shape),
                  _resident((1, d)), _resident((d, d)), mem, mem, _resident((d, d))],
        out_specs=row(d),
        compiler_params=_cparams(("parallel", "parallel")),
        name="mix_out_mem_attn",
    )(x, ya, yb, wa, wb, nw.reshape(1, d), wq, mk, mv, wo)


def _seg_sum(x, bd):
    g = bd.shape[0]
    xb = x.astype(BF16)
    return jnp.concatenate([jnp.dot(xb[:, j:j + g], bd, preferred_element_type=F32)
                            for j in range(0, x.shape[1], g)], axis=1)


def _rwkv_body(pa_ref, shift0_ref, s0_ref, mu_ref, w0_ref, w2_ref, a0_ref, a2_ref, g2_ref, kkw_ref, ka_ref,
               rk_ref, lnw_ref, lnb_ref, bd_ref, y_ref, shift_ref, s_ref, y_scr, *, nb, tb, L):
    t_idx = pl.program_id(1)
    a_dim = H_A * N_A
    n_pairs = a_dim // LANES

    @pl.when(t_idx == 0)
    def _():
        shift_ref[...] = shift0_ref[...]
        s_ref[...] = s0_ref[...]

    row = _iota((tb, pa_ref.shape[2]), 0)
    segs, prevs = [], []
    for b in range(nb):
        seg = pa_ref[b]
        segs.append(seg)
        prevs.append(jnp.where(row == 0, shift_ref[b], pltpu.roll(seg, 1, 0)))
        shift_ref[b] = seg[tb - 1:tb, :]
    pa = jnp.concatenate(segs, axis=0)
    prev = jnp.concatenate(prevs, axis=0)
    rows = nb * tb
    xa = pa + (prev - pa) * mu_ref[...]
    r = xa[:, 0:a_dim]
    k = xa[:, a_dim:2 * a_dim]
    v = xa[:, 2 * a_dim:3 * a_dim]
    xwa = xa[:, 3 * a_dim:3 * a_dim + LANES]
    xg = xa[:, 3 * a_dim + LANES:]
    wlog = -_softplus(-(w0_ref[...] + _mm(jnp.tanh(xwa), w2_ref[...]))) - 0.5
    logdec = -jnp.exp(wlog)
    a = _sigmoid(a0_ref[...] + _mm(xwa, a2_ref[...]))
    gate = _mm(_sigmoid(xg), g2_ref[...])
    bd = bd_ref[...]
    kq = k * kkw_ref[...]
    kk = kq * lax.rsqrt(_seg_sum(kq * kq, bd) + EPS)
    k2 = k * (1.0 + (a - 1.0) * ka_ref[...])
    av = -kk
    bv = kk * a

    lane = _iota((L, LANES), 1)
    m0 = lane < N_A
    r2 = _iota((2 * L, 2 * L), 0)
    c2 = _iota((2 * L, 2 * L), 1)
    bdm = jnp.logical_not(jnp.logical_xor(r2 >= L, c2 >= L))
    r_loc = jnp.where(r2 >= L, r2 - L, r2)
    c_loc = jnp.where(c2 >= L, c2 - L, c2)
    strict_t = jnp.logical_and(bdm, c_loc > r_loc)
    strict_l = jnp.bitwise_and(_iota((L, 2 * L), 1), L - 1) < _iota((L, 2 * L), 0)
    incl_l = jnp.bitwise_and(_iota((L, 4 * L), 1), L - 1) <= _iota((L, 4 * L), 0)
    bdl = jnp.logical_not(jnp.logical_xor(_iota((2 * L, LANES), 0) >= L, _iota((2 * L, LANES), 1) >= N_A))
    eye2 = (r2 == c2).astype(F32)
    zero_b = jnp.zeros((L, LANES), BF16)

    def stack2(x):
        xb = x.astype(BF16)
        return jnp.concatenate([jnp.where(m0, xb, zero_b), jnp.where(m0, zero_b, xb)], axis=0)

    def dup2(x):
        xb = x.astype(BF16)
        return jnp.concatenate([xb, xb], axis=0)

    cum = _chunk_cumsum(logdec, L)
    tot = _chunk_last(cum, L)
    e_neg = jnp.exp(-cum)
    e_out = jnp.exp(tot - cum)
    a_t = av * jnp.exp(cum - logdec)
    r_t = r * jnp.exp(cum)
    b_t = bv * e_neg
    k_t = k2 * e_neg
    b_o = bv * e_out
    k_o = k2 * e_out
    g_l = jnp.exp(tot)

    per_seg = tb // L
    items = [(c, p) for c in range(rows // L) for p in range(n_pairs)]
    sl = {(c, p): (slice(c * L, (c + 1) * L), slice(p * LANES, (p + 1) * LANES)) for c, p in items}
    xa = {it: stack2(a_t[sl[it]]) for it in items}
    ra = {it: r_t[sl[it]].astype(BF16) for it in items}
    yb = {it: dup2(b_t[sl[it]]) for it in items}
    sbk = {it: jnp.concatenate([stack2(b_t[sl[it]]), stack2(k_t[sl[it]])], axis=0) for it in items}
    v2 = {it: jnp.where(bdl, dup2(v[sl[it]]), jnp.zeros((2 * L, LANES), BF16)) for it in items}
    wst = {it: jnp.concatenate([stack2(b_o[sl[it]]), stack2(k_o[sl[it]])], axis=0) for it in items}
    n_t = {it: jnp.where(strict_t, _mm_nt(yb[it], xa[it]), 0.0) for it in items}
    ak = {it: jnp.where(strict_l, _mm_nt(a_t[sl[it]], sbk[it][2 * L:]), 0.0).astype(BF16) for it in items}
    rbk = {it: jnp.where(incl_l, _mm_nt(ra[it], sbk[it]), 0.0).astype(BF16) for it in items}
    akv = {it: stack2(_mm(ak[it], v2[it])) for it in items}
    t_t = {it: eye2 + n_t[it] for it in items}
    n_b = {it: n_t[it].astype(BF16) for it in items}
    m_t = {it: _mm(n_b[it], n_b[it]).astype(BF16) for it in items}
    power = 4
    while power < L:
        prod = {it: _mm(m_t[it], jnp.concatenate([m_t[it], t_t[it].astype(BF16)], axis=1)) for it in items}
        m_t = {it: prod[it][:, :2 * L].astype(BF16) for it in items}
        t_t = {it: t_t[it] + prod[it][:, 2 * L:] for it in items}
        power *= 2
    if L > 2:
        t_t = {it: t_t[it] + _mm(m_t[it], t_t[it]) for it in items}
    om_up = {it: _mm_tn(t_t[it], jnp.concatenate([xa[it], akv[it]], axis=1)) for it in items}
    omega = {it: om_up[it][:, :LANES].astype(BF16) for it in items}
    chains = [(b, p) for b in range(nb) for p in range(n_pairs)]
    state = {bp: s_ref[bp[0], bp[1]] for bp in chains}
    s_prev, uv = {}, {}
    for k_pos in range(per_seg):
        for b, p in chains:
            it = (b * per_seg + k_pos, p)
            s_prev[it] = state[(b, p)].astype(BF16)
            u_st = _mm_nt(omega[it], s_prev[it]) + om_up[it][:, LANES:]
            uv[it] = jnp.concatenate([u_st.astype(BF16), v2[it]], axis=0)
        for b, p in chains:
            it = (b * per_seg + k_pos, p)
            state[(b, p)] = state[(b, p)] * g_l[it[0] * L:it[0] * L + 1, sl[it][1]] + _mm_tn(uv[it], wst[it])
    for b, p in chains:
        s_ref[b, p] = state[(b, p)]
    for it in items:
        y_scr[sl[it]] = _mm_nt(ra[it], s_prev[it]) + _mm(rbk[it], uv[it])

    y = y_scr[...]
    inv_n = 1.0 / N_A
    mu_y = _seg_sum(y, bd) * inv_n
    yc = y - mu_y
    var = _seg_sum(yc * yc, bd) * inv_n
    yn = yc * lax.rsqrt(var + RWKV_LN_EPS) * lnw_ref[...] + lnb_ref[...]
    bonus = _seg_sum(r * k2 * rk_ref[...], bd) * v
    out = ((yn + bonus) * gate).astype(y_ref.dtype)
    for b in range(nb):
        y_ref[b] = out[b * tb:(b + 1) * tb, :]


def _pair_blockdiag(s):
    b, h, n, _ = s.shape
    s = s.reshape(b, h // 2, 2, n, n)
    z = jnp.zeros((b, h // 2, n, n), s.dtype)
    top = jnp.concatenate([s[:, :, 0], z], axis=-1)
    bot = jnp.concatenate([z, s[:, :, 1]], axis=-1)
    return jnp.concatenate([top, bot], axis=-2)


def _pair_unblock(s):
    b, p, n2, _ = s.shape
    n = n2 // 2
    return jnp.stack([s[:, :, :n, :n], s[:, :, n:, n:]], axis=2).reshape(b, 2 * p, n, n)


def _rwkv(pa, shift0, s0, wts, L):
    b, t, pw = pa.shape
    a_dim = H_A * N_A
    nb, tb = _mixer_block(b, t, L)
    n_pairs = a_dim // LANES
    seg = _iota((MXU_DIM, MXU_DIM), 0) // N_A == _iota((MXU_DIM, MXU_DIM), 1) // N_A
    bd = seg.astype(BF16)
    z64 = jnp.zeros((N_A, a_dim), F32)
    w2p = jnp.concatenate([wts['w2'], z64], axis=0)
    a2p = jnp.concatenate([z64, wts['a2']], axis=0)
    vec = lambda x: x.reshape(1, -1).astype(F32)
    params = [vec(wts['mu']), vec(wts['w0']), w2p.astype(BF16), vec(wts['a0']), a2p.astype(BF16),
              wts['g2'].astype(BF16), vec(wts['kk']), vec(wts['ka']), vec(wts['rk']), vec(wts['ln_w']),
              vec(wts['ln_b']), bd]
    blk_t = lambda w: pl.BlockSpec((nb, tb, w), lambda i, j: (i, j, 0))
    per_b = lambda shape: pl.BlockSpec((nb,) + shape, lambda i, j: (i,) + (0,) * len(shape))
    y, shift, s_new = pl.pallas_call(
        functools.partial(_rwkv_body, nb=nb, tb=tb, L=L),
        out_shape=[jax.ShapeDtypeStruct((b, t, a_dim), BF16), jax.ShapeDtypeStruct((b, 1, pw), F32),
                   jax.ShapeDtypeStruct((b, n_pairs, LANES, LANES), F32)],
        grid=(b // nb, t // tb),
        in_specs=[blk_t(pw), per_b((1, pw)), per_b((n_pairs, LANES, LANES))] + [_resident(p.shape) for p in params],
        out_specs=[blk_t(a_dim), per_b((1, pw)), per_b((n_pairs, LANES, LANES))],
        scratch_shapes=[pltpu.VMEM((nb * tb, a_dim), F32)],
        compiler_params=_cparams(("parallel", "arbitrary")),
        name="rwkv7",
    )(pa, shift0, _pair_blockdiag(s0.astype(F32)), *params)
    return y, shift, _pair_unblock(s_new)


def _even_in_body(x_ref, xh_ref, conv0_ref, nw_ref, wa_ref, wqkv_ref, wbab_ref, wz_ref, cw_ref,
                  pa_ref, q_ref, k_ref, v_ref, bab_ref, z_ref, tail_ref, cat_scr, *, tm):
    j = pl.program_id(1)
    hd = DK_B
    inner = H_B * hd
    nw = nw_ref[...]
    halo = jnp.dot(_rms(xh_ref[...], nw).astype(BF16), wqkv_ref[...], preferred_element_type=F32)
    cat_scr[0:CONV_CARRY, :] = jnp.where(j == 0, conv0_ref[...], halo)
    h = _rms(x_ref[...], nw).astype(BF16)
    cat_scr[CONV_CARRY:, :] = jnp.dot(h, wqkv_ref[...], preferred_element_type=F32)
    tail_ref[...] = cat_scr[tm:, :]
    cw = cw_ref[...]

    rows = min(tm, CONV_ROWS)

    def conv_piece(r0, c0):
        cols = slice(c0, c0 + hd)
        cat = cat_scr[r0:r0 + rows + CONV_CARRY, cols]
        acc = cat * cw[0:1, cols]
        for t in range(1, CONV_W):
            acc = pltpu.roll(acc, 1, 0) + cat * cw[t:t + 1, cols]
        return _silu(acc[CONV_CARRY:, :])

    def l2n(z):
        return z * lax.rsqrt(jnp.sum(z * z, axis=-1, keepdims=True) + EPS)

    pa_ref[...] = jnp.dot(h, wa_ref[...], preferred_element_type=F32)
    z_ref[...] = jnp.dot(h, wz_ref[...], preferred_element_type=F32)
    bab_ref[...] = jnp.dot(h, wbab_ref[...], preferred_element_type=F32)
    for r0 in range(0, tm, rows):
        rs = slice(r0, r0 + rows)
        for i in range(H_B):
            sl = slice(i * hd, (i + 1) * hd)
            q_ref[rs, sl] = (l2n(conv_piece(r0, i * hd)) * hd ** -0.5).astype(BF16)
            k_ref[rs, sl] = l2n(conv_piece(r0, inner + i * hd)).astype(BF16)
            v_ref[rs, sl] = conv_piece(r0, 2 * inner + i * hd).astype(BF16)


def _even_in_proj(x, nw, ws, conv0, conv_w):
    b, t, d = x.shape
    wa, wqkv, wbab, wz = ws
    cdim = wqkv.shape[1]
    inner = H_B * DK_B
    tm = _row_tile(t, 512)
    conv0p = jnp.pad(conv0.astype(F32), ((0, 0), (CONV_CARRY - (CONV_W - 1), 0), (0, 0)))
    row = lambda w: pl.BlockSpec((None, tm, w), lambda i, j: (i, j, 0))
    halo = pl.BlockSpec((None, CONV_CARRY, d), lambda i, j: (i, jnp.maximum(j * (tm // CONV_CARRY) - 1, 0), 0))
    per_b = pl.BlockSpec((None, CONV_CARRY, cdim), lambda i, j: (i, 0, 0))
    outs = pl.pallas_call(
        functools.partial(_even_in_body, tm=tm),
        out_shape=[jax.ShapeDtypeStruct((b, t, wa.shape[1]), F32)]
        + [jax.ShapeDtypeStruct((b, t, inner), BF16)] * 3
        + [jax.ShapeDtypeStruct((b, t, wbab.shape[1]), F32), jax.ShapeDtypeStruct((b, t, wz.shape[1]), F32),
           jax.ShapeDtypeStruct((b, CONV_CARRY, cdim), F32)],
        grid=(b, t // tm),
        in_specs=[row(d), halo, per_b, _resident((1, d)), _resident(wa.shape), _resident(wqkv.shape),
                  _resident(wbab.shape), _resident(wz.shape), _resident(conv_w.shape)],
        out_specs=[row(wa.shape[1]), row(inner), row(inner), row(inner), row(wbab.shape[1]), row(wz.shape[1]), per_b],
        scratch_shapes=[pltpu.VMEM((tm + CONV_CARRY, cdim), F32)],
        compiler_params=_cparams(("parallel", "arbitrary")),
        name="even_in_proj",
    )(x, x, conv0p, nw.reshape(1, d), wa, wqkv, wbab, wz, conv_w.astype(F32))
    return outs[:6], outs[6][:, CONV_CARRY - (CONV_W - 1):]


def _delta_body(q_ref, k_ref, v_ref, bab_ref, z_ref, s0_ref, alog_ref, dtb_ref, nw_ref,
                y_ref, s_ref, *, nb, tb, L):
    t_idx = pl.program_id(1)
    hd = DK_B
    rows = nb * tb
    per_seg = tb // L

    @pl.when(t_idx == 0)
    def _():
        s_ref[...] = s0_ref[...]

    flat = lambda ref: jnp.concatenate([ref[b] for b in range(nb)], axis=0)
    bab = flat(bab_ref)
    q_all, k_all, v_all = flat(q_ref), flat(k_ref), flat(v_ref)
    g_all = -jnp.exp(alog_ref[...]) * _softplus(bab + dtb_ref[...])
    beta_all = _sigmoid(bab)

    tri = _tri_incl(L).astype(BF16)
    rw = _iota((L, L), 0)
    cl_ = _iota((L, L), 1)
    strict = cl_ < rw
    incl = cl_ <= rw
    eye = (rw == cl_).astype(F32)
    nw = nw_ref[...]
    z = flat(z_ref)
    n_chunks = rows // L

    g_cum_all = _chunk_cumsum(g_all, L)
    g_tot_all = _chunk_last(g_cum_all, L)

    qh, kh, vh, kdec, qg, bv_, bk_, gb, beta_b, e_last = {}, {}, {}, {}, {}, {}, {}, {}, {}, {}
    for h in range(H_B):
        qh[h] = q_all[:, h * hd:(h + 1) * hd]
        kh[h] = k_all[:, h * hd:(h + 1) * hd]
        qs, ks = qh[h].astype(F32), kh[h].astype(F32)
        vs = v_all[:, h * hd:(h + 1) * hd].astype(F32)
        gb[h] = jnp.broadcast_to(g_all[:, h:h + 1], (rows, LANES))
        g_cum = jnp.broadcast_to(g_cum_all[:, h:h + 1], (rows, LANES))
        g_tot = jnp.broadcast_to(g_tot_all[:, h:h + 1], (rows, LANES))
        bb = jnp.broadcast_to(beta_all[:, H_B + h:H_B + h + 1], (rows, LANES))
        e_g = jnp.exp(g_cum)
        kdec[h] = (ks * jnp.exp(g_tot - g_cum)).astype(BF16)
        qg[h] = qs * e_g
        bv_[h] = bb * vs
        bk_[h] = bb * e_g * ks
        beta_b[h] = bb
        e_last[h] = jnp.exp(g_tot)

    items = [(c, h) for c in range(n_chunks) for h in range(H_B)]
    rs = {c: slice(c * L, (c + 1) * L) for c in range(n_chunks)}
    diff = {}
    for c, h in items:
        hi, mid, lo = _split3(jnp.where(strict, gb[h][rs[c], :L], 0.0))
        diff[(c, h)] = (jnp.dot(tri, hi, preferred_element_type=F32) + jnp.dot(tri, mid, preferred_element_type=F32)
                        + jnp.dot(tri, lo, preferred_element_type=F32))
    kk_ = {(c, h): _mm_nt(kh[h][rs[c]], kh[h][rs[c]]) for c, h in items}
    qk_ = {(c, h): _mm_nt(qh[h][rs[c]], kh[h][rs[c]]) for c, h in items}
    n_mat, qk_m = {}, {}
    for c, h in items:
        e_diff = jnp.exp(jnp.where(incl, diff[(c, h)], 0.0))
        n_mat[(c, h)] = -(beta_b[h][rs[c], :L] * jnp.where(strict, kk_[(c, h)] * e_diff, 0.0))
        qk_m[(c, h)] = jnp.where(incl, qk_[(c, h)] * e_diff, 0.0).astype(BF16)
    t_inv = {it: eye + n_mat[it] for it in items}
    m_pow = n_mat
    power = 2
    while power < L:
        m_pow = {it: _mm(m_pow[it], m_pow[it]) for it in items}
        t_inv = {it: t_inv[it] + _mm(t_inv[it], m_pow[it]) for it in items}
        power *= 2
    sol = {(c, h): _mm(t_inv[(c, h)], jnp.concatenate([bv_[h][rs[c]], bk_[h][rs[c]]], axis=1)) for c, h in items}
    lhs = {(c, h): jnp.concatenate([sol[(c, h)][:, hd:], qg[h][rs[c]]], axis=0).astype(BF16) for c, h in items}
    chains = [(b, h) for b in range(nb) for h in range(H_B)]
    state = {bh: s_ref[bh[0], bh[1]] for bh in chains}
    u_, o_inter = {}, {}
    for k_pos in range(per_seg):
        for b, h in chains:
            c = b * per_seg + k_pos
            ws = _mm(lhs[(c, h)], state[(b, h)])
            u_[(c, h)] = (sol[(c, h)][:, :hd] - ws[:L]).astype(BF16)
            o_inter[(c, h)] = ws[L:]
        for b, h in chains:
            c = b * per_seg + k_pos
            state[(b, h)] = e_last[h][c * L:c * L + 1, :] * state[(b, h)] + _mm_tn(kdec[h][rs[c]], u_[(c, h)])
    for b, h in chains:
        s_ref[b, h] = state[(b, h)]
    for c, h in items:
        o = o_inter[(c, h)] + _mm(qk_m[(c, h)], u_[(c, h)])
        on = o * lax.rsqrt(jnp.mean(o * o, axis=-1, keepdims=True) + EPS) * nw
        k_pos = c % per_seg
        y_ref[c // per_seg, k_pos * L:(k_pos + 1) * L, h * hd:(h + 1) * hd] = (
            on * _silu(z[rs[c], h * hd:(h + 1) * hd])).astype(y_ref.dtype)


def _delta(q, k, v, bab, z, s0, wts, L):
    b, t, inner = q.shape
    nb, tb = _mixer_block(b, t, L)
    pad = lambda x: jnp.pad(x.reshape(1, -1).astype(F32), ((0, 0), (0, LANES - x.size)))
    params = [pad(wts['A_log']), pad(wts['dt_bias']), wts['norm_w'].reshape(1, -1).astype(F32)]
    blk = lambda w: pl.BlockSpec((nb, tb, w), lambda i, j: (i, j, 0))
    st = pl.BlockSpec((nb, H_B, DK_B, DK_B), lambda i, j: (i, 0, 0, 0))
    return pl.pallas_call(
        functools.partial(_delta_body, nb=nb, tb=tb, L=L),
        out_shape=[jax.ShapeDtypeStruct((b, t, inner), BF16), jax.ShapeDtypeStruct((b, H_B, DK_B, DK_B), F32)],
        grid=(b // nb, t // tb),
        in_specs=[blk(inner), blk(inner), blk(inner), blk(LANES), blk(inner), st] + [_resident(p.shape) for p in params],
        out_specs=[blk(inner), st],
        compiler_params=_cparams(("parallel", "arbitrary")),
        name="gated_delta",
    )(q, k, v, bab, z, s0.astype(F32), *params)


def _gla_step(qk_ref, v_ref, ad_ref, g_ref, a2_ref, ab_ref, nw_ref, y_ref, s_ref, *, tb, L):
    kdim = H_C * DK_C
    n_pairs = kdim // LANES
    qk = qk_ref[...]
    log_a = -_softplus(-(_mm(ad_ref[...], a2_ref[...]) + ab_ref[...])) * (1.0 / GLA_NORMALIZER)
    vv = v_ref[...]
    gg = g_ref[...]
    nw = nw_ref[...]
    incl = _iota((L, L), 1) <= _iota((L, L), 0)
    m0 = _iota((tb, LANES), 1) < DK_C
    n_chunks = tb // L

    cum = _chunk_cumsum(log_a, L)
    tot = _chunk_last(cum, L)
    q_in = qk[:, :kdim] * DK_C ** -0.5 * jnp.exp(cum)
    k_in = (qk[:, kdim:] * jnp.exp(-cum)).astype(BF16)
    k_out = qk[:, kdim:] * jnp.exp(tot - cum)
    dec = jnp.exp(tot)
    vb16 = vv.astype(BF16)

    rs = {c: slice(c * L, (c + 1) * L) for c in range(n_chunks)}
    items = [(c, h) for c in range(n_chunks) for h in range(H_C)]
    qm, km = {}, {}
    for h in range(H_C):
        ls = slice((h // 2) * LANES, (h // 2 + 1) * LANES)
        mask = m0 if h % 2 == 0 else jnp.logical_not(m0)
        qm[h] = jnp.where(mask, q_in[:, ls], 0.0).astype(BF16)
        km[h] = jnp.where(mask, k_out[:, ls], 0.0).astype(BF16)
    scores = {(c, h): jnp.where(incl, _mm_nt(qm[h][rs[c]], k_in[rs[c], (h // 2) * LANES:(h // 2 + 1) * LANES]), 0.0)
              .astype(BF16) for c, h in items}
    upd = {(c, h): _mm_tn(vb16[rs[c], h * DV_C:(h + 1) * DV_C], km[h][rs[c]]) for c, h in items}
    state = [s_ref[p] for p in range(n_pairs)]
    s_prev = {}
    for c in range(n_chunks):
        for p in range(n_pairs):
            s_prev[(c, p)] = state[p].astype(BF16)
            state[p] = (state[p] * dec[c * L:c * L + 1, p * LANES:(p + 1) * LANES]
                        + upd[(c, 2 * p)] + upd[(c, 2 * p + 1)])
    for p in range(n_pairs):
        s_ref[p] = state[p]
    for c, h in items:
        hs = slice(h * DV_C, (h + 1) * DV_C)
        o = _mm_nt(qm[h][rs[c]], s_prev[(c, h // 2)]) + _mm(scores[(c, h)], vb16[rs[c], hs])
        on = o * lax.rsqrt(jnp.mean(o * o, axis=-1, keepdims=True) + EPS) * nw
        y_ref[rs[c], hs] = (on * _silu(gg[rs[c], hs])).astype(y_ref.dtype)


def _odd_mixers_body(qk_ref, cv_ref, ad_ref, cg_ref, gs0_ref, a2_ref, ab_ref, nw_ref,
                     dq_ref, dk_ref, dv_ref, dg_ref, cos_ref, sin_ref, rs0_ref,
                     yc_ref, gs_ref, yd_ref, rs_ref, *, tb, L):
    @pl.when(pl.program_id(1) == 0)
    def _():
        gs_ref[...] = gs0_ref[...]
        rs_ref[...] = rs0_ref[...]

    _gla_step(qk_ref, cv_ref, ad_ref, cg_ref, a2_ref, ab_ref, nw_ref, yc_ref, gs_ref, tb=tb, L=L)
    _ret_step(dq_ref, dk_ref, dv_ref, dg_ref, cos_ref, sin_ref, yd_ref, rs_ref, tb=tb, L=L)


def _odd_mixers(cqk, cv, cad, cg, gla0, wts, dq, dk, dv, dg, ret0, pos0, L):
    b, t, _ = cqk.shape
    kdim = H_C * DK_C
    vdim = H_C * DV_C
    rdim = H_D * DK_D
    n_pairs = kdim // LANES
    tb = min(t, 256)
    rank = wts['a2'].shape[0]
    a2p = jnp.pad(wts['a2'].astype(F32), ((0, LANES - rank), (0, 0))).astype(BF16)
    params = [a2p, wts['a_bias'].reshape(1, -1).astype(F32), wts['norm_w'].reshape(1, -1).astype(F32)]
    st0 = jnp.swapaxes(gla0.astype(F32), -1, -2).reshape(b, n_pairs, 2, DV_C, DK_C)
    st0 = jnp.concatenate([st0[:, :, 0], st0[:, :, 1]], axis=-1)
    inv = ROPE_BASE ** (-jnp.arange(0, DK_D, 2, dtype=F32) / DK_D)
    ang = (jnp.arange(t) + pos0).astype(F32)[:, None] * inv[None, :]
    cos = jnp.concatenate([jnp.cos(ang), jnp.cos(ang)], axis=-1)
    sin = jnp.concatenate([-jnp.sin(ang), jnp.sin(ang)], axis=-1)
    blk = lambda w: pl.BlockSpec((None, tb, w), lambda i, j: (i, j, 0))
    per_b = lambda shape: pl.BlockSpec((None,) + shape, lambda i, j: (i,) + (0,) * len(shape))
    tab = pl.BlockSpec((tb, DK_D), lambda i, j: (j, 0))
    gst, rst = per_b((n_pairs, DV_C, LANES)), per_b((H_D, DK_D, DK_D))
    y_c, st, y_d, ret_new = pl.pallas_call(
        functools.partial(_odd_mixers_body, tb=tb, L=L),
        out_shape=[jax.ShapeDtypeStruct((b, t, vdim), BF16), jax.ShapeDtypeStruct((b, n_pairs, DV_C, LANES), F32),
                   jax.ShapeDtypeStruct((b, t, rdim), BF16), jax.ShapeDtypeStruct((b, H_D, DK_D, DK_D), F32)],
        grid=(b, t // tb),
        in_specs=[blk(2 * kdim), blk(vdim), blk(LANES), blk(vdim), gst] + [_resident(p.shape) for p in params]
        + [blk(rdim), blk(rdim), blk(rdim), blk(rdim), tab, tab, rst],
        out_specs=[blk(vdim), gst, blk(rdim), rst],
        compiler_params=_cparams(("parallel", "arbitrary")),
        name="gla_retention",
    )(cqk, cv, cad, cg, st0, *params, dq, dk, dv, dg, cos, sin, ret0.astype(F32))
    st = jnp.stack([st[..., :DK_C], st[..., DK_C:]], axis=2).reshape(b, H_C, DV_C, DK_C)
    return y_c, jnp.swapaxes(st, -1, -2), y_d, ret_new


def _ret_step(q_ref, k_ref, v_ref, g_ref, cos_ref, sin_ref, y_ref, s_ref, *, tb, L):
    hd = DK_D
    cos = cos_ref[...]
    sin = sin_ref[...]
    qq, kk, vv, gg = q_ref[...], k_ref[...], v_ref[...], g_ref[...]
    rw = _iota((L, L), 0)
    cl_ = _iota((L, L), 1)
    incl = cl_ <= rw
    dist = jnp.where(incl, rw - cl_, 0).astype(F32)
    pos = _iota((L, LANES), 0).astype(F32)

    n_chunks = tb // L
    rs = {c: slice(c * L, (c + 1) * L) for c in range(n_chunks)}
    items = [(c, h) for c in range(n_chunks) for h in range(H_D)]
    lg = [math.log(1.0 - 2.0 ** (-5.0 - h)) for h in range(H_D)]
    vb16 = vv.astype(BF16)
    qh, kh, q_in, k_out, dmat = {}, {}, {}, {}, {}
    for h in range(H_D):
        hs = slice(h * hd, (h + 1) * hd)
        q_rot = qq[:, hs] * cos + pltpu.roll(qq[:, hs], hd // 2, 1) * sin
        k_rot = (kk[:, hs] * cos + pltpu.roll(kk[:, hs], hd // 2, 1) * sin) * hd ** -0.5
        dmat[h] = jnp.where(incl, jnp.exp(dist * lg[h]), 0.0)
        e_in = jnp.exp((pos + 1.0) * lg[h])
        e_out = jnp.exp((L - 1.0 - pos) * lg[h])
        qh[h], kh[h] = q_rot.astype(BF16), k_rot.astype(BF16)
        for c in range(n_chunks):
            q_in[(c, h)] = (q_rot[rs[c]] * e_in).astype(BF16)
            k_out[(c, h)] = (k_rot[rs[c]] * e_out).astype(BF16)
    scores = {(c, h): (_mm_nt(qh[h][rs[c]], kh[h][rs[c]]) * dmat[h]).astype(BF16) for c, h in items}
    upd = {(c, h): _mm_tn(k_out[(c, h)], vb16[rs[c], h * hd:(h + 1) * hd]) for c, h in items}
    state = [s_ref[h] for h in range(H_D)]
    s_prev = {}
    for c in range(n_chunks):
        for h in range(H_D):
            s_prev[(c, h)] = state[h].astype(BF16)
            state[h] = state[h] * math.exp(lg[h] * L) + upd[(c, h)]
    for h in range(H_D):
        s_ref[h] = state[h]
    for c, h in items:
        hs = slice(h * hd, (h + 1) * hd)
        o = _mm(q_in[(c, h)], s_prev[(c, h)]) + _mm(scores[(c, h)], vb16[rs[c], hs])
        mu = jnp.mean(o, axis=-1, keepdims=True)
        oc = o - mu
        on = oc * lax.rsqrt(jnp.mean(oc * oc, axis=-1, keepdims=True) + EPS)
        y_ref[rs[c], hs] = (on * _silu(gg[rs[c], hs])).astype(y_ref.dtype)


def _prep_weights(W):
    bf = lambda x: x.astype(BF16)
    a_proj = 3 * H_A * N_A + 64 + 64 + 128
    b_conv = 3 * H_B * DK_B
    P = {}
    for name in ('ffn1_wg', 'ffn1_wu', 'ffn1_wd', 'ffn2_wg', 'ffn2_wu', 'ffn2_wd', 'mem_wq', 'mem_wo', 'mem_wk',
                 'mem_wv', 'even_w_out', 'odd_w_out'):
        P[name] = bf(W[name])
    ew = W['even_w_in']
    o = a_proj
    bab = ew[:, :, o + b_conv:o + b_conv + 2 * H_B]
    P['even_in'] = [bf(ew[:, :, :a_proj]), bf(ew[:, :, o:o + b_conv]),
                    bf(jnp.pad(bab, ((0, 0), (0, 0), (0, LANES - 2 * H_B)))), bf(ew[:, :, o + b_conv + 2 * H_B:])]
    ow = W['odd_w_in']
    kd, vd, rank = H_C * DK_C, H_C * DV_C, W['gla_a2'].shape[1]
    c_proj = 2 * kd + 2 * vd + rank
    dd = H_D * DK_D
    cad = ow[:, :, 2 * kd + vd:2 * kd + vd + rank]
    P['odd_in'] = [bf(ow[:, :, :2 * kd]), bf(ow[:, :, 2 * kd:2 * kd + vd]),
                   bf(jnp.pad(cad, ((0, 0), (0, 0), (0, LANES - rank)))), bf(ow[:, :, 2 * kd + vd + rank:c_proj])] \
        + [bf(ow[:, :, c_proj + j * dd:c_proj + (j + 1) * dd]) for j in range(4)]
    return P


def _trunk(x, pos0, mem_k, mem_v, heads, shift, rwkv, conv, delta, gla, ret, W, P):
    b, t, d = x.shape
    n = b * t
    depth = W['norm_ffn1'].shape[0]
    L_delta = min(64, t)
    L_lin = min(64, t)
    flat = lambda z: z.reshape(n, z.shape[-1])
    unflat = lambda z: z.reshape(b, t, z.shape[-1])
    new = {k: [] for k in ('shift', 'rwkv', 'conv', 'delta', 'gla', 'ret')}
    x = flat(x)
    for l in range(depth):
        i = l // 2
        x = _ffn(x, W['norm_ffn1'][l], P['ffn1_wg'][l], P['ffn1_wu'][l], P['ffn1_wd'][l])
        if l % 2 == 0:
            (pa, dq_, dk_, dv_, bab, z), s3 = _even_in_proj(unflat(x), W['norm_mix'][l], [w[i] for w in P['even_in']],
                                                           conv[i], W['delta_conv_w'][i])
            wa = dict(mu=W['rwkv_mu'][i], w0=W['rwkv_w0'][i], w2=W['rwkv_w2'][i], a0=W['rwkv_a0'][i],
                      a2=W['rwkv_a2'][i], g2=W['rwkv_g2'][i], kk=W['rwkv_kk'][i], ka=W['rwkv_ka'][i],
                      rk=W['rwkv_rk'][i], ln_w=W['rwkv_ln_w'][i], ln_b=W['rwkv_ln_b'][i])
            y_a, s1, s2 = _rwkv(pa, shift[i], rwkv[i], wa, L_delta)
            wb = dict(A_log=W['delta_A_log'][i], dt_bias=W['delta_dt_bias'][i], norm_w=W['delta_norm_w'][i])
            y_b, s4 = _delta(dq_, dk_, dv_, bab, z, delta[i], wb, L_delta)
            new['shift'].append(s1)
            new['rwkv'].append(s2)
            new['conv'].append(s3)
            new['delta'].append(s4)
            split = H_A * N_A
            halves, w_out = (y_a, y_b), P['even_w_out'][i]
        else:
            cqk, cv, cad, cg, dq, dk, dv, dg = _norm_proj(x, W['norm_mix'][l], [w[i] for w in P['odd_in']],
                                                          bf16_out=(1, 6))
            wc = dict(a2=W['gla_a2'][i], a_bias=W['gla_a_bias'][i], norm_w=W['gla_norm_w'][i])
            y_c, s5, y_d, s6 = _odd_mixers(unflat(cqk), unflat(cv), unflat(cad), unflat(cg), gla[i], wc,
                                           unflat(dq), unflat(dk), unflat(dv), unflat(dg), ret[i], pos0, L_lin)
            new['gla'].append(s5)
            new['ret'].append(s6)
            split = H_C * DV_C
            halves, w_out = (y_c, y_d), P['odd_w_out'][i]
        mk = mem_k[l].astype(BF16)
        mv = mem_v[l].astype(BF16)
        x = flat(_mix_out_mem_attn(unflat(x), halves[0], halves[1], w_out[:split], w_out[split:], W['norm_mem'][l],
                                   P['mem_wq'][l], mk, mv, P['mem_wo'][l], heads))
        fw = W['final_norm'] if l == depth - 1 else None
        x = _ffn(x, W['norm_ffn2'][l], P['ffn2_wg'][l], P['ffn2_wu'][l], P['ffn2_wd'][l], fw)
    return unflat(x), new


def kernel(x_prompt, x_sample, mem_prompt, state_rwkv_shift, state_rwkv, state_delta_conv, state_delta, state_gla, state_ret, cache_mem_k, cache_mem_v, norm_ffn1, ffn1_wg, ffn1_wu, ffn1_wd, norm_mix, even_w_in, even_w_out, rwkv_mu, rwkv_w0, rwkv_w2, rwkv_a0, rwkv_a2, rwkv_g2, rwkv_kk, rwkv_ka, rwkv_rk, rwkv_ln_w, rwkv_ln_b, delta_conv_w, delta_A_log, delta_dt_bias, delta_norm_w, odd_w_in, odd_w_out, gla_a2, gla_a_bias, gla_norm_w, norm_mem, mem_norm_kv, mem_wq, mem_wk, mem_wv, mem_wo, norm_ffn2, ffn2_wg, ffn2_wu, ffn2_wd, final_norm):
    W = dict(norm_ffn1=norm_ffn1, ffn1_wg=ffn1_wg, ffn1_wu=ffn1_wu, ffn1_wd=ffn1_wd, norm_mix=norm_mix,
             even_w_in=even_w_in, even_w_out=even_w_out, rwkv_mu=rwkv_mu, rwkv_w0=rwkv_w0, rwkv_w2=rwkv_w2,
             rwkv_a0=rwkv_a0, rwkv_a2=rwkv_a2, rwkv_g2=rwkv_g2, rwkv_kk=rwkv_kk, rwkv_ka=rwkv_ka,
             rwkv_rk=rwkv_rk, rwkv_ln_w=rwkv_ln_w, rwkv_ln_b=rwkv_ln_b, delta_conv_w=delta_conv_w,
             delta_A_log=delta_A_log, delta_dt_bias=delta_dt_bias, delta_norm_w=delta_norm_w,
             odd_w_in=odd_w_in, odd_w_out=odd_w_out, gla_a2=gla_a2, gla_a_bias=gla_a_bias,
             gla_norm_w=gla_norm_w, norm_mem=norm_mem, mem_wq=mem_wq, mem_wk=mem_wk, mem_wv=mem_wv,
             mem_wo=mem_wo, norm_ffn2=norm_ffn2, ffn2_wg=ffn2_wg, ffn2_wu=ffn2_wu, ffn2_wd=ffn2_wd,
             final_norm=final_norm)
    P = _prep_weights(W)
    dt = x_prompt.dtype
    bp, _, d = x_prompt.shape
    depth = norm_ffn1.shape[0]
    n_even, n_odd = (depth + 1) // 2, depth // 2
    heads, hd = cache_mem_k.shape[3], cache_mem_k.shape[4]
    n_mem = mem_prompt.shape[1]
    pk, pv, pk_flat, pv_flat = [], [], [], []
    mem_flat = mem_prompt.reshape(bp * n_mem, d)
    for l in range(depth):
        mk, mv = _norm_proj(mem_flat, mem_norm_kv[l], [P['mem_wk'][l], P['mem_wv'][l]])
        pk_flat.append(mk.reshape(bp, n_mem, d))
        pv_flat.append(mv.reshape(bp, n_mem, d))
        pk.append(mk.reshape(bp, n_mem, heads, hd))
        pv.append(mv.reshape(bp, n_mem, heads, hd))
    zeros = lambda ref, cnt: [jnp.zeros((bp,) + ref.shape[2:], F32)] * cnt
    y_prompt, ps = _trunk(x_prompt, 0, pk_flat, pv_flat, heads, zeros(state_rwkv_shift, n_even), zeros(state_rwkv, n_even),
                          zeros(state_delta_conv, n_even), zeros(state_delta, n_even), zeros(state_gla, n_odd),
                          zeros(state_ret, n_odd), W, P)
    bs = x_sample.shape[0]
    y_sample, ss = _trunk(x_sample, PAST_LEN, [cache_mem_k[l].reshape(bs, n_mem, d) for l in range(depth)],
                          [cache_mem_v[l].reshape(bs, n_mem, d) for l in range(depth)], heads,
                          [state_rwkv_shift[i] for i in range(n_even)], [state_rwkv[i] for i in range(n_even)],
                          [state_delta_conv[i] for i in range(n_even)], [state_delta[i] for i in range(n_even)],
                          [state_gla[i] for i in range(n_odd)], [state_ret[i] for i in range(n_odd)], W, P)
    st = lambda xs: jnp.stack(xs).astype(dt)
    order = ('shift', 'rwkv', 'conv', 'delta', 'gla', 'ret')
    return ((y_prompt, y_sample) + tuple(st(ps[k]) for k in order) + (st(pk), st(pv))
            + tuple(st(ss[k]) for k in order))
```

```python
import functools
import math

import jax
import jax.numpy as jnp
from jax import lax
from jax.experimental import pallas as pl
from jax.experimental.pallas import tpu as pltpu

F32 = jnp.float32
BF16 = jnp.bfloat16
EPS = 1e-6
RWKV_LN_EPS = 64e-5
GLA_NORMALIZER = 16.0
ROPE_BASE = 10000.0
PAST_LEN = 1024

LANES = 128
MXU_DIM = 256
VMEM_LIMIT = 56 * 1024 * 1024

H_A, N_A = 8, 64
H_B, DK_B = 4, 128
H_C, DK_C, DV_C = 4, 64, 128
H_D, DK_D = 4, 128
CONV_W = 4
CONV_CARRY = 8
MIXER_ROWS = 512
CONV_ROWS = 128


def _cparams(sem):
    return pltpu.CompilerParams(dimension_semantics=sem, vmem_limit_bytes=VMEM_LIMIT)


def _mm(a, b):
    return jnp.dot(a.astype(BF16), b.astype(BF16), preferred_element_type=F32)


def _mm_nt(a, b):
    return lax.dot_general(a.astype(BF16), b.astype(BF16), (((1,), (1,)), ((), ())),
                           preferred_element_type=F32)


def _mm_tn(a, b):
    return lax.dot_general(a.astype(BF16), b.astype(BF16), (((0,), (0,)), ((), ())),
                           preferred_element_type=F32)


def _split3(x):
    hi = x.astype(BF16)
    r = x - hi.astype(F32)
    mid = r.astype(BF16)
    lo = (r - mid.astype(F32)).astype(BF16)
    return hi, mid, lo


def _mm_exact_lhs01(a01, x):
    hi, mid, lo = _split3(x)
    a = a01.astype(BF16)
    return (jnp.dot(a, hi, preferred_element_type=F32) + jnp.dot(a, mid, preferred_element_type=F32)
            + jnp.dot(a, lo, preferred_element_type=F32))


def _iota(shape, dim):
    return lax.broadcasted_iota(jnp.int32, shape, dim)


def _tri_incl(n):
    return (_iota((n, n), 1) <= _iota((n, n), 0)).astype(F32)


def _chunk_tri(tb, L):
    shift = L.bit_length() - 1
    assert 1 << shift == L
    rt, ct = _iota((tb, tb), 0), _iota((tb, tb), 1)
    same = jnp.right_shift(rt, shift) == jnp.right_shift(ct, shift)
    return jnp.logical_and(same, ct <= rt).astype(F32)


def _chunk_cumsum(x, L):
    n = x.shape[0]
    g = min(n, MXU_DIM)
    tri = _chunk_tri(g, L)
    return jnp.concatenate([_mm_exact_lhs01(tri, x[r:r + g]) for r in range(0, n, g)], axis=0)


def _chunk_last(cum, L):
    tb, w = cum.shape
    return jnp.concatenate([jnp.broadcast_to(cum[c * L + L - 1:c * L + L, :], (L, w)) for c in range(tb // L)], axis=0)


def _sigmoid(x):
    return 0.5 * (jnp.tanh(0.5 * x) + 1.0)


def _silu(x):
    return x * _sigmoid(x)


def _softplus(x):
    return jnp.maximum(x, 0.0) + jnp.log(1.0 + jnp.exp(-jnp.abs(x)))


def _rms(x, w):
    return x * lax.rsqrt(jnp.mean(x * x, axis=-1, keepdims=True) + EPS) * w


def _inv_unit_lower(n_mat, size):
    dim = n_mat.shape[0]
    eye = (_iota((dim, dim), 0) == _iota((dim, dim), 1)).astype(F32)
    t = eye + n_mat
    m = n_mat
    power = 2
    while power < size:
        m = _mm(m, m)
        t = t + _mm(t, m)
        power *= 2
    return t


def _row_tile(n, target):
    t = min(n, target)
    while n % t:
        t //= 2
    return t


def _mixer_block(b, t, L):
    tb = min(t, 2 * L)
    nb = max(1, min(b, MIXER_ROWS // tb))
    while b % nb:
        nb -= 1
    return nb, tb


def _resident(shape):
    nd = len(shape)
    return pl.BlockSpec(shape, lambda *_: (0,) * nd, pipeline_mode=pl.Buffered(1))


def _ffn_body(x_ref, nw_ref, wg_ref, wu_ref, wd_ref, fw_ref, o_ref, *, chunk, final):
    x = x_ref[...]
    h = _rms(x, nw_ref[...]).astype(BF16)
    d_ff = wg_ref.shape[1]
    n_chunks = d_ff // chunk

    def gate_up(c):
        sl = slice(c * chunk, (c + 1) * chunk)
        return (jnp.dot(h, wg_ref[:, sl], preferred_element_type=F32),
                jnp.dot(h, wu_ref[:, sl], preferred_element_type=F32))

    acc = None
    nxt = gate_up(0)
    for c in range(n_chunks):
        g, u = nxt
        if c + 1 < n_chunks:
            nxt = gate_up(c + 1)
        a = (_silu(g) * u).astype(BF16)
        d = jnp.dot(a, wd_ref[c * chunk:(c + 1) * chunk, :], preferred_element_type=F32)
        acc = d if acc is None else acc + d
    y = x + 0.5 * acc
    if final:
        y = _rms(y, fw_ref[...])
    o_ref[...] = y


def _ffn(x, nw, wg, wu, wd, fw=None):
    n, d = x.shape
    d_ff = wg.shape[1]
    tm = _row_tile(n, 1024)
    final = fw is not None
    if fw is None:
        fw = nw
    return pl.pallas_call(
        functools.partial(_ffn_body, chunk=MXU_DIM, final=final),
        out_shape=jax.ShapeDtypeStruct((n, d), F32),
        grid=(n // tm,),
        in_specs=[pl.BlockSpec((tm, d), lambda i: (i, 0)), _resident((1, d)), _resident((d, d_ff)),
                  _resident((d, d_ff)), _resident((d_ff, d)), _resident((1, d))],
        out_specs=pl.BlockSpec((tm, d), lambda i: (i, 0)),
        compiler_params=_cparams(("parallel",)),
        name="ffn",
    )(x, nw.reshape(1, d), wg, wu, wd, fw.reshape(1, d))


def _norm_proj_body(*refs, n_w):
    x_ref, nw_ref = refs[0], refs[1]
    w_refs = refs[2:2 + n_w]
    o_refs = refs[2 + n_w:]
    h = _rms(x_ref[...], nw_ref[...]).astype(BF16)
    for w_ref, o_ref in zip(w_refs, o_refs):
        o_ref[...] = jnp.dot(h, w_ref[...], preferred_element_type=F32).astype(o_ref.dtype)


def _norm_proj(x, nw, ws, bf16_out=()):
    n, d = x.shape
    tm = _row_tile(n, 1024)
    return pl.pallas_call(
        functools.partial(_norm_proj_body, n_w=len(ws)),
        out_shape=[jax.ShapeDtypeStruct((n, w.shape[1]), BF16 if i in bf16_out else F32) for i, w in enumerate(ws)],
        grid=(n // tm,),
        in_specs=[pl.BlockSpec((tm, d), lambda i: (i, 0)), _resident((1, d))] + [_resident(w.shape) for w in ws],
        out_specs=[pl.BlockSpec((tm, w.shape[1]), lambda i: (i, 0)) for w in ws],
        compiler_params=_cparams(("parallel",)),
        name="norm_proj",
    )(x, nw.reshape(1, d), *ws)


def _mix_out_mem_attn_body(x_ref, ya_ref, yb_ref, wa_ref, wb_ref, nw_ref, wq_ref, k_ref, v_ref, wo_ref, o_ref, *,
                           heads):
    x = (x_ref[...] + jnp.dot(ya_ref[...], wa_ref[...], preferred_element_type=F32)
         + jnp.dot(yb_ref[...], wb_ref[...], preferred_element_type=F32))
    h = _rms(x, nw_ref[...]).astype(BF16)
    q = jnp.dot(h, wq_ref[...], preferred_element_type=F32).astype(BF16)
    d = q.shape[1]
    hd = d // heads

    def scores(i):
        sl = slice(i * hd, (i + 1) * hd)
        return _mm_nt(q[:, sl], k_ref[:, sl]) * hd ** -0.5

    outs = []
    nxt = scores(0)
    for i in range(heads):
        s = nxt
        if i + 1 < heads:
            nxt = scores(i + 1)
        e = jnp.exp(s - jnp.max(s, axis=-1, keepdims=True))
        pr = e / jnp.sum(e, axis=-1, keepdims=True)
        outs.append(_mm(pr, v_ref[:, i * hd:(i + 1) * hd]))
    o = jnp.concatenate(outs, axis=-1).astype(BF16)
    o_ref[...] = x + jnp.dot(o, wo_ref[...], preferred_element_type=F32)


def _mix_out_mem_attn(x, ya, yb, wa, wb, nw, wq, mk, mv, wo, heads):
    b, t, d = x.shape
    m = mk.shape[1]
    tm = _row_tile(t, 1024)
    row = lambda w: pl.BlockSpec((None, tm, w), lambda i, j: (i, j, 0))
    mem = pl.BlockSpec((None, m, d), lambda i, j: (i, 0, 0))
    return pl.pallas_call(
        functools.partial(_mix_out_mem_attn_body, heads=heads),
        out_shape=jax.ShapeDtypeStruct((b, t, d), F32),
        grid=(b, t // tm),
        in_specs=[row(d), row(ya.shape[2]), row(yb.shape[2]), _resident(wa.shape), _resident(wb.shape),
                  _resident((1, d)), _resident((d, d)), mem, mem, _resident((d, d))],
        out_specs=row(d),
        compiler_params=_cparams(("parallel", "parallel")),
        name="mix_out_mem_attn",
    )(x, ya, yb, wa, wb, nw.reshape(1, d), wq, mk, mv, wo)


def _seg_sum(x, bd):
    g = bd.shape[0]
    xb = x.astype(BF16)
    return jnp.concatenate([jnp.dot(xb[:, j:j + g], bd, preferred_element_type=F32)
                            for j in range(0, x.shape[1], g)], axis=1)


def _rwkv_body(pa_ref, shift0_ref, s0_ref, mu_ref, w0_ref, w2_ref, a0_ref, a2_ref, g2_ref, kkw_ref, ka_ref,
               rk_ref, lnw_ref, lnb_ref, bd_ref, y_ref, shift_ref, s_ref, y_scr, *, nb, tb, L):
    t_idx = pl.program_id(1)
    a_dim = H_A * N_A
    n_pairs = a_dim // LANES

    @pl.when(t_idx == 0)
    def _():
        shift_ref[...] = shift0_ref[...]
        s_ref[...] = s0_ref[...]

    row = _iota((tb, pa_ref.shape[2]), 0)
    segs, prevs = [], []
    for b in range(nb):
        seg = pa_ref[b]
        segs.append(seg)
        prevs.append(jnp.where(row == 0, shift_ref[b], pltpu.roll(seg, 1, 0)))
        shift_ref[b] = seg[tb - 1:tb, :]
    pa = jnp.concatenate(segs, axis=0)
    prev = jnp.concatenate(prevs, axis=0)
    rows = nb * tb
    xa = pa + (prev - pa) * mu_ref[...]
    r = xa[:, 0:a_dim]
    k = xa[:, a_dim:2 * a_dim]
    v = xa[:, 2 * a_dim:3 * a_dim]
    xwa = xa[:, 3 * a_dim:3 * a_dim + LANES]
    xg = xa[:, 3 * a_dim + LANES:]
    wlog = -_softplus(-(w0_ref[...] + _mm(jnp.tanh(xwa), w2_ref[...]))) - 0.5
    logdec = -jnp.exp(wlog)
    a = _sigmoid(a0_ref[...] + _mm(xwa, a2_ref[...]))
    gate = _mm(_sigmoid(xg), g2_ref[...])
    bd = bd_ref[...]
    kq = k * kkw_ref[...]
    kk = kq * lax.rsqrt(_seg_sum(kq * kq, bd) + EPS)
    k2 = k * (1.0 + (a - 1.0) * ka_ref[...])
    av = -kk
    bv = kk * a

    lane = _iota((L, LANES), 1)
    m0 = lane < N_A
    r2 = _iota((2 * L, 2 * L), 0)
    c2 = _iota((2 * L, 2 * L), 1)
    bdm = jnp.logical_not(jnp.logical_xor(r2 >= L, c2 >= L))
    r_loc = jnp.where(r2 >= L, r2 - L, r2)
    c_loc = jnp.where(c2 >= L, c2 - L, c2)
    strict_t = jnp.logical_and(bdm, c_loc > r_loc)
    strict_l = jnp.bitwise_and(_iota((L, 2 * L), 1), L - 1) < _iota((L, 2 * L), 0)
    incl_l = jnp.bitwise_and(_iota((L, 4 * L), 1), L - 1) <= _iota((L, 4 * L), 0)
    bdl = jnp.logical_not(jnp.logical_xor(_iota((2 * L, LANES), 0) >= L, _iota((2 * L, LANES), 1) >= N_A))
    eye2 = (r2 == c2).astype(F32)
    zero_b = jnp.zeros((L, LANES), BF16)

    def stack2(x):
        xb = x.astype(BF16)
        return jnp.concatenate([jnp.where(m0, xb, zero_b), jnp.where(m0, zero_b, xb)], axis=0)

    def dup2(x):
        xb = x.astype(BF16)
        return jnp.concatenate([xb, xb], axis=0)

    cum = _chunk_cumsum(logdec, L)
    tot = _chunk_last(cum, L)
    e_neg = jnp.exp(-cum)
    e_out = jnp.exp(tot - cum)
    a_t = av * jnp.exp(cum - logdec)
    r_t = r * jnp.exp(cum)
    b_t = bv * e_neg
    k_t = k2 * e_neg
    b_o = bv * e_out
    k_o = k2 * e_out
    g_l = jnp.exp(tot)

    per_seg = tb // L
    items = [(c, p) for c in range(rows // L) for p in range(n_pairs)]
    sl = {(c, p): (slice(c * L, (c + 1) * L), slice(p * LANES, (p + 1) * LANES)) for c, p in items}
    xa = {it: stack2(a_t[sl[it]]) for it in items}
    ra = {it: r_t[sl[it]].astype(BF16) for it in items}
    yb = {it: dup2(b_t[sl[it]]) for it in items}
    sbk = {it: jnp.concatenate([stack2(b_t[sl[it]]), stack2(k_t[sl[it]])], axis=0) for it in items}
    v2 = {it: jnp.where(bdl, dup2(v[sl[it]]), jnp.zeros((2 * L, LANES), BF16)) for it in items}
    wst = {it: jnp.concatenate([stack2(b_o[sl[it]]), stack2(k_o[sl[it]])], axis=0) for it in items}
    n_t = {it: jnp.where(strict_t, _mm_nt(yb[it], xa[it]), 0.0) for it in items}
    ak = {it: jnp.where(strict_l, _mm_nt(a_t[sl[it]], sbk[it][2 * L:]), 0.0).astype(BF16) for it in items}
    rbk = {it: jnp.where(incl_l, _mm_nt(ra[it], sbk[it]), 0.0).astype(BF16) for it in items}
    akv = {it: stack2(_mm(ak[it], v2[it])) for it in items}
    t_t = {it: eye2 + n_t[it] for it in items}
    n_b = {it: n_t[it].astype(BF16) for it in items}
    m_t = {it: _mm(n_b[it], n_b[it]).astype(BF16) for it in items}
    power = 4
    while power < L:
        prod = {it: _mm(m_t[it], jnp.concatenate([m_t[it], t_t[it].astype(BF16)], axis=1)) for it in items}
        m_t = {it: prod[it][:, :2 * L].astype(BF16) for it in items}
        t_t = {it: t_t[it] + prod[it][:, 2 * L:] for it in items}
        power *= 2
    if L > 2:
        t_t = {it: t_t[it] + _mm(m_t[it], t_t[it]) for it in items}
    om_up = {it: _mm_tn(t_t[it], jnp.concatenate([xa[it], akv[it]], axis=1)) for it in items}
    omega = {it: om_up[it][:, :LANES].astype(BF16) for it in items}
    chains = [(b, p) for b in range(nb) for p in range(n_pairs)]
    state = {bp: s_ref[bp[0], bp[1]] for bp in chains}
    s_prev, uv = {}, {}
    for k_pos in range(per_seg):
        for b, p in chains:
            it = (b * per_seg + k_pos, p)
            s_prev[it] = state[(b, p)].astype(BF16)
            u_st = _mm_nt(omega[it], s_prev[it]) + om_up[it][:, LANES:]
            uv[it] = jnp.concatenate([u_st.astype(BF16), v2[it]], axis=0)
        for b, p in chains:
            it = (b * per_seg + k_pos, p)
            state[(b, p)] = state[(b, p)] * g_l[it[0] * L:it[0] * L + 1, sl[it][1]] + _mm_tn(uv[it], wst[it])
    for b, p in chains:
        s_ref[b, p] = state[(b, p)]
    for it in items:
        y_scr[sl[it]] = _mm_nt(ra[it], s_prev[it]) + _mm(rbk[it], uv[it])

    y = y_scr[...]
    inv_n = 1.0 / N_A
    mu_y = _seg_sum(y, bd) * inv_n
    yc = y - mu_y
    var = _seg_sum(yc * yc, bd) * inv_n
    yn = yc * lax.rsqrt(var + RWKV_LN_EPS) * lnw_ref[...] + lnb_ref[...]
    bonus = _seg_sum(r * k2 * rk_ref[...], bd) * v
    out = ((yn + bonus) * gate).astype(y_ref.dtype)
    for b in range(nb):
        y_ref[b] = out[b * tb:(b + 1) * tb, :]


def _pair_blockdiag(s):
    b, h, n, _ = s.shape
    s = s.reshape(b, h // 2, 2, n, n)
    z = jnp.zeros((b, h // 2, n, n), s.dtype)
    top = jnp.concatenate([s[:, :, 0], z], axis=-1)
    bot = jnp.concatenate([z, s[:, :, 1]], axis=-1)
    return jnp.concatenate([top, bot], axis=-2)


def _pair_unblock(s):
    b, p, n2, _ = s.shape
    n = n2 // 2
    return jnp.stack([s[:, :, :n, :n], s[:, :, n:, n:]], axis=2).reshape(b, 2 * p, n, n)


def _rwkv(pa, shift0, s0, wts, L):
    b, t, pw = pa.shape
    a_dim = H_A * N_A
    nb, tb = _mixer_block(b, t, L)
    n_pairs = a_dim // LANES
    seg = _iota((MXU_DIM, MXU_DIM), 0) // N_A == _iota((MXU_DIM, MXU_DIM), 1) // N_A
    bd = seg.astype(BF16)
    z64 = jnp.zeros((N_A, a_dim), F32)
    w2p = jnp.concatenate([wts['w2'], z64], axis=0)
    a2p = jnp.concatenate([z64, wts['a2']], axis=0)
    vec = lambda x: x.reshape(1, -1).astype(F32)
    params = [vec(wts['mu']), vec(wts['w0']), w2p.astype(BF16), vec(wts['a0']), a2p.astype(BF16),
              wts['g2'].astype(BF16), vec(wts['kk']), vec(wts['ka']), vec(wts['rk']), vec(wts['ln_w']),
              vec(wts['ln_b']), bd]
    blk_t = lambda w: pl.BlockSpec((nb, tb, w), lambda i, j: (i, j, 0))
    per_b = lambda shape: pl.BlockSpec((nb,) + shape, lambda i, j: (i,) + (0,) * len(shape))
    y, shift, s_new = pl.pallas_call(
        functools.partial(_rwkv_body, nb=nb, tb=tb, L=L),
        out_shape=[jax.ShapeDtypeStruct((b, t, a_dim), BF16), jax.ShapeDtypeStruct((b, 1, pw), F32),
                   jax.ShapeDtypeStruct((b, n_pairs, LANES, LANES), F32)],
        grid=(b // nb, t // tb),
        in_specs=[blk_t(pw), per_b((1, pw)), per_b((n_pairs, LANES, LANES))] + [_resident(p.shape) for p in params],
        out_specs=[blk_t(a_dim), per_b((1, pw)), per_b((n_pairs, LANES, LANES))],
        scratch_shapes=[pltpu.VMEM((nb * tb, a_dim), F32)],
        compiler_params=_cparams(("parallel", "arbitrary")),
        name="rwkv7",
    )(pa, shift0, _pair_blockdiag(s0.astype(F32)), *params)
    return y, shift, _pair_unblock(s_new)


def _even_in_body(x_ref, xh_ref, conv0_ref, nw_ref, wa_ref, wqkv_ref, wbab_ref, wz_ref, cw_ref,
                  pa_ref, q_ref, k_ref, v_ref, bab_ref, z_ref, tail_ref, cat_scr, *, tm):
    j = pl.program_id(1)
    hd = DK_B
    inner = H_B * hd
    nw = nw_ref[...]
    halo = jnp.dot(_rms(xh_ref[...], nw).astype(BF16), wqkv_ref[...], preferred_element_type=F32)
    cat_scr[0:CONV_CARRY, :] = jnp.where(j == 0, conv0_ref[...], halo)
    h = _rms(x_ref[...], nw).astype(BF16)
    cat_scr[CONV_CARRY:, :] = jnp.dot(h, wqkv_ref[...], preferred_element_type=F32)
    tail_ref[...] = cat_scr[tm:, :]
    cw = cw_ref[...]

    rows = min(tm, CONV_ROWS)

    def conv_piece(r0, c0):
        cols = slice(c0, c0 + hd)
        cat = cat_scr[r0:r0 + rows + CONV_CARRY, cols]
        acc = cat * cw[0:1, cols]
        for t in range(1, CONV_W):
            acc = pltpu.roll(acc, 1, 0) + cat * cw[t:t + 1, cols]
        return _silu(acc[CONV_CARRY:, :])

    def l2n(z):
        return z * lax.rsqrt(jnp.sum(z * z, axis=-1, keepdims=True) + EPS)

    pa_ref[...] = jnp.dot(h, wa_ref[...], preferred_element_type=F32)
    z_ref[...] = jnp.dot(h, wz_ref[...], preferred_element_type=F32)
    bab_ref[...] = jnp.dot(h, wbab_ref[...], preferred_element_type=F32)
    for r0 in range(0, tm, rows):
        rs = slice(r0, r0 + rows)
        for i in range(H_B):
            sl = slice(i * hd, (i + 1) * hd)
            q_ref[rs, sl] = (l2n(conv_piece(r0, i * hd)) * hd ** -0.5).astype(BF16)
            k_ref[rs, sl] = l2n(conv_piece(r0, inner + i * hd)).astype(BF16)
            v_ref[rs, sl] = conv_piece(r0, 2 * inner + i * hd).astype(BF16)


def _even_in_proj(x, nw, ws, conv0, conv_w):
    b, t, d = x.shape
    wa, wqkv, wbab, wz = ws
    cdim = wqkv.shape[1]
    inner = H_B * DK_B
    tm = _row_tile(t, 1024)
    conv0p = jnp.pad(conv0.astype(F32), ((0, 0), (CONV_CARRY - (CONV_W - 1), 0), (0, 0)))
    row = lambda w: pl.BlockSpec((None, tm, w), lambda i, j: (i, j, 0))
    halo = pl.BlockSpec((None, CONV_CARRY, d), lambda i, j: (i, jnp.maximum(j * (tm // CONV_CARRY) - 1, 0), 0))
    per_b = pl.BlockSpec((None, CONV_CARRY, cdim), lambda i, j: (i, 0, 0))
    outs = pl.pallas_call(
        functools.partial(_even_in_body, tm=tm),
        out_shape=[jax.ShapeDtypeStruct((b, t, wa.shape[1]), F32)]
        + [jax.ShapeDtypeStruct((b, t, inner), BF16)] * 3
        + [jax.ShapeDtypeStruct((b, t, wbab.shape[1]), F32), jax.ShapeDtypeStruct((b, t, wz.shape[1]), F32),
           jax.ShapeDtypeStruct((b, CONV_CARRY, cdim), F32)],
        grid=(b, t // tm),
        in_specs=[row(d), halo, per_b, _resident((1, d)), _resident(wa.shape), _resident(wqkv.shape),
                  _resident(wbab.shape), _resident(wz.shape), _resident(conv_w.shape)],
        out_specs=[row(wa.shape[1]), row(inner), row(inner), row(inner), row(wbab.shape[1]), row(wz.shape[1]), per_b],
        scratch_shapes=[pltpu.VMEM((tm + CONV_CARRY, cdim), F32)],
        compiler_params=_cparams(("parallel", "arbitrary")),
        name="even_in_proj",
    )(x, x, conv0p, nw.reshape(1, d), wa, wqkv, wbab, wz, conv_w.astype(F32))
    return outs[:6], outs[6][:, CONV_CARRY - (CONV_W - 1):]


def _delta_body(q_ref, k_ref, v_ref, bab_ref, z_ref, s0_ref, alog_ref, dtb_ref, nw_ref,
                y_ref, s_ref, *, nb, tb, L):
    t_idx = pl.program_id(1)
    hd = DK_B
    rows = nb * tb
    per_seg = tb // L

    @pl.when(t_idx == 0)
    def _():
        s_ref[...] = s0_ref[...]

    flat = lambda ref: jnp.concatenate([ref[b] for b in range(nb)], axis=0)
    bab = flat(bab_ref)
    q_all, k_all, v_all = flat(q_ref), flat(k_ref), flat(v_ref)
    g_all = -jnp.exp(alog_ref[...]) * _softplus(bab + dtb_ref[...])
    beta_all = _sigmoid(bab)

    tri = _tri_incl(L).astype(BF16)
    rw = _iota((L, L), 0)
    cl_ = _iota((L, L), 1)
    strict = cl_ < rw
    incl = cl_ <= rw
    eye = (rw == cl_).astype(F32)
    nw = nw_ref[...]
    z = flat(z_ref)
    n_chunks = rows // L

    g_cum_all = _chunk_cumsum(g_all, L)
    g_tot_all = _chunk_last(g_cum_all, L)

    qh, kh, vh, kdec, qg, bv_, bk_, gb, beta_b, e_last = {}, {}, {}, {}, {}, {}, {}, {}, {}, {}
    for h in range(H_B):
        qh[h] = q_all[:, h * hd:(h + 1) * hd]
        kh[h] = k_all[:, h * hd:(h + 1) * hd]
        qs, ks = qh[h].astype(F32), kh[h].astype(F32)
        vs = v_all[:, h * hd:(h + 1) * hd].astype(F32)
        gb[h] = jnp.broadcast_to(g_all[:, h:h + 1], (rows, LANES))
        g_cum = jnp.broadcast_to(g_cum_all[:, h:h + 1], (rows, LANES))
        g_tot = jnp.broadcast_to(g_tot_all[:, h:h + 1], (rows, LANES))
        bb = jnp.broadcast_to(beta_all[:, H_B + h:H_B + h + 1], (rows, LANES))
        e_g = jnp.exp(g_cum)
        kdec[h] = (ks * jnp.exp(g_tot - g_cum)).astype(BF16)
        qg[h] = qs * e_g
        bv_[h] = bb * vs
        bk_[h] = bb * e_g * ks
        beta_b[h] = bb
        e_last[h] = jnp.exp(g_tot)

    items = [(c, h) for c in range(n_chunks) for h in range(H_B)]
    rs = {c: slice(c * L, (c + 1) * L) for c in range(n_chunks)}
    diff = {}
    for c, h in items:
        hi, mid, lo = _split3(jnp.where(strict, gb[h][rs[c], :L], 0.0))
        diff[(c, h)] = (jnp.dot(tri, hi, preferred_element_type=F32) + jnp.dot(tri, mid, preferred_element_type=F32)
                        + jnp.dot(tri, lo, preferred_element_type=F32))
    kk_ = {(c, h): _mm_nt(kh[h][rs[c]], kh[h][rs[c]]) for c, h in items}
    qk_ = {(c, h): _mm_nt(qh[h][rs[c]], kh[h][rs[c]]) for c, h in items}
    n_mat, qk_m = {}, {}
    for c, h in items:
        e_diff = jnp.exp(jnp.where(incl, diff[(c, h)], 0.0))
        n_mat[(c, h)] = -(beta_b[h][rs[c], :L] * jnp.where(strict, kk_[(c, h)] * e_diff, 0.0))
        qk_m[(c, h)] = jnp.where(incl, qk_[(c, h)] * e_diff, 0.0).astype(BF16)
    t_inv = {it: eye + n_mat[it] for it in items}
    m_pow = n_mat
    power = 2
    while power < L:
        m_pow = {it: _mm(m_pow[it], m_pow[it]) for it in items}
        t_inv = {it: t_inv[it] + _mm(t_inv[it], m_pow[it]) for it in items}
        power *= 2
    sol = {(c, h): _mm(t_inv[(c, h)], jnp.concatenate([bv_[h][rs[c]], bk_[h][rs[c]]], axis=1)) for c, h in items}
    lhs = {(c, h): jnp.concatenate([sol[(c, h)][:, hd:], qg[h][rs[c]]], axis=0).astype(BF16) for c, h in items}
    chains = [(b, h) for b in range(nb) for h in range(H_B)]
    state = {bh: s_ref[bh[0], bh[1]] for bh in chains}
    u_, o_inter = {}, {}
    for k_pos in range(per_seg):
        for b, h in chains:
            c = b * per_seg + k_pos
            ws = _mm(lhs[(c, h)], state[(b, h)])
            u_[(c, h)] = (sol[(c, h)][:, :hd] - ws[:L]).astype(BF16)
            o_inter[(c, h)] = ws[L:]
        for b, h in chains:
            c = b * per_seg + k_pos
            state[(b, h)] = e_last[h][c * L:c * L + 1, :] * state[(b, h)] + _mm_tn(kdec[h][rs[c]], u_[(c, h)])
    for b, h in chains:
        s_ref[b, h] = state[(b, h)]
    for c, h in items:
        o = o_inter[(c, h)] + _mm(qk_m[(c, h)], u_[(c, h)])
        on = o * lax.rsqrt(jnp.mean(o * o, axis=-1, keepdims=True) + EPS) * nw
        k_pos = c % per_seg
        y_ref[c // per_seg, k_pos * L:(k_pos + 1) * L, h * hd:(h + 1) * hd] = (
            on * _silu(z[rs[c], h * hd:(h + 1) * hd])).astype(y_ref.dtype)


def _delta(q, k, v, bab, z, s0, wts, L):
    b, t, inner = q.shape
    nb, tb = _mixer_block(b, t, L)
    pad = lambda x: jnp.pad(x.reshape(1, -1).astype(F32), ((0, 0), (0, LANES - x.size)))
    params = [pad(wts['A_log']), pad(wts['dt_bias']), wts['norm_w'].reshape(1, -1).astype(F32)]
    blk = lambda w: pl.BlockSpec((nb, tb, w), lambda i, j: (i, j, 0))
    st = pl.BlockSpec((nb, H_B, DK_B, DK_B), lambda i, j: (i, 0, 0, 0))
    return pl.pallas_call(
        functools.partial(_delta_body, nb=nb, tb=tb, L=L),
        out_shape=[jax.ShapeDtypeStruct((b, t, inner), BF16), jax.ShapeDtypeStruct((b, H_B, DK_B, DK_B), F32)],
        grid=(b // nb, t // tb),
        in_specs=[blk(inner), blk(inner), blk(inner), blk(LANES), blk(inner), st] + [_resident(p.shape) for p in params],
        out_specs=[blk(inner), st],
        compiler_params=_cparams(("parallel", "arbitrary")),
        name="gated_delta",
    )(q, k, v, bab, z, s0.astype(F32), *params)


def _gla_step(qk_ref, v_ref, ad_ref, g_ref, a2_ref, ab_ref, nw_ref, y_ref, s_ref, *, tb, L):
    kdim = H_C * DK_C
    n_pairs = kdim // LANES
    qk = qk_ref[...]
    log_a = -_softplus(-(_mm(ad_ref[...], a2_ref[...]) + ab_ref[...])) * (1.0 / GLA_NORMALIZER)
    vv = v_ref[...]
    gg = g_ref[...]
    nw = nw_ref[...]
    incl = _iota((L, L), 1) <= _iota((L, L), 0)
    m0 = _iota((tb, LANES), 1) < DK_C
    n_chunks = tb // L

    cum = _chunk_cumsum(log_a, L)
    tot = _chunk_last(cum, L)
    q_in = qk[:, :kdim] * DK_C ** -0.5 * jnp.exp(cum)
    k_in = (qk[:, kdim:] * jnp.exp(-cum)).astype(BF16)
    k_out = qk[:, kdim:] * jnp.exp(tot - cum)
    dec = jnp.exp(tot)
    vb16 = vv.astype(BF16)

    rs = {c: slice(c * L, (c + 1) * L) for c in range(n_chunks)}
    items = [(c, h) for c in range(n_chunks) for h in range(H_C)]
    qm, km = {}, {}
    for h in range(H_C):
        ls = slice((h // 2) * LANES, (h // 2 + 1) * LANES)
        mask = m0 if h % 2 == 0 else jnp.logical_not(m0)
        qm[h] = jnp.where(mask, q_in[:, ls], 0.0).astype(BF16)
        km[h] = jnp.where(mask, k_out[:, ls], 0.0).astype(BF16)
    scores = {(c, h): jnp.where(incl, _mm_nt(qm[h][rs[c]], k_in[rs[c], (h // 2) * LANES:(h // 2 + 1) * LANES]), 0.0)
              .astype(BF16) for c, h in items}
    upd = {(c, h): _mm_tn(vb16[rs[c], h * DV_C:(h + 1) * DV_C], km[h][rs[c]]) for c, h in items}
    state = [s_ref[p] for p in range(n_pairs)]
    s_prev = {}
    for c in range(n_chunks):
        for p in range(n_pairs):
            s_prev[(c, p)] = state[p].astype(BF16)
            state[p] = (state[p] * dec[c * L:c * L + 1, p * LANES:(p + 1) * LANES]
                        + upd[(c, 2 * p)] + upd[(c, 2 * p + 1)])
    for p in range(n_pairs):
        s_ref[p] = state[p]
    for c, h in items:
        hs = slice(h * DV_C, (h + 1) * DV_C)
        o = _mm_nt(qm[h][rs[c]], s_prev[(c, h // 2)]) + _mm(scores[(c, h)], vb16[rs[c], hs])
        on = o * lax.rsqrt(jnp.mean(o * o, axis=-1, keepdims=True) + EPS) * nw
        y_ref[rs[c], hs] = (on * _silu(gg[rs[c], hs])).astype(y_ref.dtype)


def _odd_mixers_body(qk_ref, cv_ref, ad_ref, cg_ref, gs0_ref, a2_ref, ab_ref, nw_ref,
                     dq_ref, dk_ref, dv_ref, dg_ref, cos_ref, sin_ref, rs0_ref,
                     yc_ref, gs_ref, yd_ref, rs_ref, *, tb, L):
    @pl.when(pl.program_id(1) == 0)
    def _():
        gs_ref[...] = gs0_ref[...]
        rs_ref[...] = rs0_ref[...]

    _gla_step(qk_ref, cv_ref, ad_ref, cg_ref, a2_ref, ab_ref, nw_ref, yc_ref, gs_ref, tb=tb, L=L)
    _ret_step(dq_ref, dk_ref, dv_ref, dg_ref, cos_ref, sin_ref, yd_ref, rs_ref, tb=tb, L=L)


def _odd_mixers(cqk, cv, cad, cg, gla0, wts, dq, dk, dv, dg, ret0, pos0, L):
    b, t, _ = cqk.shape
    kdim = H_C * DK_C
    vdim = H_C * DV_C
    rdim = H_D * DK_D
    n_pairs = kdim // LANES
    tb = min(t, 256)
    rank = wts['a2'].shape[0]
    a2p = jnp.pad(wts['a2'].astype(F32), ((0, LANES - rank), (0, 0))).astype(BF16)
    params = [a2p, wts['a_bias'].reshape(1, -1).astype(F32), wts['norm_w'].reshape(1, -1).astype(F32)]
    st0 = jnp.swapaxes(gla0.astype(F32), -1, -2).reshape(b, n_pairs, 2, DV_C, DK_C)
    st0 = jnp.concatenate([st0[:, :, 0], st0[:, :, 1]], axis=-1)
    inv = ROPE_BASE ** (-jnp.arange(0, DK_D, 2, dtype=F32) / DK_D)
    ang = (jnp.arange(t) + pos0).astype(F32)[:, None] * inv[None, :]
    cos = jnp.concatenate([jnp.cos(ang), jnp.cos(ang)], axis=-1)
    sin = jnp.concatenate([-jnp.sin(ang), jnp.sin(ang)], axis=-1)
    blk = lambda w: pl.BlockSpec((None, tb, w), lambda i, j: (i, j, 0))
    per_b = lambda shape: pl.BlockSpec((None,) + shape, lambda i, j: (i,) + (0,) * len(shape))
    tab = pl.BlockSpec((tb, DK_D), lambda i, j: (j, 0))
    gst, rst = per_b((n_pairs, DV_C, LANES)), per_b((H_D, DK_D, DK_D))
    y_c, st, y_d, ret_new = pl.pallas_call(
        functools.partial(_odd_mixers_body, tb=tb, L=L),
        out_shape=[jax.ShapeDtypeStruct((b, t, vdim), BF16), jax.ShapeDtypeStruct((b, n_pairs, DV_C, LANES), F32),
                   jax.ShapeDtypeStruct((b, t, rdim), BF16), jax.ShapeDtypeStruct((b, H_D, DK_D, DK_D), F32)],
        grid=(b, t // tb),
        in_specs=[blk(2 * kdim), blk(vdim), blk(LANES), blk(vdim), gst] + [_resident(p.shape) for p in params]
        + [blk(rdim), blk(rdim), blk(rdim), blk(rdim), tab, tab, rst],
        out_specs=[blk(vdim), gst, blk(rdim), rst],
        compiler_params=_cparams(("parallel", "arbitrary")),
        name="gla_retention",
    )(cqk, cv, cad, cg, st0, *params, dq, dk, dv, dg, cos, sin, ret0.astype(F32))
    st = jnp.stack([st[..., :DK_C], st[..., DK_C:]], axis=2).reshape(b, H_C, DV_C, DK_C)
    return y_c, jnp.swapaxes(st, -1, -2), y_d, ret_new


def _ret_step(q_ref, k_ref, v_ref, g_ref, cos_ref, sin_ref, y_ref, s_ref, *, tb, L):
    hd = DK_D
    cos = cos_ref[...]
    sin = sin_ref[...]
    qq, kk, vv, gg = q_ref[...], k_ref[...], v_ref[...], g_ref[...]
    rw = _iota((L, L), 0)
    cl_ = _iota((L, L), 1)
    incl = cl_ <= rw
    dist = jnp.where(incl, rw - cl_, 0).astype(F32)
    pos = _iota((L, LANES), 0).astype(F32)

    n_chunks = tb // L
    rs = {c: slice(c * L, (c + 1) * L) for c in range(n_chunks)}
    items = [(c, h) for c in range(n_chunks) for h in range(H_D)]
    lg = [math.log(1.0 - 2.0 ** (-5.0 - h)) for h in range(H_D)]
    vb16 = vv.astype(BF16)
    qh, kh, q_in, k_out, dmat = {}, {}, {}, {}, {}
    for h in range(H_D):
        hs = slice(h * hd, (h + 1) * hd)
        q_rot = qq[:, hs] * cos + pltpu.roll(qq[:, hs], hd // 2, 1) * sin
        k_rot = (kk[:, hs] * cos + pltpu.roll(kk[:, hs], hd // 2, 1) * sin) * hd ** -0.5
        dmat[h] = jnp.where(incl, jnp.exp(dist * lg[h]), 0.0)
        e_in = jnp.exp((pos + 1.0) * lg[h])
        e_out = jnp.exp((L - 1.0 - pos) * lg[h])
        qh[h], kh[h] = q_rot.astype(BF16), k_rot.astype(BF16)
        for c in range(n_chunks):
            q_in[(c, h)] = (q_rot[rs[c]] * e_in).astype(BF16)
            k_out[(c, h)] = (k_rot[rs[c]] * e_out).astype(BF16)
    scores = {(c, h): (_mm_nt(qh[h][rs[c]], kh[h][rs[c]]) * dmat[h]).astype(BF16) for c, h in items}
    upd = {(c, h): _mm_tn(k_out[(c, h)], vb16[rs[c], h * hd:(h + 1) * hd]) for c, h in items}
    state = [s_ref[h] for h in range(H_D)]
    s_prev = {}
    for c in range(n_chunks):
        for h in range(H_D):
            s_prev[(c, h)] = state[h].astype(BF16)
            state[h] = state[h] * math.exp(lg[h] * L) + upd[(c, h)]
    for h in range(H_D):
        s_ref[h] = state[h]
    for c, h in items:
        hs = slice(h * hd, (h + 1) * hd)
        o = _mm(q_in[(c, h)], s_prev[(c, h)]) + _mm(scores[(c, h)], vb16[rs[c], hs])
        mu = jnp.mean(o, axis=-1, keepdims=True)
        oc = o - mu
        on = oc * lax.rsqrt(jnp.mean(oc * oc, axis=-1, keepdims=True) + EPS)
        y_ref[rs[c], hs] = (on * _silu(gg[rs[c], hs])).astype(y_ref.dtype)


def _prep_weights(W):
    bf = lambda x: x.astype(BF16)
    a_proj = 3 * H_A * N_A + 64 + 64 + 128
    b_conv = 3 * H_B * DK_B
    P = {}
    for name in ('ffn1_wg', 'ffn1_wu', 'ffn1_wd', 'ffn2_wg', 'ffn2_wu', 'ffn2_wd', 'mem_wq', 'mem_wo', 'mem_wk',
                 'mem_wv', 'even_w_out', 'odd_w_out'):
        P[name] = bf(W[name])
    ew = W['even_w_in']
    o = a_proj
    bab = ew[:, :, o + b_conv:o + b_conv + 2 * H_B]
    P['even_in'] = [bf(ew[:, :, :a_proj]), bf(ew[:, :, o:o + b_conv]),
                    bf(jnp.pad(bab, ((0, 0), (0, 0), (0, LANES - 2 * H_B)))), bf(ew[:, :, o + b_conv + 2 * H_B:])]
    ow = W['odd_w_in']
    kd, vd, rank = H_C * DK_C, H_C * DV_C, W['gla_a2'].shape[1]
    c_proj = 2 * kd + 2 * vd + rank
    dd = H_D * DK_D
    cad = ow[:, :, 2 * kd + vd:2 * kd + vd + rank]
    P['odd_in'] = [bf(ow[:, :, :2 * kd]), bf(ow[:, :, 2 * kd:2 * kd + vd]),
                   bf(jnp.pad(cad, ((0, 0), (0, 0), (0, LANES - rank)))), bf(ow[:, :, 2 * kd + vd + rank:c_proj])] \
        + [bf(ow[:, :, c_proj + j * dd:c_proj + (j + 1) * dd]) for j in range(4)]
    return P


def _trunk(x, pos0, mem_k, mem_v, heads, shift, rwkv, conv, delta, gla, ret, W, P):
    b, t, d = x.shape
    n = b * t
    depth = W['norm_ffn1'].shape[0]
    L_delta = min(64, t)
    L_lin = min(64, t)
    flat = lambda z: z.reshape(n, z.shape[-1])
    unflat = lambda z: z.reshape(b, t, z.shape[-1])
    new = {k: [] for k in ('shift', 'rwkv', 'conv', 'delta', 'gla', 'ret')}
    x = flat(x)
    for l in range(depth):
        i = l // 2
        x = _ffn(x, W['norm_ffn1'][l], P['ffn1_wg'][l], P['ffn1_wu'][l], P['ffn1_wd'][l])
        if l % 2 == 0:
            (pa, dq_, dk_, dv_, bab, z), s3 = _even_in_proj(unflat(x), W['norm_mix'][l], [w[i] for w in P['even_in']],
                                                           conv[i], W['delta_conv_w'][i])
            wa = dict(mu=W['rwkv_mu'][i], w0=W['rwkv_w0'][i], w2=W['rwkv_w2'][i], a0=W['rwkv_a0'][i],
                      a2=W['rwkv_a2'][i], g2=W['rwkv_g2'][i], kk=W['rwkv_kk'][i], ka=W['rwkv_ka'][i],
                      rk=W['rwkv_rk'][i], ln_w=W['rwkv_ln_w'][i], ln_b=W['rwkv_ln_b'][i])
            y_a, s1, s2 = _rwkv(pa, shift[i], rwkv[i], wa, L_delta)
            wb = dict(A_log=W['delta_A_log'][i], dt_bias=W['delta_dt_bias'][i], norm_w=W['delta_norm_w'][i])
            y_b, s4 = _delta(dq_, dk_, dv_, bab, z, delta[i], wb, L_delta)
            new['shift'].append(s1)
            new['rwkv'].append(s2)
            new['conv'].append(s3)
            new['delta'].append(s4)
            split = H_A * N_A
            halves, w_out = (y_a, y_b), P['even_w_out'][i]
        else:
            cqk, cv, cad, cg, dq, dk, dv, dg = _norm_proj(x, W['norm_mix'][l], [w[i] for w in P['odd_in']],
                                                          bf16_out=(1, 6))
            wc = dict(a2=W['gla_a2'][i], a_bias=W['gla_a_bias'][i], norm_w=W['gla_norm_w'][i])
            y_c, s5, y_d, s6 = _odd_mixers(unflat(cqk), unflat(cv), unflat(cad), unflat(cg), gla[i], wc,
                                           unflat(dq), unflat(dk), unflat(dv), unflat(dg), ret[i], pos0, L_lin)
            new['gla'].append(s5)
            new['ret'].append(s6)
            split = H_C * DV_C
            halves, w_out = (y_c, y_d), P['odd_w_out'][i]
        mk = mem_k[l].astype(BF16)
        mv = mem_v[l].astype(BF16)
        x = flat(_mix_out_mem_attn(unflat(x), halves[0], halves[1], w_out[:split], w_out[split:], W['norm_mem'][l],
                                   P['mem_wq'][l], mk, mv, P['mem_wo'][l], heads))
        fw = W['final_norm'] if l == depth - 1 else None
        x = _ffn(x, W['norm_ffn2'][l], P['ffn2_wg'][l], P['ffn2_wu'][l], P['ffn2_wd'][l], fw)
    return unflat(x), new


def kernel(x_prompt, x_sample, mem_prompt, state_rwkv_shift, state_rwkv, state_delta_conv, state_delta, state_gla, state_ret, cache_mem_k, cache_mem_v, norm_ffn1, ffn1_wg, ffn1_wu, ffn1_wd, norm_mix, even_w_in, even_w_out, rwkv_mu, rwkv_w0, rwkv_w2, rwkv_a0, rwkv_a2, rwkv_g2, rwkv_kk, rwkv_ka, rwkv_rk, rwkv_ln_w, rwkv_ln_b, delta_conv_w, delta_A_log, delta_dt_bias, delta_norm_w, odd_w_in, odd_w_out, gla_a2, gla_a_bias, gla_norm_w, norm_mem, mem_norm_kv, mem_wq, mem_wk, mem_wv, mem_wo, norm_ffn2, ffn2_wg, ffn2_wu, ffn2_wd, final_norm):
    W = dict(norm_ffn1=norm_ffn1, ffn1_wg=ffn1_wg, ffn1_wu=ffn1_wu, ffn1_wd=ffn1_wd, norm_mix=norm_mix,
             even_w_in=even_w_in, even_w_out=even_w_out, rwkv_mu=rwkv_mu, rwkv_w0=rwkv_w0, rwkv_w2=rwkv_w2,
             rwkv_a0=rwkv_a0, rwkv_a2=rwkv_a2, rwkv_g2=rwkv_g2, rwkv_kk=rwkv_kk, rwkv_ka=rwkv_ka,
             rwkv_rk=rwkv_rk, rwkv_ln_w=rwkv_ln_w, rwkv_ln_b=rwkv_ln_b, delta_conv_w=delta_conv_w,
             delta_A_log=delta_A_log, delta_dt_bias=delta_dt_bias, delta_norm_w=delta_norm_w,
             odd_w_in=odd_w_in, odd_w_out=odd_w_out, gla_a2=gla_a2, gla_a_bias=gla_a_bias,
             gla_norm_w=gla_norm_w, norm_mem=norm_mem, mem_wq=mem_wq, mem_wk=mem_wk, mem_wv=mem_wv,
             mem_wo=mem_wo, norm_ffn2=norm_ffn2, ffn2_wg=ffn2_wg, ffn2_wu=ffn2_wu, ffn2_wd=ffn2_wd,
             final_norm=final_norm)
    P = _prep_weights(W)
    dt = x_prompt.dtype
    bp, _, d = x_prompt.shape
    depth = norm_ffn1.shape[0]
    n_even, n_odd = (depth + 1) // 2, depth // 2
    heads, hd = cache_mem_k.shape[3], cache_mem_k.shape[4]
    n_mem = mem_prompt.shape[1]
    pk, pv, pk_flat, pv_flat = [], [], [], []
    mem_flat = mem_prompt.reshape(bp * n_mem, d)
    for l in range(depth):
        mk, mv = _norm_proj(mem_flat, mem_norm_kv[l], [P['mem_wk'][l], P['mem_wv'][l]])
        pk_flat.append(mk.reshape(bp, n_mem, d))
        pv_flat.append(mv.reshape(bp, n_mem, d))
        pk.append(mk.reshape(bp, n_mem, heads, hd))
        pv.append(mv.reshape(bp, n_mem, heads, hd))
    zeros = lambda ref, cnt: [jnp.zeros((bp,) + ref.shape[2:], F32)] * cnt
    y_prompt, ps = _trunk(x_prompt, 0, pk_flat, pv_flat, heads, zeros(state_rwkv_shift, n_even), zeros(state_rwkv, n_even),
                          zeros(state_delta_conv, n_even), zeros(state_delta, n_even), zeros(state_gla, n_odd),
                          zeros(state_ret, n_odd), W, P)
    bs = x_sample.shape[0]
    y_sample, ss = _trunk(x_sample, PAST_LEN, [cache_mem_k[l].reshape(bs, n_mem, d) for l in range(depth)],
                          [cache_mem_v[l].reshape(bs, n_mem, d) for l in range(depth)], heads,
                          [state_rwkv_shift[i] for i in range(n_even)], [state_rwkv[i] for i in range(n_even)],
                          [state_delta_conv[i] for i in range(n_even)], [state_delta[i] for i in range(n_even)],
                          [state_gla[i] for i in range(n_odd)], [state_ret[i] for i in range(n_odd)], W, P)
    st = lambda xs: jnp.stack(xs).astype(dt)
    order = ('shift', 'rwkv', 'conv', 'delta', 'gla', 'ret')
    return ((y_prompt, y_sample) + tuple(st(ps[k]) for k in order) + (st(pk), st(pv))
            + tuple(st(ss[k]) for k in order))
```

```python
import functools
import math

import jax
import jax.numpy as jnp
from jax import lax
from jax.experimental import pallas as pl
from jax.experimental.pallas import tpu as pltpu

F32 = jnp.float32
BF16 = jnp.bfloat16
EPS = 1e-6
RWKV_LN_EPS = 64e-5
GLA_NORMALIZER = 16.0
ROPE_BASE = 10000.0
PAST_LEN = 1024

LANES = 128
MXU_DIM = 256
VMEM_LIMIT = 56 * 1024 * 1024

H_A, N_A = 8, 64
H_B, DK_B = 4, 128
H_C, DK_C, DV_C = 4, 64, 128
H_D, DK_D = 4, 128
CONV_W = 4
CONV_CARRY = 8
MIXER_ROWS = 512
CONV_ROWS = 128


def _cparams(sem):
    return pltpu.CompilerParams(dimension_semantics=sem, vmem_limit_bytes=VMEM_LIMIT)


def _mm(a, b):
    return jnp.dot(a.astype(BF16), b.astype(BF16), preferred_element_type=F32)


def _mm_nt(a, b):
    return lax.dot_general(a.astype(BF16), b.astype(BF16), (((1,), (1,)), ((), ())),
                           preferred_element_type=F32)


def _mm_tn(a, b):
    return lax.dot_general(a.astype(BF16), b.astype(BF16), (((0,), (0,)), ((), ())),
                           preferred_element_type=F32)


def _split3(x):
    hi = x.astype(BF16)
    r = x - hi.astype(F32)
    mid = r.astype(BF16)
    lo = (r - mid.astype(F32)).astype(BF16)
    return hi, mid, lo


def _mm_exact_lhs01(a01, x):
    hi, mid, lo = _split3(x)
    a = a01.astype(BF16)
    return (jnp.dot(a, hi, preferred_element_type=F32) + jnp.dot(a, mid, preferred_element_type=F32)
            + jnp.dot(a, lo, preferred_element_type=F32))


def _iota(shape, dim):
    return lax.broadcasted_iota(jnp.int32, shape, dim)


def _tri_incl(n):
    return (_iota((n, n), 1) <= _iota((n, n), 0)).astype(F32)


def _chunk_tri(tb, L):
    shift = L.bit_length() - 1
    assert 1 << shift == L
    rt, ct = _iota((tb, tb), 0), _iota((tb, tb), 1)
    same = jnp.right_shift(rt, shift) == jnp.right_shift(ct, shift)
    return jnp.logical_and(same, ct <= rt).astype(F32)


def _chunk_cumsum(x, L):
    n = x.shape[0]
    g = min(n, MXU_DIM)
    tri = _chunk_tri(g, L)
    return jnp.concatenate([_mm_exact_lhs01(tri, x[r:r + g]) for r in range(0, n, g)], axis=0)


def _chunk_last(cum, L):
    tb, w = cum.shape
    return jnp.concatenate([jnp.broadcast_to(cum[c * L + L - 1:c * L + L, :], (L, w)) for c in range(tb // L)], axis=0)


def _sigmoid(x):
    return 0.5 * (jnp.tanh(0.5 * x) + 1.0)


def _silu(x):
    return x * _sigmoid(x)


def _softplus(x):
    return jnp.maximum(x, 0.0) + jnp.log(1.0 + jnp.exp(-jnp.abs(x)))


def _rms(x, w):
    return x * lax.rsqrt(jnp.mean(x * x, axis=-1, keepdims=True) + EPS) * w


def _inv_unit_lower(n_mat, size):
    dim = n_mat.shape[0]
    eye = (_iota((dim, dim), 0) == _iota((dim, dim), 1)).astype(F32)
    t = eye + n_mat
    m = n_mat
    power = 2
    while power < size:
        m = _mm(m, m)
        t = t + _mm(t, m)
        power *= 2
    return t


def _row_tile(n, target):
    t = min(n, target)
    while n % t:
        t //= 2
    return t


def _mixer_block(b, t, L):
    tb = min(t, 2 * L)
    nb = max(1, min(b, MIXER_ROWS // tb))
    while b % nb:
        nb -= 1
    return nb, tb


def _resident(shape):
    nd = len(shape)
    return pl.BlockSpec(shape, lambda *_: (0,) * nd, pipeline_mode=pl.Buffered(1))


def _ffn_body(x_ref, nw_ref, wg_ref, wu_ref, wd_ref, fw_ref, o_ref, *, chunk, final):
    x = x_ref[...]
    h = _rms(x, nw_ref[...]).astype(BF16)
    d_ff = wg_ref.shape[1]
    n_chunks = d_ff // chunk

    def gate_up(c):
        sl = slice(c * chunk, (c + 1) * chunk)
        return (jnp.dot(h, wg_ref[:, sl], preferred_element_type=F32),
                jnp.dot(h, wu_ref[:, sl], preferred_element_type=F32))

    acc = None
    nxt = gate_up(0)
    for c in range(n_chunks):
        g, u = nxt
        if c + 1 < n_chunks:
            nxt = gate_up(c + 1)
        a = (_silu(g) * u).astype(BF16)
        d = jnp.dot(a, wd_ref[c * chunk:(c + 1) * chunk, :], preferred_element_type=F32)
        acc = d if acc is None else acc + d
    y = x + 0.5 * acc
    if final:
        y = _rms(y, fw_ref[...])
    o_ref[...] = y


def _ffn(x, nw, wg, wu, wd, fw=None):
    n, d = x.shape
    d_ff = wg.shape[1]
    tm = _row_tile(n, 1024)
    final = fw is not None
    if fw is None:
        fw = nw
    return pl.pallas_call(
        functools.partial(_ffn_body, chunk=MXU_DIM, final=final),
        out_shape=jax.ShapeDtypeStruct((n, d), F32),
        grid=(n // tm,),
        in_specs=[pl.BlockSpec((tm, d), lambda i: (i, 0)), _resident((1, d)), _resident((d, d_ff)),
                  _resident((d, d_ff)), _resident((d_ff, d)), _resident((1, d))],
        out_specs=pl.BlockSpec((tm, d), lambda i: (i, 0)),
        compiler_params=_cparams(("parallel",)),
        name="ffn",
    )(x, nw.reshape(1, d), wg, wu, wd, fw.reshape(1, d))


def _norm_proj_body(*refs, n_w):
    x_ref, nw_ref = refs[0], refs[1]
    w_refs = refs[2:2 + n_w]
    o_refs = refs[2 + n_w:]
    h = _rms(x_ref[...], nw_ref[...]).astype(BF16)
    for w_ref, o_ref in zip(w_refs, o_refs):
        o_ref[...] = jnp.dot(h, w_ref[...], preferred_element_type=F32).astype(o_ref.dtype)


def _norm_proj(x, nw, ws, bf16_out=()):
    n, d = x.shape
    tm = _row_tile(n, 1024)
    return pl.pallas_call(
        functools.partial(_norm_proj_body, n_w=len(ws)),
        out_shape=[jax.ShapeDtypeStruct((n, w.shape[1]), BF16 if i in bf16_out else F32) for i, w in enumerate(ws)],
        grid=(n // tm,),
        in_specs=[pl.BlockSpec((tm, d), lambda i: (i, 0)), _resident((1, d))] + [_resident(w.shape) for w in ws],
        out_specs=[pl.BlockSpec((tm, w.shape[1]), lambda i: (i, 0)) for w in ws],
        compiler_params=_cparams(("parallel",)),
        name="norm_proj",
    )(x, nw.reshape(1, d), *ws)


def _mix_out_mem_attn_body(x_ref, ya_ref, yb_ref, wa_ref, wb_ref, nw_ref, wq_ref, k_ref, v_ref, wo_ref, o_ref, *,
                           heads):
    nb, tm, d = x_ref.shape
    flat = lambda ref: jnp.concatenate([ref[b] for b in range(nb)], axis=0)
    x = (flat(x_ref) + jnp.dot(flat(ya_ref), wa_ref[...], preferred_element_type=F32)
         + jnp.dot(flat(yb_ref), wb_ref[...], preferred_element_type=F32))
    h = _rms(x, nw_ref[...]).astype(BF16)
    q = jnp.dot(h, wq_ref[...], preferred_element_type=F32).astype(BF16)
    hd = d // heads

    def scores(job):
        b, i = job
        sl = slice(i * hd, (i + 1) * hd)
        return _mm_nt(q[b * tm:(b + 1) * tm, sl], k_ref[b, :, sl]) * hd ** -0.5

    jobs = [(b, i) for b in range(nb) for i in range(heads)]
    outs = [[] for _ in range(nb)]
    nxt = scores(jobs[0])
    for n, (b, i) in enumerate(jobs):
        s = nxt
        if n + 1 < len(jobs):
            nxt = scores(jobs[n + 1])
        e = jnp.exp(s - jnp.max(s, axis=-1, keepdims=True))
        pr = e / jnp.sum(e, axis=-1, keepdims=True)
        outs[b].append(_mm(pr, v_ref[b, :, i * hd:(i + 1) * hd]))
    o = jnp.concatenate([jnp.concatenate(ob, axis=-1) for ob in outs], axis=0).astype(BF16)
    out = x + jnp.dot(o, wo_ref[...], preferred_element_type=F32)
    for b in range(nb):
        o_ref[b] = out[b * tm:(b + 1) * tm, :]


def _mix_out_mem_attn(x, ya, yb, wa, wb, nw, wq, mk, mv, wo, heads):
    b, t, d = x.shape
    m = mk.shape[1]
    tm = _row_tile(t, 1024)
    nb = max(1, min(b, MIXER_ROWS // 2 // tm))
    while b % nb:
        nb -= 1
    row = lambda w: pl.BlockSpec((nb, tm, w), lambda i, j: (i, j, 0))
    mem = pl.BlockSpec((nb, m, d), lambda i, j: (i, 0, 0))
    return pl.pallas_call(
        functools.partial(_mix_out_mem_attn_body, heads=heads),
        out_shape=jax.ShapeDtypeStruct((b, t, d), F32),
        grid=(b // nb, t // tm),
        in_specs=[row(d), row(ya.shape[2]), row(yb.shape[2]), _resident(wa.shape), _resident(wb.shape),
                  _resident((1, d)), _resident((d, d)), mem, mem, _resident((d, d))],
        out_specs=row(d),
        compiler_params=_cparams(("parallel", "parallel")),
        name="mix_out_mem_attn",
    )(x, ya, yb, wa, wb, nw.reshape(1, d), wq, mk, mv, wo)


def _seg_sum(x, bd):
    g = bd.shape[0]
    xb = x.astype(BF16)
    return jnp.concatenate([jnp.dot(xb[:, j:j + g], bd, preferred_element_type=F32)
                            for j in range(0, x.shape[1], g)], axis=1)


def _rwkv_step(pa_ref, mu_ref, w0_ref, w2_ref, a0_ref, a2_ref, g2_ref, kkw_ref, ka_ref,
               rk_ref, lnw_ref, lnb_ref, bd_ref, y_ref, shift_ref, s_ref, y_scr, *, nb, tb, L):
    a_dim = H_A * N_A
    n_pairs = a_dim // LANES
    row = _iota((tb, pa_ref.shape[2]), 0)
    segs, prevs = [], []
    for b in range(nb):
        seg = pa_ref[b]
        segs.append(seg)
        prevs.append(jnp.where(row == 0, shift_ref[b], pltpu.roll(seg, 1, 0)))
        shift_ref[b] = seg[tb - 1:tb, :]
    pa = jnp.concatenate(segs, axis=0)
    prev = jnp.concatenate(prevs, axis=0)
    rows = nb * tb
    xa = pa + (prev - pa) * mu_ref[...]
    r = xa[:, 0:a_dim]
    k = xa[:, a_dim:2 * a_dim]
    v = xa[:, 2 * a_dim:3 * a_dim]
    xwa = xa[:, 3 * a_dim:3 * a_dim + LANES]
    xg = xa[:, 3 * a_dim + LANES:]
    wlog = -_softplus(-(w0_ref[...] + _mm(jnp.tanh(xwa), w2_ref[...]))) - 0.5
    logdec = -jnp.exp(wlog)
    a = _sigmoid(a0_ref[...] + _mm(xwa, a2_ref[...]))
    gate = _mm(_sigmoid(xg), g2_ref[...])
    bd = bd_ref[...]
    kq = k * kkw_ref[...]
    kk = kq * lax.rsqrt(_seg_sum(kq * kq, bd) + EPS)
    k2 = k * (1.0 + (a - 1.0) * ka_ref[...])
    av = -kk
    bv = kk * a

    lane = _iota((L, LANES), 1)
    m0 = lane < N_A
    r2 = _iota((2 * L, 2 * L), 0)
    c2 = _iota((2 * L, 2 * L), 1)
    bdm = jnp.logical_not(jnp.logical_xor(r2 >= L, c2 >= L))
    r_loc = jnp.where(r2 >= L, r2 - L, r2)
    c_loc = jnp.where(c2 >= L, c2 - L, c2)
    strict_t = jnp.logical_and(bdm, c_loc > r_loc)
    strict_l = jnp.bitwise_and(_iota((L, 2 * L), 1), L - 1) < _iota((L, 2 * L), 0)
    incl_l = jnp.bitwise_and(_iota((L, 4 * L), 1), L - 1) <= _iota((L, 4 * L), 0)
    bdl = jnp.logical_not(jnp.logical_xor(_iota((2 * L, LANES), 0) >= L, _iota((2 * L, LANES), 1) >= N_A))
    eye2 = (r2 == c2).astype(F32)
    zero_b = jnp.zeros((L, LANES), BF16)

    def stack2(x):
        xb = x.astype(BF16)
        return jnp.concatenate([jnp.where(m0, xb, zero_b), jnp.where(m0, zero_b, xb)], axis=0)

    def dup2(x):
        xb = x.astype(BF16)
        return jnp.concatenate([xb, xb], axis=0)

    cum = _chunk_cumsum(logdec, L)
    tot = _chunk_last(cum, L)
    e_neg = jnp.exp(-cum)
    e_out = jnp.exp(tot - cum)
    a_t = av * jnp.exp(cum - logdec)
    r_t = r * jnp.exp(cum)
    b_t = bv * e_neg
    k_t = k2 * e_neg
    b_o = bv * e_out
    k_o = k2 * e_out
    g_l = jnp.exp(tot)

    per_seg = tb // L
    items = [(c, p) for c in range(rows // L) for p in range(n_pairs)]
    sl = {(c, p): (slice(c * L, (c + 1) * L), slice(p * LANES, (p + 1) * LANES)) for c, p in items}
    xa = {it: stack2(a_t[sl[it]]) for it in items}
    ra = {it: r_t[sl[it]].astype(BF16) for it in items}
    yb = {it: dup2(b_t[sl[it]]) for it in items}
    sbk = {it: jnp.concatenate([stack2(b_t[sl[it]]), stack2(k_t[sl[it]])], axis=0) for it in items}
    v2 = {it: jnp.where(bdl, dup2(v[sl[it]]), jnp.zeros((2 * L, LANES), BF16)) for it in items}
    wst = {it: jnp.concatenate([stack2(b_o[sl[it]]), stack2(k_o[sl[it]])], axis=0) for it in items}
    n_t = {it: jnp.where(strict_t, _mm_nt(yb[it], xa[it]), 0.0) for it in items}
    ak = {it: jnp.where(strict_l, _mm_nt(a_t[sl[it]], sbk[it][2 * L:]), 0.0).astype(BF16) for it in items}
    rbk = {it: jnp.where(incl_l, _mm_nt(ra[it], sbk[it]), 0.0).astype(BF16) for it in items}
    akv = {it: stack2(_mm(ak[it], v2[it])) for it in items}
    t_t = {it: eye2 + n_t[it] for it in items}
    n_b = {it: n_t[it].astype(BF16) for it in items}
    m_t = {it: _mm(n_b[it], n_b[it]).astype(BF16) for it in items}
    power = 4
    while power < L:
        prod = {it: _mm(m_t[it], jnp.concatenate([m_t[it], t_t[it].astype(BF16)], axis=1)) for it in items}
        m_t = {it: prod[it][:, :2 * L].astype(BF16) for it in items}
        t_t = {it: t_t[it] + prod[it][:, 2 * L:] for it in items}
        power *= 2
    if L > 2:
        t_t = {it: t_t[it] + _mm(m_t[it], t_t[it]) for it in items}
    om_up = {it: _mm_tn(t_t[it], jnp.concatenate([xa[it], akv[it]], axis=1)) for it in items}
    omega = {it: om_up[it][:, :LANES].astype(BF16) for it in items}
    chains = [(b, p) for b in range(nb) for p in range(n_pairs)]
    state = {bp: s_ref[bp[0], bp[1]] for bp in chains}
    s_prev, uv = {}, {}
    for k_pos in range(per_seg):
        for b, p in chains:
            it = (b * per_seg + k_pos, p)
            s_prev[it] = state[(b, p)].astype(BF16)
            u_st = _mm_nt(omega[it], s_prev[it]) + om_up[it][:, LANES:]
            uv[it] = jnp.concatenate([u_st.astype(BF16), v2[it]], axis=0)
        for b, p in chains:
            it = (b * per_seg + k_pos, p)
            state[(b, p)] = state[(b, p)] * g_l[it[0] * L:it[0] * L + 1, sl[it][1]] + _mm_tn(uv[it], wst[it])
    for b, p in chains:
        s_ref[b, p] = state[(b, p)]
    for it in items:
        y_scr[sl[it]] = _mm_nt(ra[it], s_prev[it]) + _mm(rbk[it], uv[it])

    y = y_scr[...]
    inv_n = 1.0 / N_A
    mu_y = _seg_sum(y, bd) * inv_n
    yc = y - mu_y
    var = _seg_sum(yc * yc, bd) * inv_n
    yn = yc * lax.rsqrt(var + RWKV_LN_EPS) * lnw_ref[...] + lnb_ref[...]
    bonus = _seg_sum(r * k2 * rk_ref[...], bd) * v
    out = ((yn + bonus) * gate).astype(y_ref.dtype)
    for b in range(nb):
        y_ref[b] = out[b * tb:(b + 1) * tb, :]


def _pair_blockdiag(s):
    b, h, n, _ = s.shape
    s = s.reshape(b, h // 2, 2, n, n)
    z = jnp.zeros((b, h // 2, n, n), s.dtype)
    top = jnp.concatenate([s[:, :, 0], z], axis=-1)
    bot = jnp.concatenate([z, s[:, :, 1]], axis=-1)
    return jnp.concatenate([top, bot], axis=-2)


def _pair_unblock(s):
    b, p, n2, _ = s.shape
    n = n2 // 2
    return jnp.stack([s[:, :, :n, :n], s[:, :, n:, n:]], axis=2).reshape(b, 2 * p, n, n)


def _even_mixers_body(*refs, n_a, n_b, nb, tb, L):
    a_in, b_in = refs[:n_a], refs[n_a:n_a + n_b]
    ya_ref, shift_ref, sa_ref, yb_ref, sb_ref, y_scr = refs[n_a + n_b:]

    @pl.when(pl.program_id(1) == 0)
    def _():
        shift_ref[...] = a_in[1][...]
        sa_ref[...] = a_in[2][...]
        sb_ref[...] = b_in[5][...]

    _rwkv_step(a_in[0], *a_in[3:], ya_ref, shift_ref, sa_ref, y_scr, nb=nb, tb=tb, L=L)
    _delta_step(*b_in[:5], *b_in[6:], yb_ref, sb_ref, nb=nb, tb=tb, L=L)


def _even_mixers(pa, shift0, s0, wts, q, k, v, bab, z, d0, wts_d, L):
    b, t, pw = pa.shape
    a_dim = H_A * N_A
    inner = H_B * DK_B
    nb, tb = _mixer_block(b, t, L)
    n_pairs = a_dim // LANES
    seg = _iota((MXU_DIM, MXU_DIM), 0) // N_A == _iota((MXU_DIM, MXU_DIM), 1) // N_A
    bd = seg.astype(BF16)
    z64 = jnp.zeros((N_A, a_dim), F32)
    w2p = jnp.concatenate([wts['w2'], z64], axis=0)
    a2p = jnp.concatenate([z64, wts['a2']], axis=0)
    vec = lambda x: x.reshape(1, -1).astype(F32)
    params = [vec(wts['mu']), vec(wts['w0']), w2p.astype(BF16), vec(wts['a0']), a2p.astype(BF16),
              wts['g2'].astype(BF16), vec(wts['kk']), vec(wts['ka']), vec(wts['rk']), vec(wts['ln_w']),
              vec(wts['ln_b']), bd]
    pad = lambda x: jnp.pad(x.reshape(1, -1).astype(F32), ((0, 0), (0, LANES - x.size)))
    params_d = [pad(wts_d['A_log']), pad(wts_d['dt_bias']), wts_d['norm_w'].reshape(1, -1).astype(F32)]
    blk_t = lambda w: pl.BlockSpec((nb, tb, w), lambda i, j: (i, j, 0))
    per_b = lambda shape: pl.BlockSpec((nb,) + shape, lambda i, j: (i,) + (0,) * len(shape))
    sa_spec, sb_spec = per_b((n_pairs, LANES, LANES)), per_b((H_B, DK_B, DK_B))
    a_specs = [blk_t(pw), per_b((1, pw)), sa_spec] + [_resident(p.shape) for p in params]
    b_specs = [blk_t(inner)] * 3 + [blk_t(LANES), blk_t(inner), sb_spec] + [_resident(p.shape) for p in params_d]
    y_a, shift, s_new, y_b, d_new = pl.pallas_call(
        functools.partial(_even_mixers_body, n_a=len(a_specs), n_b=len(b_specs), nb=nb, tb=tb, L=L),
        out_shape=[jax.ShapeDtypeStruct((b, t, a_dim), BF16), jax.ShapeDtypeStruct((b, 1, pw), F32),
                   jax.ShapeDtypeStruct((b, n_pairs, LANES, LANES), F32),
                   jax.ShapeDtypeStruct((b, t, inner), BF16), jax.ShapeDtypeStruct((b, H_B, DK_B, DK_B), F32)],
        grid=(b // nb, t // tb),
        in_specs=a_specs + b_specs,
        out_specs=[blk_t(a_dim), per_b((1, pw)), sa_spec, blk_t(inner), sb_spec],
        scratch_shapes=[pltpu.VMEM((nb * tb, a_dim), F32)],
        compiler_params=_cparams(("parallel", "arbitrary")),
        name="rwkv7_delta",
    )(pa, shift0, _pair_blockdiag(s0.astype(F32)), *params, q, k, v, bab, z, d0.astype(F32), *params_d)
    return y_a, shift, _pair_unblock(s_new), y_b, d_new


def _even_in_body(x_ref, xh_ref, conv0_ref, nw_ref, wa_ref, wqkv_ref, wbab_ref, wz_ref, cw_ref,
                  pa_ref, q_ref, k_ref, v_ref, bab_ref, z_ref, tail_ref, cat_scr, *, tm):
    j = pl.program_id(1)
    hd = DK_B
    inner = H_B * hd
    nw = nw_ref[...]
    halo = jnp.dot(_rms(xh_ref[...], nw).astype(BF16), wqkv_ref[...], preferred_element_type=F32)
    cat_scr[0:CONV_CARRY, :] = jnp.where(j == 0, conv0_ref[...], halo)
    h = _rms(x_ref[...], nw).astype(BF16)
    cat_scr[CONV_CARRY:, :] = jnp.dot(h, wqkv_ref[...], preferred_element_type=F32)
    tail_ref[...] = cat_scr[tm:, :]
    cw = cw_ref[...]

    rows = min(tm, CONV_ROWS)

    def conv_piece(r0, c0):
        cols = slice(c0, c0 + hd)
        cat = cat_scr[r0:r0 + rows + CONV_CARRY, cols]
        acc = cat * cw[0:1, cols]
        for t in range(1, CONV_W):
            acc = pltpu.roll(acc, 1, 0) + cat * cw[t:t + 1, cols]
        return _silu(acc[CONV_CARRY:, :])

    def l2n(z):
        return z * lax.rsqrt(jnp.sum(z * z, axis=-1, keepdims=True) + EPS)

    pa_ref[...] = jnp.dot(h, wa_ref[...], preferred_element_type=F32)
    z_ref[...] = jnp.dot(h, wz_ref[...], preferred_element_type=F32)
    bab_ref[...] = jnp.dot(h, wbab_ref[...], preferred_element_type=F32)
    for r0 in range(0, tm, rows):
        rs = slice(r0, r0 + rows)
        for i in range(H_B):
            sl = slice(i * hd, (i + 1) * hd)
            q_ref[rs, sl] = (l2n(conv_piece(r0, i * hd)) * hd ** -0.5).astype(BF16)
            k_ref[rs, sl] = l2n(conv_piece(r0, inner + i * hd)).astype(BF16)
            v_ref[rs, sl] = conv_piece(r0, 2 * inner + i * hd).astype(BF16)


def _even_in_proj(x, nw, ws, conv0, conv_w):
    b, t, d = x.shape
    wa, wqkv, wbab, wz = ws
    cdim = wqkv.shape[1]
    inner = H_B * DK_B
    tm = _row_tile(t, 1024)
    conv0p = jnp.pad(conv0.astype(F32), ((0, 0), (CONV_CARRY - (CONV_W - 1), 0), (0, 0)))
    row = lambda w: pl.BlockSpec((None, tm, w), lambda i, j: (i, j, 0))
    halo = pl.BlockSpec((None, CONV_CARRY, d), lambda i, j: (i, jnp.maximum(j * (tm // CONV_CARRY) - 1, 0), 0))
    per_b = pl.BlockSpec((None, CONV_CARRY, cdim), lambda i, j: (i, 0, 0))
    outs = pl.pallas_call(
        functools.partial(_even_in_body, tm=tm),
        out_shape=[jax.ShapeDtypeStruct((b, t, wa.shape[1]), F32)]
        + [jax.ShapeDtypeStruct((b, t, inner), BF16)] * 3
        + [jax.ShapeDtypeStruct((b, t, wbab.shape[1]), F32), jax.ShapeDtypeStruct((b, t, wz.shape[1]), F32),
           jax.ShapeDtypeStruct((b, CONV_CARRY, cdim), F32)],
        grid=(b, t // tm),
        in_specs=[row(d), halo, per_b, _resident((1, d)), _resident(wa.shape), _resident(wqkv.shape),
                  _resident(wbab.shape), _resident(wz.shape), _resident(conv_w.shape)],
        out_specs=[row(wa.shape[1]), row(inner), row(inner), row(inner), row(wbab.shape[1]), row(wz.shape[1]), per_b],
        scratch_shapes=[pltpu.VMEM((tm + CONV_CARRY, cdim), F32)],
        compiler_params=_cparams(("parallel", "arbitrary")),
        name="even_in_proj",
    )(x, x, conv0p, nw.reshape(1, d), wa, wqkv, wbab, wz, conv_w.astype(F32))
    return outs[:6], outs[6][:, CONV_CARRY - (CONV_W - 1):]


def _delta_step(q_ref, k_ref, v_ref, bab_ref, z_ref, alog_ref, dtb_ref, nw_ref, y_ref, s_ref, *, nb, tb, L):
    hd = DK_B
    rows = nb * tb
    per_seg = tb // L
    flat = lambda ref: jnp.concatenate([ref[b] for b in range(nb)], axis=0)
    bab = flat(bab_ref)
    q_all, k_all, v_all = flat(q_ref), flat(k_ref), flat(v_ref)
    g_all = -jnp.exp(alog_ref[...]) * _softplus(bab + dtb_ref[...])
    beta_all = _sigmoid(bab)

    tri = _tri_incl(L).astype(BF16)
    rw = _iota((L, L), 0)
    cl_ = _iota((L, L), 1)
    strict = cl_ < rw
    incl = cl_ <= rw
    eye = (rw == cl_).astype(F32)
    nw = nw_ref[...]
    z = flat(z_ref)
    n_chunks = rows // L

    g_cum_all = _chunk_cumsum(g_all, L)
    g_tot_all = _chunk_last(g_cum_all, L)

    qh, kh, vh, kdec, qg, bv_, bk_, gb, beta_b, e_last = {}, {}, {}, {}, {}, {}, {}, {}, {}, {}
    for h in range(H_B):
        qh[h] = q_all[:, h * hd:(h + 1) * hd]
        kh[h] = k_all[:, h * hd:(h + 1) * hd]
        qs, ks = qh[h].astype(F32), kh[h].astype(F32)
        vs = v_all[:, h * hd:(h + 1) * hd].astype(F32)
        gb[h] = jnp.broadcast_to(g_all[:, h:h + 1], (rows, LANES))
        g_cum = jnp.broadcast_to(g_cum_all[:, h:h + 1], (rows, LANES))
        g_tot = jnp.broadcast_to(g_tot_all[:, h:h + 1], (rows, LANES))
        bb = jnp.broadcast_to(beta_all[:, H_B + h:H_B + h + 1], (rows, LANES))
        e_g = jnp.exp(g_cum)
        kdec[h] = (ks * jnp.exp(g_tot - g_cum)).astype(BF16)
        qg[h] = qs * e_g
        bv_[h] = bb * vs
        bk_[h] = bb * e_g * ks
        beta_b[h] = bb
        e_last[h] = jnp.exp(g_tot)

    items = [(c, h) for c in range(n_chunks) for h in range(H_B)]
    rs = {c: slice(c * L, (c + 1) * L) for c in range(n_chunks)}
    diff = {}
    for c, h in items:
        hi, mid, lo = _split3(jnp.where(strict, gb[h][rs[c], :L], 0.0))
        diff[(c, h)] = (jnp.dot(tri, hi, preferred_element_type=F32) + jnp.dot(tri, mid, preferred_element_type=F32)
                        + jnp.dot(tri, lo, preferred_element_type=F32))
    kk_ = {(c, h): _mm_nt(kh[h][rs[c]], kh[h][rs[c]]) for c, h in items}
    qk_ = {(c, h): _mm_nt(qh[h][rs[c]], kh[h][rs[c]]) for c, h in items}
    n_mat, qk_m = {}, {}
    for c, h in items:
        e_diff = jnp.exp(jnp.where(incl, diff[(c, h)], 0.0))
        n_mat[(c, h)] = -(beta_b[h][rs[c], :L] * jnp.where(strict, kk_[(c, h)] * e_diff, 0.0))
        qk_m[(c, h)] = jnp.where(incl, qk_[(c, h)] * e_diff, 0.0).astype(BF16)
    t_inv = {it: eye + n_mat[it] for it in items}
    m_pow = n_mat
    power = 2
    while power < L:
        m_pow = {it: _mm(m_pow[it], m_pow[it]) for it in items}
        t_inv = {it: t_inv[it] + _mm(t_inv[it], m_pow[it]) for it in items}
        power *= 2
    sol = {(c, h): _mm(t_inv[(c, h)], jnp.concatenate([bv_[h][rs[c]], bk_[h][rs[c]]], axis=1)) for c, h in items}
    lhs = {(c, h): jnp.concatenate([sol[(c, h)][:, hd:], qg[h][rs[c]]], axis=0).astype(BF16) for c, h in items}
    chains = [(b, h) for b in range(nb) for h in range(H_B)]
    state = {bh: s_ref[bh[0], bh[1]] for bh in chains}
    u_, o_inter = {}, {}
    for k_pos in range(per_seg):
        for b, h in chains:
            c = b * per_seg + k_pos
            ws = _mm(lhs[(c, h)], state[(b, h)])
            u_[(c, h)] = (sol[(c, h)][:, :hd] - ws[:L]).astype(BF16)
            o_inter[(c, h)] = ws[L:]
        for b, h in chains:
            c = b * per_seg + k_pos
            state[(b, h)] = e_last[h][c * L:c * L + 1, :] * state[(b, h)] + _mm_tn(kdec[h][rs[c]], u_[(c, h)])
    for b, h in chains:
        s_ref[b, h] = state[(b, h)]
    for c, h in items:
        o = o_inter[(c, h)] + _mm(qk_m[(c, h)], u_[(c, h)])
        on = o * lax.rsqrt(jnp.mean(o * o, axis=-1, keepdims=True) + EPS) * nw
        k_pos = c % per_seg
        y_ref[c // per_seg, k_pos * L:(k_pos + 1) * L, h * hd:(h + 1) * hd] = (
            on * _silu(z[rs[c], h * hd:(h + 1) * hd])).astype(y_ref.dtype)


def _gla_step(qk_ref, v_ref, ad_ref, g_ref, a2_ref, ab_ref, nw_ref, y_ref, s_ref, *, tb, L):
    kdim = H_C * DK_C
    n_pairs = kdim // LANES
    qk = qk_ref[...]
    log_a = -_softplus(-(_mm(ad_ref[...], a2_ref[...]) + ab_ref[...])) * (1.0 / GLA_NORMALIZER)
    vv = v_ref[...]
    gg = g_ref[...]
    nw = nw_ref[...]
    incl = _iota((L, L), 1) <= _iota((L, L), 0)
    m0 = _iota((tb, LANES), 1) < DK_C
    n_chunks = tb // L

    cum = _chunk_cumsum(log_a, L)
    tot = _chunk_last(cum, L)
    q_in = qk[:, :kdim] * DK_C ** -0.5 * jnp.exp(cum)
    k_in = (qk[:, kdim:] * jnp.exp(-cum)).astype(BF16)
    k_out = qk[:, kdim:] * jnp.exp(tot - cum)
    dec = jnp.exp(tot)
    vb16 = vv.astype(BF16)

    rs = {c: slice(c * L, (c + 1) * L) for c in range(n_chunks)}
    items = [(c, h) for c in range(n_chunks) for h in range(H_C)]
    qm, km = {}, {}
    for h in range(H_C):
        ls = slice((h // 2) * LANES, (h // 2 + 1) * LANES)
        mask = m0 if h % 2 == 0 else jnp.logical_not(m0)
        qm[h] = jnp.where(mask, q_in[:, ls], 0.0).astype(BF16)
        km[h] = jnp.where(mask, k_out[:, ls], 0.0).astype(BF16)
    scores = {(c, h): jnp.where(incl, _mm_nt(qm[h][rs[c]], k_in[rs[c], (h // 2) * LANES:(h // 2 + 1) * LANES]), 0.0)
              .astype(BF16) for c, h in items}
    upd = {(c, h): _mm_tn(vb16[rs[c], h * DV_C:(h + 1) * DV_C], km[h][rs[c]]) for c, h in items}
    state = [s_ref[p] for p in range(n_pairs)]
    s_prev = {}
    for c in range(n_chunks):
        for p in range(n_pairs):
            s_prev[(c, p)] = state[p].astype(BF16)
            state[p] = (state[p] * dec[c * L:c * L + 1, p * LANES:(p + 1) * LANES]
                        + upd[(c, 2 * p)] + upd[(c, 2 * p + 1)])
    for p in range(n_pairs):
        s_ref[p] = state[p]
    for c, h in items:
        hs = slice(h * DV_C, (h + 1) * DV_C)
        o = _mm_nt(qm[h][rs[c]], s_prev[(c, h // 2)]) + _mm(scores[(c, h)], vb16[rs[c], hs])
        on = o * lax.rsqrt(jnp.mean(o * o, axis=-1, keepdims=True) + EPS) * nw
        y_ref[rs[c], hs] = (on * _silu(gg[rs[c], hs])).astype(y_ref.dtype)


def _odd_mixers_body(qk_ref, cv_ref, ad_ref, cg_ref, gs0_ref, a2_ref, ab_ref, nw_ref,
                     dq_ref, dk_ref, dv_ref, dg_ref, cos_ref, sin_ref, rs0_ref,
                     yc_ref, gs_ref, yd_ref, rs_ref, *, tb, L):
    @pl.when(pl.program_id(1) == 0)
    def _():
        gs_ref[...] = gs0_ref[...]
        rs_ref[...] = rs0_ref[...]

    _gla_step(qk_ref, cv_ref, ad_ref, cg_ref, a2_ref, ab_ref, nw_ref, yc_ref, gs_ref, tb=tb, L=L)
    _ret_step(dq_ref, dk_ref, dv_ref, dg_ref, cos_ref, sin_ref, yd_ref, rs_ref, tb=tb, L=L)


def _odd_mixers(cqk, cv, cad, cg, gla0, wts, dq, dk, dv, dg, ret0, pos0, L):
    b, t, _ = cqk.shape
    kdim = H_C * DK_C
    vdim = H_C * DV_C
    rdim = H_D * DK_D
    n_pairs = kdim // LANES
    tb = min(t, 256)
    rank = wts['a2'].shape[0]
    a2p = jnp.pad(wts['a2'].astype(F32), ((0, LANES - rank), (0, 0))).astype(BF16)
    params = [a2p, wts['a_bias'].reshape(1, -1).astype(F32), wts['norm_w'].reshape(1, -1).astype(F32)]
    st0 = jnp.swapaxes(gla0.astype(F32), -1, -2).reshape(b, n_pairs, 2, DV_C, DK_C)
    st0 = jnp.concatenate([st0[:, :, 0], st0[:, :, 1]], axis=-1)
    inv = ROPE_BASE ** (-jnp.arange(0, DK_D, 2, dtype=F32) / DK_D)
    ang = (jnp.arange(t) + pos0).astype(F32)[:, None] * inv[None, :]
    cos = jnp.concatenate([jnp.cos(ang), jnp.cos(ang)], axis=-1)
    sin = jnp.concatenate([-jnp.sin(ang), jnp.sin(ang)], axis=-1)
    blk = lambda w: pl.BlockSpec((None, tb, w), lambda i, j: (i, j, 0))
    per_b = lambda shape: pl.BlockSpec((None,) + shape, lambda i, j: (i,) + (0,) * len(shape))
    tab = pl.BlockSpec((tb, DK_D), lambda i, j: (j, 0))
    gst, rst = per_b((n_pairs, DV_C, LANES)), per_b((H_D, DK_D, DK_D))
    y_c, st, y_d, ret_new = pl.pallas_call(
        functools.partial(_odd_mixers_body, tb=tb, L=L),
        out_shape=[jax.ShapeDtypeStruct((b, t, vdim), BF16), jax.ShapeDtypeStruct((b, n_pairs, DV_C, LANES), F32),
                   jax.ShapeDtypeStruct((b, t, rdim), BF16), jax.ShapeDtypeStruct((b, H_D, DK_D, DK_D), F32)],
        grid=(b, t // tb),
        in_specs=[blk(2 * kdim), blk(vdim), blk(LANES), blk(vdim), gst] + [_resident(p.shape) for p in params]
        + [blk(rdim), blk(rdim), blk(rdim), blk(rdim), tab, tab, rst],
        out_specs=[blk(vdim), gst, blk(rdim), rst],
        compiler_params=_cparams(("parallel", "arbitrary")),
        name="gla_retention",
    )(cqk, cv, cad, cg, st0, *params, dq, dk, dv, dg, cos, sin, ret0.astype(F32))
    st = jnp.stack([st[..., :DK_C], st[..., DK_C:]], axis=2).reshape(b, H_C, DV_C, DK_C)
    return y_c, jnp.swapaxes(st, -1, -2), y_d, ret_new


def _ret_step(q_ref, k_ref, v_ref, g_ref, cos_ref, sin_ref, y_ref, s_ref, *, tb, L):
    hd = DK_D
    cos = cos_ref[...]
    sin = sin_ref[...]
    qq, kk, vv, gg = q_ref[...], k_ref[...], v_ref[...], g_ref[...]
    rw = _iota((L, L), 0)
    cl_ = _iota((L, L), 1)
    incl = cl_ <= rw
    dist = jnp.where(incl, rw - cl_, 0).astype(F32)
    pos = _iota((L, LANES), 0).astype(F32)

    n_chunks = tb // L
    rs = {c: slice(c * L, (c + 1) * L) for c in range(n_chunks)}
    items = [(c, h) for c in range(n_chunks) for h in range(H_D)]
    lg = [math.log(1.0 - 2.0 ** (-5.0 - h)) for h in range(H_D)]
    vb16 = vv.astype(BF16)
    qh, kh, q_in, k_out, dmat = {}, {}, {}, {}, {}
    for h in range(H_D):
        hs = slice(h * hd, (h + 1) * hd)
        q_rot = qq[:, hs] * cos + pltpu.roll(qq[:, hs], hd // 2, 1) * sin
        k_rot = (kk[:, hs] * cos + pltpu.roll(kk[:, hs], hd // 2, 1) * sin) * hd ** -0.5
        dmat[h] = jnp.where(incl, jnp.exp(dist * lg[h]), 0.0)
        e_in = jnp.exp((pos + 1.0) * lg[h])
        e_out = jnp.exp((L - 1.0 - pos) * lg[h])
        qh[h], kh[h] = q_rot.astype(BF16), k_rot.astype(BF16)
        for c in range(n_chunks):
            q_in[(c, h)] = (q_rot[rs[c]] * e_in).astype(BF16)
            k_out[(c, h)] = (k_rot[rs[c]] * e_out).astype(BF16)
    scores = {(c, h): (_mm_nt(qh[h][rs[c]], kh[h][rs[c]]) * dmat[h]).astype(BF16) for c, h in items}
    upd = {(c, h): _mm_tn(k_out[(c, h)], vb16[rs[c], h * hd:(h + 1) * hd]) for c, h in items}
    state = [s_ref[h] for h in range(H_D)]
    s_prev = {}
    for c in range(n_chunks):
        for h in range(H_D):
            s_prev[(c, h)] = state[h].astype(BF16)
            state[h] = state[h] * math.exp(lg[h] * L) + upd[(c, h)]
    for h in range(H_D):
        s_ref[h] = state[h]
    for c, h in items:
        hs = slice(h * hd, (h + 1) * hd)
        o = _mm(q_in[(c, h)], s_prev[(c, h)]) + _mm(scores[(c, h)], vb16[rs[c], hs])
        mu = jnp.mean(o, axis=-1, keepdims=True)
        oc = o - mu
        on = oc * lax.rsqrt(jnp.mean(oc * oc, axis=-1, keepdims=True) + EPS)
        y_ref[rs[c], hs] = (on * _silu(gg[rs[c], hs])).astype(y_ref.dtype)


def _prep_weights(W):
    bf = lambda x: x.astype(BF16)
    a_proj = 3 * H_A * N_A + 64 + 64 + 128
    b_conv = 3 * H_B * DK_B
    P = {}
    for name in ('ffn1_wg', 'ffn1_wu', 'ffn1_wd', 'ffn2_wg', 'ffn2_wu', 'ffn2_wd', 'mem_wq', 'mem_wo', 'mem_wk',
                 'mem_wv', 'even_w_out', 'odd_w_out'):
        P[name] = bf(W[name])
    ew = W['even_w_in']
    o = a_proj
    bab = ew[:, :, o + b_conv:o + b_conv + 2 * H_B]
    P['even_in'] = [bf(ew[:, :, :a_proj]), bf(ew[:, :, o:o + b_conv]),
                    bf(jnp.pad(bab, ((0, 0), (0, 0), (0, LANES - 2 * H_B)))), bf(ew[:, :, o + b_conv + 2 * H_B:])]
    ow = W['odd_w_in']
    kd, vd, rank = H_C * DK_C, H_C * DV_C, W['gla_a2'].shape[1]
    c_proj = 2 * kd + 2 * vd + rank
    dd = H_D * DK_D
    cad = ow[:, :, 2 * kd + vd:2 * kd + vd + rank]
    P['odd_in'] = [bf(ow[:, :, :2 * kd]), bf(ow[:, :, 2 * kd:2 * kd + vd]),
                   bf(jnp.pad(cad, ((0, 0), (0, 0), (0, LANES - rank)))), bf(ow[:, :, 2 * kd + vd + rank:c_proj])] \
        + [bf(ow[:, :, c_proj + j * dd:c_proj + (j + 1) * dd]) for j in range(4)]
    return P


def _trunk(x, pos0, mem_k, mem_v, heads, shift, rwkv, conv, delta, gla, ret, W, P):
    b, t, d = x.shape
    n = b * t
    depth = W['norm_ffn1'].shape[0]
    L_delta = min(64, t)
    L_lin = min(64, t)
    flat = lambda z: z.reshape(n, z.shape[-1])
    unflat = lambda z: z.reshape(b, t, z.shape[-1])
    new = {k: [] for k in ('shift', 'rwkv', 'conv', 'delta', 'gla', 'ret')}
    x = flat(x)
    for l in range(depth):
        i = l // 2
        x = _ffn(x, W['norm_ffn1'][l], P['ffn1_wg'][l], P['ffn1_wu'][l], P['ffn1_wd'][l])
        if l % 2 == 0:
            (pa, dq_, dk_, dv_, bab, z), s3 = _even_in_proj(unflat(x), W['norm_mix'][l], [w[i] for w in P['even_in']],
                                                           conv[i], W['delta_conv_w'][i])
            wa = dict(mu=W['rwkv_mu'][i], w0=W['rwkv_w0'][i], w2=W['rwkv_w2'][i], a0=W['rwkv_a0'][i],
                      a2=W['rwkv_a2'][i], g2=W['rwkv_g2'][i], kk=W['rwkv_kk'][i], ka=W['rwkv_ka'][i],
                      rk=W['rwkv_rk'][i], ln_w=W['rwkv_ln_w'][i], ln_b=W['rwkv_ln_b'][i])
            wb = dict(A_log=W['delta_A_log'][i], dt_bias=W['delta_dt_bias'][i], norm_w=W['delta_norm_w'][i])
            y_a, s1, s2, y_b, s4 = _even_mixers(pa, shift[i], rwkv[i], wa, dq_, dk_, dv_, bab, z, delta[i], wb, L_delta)
            new['shift'].append(s1)
            new['rwkv'].append(s2)
            new['conv'].append(s3)
            new['delta'].append(s4)
            split = H_A * N_A
            halves, w_out = (y_a, y_b), P['even_w_out'][i]
        else:
            cqk, cv, cad, cg, dq, dk, dv, dg = _norm_proj(x, W['norm_mix'][l], [w[i] for w in P['odd_in']],
                                                          bf16_out=(1, 6))
            wc = dict(a2=W['gla_a2'][i], a_bias=W['gla_a_bias'][i], norm_w=W['gla_norm_w'][i])
            y_c, s5, y_d, s6 = _odd_mixers(unflat(cqk), unflat(cv), unflat(cad), unflat(cg), gla[i], wc,
                                           unflat(dq), unflat(dk), unflat(dv), unflat(dg), ret[i], pos0, L_lin)
            new['gla'].append(s5)
            new['ret'].append(s6)
            split = H_C * DV_C
            halves, w_out = (y_c, y_d), P['odd_w_out'][i]
        mk = mem_k[l].astype(BF16)
        mv = mem_v[l].astype(BF16)
        x = flat(_mix_out_mem_attn(unflat(x), halves[0], halves[1], w_out[:split], w_out[split:], W['norm_mem'][l],
                                   P['mem_wq'][l], mk, mv, P['mem_wo'][l], heads))
        fw = W['final_norm'] if l == depth - 1 else None
        x = _ffn(x, W['norm_ffn2'][l], P['ffn2_wg'][l], P['ffn2_wu'][l], P['ffn2_wd'][l], fw)
    return unflat(x), new


def kernel(x_prompt, x_sample, mem_prompt, state_rwkv_shift, state_rwkv, state_delta_conv, state_delta, state_gla, state_ret, cache_mem_k, cache_mem_v, norm_ffn1, ffn1_wg, ffn1_wu, ffn1_wd, norm_mix, even_w_in, even_w_out, rwkv_mu, rwkv_w0, rwkv_w2, rwkv_a0, rwkv_a2, rwkv_g2, rwkv_kk, rwkv_ka, rwkv_rk, rwkv_ln_w, rwkv_ln_b, delta_conv_w, delta_A_log, delta_dt_bias, delta_norm_w, odd_w_in, odd_w_out, gla_a2, gla_a_bias, gla_norm_w, norm_mem, mem_norm_kv, mem_wq, mem_wk, mem_wv, mem_wo, norm_ffn2, ffn2_wg, ffn2_wu, ffn2_wd, final_norm):
    W = dict(norm_ffn1=norm_ffn1, ffn1_wg=ffn1_wg, ffn1_wu=ffn1_wu, ffn1_wd=ffn1_wd, norm_mix=norm_mix,
             even_w_in=even_w_in, even_w_out=even_w_out, rwkv_mu=rwkv_mu, rwkv_w0=rwkv_w0, rwkv_w2=rwkv_w2,
             rwkv_a0=rwkv_a0, rwkv_a2=rwkv_a2, rwkv_g2=rwkv_g2, rwkv_kk=rwkv_kk, rwkv_ka=rwkv_ka,
             rwkv_rk=rwkv_rk, rwkv_ln_w=rwkv_ln_w, rwkv_ln_b=rwkv_ln_b, delta_conv_w=delta_conv_w,
             delta_A_log=delta_A_log, delta_dt_bias=delta_dt_bias, delta_norm_w=delta_norm_w,
             odd_w_in=odd_w_in, odd_w_out=odd_w_out, gla_a2=gla_a2, gla_a_bias=gla_a_bias,
             gla_norm_w=gla_norm_w, norm_mem=norm_mem, mem_wq=mem_wq, mem_wk=mem_wk, mem_wv=mem_wv,
             mem_wo=mem_wo, norm_ffn2=norm_ffn2, ffn2_wg=ffn2_wg, ffn2_wu=ffn2_wu, ffn2_wd=ffn2_wd,
             final_norm=final_norm)
    P = _prep_weights(W)
    dt = x_prompt.dtype
    bp, _, d = x_prompt.shape
    depth = norm_ffn1.shape[0]
    n_even, n_odd = (depth + 1) // 2, depth // 2
    heads, hd = cache_mem_k.shape[3], cache_mem_k.shape[4]
    n_mem = mem_prompt.shape[1]
    pk, pv, pk_flat, pv_flat = [], [], [], []
    mem_flat = mem_prompt.reshape(bp * n_mem, d)
    for l in range(depth):
        mk, mv = _norm_proj(mem_flat, mem_norm_kv[l], [P['mem_wk'][l], P['mem_wv'][l]])
        pk_flat.append(mk.reshape(bp, n_mem, d))
        pv_flat.append(mv.reshape(bp, n_mem, d))
        pk.append(mk.reshape(bp, n_mem, heads, hd))
        pv.append(mv.reshape(bp, n_mem, heads, hd))
    zeros = lambda ref, cnt: [jnp.zeros((bp,) + ref.shape[2:], F32)] * cnt
    y_prompt, ps = _trunk(x_prompt, 0, pk_flat, pv_flat, heads, zeros(state_rwkv_shift, n_even), zeros(state_rwkv, n_even),
                          zeros(state_delta_conv, n_even), zeros(state_delta, n_even), zeros(state_gla, n_odd),
                          zeros(state_ret, n_odd), W, P)
    bs = x_sample.shape[0]
    y_sample, ss = _trunk(x_sample, PAST_LEN, [cache_mem_k[l].reshape(bs, n_mem, d) for l in range(depth)],
                          [cache_mem_v[l].reshape(bs, n_mem, d) for l in range(depth)], heads,
                          [state_rwkv_shift[i] for i in range(n_even)], [state_rwkv[i] for i in range(n_even)],
                          [state_delta_conv[i] for i in range(n_even)], [state_delta[i] for i in range(n_even)],
                          [state_gla[i] for i in range(n_odd)], [state_ret[i] for i in range(n_odd)], W, P)
    st = lambda xs: jnp.stack(xs).astype(dt)
    order = ('shift', 'rwkv', 'conv', 'delta', 'gla', 'ret')
    return ((y_prompt, y_sample) + tuple(st(ps[k]) for k in order) + (st(pk), st(pv))
            + tuple(st(ss[k]) for k in order))
```

```python
import functools
import math

import jax
import jax.numpy as jnp
from jax import lax
from jax.experimental import pallas as pl
from jax.experimental.pallas import tpu as pltpu

F32 = jnp.float32
BF16 = jnp.bfloat16
EPS = 1e-6
RWKV_LN_EPS = 64e-5
GLA_NORMALIZER = 16.0
ROPE_BASE = 10000.0
PAST_LEN = 1024

LANES = 128
MXU_DIM = 256
VMEM_LIMIT = 56 * 1024 * 1024

H_A, N_A = 8, 64
H_B, DK_B = 4, 128
H_C, DK_C, DV_C = 4, 64, 128
H_D, DK_D = 4, 128
CONV_W = 4
CONV_CARRY = 8
MIXER_ROWS = 512
SHORT_SEQ_ROWS = 256
CONV_ROWS = 128


def _cparams(sem):
    return pltpu.CompilerParams(dimension_semantics=sem, vmem_limit_bytes=VMEM_LIMIT)


def _mm(a, b):
    return jnp.dot(a.astype(BF16), b.astype(BF16), preferred_element_type=F32)


def _mm_nt(a, b):
    return lax.dot_general(a.astype(BF16), b.astype(BF16), (((1,), (1,)), ((), ())),
                           preferred_element_type=F32)


def _mm_tn(a, b):
    return lax.dot_general(a.astype(BF16), b.astype(BF16), (((0,), (0,)), ((), ())),
                           preferred_element_type=F32)


def _split3(x):
    hi = x.astype(BF16)
    r = x - hi.astype(F32)
    mid = r.astype(BF16)
    lo = (r - mid.astype(F32)).astype(BF16)
    return hi, mid, lo


def _mm_exact_lhs01(a01, x):
    hi, mid, lo = _split3(x)
    a = a01.astype(BF16)
    return (jnp.dot(a, hi, preferred_element_type=F32) + jnp.dot(a, mid, preferred_element_type=F32)
            + jnp.dot(a, lo, preferred_element_type=F32))


def _iota(shape, dim):
    return lax.broadcasted_iota(jnp.int32, shape, dim)


def _tri_incl(n):
    return (_iota((n, n), 1) <= _iota((n, n), 0)).astype(F32)


def _chunk_tri(tb, L):
    shift = L.bit_length() - 1
    assert 1 << shift == L
    rt, ct = _iota((tb, tb), 0), _iota((tb, tb), 1)
    same = jnp.right_shift(rt, shift) == jnp.right_shift(ct, shift)
    return jnp.logical_and(same, ct <= rt).astype(F32)


def _chunk_cumsum(x, L):
    n = x.shape[0]
    g = min(n, MXU_DIM)
    tri = _chunk_tri(g, L)
    return jnp.concatenate([_mm_exact_lhs01(tri, x[r:r + g]) for r in range(0, n, g)], axis=0)


def _chunk_last(cum, L):
    tb, w = cum.shape
    return jnp.concatenate([jnp.broadcast_to(cum[c * L + L - 1:c * L + L, :], (L, w)) for c in range(tb // L)], axis=0)


def _sigmoid(x):
    return 0.5 * (jnp.tanh(0.5 * x) + 1.0)


def _silu(x):
    return x * _sigmoid(x)


def _softplus(x):
    return jnp.maximum(x, 0.0) + jnp.log(1.0 + jnp.exp(-jnp.abs(x)))


def _rms(x, w):
    return x * lax.rsqrt(jnp.mean(x * x, axis=-1, keepdims=True) + EPS) * w


def _row_tile(n, target):
    t = min(n, target)
    while n % t:
        t //= 2
    return t


def _mixer_block(b, t, L):
    tb = min(t, 2 * L)
    nb = max(1, min(b, MIXER_ROWS // tb))
    while b % nb:
        nb -= 1
    return nb, tb


def _resident(shape):
    nd = len(shape)
    return pl.BlockSpec(shape, lambda *_: (0,) * nd, pipeline_mode=pl.Buffered(1))


def _ffn_body(x_ref, nw_ref, wg_ref, wu_ref, wd_ref, fw_ref, o_ref, *, chunk, final):
    x = x_ref[...]
    h = _rms(x, nw_ref[...]).astype(BF16)
    d_ff = wg_ref.shape[1]
    n_chunks = d_ff // chunk

    def gate_up(c):
        sl = slice(c * chunk, (c + 1) * chunk)
        return (jnp.dot(h, wg_ref[:, sl], preferred_element_type=F32),
                jnp.dot(h, wu_ref[:, sl], preferred_element_type=F32))

    acc = None
    nxt = gate_up(0)
    for c in range(n_chunks):
        g, u = nxt
        if c + 1 < n_chunks:
            nxt = gate_up(c + 1)
        a = (_silu(g) * u).astype(BF16)
        d = jnp.dot(a, wd_ref[c * chunk:(c + 1) * chunk, :], preferred_element_type=F32)
        acc = d if acc is None else acc + d
    y = x + 0.5 * acc
    if final:
        y = _rms(y, fw_ref[...])
    o_ref[...] = y


def _ffn(x, nw, wg, wu, wd, fw=None):
    n, d = x.shape
    d_ff = wg.shape[1]
    tm = _row_tile(n, 1024)
    final = fw is not None
    if fw is None:
        fw = nw
    return pl.pallas_call(
        functools.partial(_ffn_body, chunk=MXU_DIM, final=final),
        out_shape=jax.ShapeDtypeStruct((n, d), F32),
        grid=(n // tm,),
        in_specs=[pl.BlockSpec((tm, d), lambda i: (i, 0)), _resident((1, d)), _resident((d, d_ff)),
                  _resident((d, d_ff)), _resident((d_ff, d)), _resident((1, d))],
        out_specs=pl.BlockSpec((tm, d), lambda i: (i, 0)),
        compiler_params=_cparams(("parallel",)),
        name="ffn",
    )(x, nw.reshape(1, d), wg, wu, wd, fw.reshape(1, d))


def _norm_proj_body(*refs, n_w):
    x_ref, nw_ref = refs[0], refs[1]
    w_refs = refs[2:2 + n_w]
    o_refs = refs[2 + n_w:]
    h = _rms(x_ref[...], nw_ref[...]).astype(BF16)
    for w_ref, o_ref in zip(w_refs, o_refs):
        o_ref[...] = jnp.dot(h, w_ref[...], preferred_element_type=F32).astype(o_ref.dtype)


def _norm_proj(x, nw, ws, bf16_out=()):
    n, d = x.shape
    tm = _row_tile(n, 1024)
    return pl.pallas_call(
        functools.partial(_norm_proj_body, n_w=len(ws)),
        out_shape=[jax.ShapeDtypeStruct((n, w.shape[1]), BF16 if i in bf16_out else F32) for i, w in enumerate(ws)],
        grid=(n // tm,),
        in_specs=[pl.BlockSpec((tm, d), lambda i: (i, 0)), _resident((1, d))] + [_resident(w.shape) for w in ws],
        out_specs=[pl.BlockSpec((tm, w.shape[1]), lambda i: (i, 0)) for w in ws],
        compiler_params=_cparams(("parallel",)),
        name="norm_proj",
    )(x, nw.reshape(1, d), *ws)


def _mix_out_mem_attn_body(x_ref, ya_ref, yb_ref, wa_ref, wb_ref, nw_ref, wq_ref, k_ref, v_ref, wo_ref, o_ref, *,
                           heads):
    nb, tm, d = x_ref.shape
    flat = lambda ref: jnp.concatenate([ref[b] for b in range(nb)], axis=0)
    x = (flat(x_ref) + jnp.dot(flat(ya_ref), wa_ref[...], preferred_element_type=F32)
         + jnp.dot(flat(yb_ref), wb_ref[...], preferred_element_type=F32))
    h = _rms(x, nw_ref[...]).astype(BF16)
    q = jnp.dot(h, wq_ref[...], preferred_element_type=F32).astype(BF16)
    hd = d // heads

    def scores(job):
        b, i = job
        sl = slice(i * hd, (i + 1) * hd)
        return _mm_nt(q[b * tm:(b + 1) * tm, sl], k_ref[b, :, sl]) * hd ** -0.5

    jobs = [(b, i) for b in range(nb) for i in range(heads)]
    outs = [[] for _ in range(nb)]
    nxt = scores(jobs[0])
    for n, (b, i) in enumerate(jobs):
        s = nxt
        if n + 1 < len(jobs):
            nxt = scores(jobs[n + 1])
        e = jnp.exp(s - jnp.max(s, axis=-1, keepdims=True))
        pr = e / jnp.sum(e, axis=-1, keepdims=True)
        outs[b].append(_mm(pr, v_ref[b, :, i * hd:(i + 1) * hd]))
    o = jnp.concatenate([jnp.concatenate(ob, axis=-1) for ob in outs], axis=0).astype(BF16)
    out = x + jnp.dot(o, wo_ref[...], preferred_element_type=F32)
    for b in range(nb):
        o_ref[b] = out[b * tm:(b + 1) * tm, :]


def _mix_out_mem_attn(x, ya, yb, wa, wb, nw, wq, mk, mv, wo, heads):
    b, t, d = x.shape
    m = mk.shape[1]
    tm = _row_tile(t, 1024)
    nb = max(1, min(b, SHORT_SEQ_ROWS // tm))
    while b % nb:
        nb -= 1
    row = lambda w: pl.BlockSpec((nb, tm, w), lambda i, j: (i, j, 0))
    mem = pl.BlockSpec((nb, m, d), lambda i, j: (i, 0, 0))
    return pl.pallas_call(
        functools.partial(_mix_out_mem_attn_body, heads=heads),
        out_shape=jax.ShapeDtypeStruct((b, t, d), F32),
        grid=(b // nb, t // tm),
        in_specs=[row(d), row(ya.shape[2]), row(yb.shape[2]), _resident(wa.shape), _resident(wb.shape),
                  _resident((1, d)), _resident((d, d)), mem, mem, _resident((d, d))],
        out_specs=row(d),
        compiler_params=_cparams(("parallel", "parallel")),
        name="mix_out_mem_attn",
    )(x, ya, yb, wa, wb, nw.reshape(1, d), wq, mk, mv, wo)


def _seg_sum(x, bd):
    g = bd.shape[0]
    xb = x.astype(BF16)
    return jnp.concatenate([jnp.dot(xb[:, j:j + g], bd, preferred_element_type=F32)
                            for j in range(0, x.shape[1], g)], axis=1)


def _rwkv_step(pa_ref, mu_ref, w0_ref, w2_ref, a0_ref, a2_ref, g2_ref, kkw_ref, ka_ref,
               rk_ref, lnw_ref, lnb_ref, bd_ref, y_ref, shift_ref, s_ref, y_scr, *, nb, tb, L):
    a_dim = H_A * N_A
    n_pairs = a_dim // LANES
    row = _iota((tb, pa_ref.shape[2]), 0)
    rows = nb * tb
    bd = bd_ref[...]

    def prologue(b):
        pa = pa_ref[b]
        prev = jnp.where(row == 0, shift_ref[b], pltpu.roll(pa, 1, 0))
        shift_ref[b] = pa[tb - 1:tb, :]
        xa = pa + (prev - pa) * mu_ref[...]
        r = xa[:, 0:a_dim]
        k = xa[:, a_dim:2 * a_dim]
        v = xa[:, 2 * a_dim:3 * a_dim]
        xwa = xa[:, 3 * a_dim:3 * a_dim + LANES]
        xg = xa[:, 3 * a_dim + LANES:]
        wlog = -_softplus(-(w0_ref[...] + _mm(jnp.tanh(xwa), w2_ref[...]))) - 0.5
        logdec = -jnp.exp(wlog)
        a = _sigmoid(a0_ref[...] + _mm(xwa, a2_ref[...]))
        gate = _mm(_sigmoid(xg), g2_ref[...])
        kq = k * kkw_ref[...]
        kk = kq * lax.rsqrt(_seg_sum(kq * kq, bd) + EPS)
        k2 = k * (1.0 + (a - 1.0) * ka_ref[...])
        av = -kk
        bv = kk * a
        cum = _chunk_cumsum(logdec, L)
        tot = _chunk_last(cum, L)
        e_neg = jnp.exp(-cum)
        e_out = jnp.exp(tot - cum)
        return dict(a_t=av * jnp.exp(cum - logdec), r_t=r * jnp.exp(cum), b_t=bv * e_neg, k_t=k2 * e_neg,
                    b_o=bv * e_out, k_o=k2 * e_out, g_l=jnp.exp(tot), v=v, gate=gate, rk=r * k2 * rk_ref[...])

    lane = _iota((L, LANES), 1)
    m0 = lane < N_A
    r2 = _iota((2 * L, 2 * L), 0)
    c2 = _iota((2 * L, 2 * L), 1)
    bdm = jnp.logical_not(jnp.logical_xor(r2 >= L, c2 >= L))
    r_loc = jnp.where(r2 >= L, r2 - L, r2)
    c_loc = jnp.where(c2 >= L, c2 - L, c2)
    strict_t = jnp.logical_and(bdm, c_loc > r_loc)
    strict_l = jnp.bitwise_and(_iota((L, 2 * L), 1), L - 1) < _iota((L, 2 * L), 0)
    incl_l = jnp.bitwise_and(_iota((L, 4 * L), 1), L - 1) <= _iota((L, 4 * L), 0)
    bdl = jnp.logical_not(jnp.logical_xor(_iota((2 * L, LANES), 0) >= L, _iota((2 * L, LANES), 1) >= N_A))
    eye2 = (r2 == c2).astype(F32)
    zero_b = jnp.zeros((L, LANES), BF16)

    def stack2(x):
        xb = x.astype(BF16)
        return jnp.concatenate([jnp.where(m0, xb, zero_b), jnp.where(m0, zero_b, xb)], axis=0)

    def dup2(x):
        xb = x.astype(BF16)
        return jnp.concatenate([xb, xb], axis=0)

    seg = [prologue(b) for b in range(nb)]

    per_seg = tb // L
    items = [(c, p) for c in range(rows // L) for p in range(n_pairs)]
    sl = {(c, p): (slice((c % per_seg) * L, (c % per_seg + 1) * L), slice(p * LANES, (p + 1) * LANES))
          for c, p in items}
    src = lambda name, it: seg[it[0] // per_seg][name][sl[it]]
    xa = {it: stack2(src('a_t', it)) for it in items}
    ra = {it: src('r_t', it).astype(BF16) for it in items}
    yb = {it: dup2(src('b_t', it)) for it in items}
    sbk = {it: jnp.concatenate([stack2(src('b_t', it)), stack2(src('k_t', it))], axis=0) for it in items}
    v2 = {it: jnp.where(bdl, dup2(src('v', it)), jnp.zeros((2 * L, LANES), BF16)) for it in items}
    wst = {it: jnp.concatenate([stack2(src('b_o', it)), stack2(src('k_o', it))], axis=0) for it in items}
    n_t = {it: jnp.where(strict_t, _mm_nt(yb[it], xa[it]), 0.0) for it in items}
    ak = {it: jnp.where(strict_l, _mm_nt(src('a_t', it), sbk[it][2 * L:]), 0.0).astype(BF16) for it in items}
    rbk = {it: jnp.where(incl_l, _mm_nt(ra[it], sbk[it]), 0.0).astype(BF16) for it in items}
    akv = {it: stack2(_mm(ak[it], v2[it])) for it in items}
    t_t = {it: eye2 + n_t[it] for it in items}
    n_b = {it: n_t[it].astype(BF16) for it in items}
    m_t = {it: _mm(n_b[it], n_b[it]).astype(BF16) for it in items}
    power = 4
    while power < L:
        prod = {it: _mm(m_t[it], jnp.concatenate([m_t[it], t_t[it].astype(BF16)], axis=1)) for it in items}
        m_t = {it: prod[it][:, :2 * L].astype(BF16) for it in items}
        t_t = {it: t_t[it] + prod[it][:, 2 * L:] for it in items}
        power *= 2
    if L > 2:
        t_t = {it: t_t[it] + _mm(m_t[it], t_t[it]) for it in items}
    om_up = {it: _mm_tn(t_t[it], jnp.concatenate([xa[it], akv[it]], axis=1)) for it in items}
    omega = {it: om_up[it][:, :LANES].astype(BF16) for it in items}
    chains = [(b, p) for b in range(nb) for p in range(n_pairs)]
    state = {bp: s_ref[bp[0], bp[1]] for bp in chains}
    s_prev, uv = {}, {}
    for k_pos in range(per_seg):
        for b, p in chains:
            it = (b * per_seg + k_pos, p)
            s_prev[it] = state[(b, p)].astype(BF16)
            u_st = _mm_nt(omega[it], s_prev[it]) + om_up[it][:, LANES:]
            uv[it] = jnp.concatenate([u_st.astype(BF16), v2[it]], axis=0)
        for b, p in chains:
            it = (b * per_seg + k_pos, p)
            state[(b, p)] = (state[(b, p)] * seg[b]['g_l'][k_pos * L:k_pos * L + 1, sl[it][1]]
                             + _mm_tn(uv[it], wst[it]))
    for b, p in chains:
        s_ref[b, p] = state[(b, p)]
    for c, p in items:
        y_scr[c * L:(c + 1) * L, sl[(c, p)][1]] = _mm_nt(ra[(c, p)], s_prev[(c, p)]) + _mm(rbk[(c, p)], uv[(c, p)])

    inv_n = 1.0 / N_A
    for b in range(nb):
        y = y_scr[b * tb:(b + 1) * tb, :]
        mu_y = _seg_sum(y, bd) * inv_n
        yc = y - mu_y
        var = _seg_sum(yc * yc, bd) * inv_n
        yn = yc * lax.rsqrt(var + RWKV_LN_EPS) * lnw_ref[...] + lnb_ref[...]
        bonus = _seg_sum(seg[b]['rk'], bd) * seg[b]['v']
        y_ref[b] = ((yn + bonus) * seg[b]['gate']).astype(y_ref.dtype)


def _pair_blockdiag(s):
    b, h, n, _ = s.shape
    s = s.reshape(b, h // 2, 2, n, n)
    z = jnp.zeros((b, h // 2, n, n), s.dtype)
    top = jnp.concatenate([s[:, :, 0], z], axis=-1)
    bot = jnp.concatenate([z, s[:, :, 1]], axis=-1)
    return jnp.concatenate([top, bot], axis=-2)


def _pair_unblock(s):
    b, p, n2, _ = s.shape
    n = n2 // 2
    return jnp.stack([s[:, :, :n, :n], s[:, :, n:, n:]], axis=2).reshape(b, 2 * p, n, n)


def _even_mixers_body(*refs, n_a, n_b, nb, tb, L):
    a_in, b_in = refs[:n_a], refs[n_a:n_a + n_b]
    ya_ref, shift_ref, sa_ref, yb_ref, sb_ref, y_scr = refs[n_a + n_b:]

    @pl.when(pl.program_id(1) == 0)
    def _():
        shift_ref[...] = a_in[1][...]
        sa_ref[...] = a_in[2][...]
        sb_ref[...] = b_in[5][...]

    _rwkv_step(a_in[0], *a_in[3:], ya_ref, shift_ref, sa_ref, y_scr, nb=nb, tb=tb, L=L)
    _delta_step(*b_in[:5], *b_in[6:], yb_ref, sb_ref, nb=nb, tb=tb, L=L)


def _even_mixers(pa, shift0, s0, wts, q, k, v, bab, z, d0, wts_d, L):
    b, t, pw = pa.shape
    a_dim = H_A * N_A
    inner = H_B * DK_B
    nb, tb = _mixer_block(b, t, L)
    n_pairs = a_dim // LANES
    seg = _iota((MXU_DIM, MXU_DIM), 0) // N_A == _iota((MXU_DIM, MXU_DIM), 1) // N_A
    bd = seg.astype(BF16)
    z64 = jnp.zeros((N_A, a_dim), F32)
    w2p = jnp.concatenate([wts['w2'], z64], axis=0)
    a2p = jnp.concatenate([z64, wts['a2']], axis=0)
    vec = lambda x: x.reshape(1, -1).astype(F32)
    params = [vec(wts['mu']), vec(wts['w0']), w2p.astype(BF16), vec(wts['a0']), a2p.astype(BF16),
              wts['g2'].astype(BF16), vec(wts['kk']), vec(wts['ka']), vec(wts['rk']), vec(wts['ln_w']),
              vec(wts['ln_b']), bd]
    pad = lambda x: jnp.pad(x.reshape(1, -1).astype(F32), ((0, 0), (0, LANES - x.size)))
    params_d = [pad(wts_d['A_log']), pad(wts_d['dt_bias']), wts_d['norm_w'].reshape(1, -1).astype(F32)]
    blk_t = lambda w: pl.BlockSpec((nb, tb, w), lambda i, j: (i, j, 0))
    per_b = lambda shape: pl.BlockSpec((nb,) + shape, lambda i, j: (i,) + (0,) * len(shape))
    sa_spec, sb_spec = per_b((n_pairs, LANES, LANES)), per_b((H_B, DK_B, DK_B))
    a_specs = [blk_t(pw), per_b((1, pw)), sa_spec] + [_resident(p.shape) for p in params]
    b_specs = [blk_t(inner)] * 3 + [blk_t(LANES), blk_t(inner), sb_spec] + [_resident(p.shape) for p in params_d]
    y_a, shift, s_new, y_b, d_new = pl.pallas_call(
        functools.partial(_even_mixers_body, n_a=len(a_specs), n_b=len(b_specs), nb=nb, tb=tb, L=L),
        out_shape=[jax.ShapeDtypeStruct((b, t, a_dim), BF16), jax.ShapeDtypeStruct((b, 1, pw), F32),
                   jax.ShapeDtypeStruct((b, n_pairs, LANES, LANES), F32),
                   jax.ShapeDtypeStruct((b, t, inner), BF16), jax.ShapeDtypeStruct((b, H_B, DK_B, DK_B), F32)],
        grid=(b // nb, t // tb),
        in_specs=a_specs + b_specs,
        out_specs=[blk_t(a_dim), per_b((1, pw)), sa_spec, blk_t(inner), sb_spec],
        scratch_shapes=[pltpu.VMEM((nb * tb, a_dim), F32)],
        compiler_params=_cparams(("parallel", "arbitrary")),
        name="rwkv7_delta",
    )(pa, shift0, _pair_blockdiag(s0.astype(F32)), *params, q, k, v, bab, z, d0.astype(F32), *params_d)
    return y_a, shift, _pair_unblock(s_new), y_b, d_new


def _even_in_body(x_ref, xh_ref, conv0_ref, nw_ref, wa_ref, wqkv_ref, wbab_ref, wz_ref, cw_ref,
                  pa_ref, q_ref, k_ref, v_ref, bab_ref, z_ref, tail_ref, cat_scr, *, tm):
    j = pl.program_id(1)
    hd = DK_B
    inner = H_B * hd
    nw = nw_ref[...]
    halo = jnp.dot(_rms(xh_ref[...], nw).astype(BF16), wqkv_ref[...], preferred_element_type=F32)
    cat_scr[0:CONV_CARRY, :] = jnp.where(j == 0, conv0_ref[...], halo)
    h = _rms(x_ref[...], nw).astype(BF16)
    cat_scr[CONV_CARRY:, :] = jnp.dot(h, wqkv_ref[...], preferred_element_type=F32)
    tail_ref[...] = cat_scr[tm:, :]
    cw = cw_ref[...]

    rows = min(tm, CONV_ROWS)

    def conv_piece(r0, c0):
        cols = slice(c0, c0 + hd)
        cat = cat_scr[r0:r0 + rows + CONV_CARRY, cols]
        acc = cat * cw[0:1, cols]
        for t in range(1, CONV_W):
            acc = pltpu.roll(acc, 1, 0) + cat * cw[t:t + 1, cols]
        return _silu(acc[CONV_CARRY:, :])

    def l2n(z):
        return z * lax.rsqrt(jnp.sum(z * z, axis=-1, keepdims=True) + EPS)

    pa_ref[...] = jnp.dot(h, wa_ref[...], preferred_element_type=F32)
    z_ref[...] = jnp.dot(h, wz_ref[...], preferred_element_type=F32)
    bab_ref[...] = jnp.dot(h, wbab_ref[...], preferred_element_type=F32)
    for r0 in range(0, tm, rows):
        rs = slice(r0, r0 + rows)
        for i in range(H_B):
            sl = slice(i * hd, (i + 1) * hd)
            q_ref[rs, sl] = (l2n(conv_piece(r0, i * hd)) * hd ** -0.5).astype(BF16)
            k_ref[rs, sl] = l2n(conv_piece(r0, inner + i * hd)).astype(BF16)
            v_ref[rs, sl] = conv_piece(r0, 2 * inner + i * hd).astype(BF16)


def _even_in_proj(x, nw, ws, conv0, conv_w):
    b, t, d = x.shape
    wa, wqkv, wbab, wz = ws
    cdim = wqkv.shape[1]
    inner = H_B * DK_B
    tm = _row_tile(t, 1024)
    conv0p = jnp.pad(conv0.astype(F32), ((0, 0), (CONV_CARRY - (CONV_W - 1), 0), (0, 0)))
    row = lambda w: pl.BlockSpec((None, tm, w), lambda i, j: (i, j, 0))
    halo = pl.BlockSpec((None, CONV_CARRY, d), lambda i, j: (i, jnp.maximum(j * (tm // CONV_CARRY) - 1, 0), 0))
    per_b = pl.BlockSpec((None, CONV_CARRY, cdim), lambda i, j: (i, 0, 0))
    outs = pl.pallas_call(
        functools.partial(_even_in_body, tm=tm),
        out_shape=[jax.ShapeDtypeStruct((b, t, wa.shape[1]), F32)]
        + [jax.ShapeDtypeStruct((b, t, inner), BF16)] * 3
        + [jax.ShapeDtypeStruct((b, t, wbab.shape[1]), F32), jax.ShapeDtypeStruct((b, t, wz.shape[1]), F32),
           jax.ShapeDtypeStruct((b, CONV_CARRY, cdim), F32)],
        grid=(b, t // tm),
        in_specs=[row(d), halo, per_b, _resident((1, d)), _resident(wa.shape), _resident(wqkv.shape),
                  _resident(wbab.shape), _resident(wz.shape), _resident(conv_w.shape)],
        out_specs=[row(wa.shape[1]), row(inner), row(inner), row(inner), row(wbab.shape[1]), row(wz.shape[1]), per_b],
        scratch_shapes=[pltpu.VMEM((tm + CONV_CARRY, cdim), F32)],
        compiler_params=_cparams(("parallel", "arbitrary")),
        name="even_in_proj",
    )(x, x, conv0p, nw.reshape(1, d), wa, wqkv, wbab, wz, conv_w.astype(F32))
    return outs[:6], outs[6][:, CONV_CARRY - (CONV_W - 1):]


def _delta_step(q_ref, k_ref, v_ref, bab_ref, z_ref, alog_ref, dtb_ref, nw_ref, y_ref, s_ref, *, nb, tb, L):
    hd = DK_B
    rows = nb * tb
    per_seg = tb // L
    flat = lambda ref: jnp.concatenate([ref[b] for b in range(nb)], axis=0)
    bab = flat(bab_ref)
    q_all, k_all, v_all = flat(q_ref), flat(k_ref), flat(v_ref)
    g_all = -jnp.exp(alog_ref[...]) * _softplus(bab + dtb_ref[...])
    beta_all = _sigmoid(bab)

    tri = _tri_incl(L).astype(BF16)
    rw = _iota((L, L), 0)
    cl_ = _iota((L, L), 1)
    strict = cl_ < rw
    incl = cl_ <= rw
    eye = (rw == cl_).astype(F32)
    nw = nw_ref[...]
    z = flat(z_ref)
    n_chunks = rows // L

    g_cum_all = _chunk_cumsum(g_all, L)
    g_tot_all = _chunk_last(g_cum_all, L)

    qh, kh, vh, kdec, qg, bv_, bk_, gb, beta_b, e_last = {}, {}, {}, {}, {}, {}, {}, {}, {}, {}
    for h in range(H_B):
        qh[h] = q_all[:, h * hd:(h + 1) * hd]
        kh[h] = k_all[:, h * hd:(h + 1) * hd]
        qs, ks = qh[h].astype(F32), kh[h].astype(F32)
        vs = v_all[:, h * hd:(h + 1) * hd].astype(F32)
        gb[h] = jnp.broadcast_to(g_all[:, h:h + 1], (rows, LANES))
        g_cum = jnp.broadcast_to(g_cum_all[:, h:h + 1], (rows, LANES))
        g_tot = jnp.broadcast_to(g_tot_all[:, h:h + 1], (rows, LANES))
        bb = jnp.broadcast_to(beta_all[:, H_B + h:H_B + h + 1], (rows, LANES))
        e_g = jnp.exp(g_cum)
        kdec[h] = (ks * jnp.exp(g_tot - g_cum)).astype(BF16)
        qg[h] = qs * e_g
        bv_[h] = bb * vs
        bk_[h] = bb * e_g * ks
        beta_b[h] = bb
        e_last[h] = jnp.exp(g_tot)

    items = [(c, h) for c in range(n_chunks) for h in range(H_B)]
    rs = {c: slice(c * L, (c + 1) * L) for c in range(n_chunks)}
    diff = {}
    for c, h in items:
        hi, mid, lo = _split3(jnp.where(strict, gb[h][rs[c], :L], 0.0))
        diff[(c, h)] = (jnp.dot(tri, hi, preferred_element_type=F32) + jnp.dot(tri, mid, preferred_element_type=F32)
                        + jnp.dot(tri, lo, preferred_element_type=F32))
    kk_ = {(c, h): _mm_nt(kh[h][rs[c]], kh[h][rs[c]]) for c, h in items}
    qk_ = {(c, h): _mm_nt(qh[h][rs[c]], kh[h][rs[c]]) for c, h in items}
    n_mat, qk_m = {}, {}
    for c, h in items:
        e_diff = jnp.exp(jnp.where(incl, diff[(c, h)], 0.0))
        n_mat[(c, h)] = -(beta_b[h][rs[c], :L] * jnp.where(strict, kk_[(c, h)] * e_diff, 0.0))
        qk_m[(c, h)] = jnp.where(incl, qk_[(c, h)] * e_diff, 0.0).astype(BF16)
    t_inv = {it: eye + n_mat[it] for it in items}
    m_pow = n_mat
    power = 2
    while power < L:
        m_pow = {it: _mm(m_pow[it], m_pow[it]) for it in items}
        t_inv = {it: t_inv[it] + _mm(t_inv[it], m_pow[it]) for it in items}
        power *= 2
    sol = {(c, h): _mm(t_inv[(c, h)], jnp.concatenate([bv_[h][rs[c]], bk_[h][rs[c]]], axis=1)) for c, h in items}
    lhs = {(c, h): jnp.concatenate([sol[(c, h)][:, hd:], qg[h][rs[c]]], axis=0).astype(BF16) for c, h in items}
    chains = [(b, h) for b in range(nb) for h in range(H_B)]
    state = {bh: s_ref[bh[0], bh[1]] for bh in chains}
    u_, o_inter = {}, {}
    for k_pos in range(per_seg):
        for b, h in chains:
            c = b * per_seg + k_pos
            ws = _mm(lhs[(c, h)], state[(b, h)])
            u_[(c, h)] = (sol[(c, h)][:, :hd] - ws[:L]).astype(BF16)
            o_inter[(c, h)] = ws[L:]
        for b, h in chains:
            c = b * per_seg + k_pos
            state[(b, h)] = e_last[h][c * L:c * L + 1, :] * state[(b, h)] + _mm_tn(kdec[h][rs[c]], u_[(c, h)])
    for b, h in chains:
        s_ref[b, h] = state[(b, h)]
    for c, h in items:
        o = o_inter[(c, h)] + _mm(qk_m[(c, h)], u_[(c, h)])
        on = o * lax.rsqrt(jnp.mean(o * o, axis=-1, keepdims=True) + EPS) * nw
        k_pos = c % per_seg
        y_ref[c // per_seg, k_pos * L:(k_pos + 1) * L, h * hd:(h + 1) * hd] = (
            on * _silu(z[rs[c], h * hd:(h + 1) * hd])).astype(y_ref.dtype)


def _gla_step(qk_ref, v_ref, ad_ref, g_ref, a2_ref, ab_ref, nw_ref, y_ref, s_ref, *, tb, L):
    kdim = H_C * DK_C
    n_pairs = kdim // LANES
    qk = qk_ref[...]
    log_a = -_softplus(-(_mm(ad_ref[...], a2_ref[...]) + ab_ref[...])) * (1.0 / GLA_NORMALIZER)
    vv = v_ref[...]
    gg = g_ref[...]
    nw = nw_ref[...]
    incl = _iota((L, L), 1) <= _iota((L, L), 0)
    m0 = _iota((tb, LANES), 1) < DK_C
    n_chunks = tb // L

    cum = _chunk_cumsum(log_a, L)
    tot = _chunk_last(cum, L)
    q_in = qk[:, :kdim] * DK_C ** -0.5 * jnp.exp(cum)
    k_in = (qk[:, kdim:] * jnp.exp(-cum)).astype(BF16)
    k_out = qk[:, kdim:] * jnp.exp(tot - cum)
    dec = jnp.exp(tot)
    vb16 = vv.astype(BF16)

    rs = {c: slice(c * L, (c + 1) * L) for c in range(n_chunks)}
    items = [(c, h) for c in range(n_chunks) for h in range(H_C)]
    qm, km = {}, {}
    for h in range(H_C):
        ls = slice((h // 2) * LANES, (h // 2 + 1) * LANES)
        mask = m0 if h % 2 == 0 else jnp.logical_not(m0)
        qm[h] = jnp.where(mask, q_in[:, ls], 0.0).astype(BF16)
        km[h] = jnp.where(mask, k_out[:, ls], 0.0).astype(BF16)
    scores = {(c, h): jnp.where(incl, _mm_nt(qm[h][rs[c]], k_in[rs[c], (h // 2) * LANES:(h // 2 + 1) * LANES]), 0.0)
              .astype(BF16) for c, h in items}
    upd = {(c, h): _mm_tn(vb16[rs[c], h * DV_C:(h + 1) * DV_C], km[h][rs[c]]) for c, h in items}
    state = [s_ref[p] for p in range(n_pairs)]
    s_prev = {}
    for c in range(n_chunks):
        for p in range(n_pairs):
            s_prev[(c, p)] = state[p].astype(BF16)
            state[p] = (state[p] * dec[c * L:c * L + 1, p * LANES:(p + 1) * LANES]
                        + upd[(c, 2 * p)] + upd[(c, 2 * p + 1)])
    for p in range(n_pairs):
        s_ref[p] = state[p]
    for c, h in items:
        hs = slice(h * DV_C, (h + 1) * DV_C)
        o = _mm_nt(qm[h][rs[c]], s_prev[(c, h // 2)]) + _mm(scores[(c, h)], vb16[rs[c], hs])
        on = o * lax.rsqrt(jnp.mean(o * o, axis=-1, keepdims=True) + EPS) * nw
        y_ref[rs[c], hs] = (on * _silu(gg[rs[c], hs])).astype(y_ref.dtype)


def _odd_mixers_body(qk_ref, cv_ref, ad_ref, cg_ref, gs0_ref, a2_ref, ab_ref, nw_ref,
                     dq_ref, dk_ref, dv_ref, dg_ref, cos_ref, sin_ref, rs0_ref,
                     yc_ref, gs_ref, yd_ref, rs_ref, *, tb, L):
    @pl.when(pl.program_id(1) == 0)
    def _():
        gs_ref[...] = gs0_ref[...]
        rs_ref[...] = rs0_ref[...]

    _gla_step(qk_ref, cv_ref, ad_ref, cg_ref, a2_ref, ab_ref, nw_ref, yc_ref, gs_ref, tb=tb, L=L)
    _ret_step(dq_ref, dk_ref, dv_ref, dg_ref, cos_ref, sin_ref, yd_ref, rs_ref, tb=tb, L=L)


def _odd_mixers(cqk, cv, cad, cg, gla0, wts, dq, dk, dv, dg, ret0, pos0, L):
    b, t, _ = cqk.shape
    kdim = H_C * DK_C
    vdim = H_C * DV_C
    rdim = H_D * DK_D
    n_pairs = kdim // LANES
    tb = min(t, 1024)
    rank = wts['a2'].shape[0]
    a2p = jnp.pad(wts['a2'].astype(F32), ((0, LANES - rank), (0, 0))).astype(BF16)
    params = [a2p, wts['a_bias'].reshape(1, -1).astype(F32), wts['norm_w'].reshape(1, -1).astype(F32)]
    st0 = jnp.swapaxes(gla0.astype(F32), -1, -2).reshape(b, n_pairs, 2, DV_C, DK_C)
    st0 = jnp.concatenate([st0[:, :, 0], st0[:, :, 1]], axis=-1)
    inv = ROPE_BASE ** (-jnp.arange(0, DK_D, 2, dtype=F32) / DK_D)
    ang = (jnp.arange(t) + pos0).astype(F32)[:, None] * inv[None, :]
    cos = jnp.concatenate([jnp.cos(ang), jnp.cos(ang)], axis=-1)
    sin = jnp.concatenate([-jnp.sin(ang), jnp.sin(ang)], axis=-1)
    blk = lambda w: pl.BlockSpec((None, tb, w), lambda i, j: (i, j, 0))
    per_b = lambda shape: pl.BlockSpec((None,) + shape, lambda i, j: (i,) + (0,) * len(shape))
    tab = pl.BlockSpec((tb, DK_D), lambda i, j: (j, 0))
    gst, rst = per_b((n_pairs, DV_C, LANES)), per_b((H_D, DK_D, DK_D))
    y_c, st, y_d, ret_new = pl.pallas_call(
        functools.partial(_odd_mixers_body, tb=tb, L=L),
        out_shape=[jax.ShapeDtypeStruct((b, t, vdim), BF16), jax.ShapeDtypeStruct((b, n_pairs, DV_C, LANES), F32),
                   jax.ShapeDtypeStruct((b, t, rdim), BF16), jax.ShapeDtypeStruct((b, H_D, DK_D, DK_D), F32)],
        grid=(b, t // tb),
        in_specs=[blk(2 * kdim), blk(vdim), blk(LANES), blk(vdim), gst] + [_resident(p.shape) for p in params]
        + [blk(rdim), blk(rdim), blk(rdim), blk(rdim), tab, tab, rst],
        out_specs=[blk(vdim), gst, blk(rdim), rst],
        compiler_params=_cparams(("parallel", "arbitrary")),
        name="gla_retention",
    )(cqk, cv, cad, cg, st0, *params, dq, dk, dv, dg, cos, sin, ret0.astype(F32))
    st = jnp.stack([st[..., :DK_C], st[..., DK_C:]], axis=2).reshape(b, H_C, DV_C, DK_C)
    return y_c, jnp.swapaxes(st, -1, -2), y_d, ret_new


def _ret_step(q_ref, k_ref, v_ref, g_ref, cos_ref, sin_ref, y_ref, s_ref, *, tb, L):
    hd = DK_D
    cos = cos_ref[...]
    sin = sin_ref[...]
    qq, kk, vv, gg = q_ref[...], k_ref[...], v_ref[...], g_ref[...]
    rw = _iota((L, L), 0)
    cl_ = _iota((L, L), 1)
    incl = cl_ <= rw
    dist = jnp.where(incl, rw - cl_, 0).astype(F32)
    pos = _iota((L, LANES), 0).astype(F32)

    n_chunks = tb // L
    rs = {c: slice(c * L, (c + 1) * L) for c in range(n_chunks)}
    items = [(c, h) for c in range(n_chunks) for h in range(H_D)]
    lg = [math.log(1.0 - 2.0 ** (-5.0 - h)) for h in range(H_D)]
    vb16 = vv.astype(BF16)
    qh, kh, q_in, k_out, dmat = {}, {}, {}, {}, {}
    for h in range(H_D):
        hs = slice(h * hd, (h + 1) * hd)
        q_rot = qq[:, hs] * cos + pltpu.roll(qq[:, hs], hd // 2, 1) * sin
        k_rot = (kk[:, hs] * cos + pltpu.roll(kk[:, hs], hd // 2, 1) * sin) * hd ** -0.5
        dmat[h] = jnp.where(incl, jnp.exp(dist * lg[h]), 0.0)
        e_in = jnp.exp((pos + 1.0) * lg[h])
        e_out = jnp.exp((L - 1.0 - pos) * lg[h])
        qh[h], kh[h] = q_rot.astype(BF16), k_rot.astype(BF16)
        for c in range(n_chunks):
            q_in[(c, h)] = (q_rot[rs[c]] * e_in).astype(BF16)
            k_out[(c, h)] = (k_rot[rs[c]] * e_out).astype(BF16)
    scores = {(c, h): (_mm_nt(qh[h][rs[c]], kh[h][rs[c]]) * dmat[h]).astype(BF16) for c, h in items}
    upd = {(c, h): _mm_tn(k_out[(c, h)], vb16[rs[c], h * hd:(h + 1) * hd]) for c, h in items}
    state = [s_ref[h] for h in range(H_D)]
    s_prev = {}
    for c in range(n_chunks):
        for h in range(H_D):
            s_prev[(c, h)] = state[h].astype(BF16)
            state[h] = state[h] * math.exp(lg[h] * L) + upd[(c, h)]
    for h in range(H_D):
        s_ref[h] = state[h]
    for c, h in items:
        hs = slice(h * hd, (h + 1) * hd)
        o = _mm(q_in[(c, h)], s_prev[(c, h)]) + _mm(scores[(c, h)], vb16[rs[c], hs])
        mu = jnp.mean(o, axis=-1, keepdims=True)
        oc = o - mu
        on = oc * lax.rsqrt(jnp.mean(oc * oc, axis=-1, keepdims=True) + EPS)
        y_ref[rs[c], hs] = (on * _silu(gg[rs[c], hs])).astype(y_ref.dtype)


def _prep_weights(W):
    bf = lambda x: x.astype(BF16)
    a_proj = 3 * H_A * N_A + 64 + 64 + 128
    b_conv = 3 * H_B * DK_B
    P = {}
    for name in ('ffn1_wg', 'ffn1_wu', 'ffn1_wd', 'ffn2_wg', 'ffn2_wu', 'ffn2_wd', 'mem_wq', 'mem_wo', 'mem_wk',
                 'mem_wv', 'even_w_out', 'odd_w_out'):
        P[name] = bf(W[name])
    ew = W['even_w_in']
    o = a_proj
    bab = ew[:, :, o + b_conv:o + b_conv + 2 * H_B]
    P['even_in'] = [bf(ew[:, :, :a_proj]), bf(ew[:, :, o:o + b_conv]),
                    bf(jnp.pad(bab, ((0, 0), (0, 0), (0, LANES - 2 * H_B)))), bf(ew[:, :, o + b_conv + 2 * H_B:])]
    ow = W['odd_w_in']
    kd, vd, rank = H_C * DK_C, H_C * DV_C, W['gla_a2'].shape[1]
    c_proj = 2 * kd + 2 * vd + rank
    dd = H_D * DK_D
    cad = ow[:, :, 2 * kd + vd:2 * kd + vd + rank]
    P['odd_in'] = [bf(ow[:, :, :2 * kd]), bf(ow[:, :, 2 * kd:2 * kd + vd]),
                   bf(jnp.pad(cad, ((0, 0), (0, 0), (0, LANES - rank)))), bf(ow[:, :, 2 * kd + vd + rank:c_proj])] \
        + [bf(ow[:, :, c_proj + j * dd:c_proj + (j + 1) * dd]) for j in range(4)]
    return P


def _trunk(x, pos0, mem_k, mem_v, heads, shift, rwkv, conv, delta, gla, ret, W, P):
    b, t, d = x.shape
    n = b * t
    depth = W['norm_ffn1'].shape[0]
    L_delta = min(64, t)
    L_lin = min(64, t)
    flat = lambda z: z.reshape(n, z.shape[-1])
    unflat = lambda z: z.reshape(b, t, z.shape[-1])
    new = {k: [] for k in ('shift', 'rwkv', 'conv', 'delta', 'gla', 'ret')}
    x = flat(x)
    for l in range(depth):
        i = l // 2
        x = _ffn(x, W['norm_ffn1'][l], P['ffn1_wg'][l], P['ffn1_wu'][l], P['ffn1_wd'][l])
        if l % 2 == 0:
            (pa, dq_, dk_, dv_, bab, z), s3 = _even_in_proj(unflat(x), W['norm_mix'][l], [w[i] for w in P['even_in']],
                                                           conv[i], W['delta_conv_w'][i])
            wa = dict(mu=W['rwkv_mu'][i], w0=W['rwkv_w0'][i], w2=W['rwkv_w2'][i], a0=W['rwkv_a0'][i],
                      a2=W['rwkv_a2'][i], g2=W['rwkv_g2'][i], kk=W['rwkv_kk'][i], ka=W['rwkv_ka'][i],
                      rk=W['rwkv_rk'][i], ln_w=W['rwkv_ln_w'][i], ln_b=W['rwkv_ln_b'][i])
            wb = dict(A_log=W['delta_A_log'][i], dt_bias=W['delta_dt_bias'][i], norm_w=W['delta_norm_w'][i])
            y_a, s1, s2, y_b, s4 = _even_mixers(pa, shift[i], rwkv[i], wa, dq_, dk_, dv_, bab, z, delta[i], wb, L_delta)
            new['shift'].append(s1)
            new['rwkv'].append(s2)
            new['conv'].append(s3)
            new['delta'].append(s4)
            split = H_A * N_A
            halves, w_out = (y_a, y_b), P['even_w_out'][i]
        else:
            cqk, cv, cad, cg, dq, dk, dv, dg = _norm_proj(x, W['norm_mix'][l], [w[i] for w in P['odd_in']],
                                                          bf16_out=(1, 6))
            wc = dict(a2=W['gla_a2'][i], a_bias=W['gla_a_bias'][i], norm_w=W['gla_norm_w'][i])
            y_c, s5, y_d, s6 = _odd_mixers(unflat(cqk), unflat(cv), unflat(cad), unflat(cg), gla[i], wc,
                                           unflat(dq), unflat(dk), unflat(dv), unflat(dg), ret[i], pos0, L_lin)
            new['gla'].append(s5)
            new['ret'].append(s6)
            split = H_C * DV_C
            halves, w_out = (y_c, y_d), P['odd_w_out'][i]
        mk = mem_k[l].astype(BF16)
        mv = mem_v[l].astype(BF16)
        x = flat(_mix_out_mem_attn(unflat(x), halves[0], halves[1], w_out[:split], w_out[split:], W['norm_mem'][l],
                                   P['mem_wq'][l], mk, mv, P['mem_wo'][l], heads))
        fw = W['final_norm'] if l == depth - 1 else None
        x = _ffn(x, W['norm_ffn2'][l], P['ffn2_wg'][l], P['ffn2_wu'][l], P['ffn2_wd'][l], fw)
    return unflat(x), new


def kernel(x_prompt, x_sample, mem_prompt, state_rwkv_shift, state_rwkv, state_delta_conv, state_delta, state_gla, state_ret, cache_mem_k, cache_mem_v, norm_ffn1, ffn1_wg, ffn1_wu, ffn1_wd, norm_mix, even_w_in, even_w_out, rwkv_mu, rwkv_w0, rwkv_w2, rwkv_a0, rwkv_a2, rwkv_g2, rwkv_kk, rwkv_ka, rwkv_rk, rwkv_ln_w, rwkv_ln_b, delta_conv_w, delta_A_log, delta_dt_bias, delta_norm_w, odd_w_in, odd_w_out, gla_a2, gla_a_bias, gla_norm_w, norm_mem, mem_norm_kv, mem_wq, mem_wk, mem_wv, mem_wo, norm_ffn2, ffn2_wg, ffn2_wu, ffn2_wd, final_norm):
    W = dict(norm_ffn1=norm_ffn1, ffn1_wg=ffn1_wg, ffn1_wu=ffn1_wu, ffn1_wd=ffn1_wd, norm_mix=norm_mix,
             even_w_in=even_w_in, even_w_out=even_w_out, rwkv_mu=rwkv_mu, rwkv_w0=rwkv_w0, rwkv_w2=rwkv_w2,
             rwkv_a0=rwkv_a0, rwkv_a2=rwkv_a2, rwkv_g2=rwkv_g2, rwkv_kk=rwkv_kk, rwkv_ka=rwkv_ka,
             rwkv_rk=rwkv_rk, rwkv_ln_w=rwkv_ln_w, rwkv_ln_b=rwkv_ln_b, delta_conv_w=delta_conv_w,
             delta_A_log=delta_A_log, delta_dt_bias=delta_dt_bias, delta_norm_w=delta_norm_w,
             odd_w_in=odd_w_in, odd_w_out=odd_w_out, gla_a2=gla_a2, gla_a_bias=gla_a_bias,
             gla_norm_w=gla_norm_w, norm_mem=norm_mem, mem_wq=mem_wq, mem_wk=mem_wk, mem_wv=mem_wv,
             mem_wo=mem_wo, norm_ffn2=norm_ffn2, ffn2_wg=ffn2_wg, ffn2_wu=ffn2_wu, ffn2_wd=ffn2_wd,
             final_norm=final_norm)
    P = _prep_weights(W)
    dt = x_prompt.dtype
    bp, _, d = x_prompt.shape
    depth = norm_ffn1.shape[0]
    n_even, n_odd = (depth + 1) // 2, depth // 2
    heads, hd = cache_mem_k.shape[3], cache_mem_k.shape[4]
    n_mem = mem_prompt.shape[1]
    pk, pv, pk_flat, pv_flat = [], [], [], []
    mem_flat = mem_prompt.reshape(bp * n_mem, d)
    for l in range(depth):
        mk, mv = _norm_proj(mem_flat, mem_norm_kv[l], [P['mem_wk'][l], P['mem_wv'][l]])
        pk_flat.append(mk.reshape(bp, n_mem, d))
        pv_flat.append(mv.reshape(bp, n_mem, d))
        pk.append(mk.reshape(bp, n_mem, heads, hd))
        pv.append(mv.reshape(bp, n_mem, heads, hd))
    zeros = lambda ref, cnt: [jnp.zeros((bp,) + ref.shape[2:], F32)] * cnt
    y_prompt, ps = _trunk(x_prompt, 0, pk_flat, pv_flat, heads, zeros(state_rwkv_shift, n_even), zeros(state_rwkv, n_even),
                          zeros(state_delta_conv, n_even), zeros(state_delta, n_even), zeros(state_gla, n_odd),
                          zeros(state_ret, n_odd), W, P)
    bs = x_sample.shape[0]
    y_sample, ss = _trunk(x_sample, PAST_LEN, [cache_mem_k[l].reshape(bs, n_mem, d) for l in range(depth)],
                          [cache_mem_v[l].reshape(bs, n_mem, d) for l in range(depth)], heads,
                          [state_rwkv_shift[i] for i in range(n_even)], [state_rwkv[i] for i in range(n_even)],
                          [state_delta_conv[i] for i in range(n_even)], [state_delta[i] for i in range(n_even)],
                          [state_gla[i] for i in range(n_odd)], [state_ret[i] for i in range(n_odd)], W, P)
    st = lambda xs: jnp.stack(xs).astype(dt)
    order = ('shift', 'rwkv', 'conv', 'delta', 'gla', 'ret')
    return ((y_prompt, y_sample) + tuple(st(ps[k]) for k in order) + (st(pk), st(pv))
            + tuple(st(ss[k]) for k in order))
```
